```python
import jax
import jax.numpy as jnp
from jax import lax
import numpy as np

D_MODEL = 1024
BATCH = 2
SEQ = 8192
DEPTH = 2

GRID_W = 64
CTX_LEN = 256
EPS = 1e-6
ROPE_BASE = 10000.0

MLA_HEADS = 8
MLA_NOPE = 64
MLA_ROPE = 32
MLA_V = 64
MLA_Q_RANK = 256
MLA_KV_RANK = 128
Q_BLOCK = 128

RET_HEADS = 4
RET_DK = 64
RET_DV = 128
RET_CHUNK = 128

GLA_HEADS = 4
GLA_DK = 64
GLA_DV = 128
GLA_GATE_RANK = 16
GLA_TAU = 16.0
GLA_CHUNK = 64

N_BRANCH = 3
BRANCH_W = 512

D_FF = 2816
N_EXPERTS = 8
TOP_K = 2
N_DENSE = (DEPTH + 1) // 2
N_MOE = DEPTH // 2

IN_SPLITS = (
    MLA_Q_RANK, MLA_KV_RANK, MLA_ROPE,
    RET_HEADS * RET_DK, RET_HEADS * RET_DK, RET_HEADS * RET_DV, RET_HEADS * RET_DV,
    GLA_HEADS * GLA_DK, GLA_HEADS * GLA_DK, GLA_HEADS * GLA_DV, GLA_HEADS * GLA_DV,
    2 * GLA_GATE_RANK,
    N_BRANCH * D_MODEL,
)
IN_TOTAL = sum(IN_SPLITS)

kernel_name = 'hybrid_mla_retnet_gla_moe_prefix_dit'


def rmsnorm(x, w):
    x32 = x.astype(jnp.float32)
    y = x32 * lax.rsqrt(jnp.mean(x32 * x32, axis=-1, keepdims=True) + EPS)
    return (y * w.astype(jnp.float32)).astype(x.dtype)


def modulate(h, shift, scale):
    return h * (1.0 + scale) + shift


def split_heads(a, n_heads):
    b, l, _ = a.shape
    return a.reshape(b, l, n_heads, -1).transpose(0, 2, 1, 3)


def merge_heads(a):
    b, h, l, d = a.shape
    return a.transpose(0, 2, 1, 3).reshape(b, l, h * d)


def rope(x, pos):
    half = x.shape[-1] // 2
    inv_freq = ROPE_BASE ** (-jnp.arange(half, dtype=jnp.float32) / half)
    ang = pos.astype(jnp.float32)[:, None] * inv_freq[None, :]
    cos, sin = jnp.cos(ang), jnp.sin(ang)
    x1 = x[..., :half].astype(jnp.float32)
    x2 = x[..., half:].astype(jnp.float32)
    return jnp.concatenate([x1 * cos - x2 * sin, x1 * sin + x2 * cos], axis=-1).astype(x.dtype)


def axial_rope(x, row, col):
    half = x.shape[-1] // 2
    return jnp.concatenate([rope(x[..., :half], row), rope(x[..., half:], col)], axis=-1)


def to_chunks(a, chunk):
    b, h, l, d = a.shape
    return a.reshape(b, h, l // chunk, chunk, d).transpose(2, 0, 1, 3, 4)


def from_chunks(a):
    n, b, h, chunk, d = a.shape
    return a.transpose(1, 2, 0, 3, 4).reshape(b, h, n * chunk, d)


def mla_qkv(cq, ckv, k_rope, q_norm, w_uq, kv_norm, w_ukv, row=None, col=None):
    q = split_heads(rmsnorm(cq, q_norm) @ w_uq, MLA_HEADS)
    kv = split_heads(rmsnorm(ckv, kv_norm) @ w_ukv, MLA_HEADS)
    q_nope, q_rope = q[..., :MLA_NOPE], q[..., MLA_NOPE:]
    k_nope, v = kv[..., :MLA_NOPE], kv[..., MLA_NOPE:]
    if row is not None:
        q_rope = axial_rope(q_rope, row, col)
        k_rope = axial_rope(k_rope, row, col)
    k_rope = jnp.broadcast_to(k_rope[:, None], k_nope.shape[:-1] + (MLA_ROPE,))
    q = jnp.concatenate([q_nope, q_rope], axis=-1)
    k = jnp.concatenate([k_nope, k_rope], axis=-1)
    return q, k, v


def attend(q, k, v):
    s = jnp.einsum('bhqd,bhkd->bhqk', q, k).astype(jnp.float32) * (q.shape[-1] ** -0.5)
    p = jax.nn.softmax(s, axis=-1).astype(v.dtype)
    return jnp.einsum('bhqk,bhkd->bhqd', p, v)


def blocked_attend(q, k, v):
    b, h, l, d = q.shape
    nb = l // Q_BLOCK
    qb = q.reshape(b, h, nb, Q_BLOCK, d).transpose(2, 0, 1, 3, 4)
    ob = lax.map(lambda qi: attend(qi, k, v), qb)
    return ob.transpose(1, 2, 0, 3, 4).reshape(b, h, l, v.shape[-1])


def retention_scan(q, k, v, log_gamma, s0):
    C = RET_CHUNK
    idx = jnp.arange(C, dtype=jnp.float32)
    lg = log_gamma.astype(jnp.float32)[:, None]
    diff = idx[:, None] - idx[None, :]
    intra = jnp.where(diff >= 0, jnp.exp(lg[:, :, None] * jnp.maximum(diff, 0.0)), 0.0)
    q_dec = jnp.exp(lg * (idx + 1.0))[:, :, None]
    k_dec = jnp.exp(lg * (C - 1.0 - idx))[:, :, None]
    chunk_dec = jnp.exp(lg * C)[:, :, None]

    def step(s, xs):
        qc, kc, vc = xs
        scores = jnp.einsum('bhid,bhjd->bhij', qc, kc) * intra
        o = jnp.einsum('bhij,bhjv->bhiv', scores, vc) + jnp.einsum('bhid,bhdv->bhiv', qc, s) * q_dec
        s = s * chunk_dec + jnp.einsum('bhjd,bhjv->bhdv', kc * k_dec, vc)
        return s.astype(jnp.float32), o.astype(jnp.float32)

    s, o = lax.scan(step, s0, (to_chunks(q, C), to_chunks(k, C), to_chunks(v, C)))
    return from_chunks(o), s


def gla_scan(q, k, v, log_a, s0):
    C = GLA_CHUNK
    lower = jnp.tril(jnp.ones((C, C), dtype=bool))[:, :, None]

    def step(s, xs):
        qc, kc, vc, ac = xs
        b = jnp.cumsum(ac.astype(jnp.float32), axis=2)
        rel = b[:, :, :, None, :] - b[:, :, None, :, :]
        decay = jnp.exp(jnp.where(lower, rel, -jnp.inf))
        scores = jnp.einsum('bhid,bhjd,bhijd->bhij', qc, kc, decay)
        o = jnp.einsum('bhij,bhjv->bhiv', scores, vc) + jnp.einsum('bhid,bhdv->bhiv', qc * jnp.exp(b), s)
        b_end = b[:, :, -1:, :]
        s = s * jnp.swapaxes(jnp.exp(b_end), 2, 3) + jnp.einsum('bhjd,bhjv->bhdv', kc * jnp.exp(b_end - b), vc)
        return s.astype(jnp.float32), o.astype(jnp.float32)

    s, o = lax.scan(step, s0, (to_chunks(q, C), to_chunks(k, C), to_chunks(v, C), to_chunks(log_a, C)))
    return from_chunks(o), s


def two_way(scan_fn, q, k, v, dec_f, dec_b, s0_f, s0_b, per_token):
    flip = lambda a: jnp.flip(a, axis=2)
    o_f, s_f = scan_fn(q, k, v, dec_f, s0_f)
    o_b, s_b = scan_fn(flip(q), flip(k), flip(v), flip(dec_b) if per_token else dec_b, s0_b)
    return o_f + flip(o_b), s_f, s_b


def gla_log_gates(gz, w_gate, b_gate):
    z_f = gz[..., :GLA_GATE_RANK] @ w_gate[0] + b_gate[0]
    z_b = gz[..., GLA_GATE_RANK:] @ w_gate[1] + b_gate[1]
    to_log_gate = lambda z: split_heads(jax.nn.log_sigmoid(z.astype(jnp.float32)) / GLA_TAU, GLA_HEADS)
    return to_log_gate(z_f), to_log_gate(z_b)


def head_norm(o, w, center):
    o = o.astype(jnp.float32)
    if center:
        o = o - jnp.mean(o, axis=-1, keepdims=True)
    o = o * lax.rsqrt(jnp.mean(o * o, axis=-1, keepdims=True) + EPS)
    return merge_heads(o) * w.astype(jnp.float32)


def mixer_block(u_c, u_x, row, col, pos, w_in, mla_q_norm, mla_w_uq, mla_kv_norm, mla_w_ukv,
                ret_decay_logit, ret_norm_w, gla_w_gate, gla_b_gate, gla_norm_w, w_branch, w_out,
                need_ctx):
    b = u_x.shape[0]
    offs = np.cumsum(IN_SPLITS)[:-1].tolist()
    (cq_c, ckv_c, kr_c, rq_c, rk_c, rv_c, rg_c, gq_c, gk_c, gv_c, gr_c, gz_c, bg_c) = jnp.split(u_c @ w_in, offs, axis=-1)
    (cq_x, ckv_x, kr_x, rq_x, rk_x, rv_x, rg_x, gq_x, gk_x, gv_x, gr_x, gz_x, bg_x) = jnp.split(u_x @ w_in, offs, axis=-1)

    qa_c, ka_c, va_c = mla_qkv(cq_c, ckv_c, kr_c, mla_q_norm, mla_w_uq, mla_kv_norm, mla_w_ukv)
    qa_x, ka_x, va_x = mla_qkv(cq_x, ckv_x, kr_x, mla_q_norm, mla_w_uq, mla_kv_norm, mla_w_ukv, row, col)
    oa_x = blocked_attend(qa_x, jnp.concatenate([ka_c, ka_x], axis=2), jnp.concatenate([va_c, va_x], axis=2))

    k_scale = RET_DK ** -0.5
    rq_c, rk_c, rv_c = split_heads(rq_c, RET_HEADS), split_heads(rk_c, RET_HEADS) * k_scale, split_heads(rv_c, RET_HEADS)
    rq_x = rope(split_heads(rq_x, RET_HEADS), pos)
    rk_x = rope(split_heads(rk_x, RET_HEADS), pos) * k_scale
    rv_x = split_heads(rv_x, RET_HEADS)
    log_gamma = jax.nn.log_sigmoid(ret_decay_logit.astype(jnp.float32))
    s0_r = jnp.zeros((b, RET_HEADS, RET_DK, RET_DV), jnp.float32)
    ob_c, sr_f, sr_b = two_way(retention_scan, rq_c, rk_c, rv_c, log_gamma[0], log_gamma[1], s0_r, s0_r, False)
    ob_x, _, _ = two_way(retention_scan, rq_x, rk_x, rv_x, log_gamma[0], log_gamma[1], sr_f, sr_b, False)

    q_scale = GLA_DK ** -0.5
    gq_c, gk_c, gv_c = split_heads(gq_c, GLA_HEADS) * q_scale, split_heads(gk_c, GLA_HEADS), split_heads(gv_c, GLA_HEADS)
    gq_x, gk_x, gv_x = split_heads(gq_x, GLA_HEADS) * q_scale, split_heads(gk_x, GLA_HEADS), split_heads(gv_x, GLA_HEADS)
    la_f_c, la_b_c = gla_log_gates(gz_c, gla_w_gate, gla_b_gate)
    la_f_x, la_b_x = gla_log_gates(gz_x, gla_w_gate, gla_b_gate)
    s0_g = jnp.zeros((b, GLA_HEADS, GLA_DK, GLA_DV), jnp.float32)
    oc_c, sg_f, sg_b = two_way(gla_scan, gq_c, gk_c, gv_c, la_f_c, la_b_c, s0_g, s0_g, True)
    oc_x, _, _ = two_way(gla_scan, gq_x, gk_x, gv_x, la_f_x, la_b_x, sg_f, sg_b, True)

    def merge(oa, ob, rg, oc, gr, bg, dtype):
        ya = merge_heads(oa)
        yb = jax.nn.silu(rg) * head_norm(ob, ret_norm_w, True)
        yc = jax.nn.silu(gr) * head_norm(oc, gla_norm_w, False)
        g_a, g_b, g_c = jnp.split(jax.nn.sigmoid(bg.astype(jnp.float32)), N_BRANCH, axis=-1)
        z = g_a * (ya @ w_branch[0]) + g_b * (yb @ w_branch[1]) + g_c * (yc @ w_branch[2])
        return (z @ w_out).astype(dtype)

    y_x = merge(oa_x, ob_x, rg_x, oc_x, gr_x, bg_x, u_x.dtype)
    y_c = merge(attend(qa_c, ka_c, va_c), ob_c, rg_c, oc_c, gr_c, bg_c, u_c.dtype) if need_ctx else None
    return y_c, y_x


def swiglu(u, w_gate, w_up, w_down):
    return (jax.nn.silu(u @ w_gate) * (u @ w_up)) @ w_down


def moe_swiglu(u, router, w_gate, w_up, w_down):
    logits = (u @ router).astype(jnp.float32)
    top_val, top_idx = lax.top_k(logits, TOP_K)
    top_w = jax.nn.softmax(top_val, axis=-1)
    gates = jnp.einsum('blk,blke->ble', top_w, jax.nn.one_hot(top_idx, N_EXPERTS, dtype=jnp.float32))
    y = jnp.zeros(u.shape, jnp.float32)
    for e in range(N_EXPERTS):
        y = y + gates[..., e:e + 1] * swiglu(u, w_gate[e], w_up[e], w_down[e])
    return y.astype(u.dtype)


def channel_mixer(u, l, ffn_w_gate, ffn_w_up, ffn_w_down, moe_router, moe_w_gate, moe_w_up, moe_w_down):
    i = l // 2
    if l % 2 == 0:
        return swiglu(u, ffn_w_gate[i], ffn_w_up[i], ffn_w_down[i])
    return moe_swiglu(u, moe_router[i], moe_w_gate[i], moe_w_up[i], moe_w_down[i])


def setup_inputs(seed: int = 0) -> dict:
    key = jax.random.key(seed)
    k = jax.random.split(key, 28)
    f32 = jnp.float32

    def w(i, shape, fan_in, gain=1.0):
        return jax.random.normal(k[i], shape, f32) * (gain * fan_in ** -0.5)

    def g(i, shape):
        return 1.0 + 0.05 * jax.random.normal(k[i], shape, f32)

    head_ids = jnp.arange(RET_HEADS, dtype=f32)
    ret_logit0 = jnp.log(2.0 ** (5.0 + head_ids) - 1.0)
    return {
        'x': jax.random.normal(k[0], (BATCH, SEQ, D_MODEL), f32),
        'c': jax.random.normal(k[1], (BATCH, D_MODEL), f32),
        'ctx': jax.random.normal(k[2], (BATCH, CTX_LEN, D_MODEL), f32),
        'c_ctx': jax.random.normal(k[3], (D_MODEL,), f32),
        'mod_w': w(4, (DEPTH, D_MODEL, 6 * D_MODEL), D_MODEL, 0.5),
        'mod_b': 0.02 * jax.random.normal(k[5], (DEPTH, 6 * D_MODEL), f32),
        'norm1_w': g(6, (DEPTH, D_MODEL)),
        'norm2_w': g(7, (DEPTH, D_MODEL)),
        'w_in': w(8, (DEPTH, D_MODEL, IN_TOTAL), D_MODEL),
        'mla_q_norm': g(9, (DEPTH, MLA_Q_RANK)),
        'mla_w_uq': w(10, (DEPTH, MLA_Q_RANK, MLA_HEADS * (MLA_NOPE + MLA_ROPE)), MLA_Q_RANK),
        'mla_kv_norm': g(11, (DEPTH, MLA_KV_RANK)),
        'mla_w_ukv': w(12, (DEPTH, MLA_KV_RANK, MLA_HEADS * (MLA_NOPE + MLA_V)), MLA_KV_RANK),
        'ret_decay_logit': ret_logit0 + 0.1 * jax.random.normal(k[13], (DEPTH, 2, RET_HEADS), f32),
        'ret_norm_w': g(14, (DEPTH, RET_HEADS * RET_DV)),
        'gla_w_gate': w(15, (DEPTH, 2, GLA_GATE_RANK, GLA_HEADS * GLA_DK), GLA_GATE_RANK),
        'gla_b_gate': 0.1 * jax.random.normal(k[16], (DEPTH, 2, GLA_HEADS * GLA_DK), f32),
        'gla_norm_w': g(17, (DEPTH, GLA_HEADS * GLA_DV)),
        'w_branch': w(18, (DEPTH, N_BRANCH, BRANCH_W, D_MODEL), BRANCH_W),
        'w_out': w(19, (DEPTH, D_MODEL, D_MODEL), D_MODEL),
        'ffn_w_gate': w(20, (N_DENSE, D_MODEL, D_FF), D_MODEL),
        'ffn_w_up': w(21, (N_DENSE, D_MODEL, D_FF), D_MODEL),
        'ffn_w_down': w(22, (N_DENSE, D_FF, D_MODEL), D_FF),
        'moe_router': w(23, (N_MOE, D_MODEL, N_EXPERTS), D_MODEL),
        'moe_w_gate': w(24, (N_MOE, N_EXPERTS, D_MODEL, D_FF), D_MODEL),
        'moe_w_up': w(25, (N_MOE, N_EXPERTS, D_MODEL, D_FF), D_MODEL),
        'moe_w_down': w(26, (N_MOE, N_EXPERTS, D_FF, D_MODEL), D_FF),
        'final_norm_w': g(27, (D_MODEL,)),
    }


def reference(x, c, ctx, c_ctx, mod_w, mod_b, norm1_w, norm2_w, w_in, mla_q_norm, mla_w_uq,
              mla_kv_norm, mla_w_ukv, ret_decay_logit, ret_norm_w, gla_w_gate, gla_b_gate,
              gla_norm_w, w_branch, w_out, ffn_w_gate, ffn_w_up, ffn_w_down, moe_router,
              moe_w_gate, moe_w_up, moe_w_down, final_norm_w):
    L = x.shape[1]
    rows = L // GRID_W
    row = jnp.repeat(jnp.arange(rows, dtype=jnp.int32), GRID_W)
    col = jnp.tile(jnp.arange(GRID_W, dtype=jnp.int32), rows)
    pos = jnp.arange(L, dtype=jnp.int32)
    for l in range(DEPTH):
        last = l == DEPTH - 1
        mod_x = jax.nn.silu(c) @ mod_w[l] + mod_b[l]
        mod_c = jax.nn.silu(c_ctx) @ mod_w[l] + mod_b[l]
        sh1x, sc1x, g1x, sh2x, sc2x, g2x = [m[:, None, :] for m in jnp.split(mod_x, 6, axis=-1)]
        sh1c, sc1c, g1c, sh2c, sc2c, g2c = jnp.split(mod_c, 6, axis=-1)

        u_x = modulate(rmsnorm(x, norm1_w[l]), sh1x, sc1x)
        u_c = modulate(rmsnorm(ctx, norm1_w[l]), sh1c, sc1c)
        y_c, y_x = mixer_block(u_c, u_x, row, col, pos, w_in[l], mla_q_norm[l], mla_w_uq[l],
                               mla_kv_norm[l], mla_w_ukv[l], ret_decay_logit[l], ret_norm_w[l],
                               gla_w_gate[l], gla_b_gate[l], gla_norm_w[l], w_branch[l], w_out[l],
                               not last)
        x = x + g1x * y_x
        v_x = modulate(rmsnorm(x, norm2_w[l]), sh2x, sc2x)
        x = x + g2x * channel_mixer(v_x, l, ffn_w_gate, ffn_w_up, ffn_w_down,
                                    moe_router, moe_w_gate, moe_w_up, moe_w_down)
        if not last:
            ctx = ctx + g1c * y_c
            v_c = modulate(rmsnorm(ctx, norm2_w[l]), sh2c, sc2c)
            ctx = ctx + g2c * channel_mixer(v_c, l, ffn_w_gate, ffn_w_up, ffn_w_down,
                                            moe_router, moe_w_gate, moe_w_up, moe_w_down)
    return rmsnorm(x, final_norm_w)
```

```python
import functools

import numpy as np
import jax
import jax.numpy as jnp
from jax import lax
from jax.experimental import pallas as pl
from jax.experimental.pallas import tpu as pltpu

D_MODEL = 1024
GRID_W = 64
EPS = 1e-6
ROPE_BASE = 10000.0

MLA_HEADS = 8
MLA_NOPE = 64
MLA_ROPE = 32
MLA_V = 64
MLA_Q_RANK = 256
MLA_KV_RANK = 128

RET_HEADS = 4
RET_DK = 64
RET_DV = 128

GLA_HEADS = 4
GLA_DK = 64
GLA_DV = 128
GLA_GATE_RANK = 16
GLA_TAU = 16.0

N_BRANCH = 3
BRANCH_W = 512
D_FF = 2816
N_EXPERTS = 8

IN_SPLITS = (
    MLA_Q_RANK, MLA_KV_RANK, MLA_ROPE,
    RET_HEADS * RET_DK, RET_HEADS * RET_DK, RET_HEADS * RET_DV, RET_HEADS * RET_DV,
    GLA_HEADS * GLA_DK, GLA_HEADS * GLA_DK, GLA_HEADS * GLA_DV, GLA_HEADS * GLA_DV,
    2 * GLA_GATE_RANK,
    N_BRANCH * D_MODEL,
)

F32 = jnp.float32
BF16 = jnp.bfloat16
HI = lax.Precision.HIGHEST

LANES = 128
HEAD_SLOT = 128
TOKEN_TILE = 256
ATT_Q_TILE = 256
ATT_KV_TILE = 512
RET_CHUNK = 256
GLA_CHUNK = 128
GLA_BASE = 16
MOD_COLS_TILE = 1536
VMEM_LIMIT = 56 * 1024 * 1024


def _silu(x):
    return x / (1.0 + jnp.exp(-x))


def _sigmoid(x):
    return 1.0 / (1.0 + jnp.exp(-x))


def _log_sigmoid(z):
    return jnp.minimum(z, 0.0) - jnp.log(1.0 + jnp.exp(-jnp.abs(z)))


def _rms(x, w):
    return x * lax.rsqrt(jnp.mean(x * x, axis=-1, keepdims=True) + EPS) * w


def _dot(a, b, precision=None):
    return jnp.dot(a, b, preferred_element_type=F32, precision=precision)


def _dot_nt(a, b):
    return lax.dot_general(a, b, (((1,), (1,)), ((), ())), preferred_element_type=F32)


def _params(sem):
    return pltpu.CompilerParams(dimension_semantics=sem, vmem_limit_bytes=VMEM_LIMIT)


def _const_spec(shape):
    nd = len(shape)
    return pl.BlockSpec(shape, lambda *_: (0,) * nd)


def _mod_rows(mod_ref, is_ctx):
    m = mod_ref[0]
    return jnp.where(is_ctx, m[0:1], m[1:2])


def _mod_kernel(c_ref, w_ref, b_ref, o_ref):
    o_ref[0] = _dot(_silu(c_ref[...]), w_ref[0], HI) + b_ref[0]


def _modulation(cvec, mod_w, mod_b):
    depth, d, n6 = mod_w.shape
    rows = cvec.shape[0]
    tn = MOD_COLS_TILE
    return pl.pallas_call(
        _mod_kernel,
        grid=(depth, n6 // tn),
        in_specs=[pl.BlockSpec((rows, d), lambda l, j: (0, 0)),
                  pl.BlockSpec((1, d, tn), lambda l, j: (l, 0, j)),
                  pl.BlockSpec((1, 1, tn), lambda l, j: (l, 0, j))],
        out_specs=pl.BlockSpec((1, rows, tn), lambda l, j: (l, 0, j)),
        out_shape=jax.ShapeDtypeStruct((depth, rows, n6), F32),
        compiler_params=_params(("arbitrary", "arbitrary")),
    )(cvec, mod_w, mod_b.reshape(depth, 1, n6))


_C_CQ = (0, 256)
_C_CKV = (256, 384)
_C_KRA = (384, 512)
_C_KRB = (512, 640)
_C_RQ = (640, 896)
_C_RQS = (896, 1152)
_C_RK = (1152, 1408)
_C_RKS = (1408, 1664)
_C_RV = (1664, 2176)
_C_RG = (2176, 2688)
_C_GQ = (2688, 2944)
_C_GK = (2944, 3200)
_C_GV = (3200, 3712)
_C_GR = (3712, 4224)
_C_GZ = (4224, 4352)
_W1_COLS = 4352


def _inproj_kernel(nct, x_ref, mod_ref, nw_ref, w_ref, qn_ref, wq_ref, wqs_ref, kvn_ref, wk_ref,
                   wv_ref, wg2_ref, bg2_ref, cq_ref, sq_ref, cr_ref, sr_ref,
                   q_out, k_out, v_out, rq_out, rk_out, rv_out, rg_out,
                   gq_out, gk_out, gv_out, gr_out, la_out):
    d = D_MODEL
    row = _mod_rows(mod_ref, pl.program_id(1) < nct)
    sh, sc = row[:, 0:d], row[:, d:2 * d]
    u = (_rms(x_ref[0], nw_ref[...]) * (1.0 + sc) + sh).astype(BF16)

    def seg(c):
        return _dot(u, w_ref[:, c[0]:c[1]])

    cqn = _rms(seg(_C_CQ), qn_ref[...]).astype(BF16)
    q_all = _dot(cqn, wq_ref[...])
    q_swp = _dot(cqn, wqs_ref[...])
    ckvn = _rms(seg(_C_CKV), kvn_ref[...]).astype(BF16)
    k_all = _dot(ckvn, wk_ref[...])
    v_out[0] = _dot(ckvn, wv_ref[...]).astype(BF16)
    cosq, sinq = cq_ref[...], sq_ref[...]
    k_rope = seg(_C_KRA) * cosq + seg(_C_KRB) * sinq
    for h in range(MLA_HEADS):
        sl = slice(h * HEAD_SLOT, (h + 1) * HEAD_SLOT)
        q_out[0, h] = (q_all[:, sl] * cosq + q_swp[:, sl] * sinq).astype(BF16)
        k_out[0, h] = (k_all[:, sl] + k_rope).astype(BF16)

    cosr, sinr = cr_ref[...], sr_ref[...]
    rq_out[0] = (seg(_C_RQ) * cosr + seg(_C_RQS) * sinr).astype(BF16)
    rk_out[0] = (seg(_C_RK) * cosr + seg(_C_RKS) * sinr).astype(BF16)
    rv_out[0] = seg(_C_RV).astype(BF16)
    rg_out[0] = seg(_C_RG).astype(BF16)

    gq_out[0] = seg(_C_GQ).astype(BF16)
    gk_out[0] = seg(_C_GK).astype(BF16)
    gv_out[0] = seg(_C_GV).astype(BF16)
    gr_out[0] = seg(_C_GR).astype(BF16)
    z = _dot(seg(_C_GZ), wg2_ref[...], HI) + bg2_ref[...]
    la_out[0] = _log_sigmoid(z) * (1.0 / GLA_TAU)


def _inproj(xs, modsel, nw, w1, qn, wq, wqs, kvn, wk, wv, wg2, bg2, cosq, sinq, cosr, sinr, nct):
    b, n, d = xs.shape
    tm = TOKEN_TILE
    nt = n // tm
    h = MLA_HEADS
    tok = lambda w: pl.BlockSpec((1, tm, w), lambda i, t: (i, t, 0))
    tab = lambda w: pl.BlockSpec((tm, w), lambda i, t: (t, 0))
    qk_spec = pl.BlockSpec((1, h, tm, HEAD_SLOT), lambda i, t: (i, 0, t, 0))
    sds = lambda w, dt=BF16: jax.ShapeDtypeStruct((b, n, w), dt)
    qk_sds = jax.ShapeDtypeStruct((b, h, n, HEAD_SLOT), BF16)
    return pl.pallas_call(
        functools.partial(_inproj_kernel, nct),
        grid=(b, nt),
        in_specs=[tok(d), pl.BlockSpec((1, 2, 6 * d), lambda i, t: (i, 0, 0)), _const_spec(nw.shape),
                  _const_spec(w1.shape), _const_spec(qn.shape), _const_spec(wq.shape),
                  _const_spec(wqs.shape), _const_spec(kvn.shape), _const_spec(wk.shape),
                  _const_spec(wv.shape), _const_spec(wg2.shape), _const_spec(bg2.shape),
                  tab(HEAD_SLOT), tab(HEAD_SLOT), tab(256), tab(256)],
        out_specs=[qk_spec, qk_spec, tok(512), tok(256), tok(256), tok(512), tok(512),
                   tok(256), tok(256), tok(512), tok(512), tok(512)],
        out_shape=[qk_sds, qk_sds, sds(512), sds(256), sds(256), sds(512), sds(512),
                   sds(256), sds(256), sds(512), sds(512), sds(512, F32)],
        compiler_params=_params(("arbitrary", "arbitrary")),
    )(xs, modsel, nw, w1, qn, wq, wqs, kvn, wk, wv, wg2, bg2, cosq, sinq, cosr, sinr)


def _attn_kernel(nct, n_ctx, n_kv, q_ref, k_ref, v_ref, o_ref):
    tq, tk = ATT_Q_TILE, ATT_KV_TILE
    n_chunks = jnp.where(pl.program_id(2) < nct, 0, (n_kv - n_ctx) // tk)
    outs = []
    for j in range(2):
        q = q_ref[0, j]

        def update(carry, start, size, j=j, q=q):
            m, l, acc = carry
            s = _dot_nt(q, k_ref[0, j, pl.ds(start, size), :])
            m_new = jnp.maximum(m, jnp.max(s, axis=-1, keepdims=True))
            p = jnp.exp(s - m_new)
            alpha = jnp.exp(m - m_new)
            l = alpha * l + jnp.sum(p, axis=-1, keepdims=True)
            acc = alpha * acc + _dot(p.astype(BF16), v_ref[0, pl.ds(start, size), :])
            return m_new, l, acc

        init = (jnp.full((tq, 1), -jnp.inf, F32), jnp.zeros((tq, 1), F32), jnp.zeros((tq, LANES), F32))
        carry = update(init, 0, n_ctx)

        def body(c, carry, update=update):
            return update(carry, pl.multiple_of(n_ctx + c * tk, LANES), tk)

        m, l, acc = lax.fori_loop(0, n_chunks, body, carry)
        outs.append(acc / l)
    lane = lax.broadcasted_iota(jnp.int32, (tq, LANES), 1)
    o_ref[0] = jnp.where(lane < MLA_V, outs[0], outs[1]).astype(BF16)


def _attention(q, k, v, nct, n_ctx):
    b, h, n, w = q.shape
    tq = ATT_Q_TILE
    return pl.pallas_call(
        functools.partial(_attn_kernel, nct, n_ctx, n),
        grid=(b, h // 2, n // tq),
        in_specs=[pl.BlockSpec((1, 2, tq, w), lambda i, p, t: (i, p, t, 0)),
                  pl.BlockSpec((1, 2, n, w), lambda i, p, t: (i, p, 0, 0)),
                  pl.BlockSpec((1, n, LANES), lambda i, p, t: (i, 0, p))],
        out_specs=pl.BlockSpec((1, tq, LANES), lambda i, p, t: (i, t, p)),
        out_shape=jax.ShapeDtypeStruct((b, n, h * MLA_V), BF16),
        compiler_params=_params(("arbitrary", "arbitrary", "arbitrary")),
    )(q, k, v)


def _bwd_chunk(s, nct, nch):
    return jnp.where(s < nct, nct - 1 - s, nch - 1 - (s - nct))


def _ret_kernel(dl_ref, qf, kf, vf, qb, kb, vb, of_ref, ob_ref, st_ref):
    c = RET_CHUNK
    hw = RET_HEADS * RET_DK

    @pl.when(pl.program_id(1) == 0)
    def _():
        st_ref[...] = jnp.zeros_like(st_ref)

    ii = lax.broadcasted_iota(jnp.int32, (c, c), 0)
    jj = lax.broadcasted_iota(jnp.int32, (c, c), 1)
    pos_i = lax.broadcasted_iota(jnp.int32, (c, 1), 0).astype(F32)
    pos_j = lax.broadcasted_iota(jnp.int32, (1, c), 1).astype(F32)
    lane_head = lax.broadcasted_iota(jnp.int32, (c, hw), 1) // RET_DK
    for d, (q_ref, k_ref, v_ref, o_ref) in enumerate(((qf, kf, vf, of_ref), (qb, kb, vb, ob_ref))):
        rev = d == 1
        q, v = q_ref[0], v_ref[0]
        k_t = k_ref[0].astype(F32).T
        k_tb = k_t.astype(BF16)
        st = st_ref[d]
        st_b = st.astype(BF16)
        dist = ((jj - ii) if rev else (ii - jj)).astype(F32)
        for h in range(RET_HEADS):
            lg = _log_sigmoid(dl_ref[d, h])[0:1, 0:1]
            intra = jnp.where(dist >= 0.0, jnp.exp(lg * jnp.maximum(dist, 0.0)), 0.0)
            q_dec = jnp.exp(lg * ((c - pos_i) if rev else (pos_i + 1.0)))
            k_dec = jnp.exp(lg * (pos_j if rev else (c - 1.0 - pos_j)))
            rows = slice(h * RET_DK, (h + 1) * RET_DK)
            cols = slice(h * RET_DV, (h + 1) * RET_DV)
            qm = jnp.where(lane_head == h, q, jnp.zeros_like(q))
            scores = _dot(qm, k_tb) * intra
            o = _dot(scores.astype(BF16), v[:, cols]) + _dot(qm, st_b) * q_dec
            o_ref[0, :, cols] = o
            kd = (k_t[rows] * k_dec).astype(BF16)
            st_ref[d, rows, :] = st[rows] * jnp.exp(lg * float(c)) + _dot(kd, v[:, cols])


def _retention(dl, rq, rk, rv, nct):
    b, n, hw = rq.shape
    c = RET_CHUNK
    nch = n // c
    vw = rv.shape[-1]
    fwd = lambda w: pl.BlockSpec((1, c, w), lambda i, s: (i, s, 0))
    bwd = lambda w: pl.BlockSpec((1, c, w), lambda i, s: (i, _bwd_chunk(s, nct, nch), 0))
    return pl.pallas_call(
        _ret_kernel,
        grid=(b, nch),
        in_specs=[_const_spec(dl.shape), fwd(hw), fwd(hw), fwd(vw), bwd(hw), bwd(hw), bwd(vw)],
        out_specs=[fwd(vw), bwd(vw)],
        out_shape=[jax.ShapeDtypeStruct((b, n, vw), F32)] * 2,
        scratch_shapes=[pltpu.VMEM((2, hw, RET_DV), F32)],
        compiler_params=_params(("arbitrary", "arbitrary")),
    )(dl, rq, rk, rv, rq, rk, rv)


def _block_start(i, size):
    return i & ~(size - 1)


def _gla_levels():
    half = GLA_CHUNK // 2
    out = []
    while half >= GLA_BASE:
        out.append(half)
        half //= 2
    return out


def _gla_kernel(qf, kf, vf, laf, qb, kb, vb, lab, of_ref, ob_ref, st_ref):
    c = GLA_CHUNK
    hw = GLA_HEADS * GLA_DK
    levels = _gla_levels()

    @pl.when(pl.program_id(1) == 0)
    def _():
        st_ref[...] = jnp.zeros_like(st_ref)

    ii = lax.broadcasted_iota(jnp.int32, (c, c), 0)
    jj = lax.broadcasted_iota(jnp.int32, (c, c), 1)
    pos = lax.broadcasted_iota(jnp.int32, (c, 1), 0)
    lane_head = lax.broadcasted_iota(jnp.int32, (c, hw), 1) // GLA_DK
    for d, (q_ref, k_ref, v_ref, la_ref, o_ref) in enumerate(((qf, kf, vf, laf, of_ref),
                                                               (qb, kb, vb, lab, ob_ref))):
        rev = d == 1
        q, k, v, la = q_ref[0].astype(F32), k_ref[0].astype(F32), v_ref[0], la_ref[0]

        def prefix(ref_row):
            return ((jj >= ref_row) if rev else (jj <= ref_row)).astype(F32)

        mats = [prefix(ii)]
        for half in levels:
            blk = _block_start(ii, 2 * half)
            mats.append(prefix(blk + (half if rev else half - 1)))
        blk = _block_start(ii, GLA_BASE)
        mats.append(prefix(blk + (GLA_BASE - 1 if rev else 0)))
        cums = _dot(jnp.concatenate(mats, axis=0), la, HI)
        b = cums[0:c]
        b_end = b[0:1] if rev else b[c - 1:c]

        q_t = (q * jnp.exp(b)).astype(BF16)
        k_tt = (k * jnp.exp(b_end - b)).T
        dec_col = jnp.exp(jnp.broadcast_to(b_end, (LANES, hw)).T[:, 0:1])

        terms = []
        for n_l, half in enumerate(levels):
            e = cums[(n_l + 1) * c:(n_l + 2) * c]
            inner = pos & (2 * half - 1)
            late = (inner < half) if rev else (inner >= half)
            qh = jnp.where(late, q * jnp.exp(jnp.where(late, b - e, 0.0)), 0.0)
            kh = jnp.where(late, 0.0, k * jnp.exp(jnp.where(late, 0.0, e - b)))
            terms.append((qh.astype(BF16), kh.astype(BF16), _block_start(ii, 2 * half) == _block_start(jj, 2 * half)))
        r = cums[(len(levels) + 1) * c:(len(levels) + 2) * c]
        tri = (jj >= ii) if rev else (jj <= ii)
        terms.append(((q * jnp.exp(b - r)).astype(BF16), (k * jnp.exp(r - b)).astype(BF16),
                      (_block_start(ii, GLA_BASE) == _block_start(jj, GLA_BASE)) & tri))

        st = st_ref[d]
        st_b = st.astype(BF16)
        for h in range(GLA_HEADS):
            rows = slice(h * GLA_DK, (h + 1) * GLA_DK)
            cols = slice(h * GLA_DV, (h + 1) * GLA_DV)
            sel = lane_head == h
            a = jnp.zeros((c, c), F32)
            for qh, kh, mask in terms:
                a = a + jnp.where(mask, _dot_nt(jnp.where(sel, qh, jnp.zeros_like(qh)), kh), 0.0)
            vh = v[:, cols]
            o_ref[0, :, cols] = _dot(a.astype(BF16), vh) + _dot(jnp.where(sel, q_t, jnp.zeros_like(q_t)), st_b)
            st_ref[d, rows, :] = st[rows] * dec_col[rows] + _dot(k_tt[rows].astype(BF16), vh)


def _gla(gq, gk, gv, la, nct_tokens):
    b, n, hw = gq.shape
    c = GLA_CHUNK
    nch = n // c
    nct = nct_tokens // c
    vw = gv.shape[-1]
    fwd = lambda w, col=0: pl.BlockSpec((1, c, w), lambda i, s: (i, s, col))
    bwd = lambda w, col=0: pl.BlockSpec((1, c, w), lambda i, s: (i, _bwd_chunk(s, nct, nch), col))
    return pl.pallas_call(
        _gla_kernel,
        grid=(b, nch),
        in_specs=[fwd(hw), fwd(hw), fwd(vw), fwd(hw, 0), bwd(hw), bwd(hw), bwd(vw), bwd(hw, 1)],
        out_specs=[fwd(vw), bwd(vw)],
        out_shape=[jax.ShapeDtypeStruct((b, n, vw), F32)] * 2,
        scratch_shapes=[pltpu.VMEM((2, hw, GLA_DV), F32)],
        compiler_params=_params(("arbitrary", "arbitrary")),
    )(gq, gk, gv, la, gq, gk, gv, la)


def _head_norm(o, w, center):
    parts = []
    for h in range(o.shape[-1] // LANES):
        oh = o[:, h * LANES:(h + 1) * LANES]
        if center:
            oh = oh - jnp.mean(oh, axis=-1, keepdims=True)
        parts.append(oh * lax.rsqrt(jnp.mean(oh * oh, axis=-1, keepdims=True) + EPS))
    return jnp.concatenate(parts, axis=-1) * w


def _merge_kernel(nct, x_ref, mod_ref, nw_ref, oa_ref, rof_ref, rob_ref, rg_ref, gof_ref, gob_ref,
                  gr_ref, wbg_ref, wbr_ref, wout_ref, rnw_ref, gnw_ref, o_ref):
    d = D_MODEL
    row = _mod_rows(mod_ref, pl.program_id(1) < nct)
    sh, sc, g1 = row[:, 0:d], row[:, d:2 * d], row[:, 2 * d:3 * d]
    x = x_ref[0]
    u = (_rms(x, nw_ref[...]) * (1.0 + sc) + sh).astype(BF16)
    gate = _sigmoid(_dot(u, wbg_ref[...]))
    yb = _silu(rg_ref[0].astype(F32)) * _head_norm(rof_ref[0] + rob_ref[0], rnw_ref[...], True)
    yc = _silu(gr_ref[0].astype(F32)) * _head_norm(gof_ref[0] + gob_ref[0], gnw_ref[...], False)
    z = (gate[:, 0:d] * _dot(oa_ref[0], wbr_ref[0])
         + gate[:, d:2 * d] * _dot(yb.astype(BF16), wbr_ref[1])
         + gate[:, 2 * d:3 * d] * _dot(yc.astype(BF16), wbr_ref[2]))
    o_ref[0] = x + g1 * _dot(z.astype(BF16), wout_ref[...])


def _merge(xs, modsel, nw, oa, rof, rob, rg, gof, gob, gr, wbg, wbr, wout, rnw, gnw, nct):
    b, n, d = xs.shape
    tm = TOKEN_TILE
    tok = lambda w: pl.BlockSpec((1, tm, w), lambda i, t: (i, t, 0))
    return pl.pallas_call(
        functools.partial(_merge_kernel, nct),
        grid=(b, n // tm),
        in_specs=[tok(d), pl.BlockSpec((1, 2, 6 * d), lambda i, t: (i, 0, 0)), _const_spec(nw.shape),
                  tok(512), tok(512), tok(512), tok(512), tok(512), tok(512), tok(512),
                  _const_spec(wbg.shape), _const_spec(wbr.shape), _const_spec(wout.shape),
                  _const_spec(rnw.shape), _const_spec(gnw.shape)],
        out_specs=tok(d),
        out_shape=jax.ShapeDtypeStruct((b, n, d), F32),
        compiler_params=_params(("arbitrary", "arbitrary")),
    )(xs, modsel, nw, oa, rof, rob, rg, gof, gob, gr, wbg, wbr, wout, rnw, gnw)


def _norm2(x, row, nw_ref):
    d = D_MODEL
    sh, sc = row[:, 3 * d:4 * d], row[:, 4 * d:5 * d]
    return _rms(x, nw_ref[...]) * (1.0 + sc) + sh


def _finish(x, y, row, final, fnw_ref):
    d = D_MODEL
    out = x + row[:, 5 * d:6 * d] * y
    if final:
        out = _rms(out, fnw_ref[...])
    return out


def _ffn_kernel(nct, off, final, x_ref, mod_ref, nw_ref, wg_ref, wu_ref, wd_ref, fnw_ref, o_ref):
    row = _mod_rows(mod_ref, pl.program_id(1) + off < nct)
    x = x_ref[0]
    v = _norm2(x, row, nw_ref).astype(BF16)
    hdn = _silu(_dot(v, wg_ref[...])) * _dot(v, wu_ref[...])
    o_ref[0] = _finish(x, _dot(hdn.astype(BF16), wd_ref[...]), row, final, fnw_ref)


def _ffn(xs, modsel, nw, wg, wu, wd, fnw, nct, final):
    b, n, d = xs.shape
    tm = TOKEN_TILE
    off = nct if final else 0
    nt = n // tm - off
    return pl.pallas_call(
        functools.partial(_ffn_kernel, nct, off, final),
        grid=(b, nt),
        in_specs=[pl.BlockSpec((1, tm, d), lambda i, t: (i, t + off, 0)),
                  pl.BlockSpec((1, 2, 6 * d), lambda i, t: (i, 0, 0)), _const_spec(nw.shape),
                  _const_spec(wg.shape), _const_spec(wu.shape), _const_spec(wd.shape),
                  _const_spec(fnw.shape)],
        out_specs=pl.BlockSpec((1, tm, d), lambda i, t: (i, t, 0)),
        out_shape=jax.ShapeDtypeStruct((b, nt * tm, d), F32),
        compiler_params=_params(("arbitrary", "arbitrary")),
    )(xs, modsel, nw, wg, wu, wd, fnw)


def _top2_gates(logits):
    lane = lax.broadcasted_iota(jnp.int32, logits.shape, 1)
    m1 = jnp.max(logits, axis=-1, keepdims=True)
    i1 = jnp.min(jnp.where(logits == m1, lane, LANES), axis=-1, keepdims=True)
    rest = jnp.where(lane == i1, -jnp.inf, logits)
    m2 = jnp.max(rest, axis=-1, keepdims=True)
    i2 = jnp.min(jnp.where(rest == m2, lane, LANES), axis=-1, keepdims=True)
    e2 = jnp.exp(m2 - m1)
    w1 = 1.0 / (1.0 + e2)
    return jnp.where(lane == i1, w1, jnp.where(lane == i2, e2 * w1, 0.0))


def _moe_kernel(nct, off, final, x_ref, mod_ref, nw_ref, rt_ref, wg_ref, wu_ref, wd_ref, fnw_ref,
                o_ref, v_sc, gate_sc, acc_sc):
    e = pl.program_id(2)
    row = _mod_rows(mod_ref, pl.program_id(1) + off < nct)

    @pl.when(e == 0)
    def _():
        v = _norm2(x_ref[0], row, nw_ref)
        v_sc[...] = v.astype(BF16)
        lane = lax.broadcasted_iota(jnp.int32, (v.shape[0], LANES), 1)
        logits = jnp.where(lane < N_EXPERTS, _dot(v, rt_ref[...], HI), -jnp.inf)
        gate_sc[...] = _top2_gates(logits)
        acc_sc[...] = jnp.zeros_like(acc_sc)

    v = v_sc[...]
    hdn = _silu(_dot(v, wg_ref[0])) * _dot(v, wu_ref[0])
    lane = lax.broadcasted_iota(jnp.int32, gate_sc.shape, 1)
    g = jnp.sum(jnp.where(lane == e, gate_sc[...], 0.0), axis=-1, keepdims=True)
    acc_sc[...] += g * _dot(hdn.astype(BF16), wd_ref[0])

    @pl.when(e == N_EXPERTS - 1)
    def _():
        o_ref[0] = _finish(x_ref[0], acc_sc[...], row, final, fnw_ref)


def _moe(xs, modsel, nw, rt, wg, wu, wd, fnw, nct, final):
    b, n, d = xs.shape
    tm = TOKEN_TILE
    off = nct if final else 0
    nt = n // tm - off
    ne, _, f = wg.shape
    return pl.pallas_call(
        functools.partial(_moe_kernel, nct, off, final),
        grid=(b, nt, ne),
        in_specs=[pl.BlockSpec((1, tm, d), lambda i, t, e: (i, t + off, 0)),
                  pl.BlockSpec((1, 2, 6 * d), lambda i, t, e: (i, 0, 0)), _const_spec(nw.shape),
                  _const_spec(rt.shape),
                  pl.BlockSpec((1, d, f), lambda i, t, e: (e, 0, 0)),
                  pl.BlockSpec((1, d, f), lambda i, t, e: (e, 0, 0)),
                  pl.BlockSpec((1, f, d), lambda i, t, e: (e, 0, 0)),
                  _const_spec(fnw.shape)],
        out_specs=pl.BlockSpec((1, tm, d), lambda i, t, e: (i, t, 0)),
        out_shape=jax.ShapeDtypeStruct((b, nt * tm, d), F32),
        scratch_shapes=[pltpu.VMEM((tm, d), BF16), pltpu.VMEM((tm, LANES), F32), pltpu.VMEM((tm, d), F32)],
        compiler_params=_params(("arbitrary", "arbitrary", "arbitrary")),
    )(xs, modsel, nw, rt, wg, wu, wd, fnw)


def _axial_perm():
    dd = np.arange(MLA_ROPE)
    return np.where(dd % 16 < 8, dd + 8, dd - 8)


def _ret_perm():
    dd = np.arange(RET_HEADS * RET_DK)
    return np.where(dd % RET_DK < RET_DK // 2, dd + RET_DK // 2, dd - RET_DK // 2)


def _prep_w_in(w):
    offs = np.cumsum((0,) + IN_SPLITS)
    cq, ckv, kr, rq, rk, rv, rg, gq, gk, gv, gr, gz, bg = [w[:, offs[i]:offs[i + 1]] for i in range(13)]
    zeros = lambda n: jnp.zeros((w.shape[0], n), w.dtype)
    rk = rk * (RET_DK ** -0.5)
    gq = gq * (GLA_DK ** -0.5)
    w1 = jnp.concatenate([
        cq, ckv,
        zeros(MLA_NOPE), kr, zeros(HEAD_SLOT - MLA_NOPE - MLA_ROPE),
        zeros(MLA_NOPE), kr[:, _axial_perm()], zeros(HEAD_SLOT - MLA_NOPE - MLA_ROPE),
        rq, rq[:, _ret_perm()], rk, rk[:, _ret_perm()], rv, rg,
        gq, gk, gv, gr, gz, zeros(LANES - 2 * GLA_GATE_RANK)], axis=1)
    return w1.astype(BF16), bg.astype(BF16)


def _prep_mla(w_uq, w_ukv):
    r = w_uq.shape[0]
    scale = (MLA_NOPE + MLA_ROPE) ** -0.5
    wq3 = w_uq.reshape(r, MLA_HEADS, MLA_NOPE + MLA_ROPE) * scale
    pad = jnp.zeros((r, MLA_HEADS, HEAD_SLOT - MLA_NOPE - MLA_ROPE), w_uq.dtype)
    wq = jnp.concatenate([wq3, pad], axis=-1)
    wqs = jnp.concatenate([jnp.zeros((r, MLA_HEADS, MLA_NOPE), w_uq.dtype),
                           wq3[:, :, MLA_NOPE:][:, :, _axial_perm()], pad], axis=-1)
    rk = w_ukv.shape[0]
    wkv3 = w_ukv.reshape(rk, MLA_HEADS, MLA_NOPE + MLA_V)
    wk = jnp.concatenate([wkv3[:, :, :MLA_NOPE],
                          jnp.zeros((rk, MLA_HEADS, HEAD_SLOT - MLA_NOPE), w_ukv.dtype)], axis=-1)
    wv = wkv3[:, :, MLA_NOPE:]
    flat = lambda a: a.reshape(a.shape[0], -1).astype(BF16)
    return flat(wq), flat(wqs), flat(wk), flat(wv)


def _prep_gla_gate(w_gate, b_gate):
    hw = GLA_HEADS * GLA_DK
    wg2 = jnp.zeros((LANES, 2 * hw), F32)
    wg2 = wg2.at[0:GLA_GATE_RANK, 0:hw].set(w_gate[0])
    wg2 = wg2.at[GLA_GATE_RANK:2 * GLA_GATE_RANK, hw:2 * hw].set(w_gate[1])
    return wg2, b_gate.reshape(1, 2 * hw)


def _rope_tables(n_ctx, seq):
    t = jnp.arange(seq, dtype=jnp.int32)
    half = MLA_ROPE // 4
    inv = ROPE_BASE ** (-jnp.arange(half, dtype=F32) / half)
    lane = np.arange(HEAD_SLOT)
    dd = lane - MLA_NOPE
    active = (dd >= 0) & (dd < MLA_ROPE)
    dd = np.clip(dd, 0, MLA_ROPE - 1)
    by_col = (dd // 16) == 1
    first = (dd % 16) < half
    pos = jnp.where(by_col[None, :], (t % GRID_W)[:, None], (t // GRID_W)[:, None]).astype(F32)
    ang = pos * inv[dd % half][None, :]
    cosq = jnp.where(active[None, :], jnp.cos(ang), 1.0)
    sinq = jnp.where(active[None, :], jnp.where(first[None, :], -jnp.sin(ang), jnp.sin(ang)), 0.0)
    rhalf = RET_DK // 2
    rinv = ROPE_BASE ** (-jnp.arange(rhalf, dtype=F32) / rhalf)
    rl = np.arange(RET_HEADS * RET_DK) % RET_DK
    rang = t.astype(F32)[:, None] * rinv[rl % rhalf][None, :]
    cosr = jnp.cos(rang)
    sinr = jnp.where((rl < rhalf)[None, :], -jnp.sin(rang), jnp.sin(rang))
    ctx = lambda a, fill: jnp.concatenate([jnp.full((n_ctx, a.shape[1]), fill, F32), a], axis=0)
    return ctx(cosq, 1.0), ctx(sinq, 0.0), ctx(cosr, 1.0), ctx(sinr, 0.0)


def kernel(x, c, ctx, c_ctx, mod_w, mod_b, norm1_w, norm2_w, w_in, mla_q_norm, mla_w_uq, mla_kv_norm, mla_w_ukv, ret_decay_logit, ret_norm_w, gla_w_gate, gla_b_gate, gla_norm_w, w_branch, w_out, ffn_w_gate, ffn_w_up, ffn_w_down, moe_router, moe_w_gate, moe_w_up, moe_w_down, final_norm_w):
    b, seq, d = x.shape
    n_ctx = ctx.shape[1]
    depth = mod_w.shape[0]
    assert d == D_MODEL and seq % GRID_W == 0
    assert n_ctx % TOKEN_TILE == 0 and n_ctx % RET_CHUNK == 0 and n_ctx % GLA_CHUNK == 0
    assert seq % ATT_KV_TILE == 0 and seq % RET_CHUNK == 0
    nct = n_ctx // TOKEN_TILE

    rows = 8 * ((b + 1 + 7) // 8)
    cvec = jnp.zeros((rows, d), F32).at[0:b].set(c).at[b].set(c_ctx)
    mod = _modulation(cvec, mod_w, mod_b)
    cosq, sinq, cosr, sinr = _rope_tables(n_ctx, seq)
    fnw = final_norm_w.reshape(1, d)
    row2 = lambda a: a.reshape(1, -1)

    xs = jnp.concatenate([ctx, x], axis=1)
    for l in range(depth):
        last = l == depth - 1
        modsel = jnp.stack([jnp.broadcast_to(mod[l, b], (b, 6 * d)), mod[l, 0:b]], axis=1)
        w1, wbg = _prep_w_in(w_in[l])
        wq, wqs, wk, wv = _prep_mla(mla_w_uq[l], mla_w_ukv[l])
        wg2, bg2 = _prep_gla_gate(gla_w_gate[l], gla_b_gate[l])
        nw1, nw2 = row2(norm1_w[l]), row2(norm2_w[l])

        (q, k, v, rq, rk, rv, rg, gq, gk, gv, gr, la) = _inproj(
            xs, modsel, nw1, w1, row2(mla_q_norm[l]), wq, wqs, row2(mla_kv_norm[l]), wk, wv,
            wg2, bg2, cosq, sinq, cosr, sinr, nct)
        oa = _attention(q, k, v, nct, n_ctx)
        dl = jnp.broadcast_to(ret_decay_logit[l][:, :, None, None], (2, RET_HEADS, 8, LANES))
        rof, rob = _retention(dl, rq, rk, rv, n_ctx // RET_CHUNK)
        gof, gob = _gla(gq, gk, gv, la, n_ctx)
        xs = _merge(xs, modsel, nw1, oa, rof, rob, rg, gof, gob, gr, wbg,
                    w_branch[l].astype(BF16), w_out[l].astype(BF16),
                    row2(ret_norm_w[l]), row2(gla_norm_w[l]), nct)
        i = l // 2
        if l % 2 == 0:
            xs = _ffn(xs, modsel, nw2, ffn_w_gate[i].astype(BF16), ffn_w_up[i].astype(BF16),
                      ffn_w_down[i].astype(BF16), fnw, nct, last)
        else:
            rt = jnp.zeros((d, LANES), F32).at[:, 0:N_EXPERTS].set(moe_router[i])
            xs = _moe(xs, modsel, nw2, rt, moe_w_gate[i].astype(BF16), moe_w_up[i].astype(BF16),
                      moe_w_down[i].astype(BF16), fnw, nct, last)
    return xs
```

```python
import functools

import numpy as np
import jax
import jax.numpy as jnp
from jax import lax
from jax.experimental import pallas as pl
from jax.experimental.pallas import tpu as pltpu

D_MODEL = 1024
GRID_W = 64
EPS = 1e-6
ROPE_BASE = 10000.0

MLA_HEADS = 8
MLA_NOPE = 64
MLA_ROPE = 32
MLA_V = 64
MLA_Q_RANK = 256
MLA_KV_RANK = 128

RET_HEADS = 4
RET_DK = 64
RET_DV = 128

GLA_HEADS = 4
GLA_DK = 64
GLA_DV = 128
GLA_GATE_RANK = 16
GLA_TAU = 16.0

N_BRANCH = 3
BRANCH_W = 512
D_FF = 2816
N_EXPERTS = 8

IN_SPLITS = (
    MLA_Q_RANK, MLA_KV_RANK, MLA_ROPE,
    RET_HEADS * RET_DK, RET_HEADS * RET_DK, RET_HEADS * RET_DV, RET_HEADS * RET_DV,
    GLA_HEADS * GLA_DK, GLA_HEADS * GLA_DK, GLA_HEADS * GLA_DV, GLA_HEADS * GLA_DV,
    2 * GLA_GATE_RANK,
    N_BRANCH * D_MODEL,
)

F32 = jnp.float32
BF16 = jnp.bfloat16
HI = lax.Precision.HIGHEST

LANES = 128
HEAD_SLOT = 128
TOKEN_TILE = 256
ATT_SUM_ROWS = 16
RET_CHUNK = 256
GLA_CHUNK = 128
GLA_BASE = 16
MOD_COLS_TILE = 1536
VMEM_LIMIT = 56 * 1024 * 1024


def _silu(x):
    return x / (1.0 + jnp.exp(-x))


def _sigmoid(x):
    return 1.0 / (1.0 + jnp.exp(-x))


def _log_sigmoid(z):
    return jnp.minimum(z, 0.0) - jnp.log(1.0 + jnp.exp(-jnp.abs(z)))


def _rms(x, w):
    return x * lax.rsqrt(jnp.mean(x * x, axis=-1, keepdims=True) + EPS) * w


def _dot(a, b, precision=None):
    return jnp.dot(a, b, preferred_element_type=F32, precision=precision)


def _dot_nt(a, b):
    return lax.dot_general(a, b, (((1,), (1,)), ((), ())), preferred_element_type=F32)


def _params(sem):
    return pltpu.CompilerParams(dimension_semantics=sem, vmem_limit_bytes=VMEM_LIMIT)


def _const_spec(shape):
    nd = len(shape)
    return pl.BlockSpec(shape, lambda *_: (0,) * nd)


def _mod_rows(mod_ref, is_ctx):
    m = mod_ref[0]
    return jnp.where(is_ctx, m[0:1], m[1:2])


def _mod_kernel(c_ref, w_ref, b_ref, o_ref):
    o_ref[0] = _dot(_silu(c_ref[...]), w_ref[0], HI) + b_ref[0]


def _modulation(cvec, mod_w, mod_b):
    depth, d, n6 = mod_w.shape
    rows = cvec.shape[0]
    tn = MOD_COLS_TILE
    return pl.pallas_call(
        _mod_kernel,
        grid=(depth, n6 // tn),
        in_specs=[pl.BlockSpec((rows, d), lambda l, j: (0, 0)),
                  pl.BlockSpec((1, d, tn), lambda l, j: (l, 0, j)),
                  pl.BlockSpec((1, 1, tn), lambda l, j: (l, 0, j))],
        out_specs=pl.BlockSpec((1, rows, tn), lambda l, j: (l, 0, j)),
        out_shape=jax.ShapeDtypeStruct((depth, rows, n6), F32),
        compiler_params=_params(("arbitrary", "arbitrary")),
    )(cvec, mod_w, mod_b.reshape(depth, 1, n6))


_C_CQ = (0, 256)
_C_CKV = (256, 384)
_C_KRA = (384, 512)
_C_KRB = (512, 640)
_C_RQ = (640, 896)
_C_RQS = (896, 1152)
_C_RK = (1152, 1408)
_C_RKS = (1408, 1664)
_C_RV = (1664, 2176)
_C_RG = (2176, 2688)
_C_GQ = (2688, 2944)
_C_GK = (2944, 3200)
_C_GV = (3200, 3712)
_C_GR = (3712, 4224)
_C_GZ = (4224, 4352)
_W1_COLS = 4352


def _inproj_kernel(nct, x_ref, mod_ref, nw_ref, w_ref, qn_ref, wq_ref, wqs_ref, kvn_ref, wk_ref,
                   wv_ref, wg2_ref, bg2_ref, cq_ref, sq_ref, cr_ref, sr_ref,
                   q_out, k_out, v_out, rq_out, rk_out, rv_out, rg_out,
                   gq_out, gk_out, gv_out, gr_out, la_out):
    d = D_MODEL
    row = _mod_rows(mod_ref, pl.program_id(1) < nct)
    sh, sc = row[:, 0:d], row[:, d:2 * d]
    u = (_rms(x_ref[0], nw_ref[...]) * (1.0 + sc) + sh).astype(BF16)

    def seg(c):
        return _dot(u, w_ref[:, c[0]:c[1]])

    cqn = _rms(seg(_C_CQ), qn_ref[...]).astype(BF16)
    q_all = _dot(cqn, wq_ref[...])
    q_swp = _dot(cqn, wqs_ref[...])
    ckvn = _rms(seg(_C_CKV), kvn_ref[...]).astype(BF16)
    k_all = _dot(ckvn, wk_ref[...])
    v_out[0, 0] = _dot(ckvn, wv_ref[...]).T.astype(BF16)
    cosq, sinq = cq_ref[...], sq_ref[...]
    k_rope = seg(_C_KRA) * cosq + seg(_C_KRB) * sinq
    for h in range(MLA_HEADS):
        sl = slice(h * HEAD_SLOT, (h + 1) * HEAD_SLOT)
        q_out[0, h] = (q_all[:, sl] * cosq + q_swp[:, sl] * sinq).T.astype(BF16)
        k_out[0, h] = (k_all[:, sl] + k_rope).astype(BF16)

    cosr, sinr = cr_ref[...], sr_ref[...]
    rq_out[0] = (seg(_C_RQ) * cosr + seg(_C_RQS) * sinr).astype(BF16)
    rk_out[0] = (seg(_C_RK) * cosr + seg(_C_RKS) * sinr).astype(BF16)
    rv_out[0] = seg(_C_RV).astype(BF16)
    rg_out[0] = seg(_C_RG).astype(BF16)

    gq_out[0] = seg(_C_GQ).astype(BF16)
    gk_out[0] = seg(_C_GK).astype(BF16)
    gv_out[0] = seg(_C_GV).astype(BF16)
    gr_out[0] = seg(_C_GR).astype(BF16)
    z = _dot(seg(_C_GZ), wg2_ref[...], HI) + bg2_ref[...]
    la_out[0] = _log_sigmoid(z) * (1.0 / GLA_TAU)


def _inproj(xs, modsel, nw, w1, qn, wq, wqs, kvn, wk, wv, wg2, bg2, cosq, sinq, cosr, sinr, nct):
    b, n, d = xs.shape
    tm = TOKEN_TILE
    nt = n // tm
    h = MLA_HEADS
    tok = lambda w: pl.BlockSpec((1, tm, w), lambda i, t: (i, t, 0))
    tab = lambda w: pl.BlockSpec((tm, w), lambda i, t: (t, 0))
    q_spec = pl.BlockSpec((1, h, HEAD_SLOT, tm), lambda i, t: (i, 0, 0, t))
    k_spec = pl.BlockSpec((1, h, tm, HEAD_SLOT), lambda i, t: (i, 0, t, 0))
    v_spec = pl.BlockSpec((1, 1, h * MLA_V, tm), lambda i, t: (i, t, 0, 0))
    sds = lambda w, dt=BF16: jax.ShapeDtypeStruct((b, n, w), dt)
    q_sds = jax.ShapeDtypeStruct((b, h, HEAD_SLOT, n), BF16)
    k_sds = jax.ShapeDtypeStruct((b, h, n, HEAD_SLOT), BF16)
    v_sds = jax.ShapeDtypeStruct((b, nt, h * MLA_V, tm), BF16)
    return pl.pallas_call(
        functools.partial(_inproj_kernel, nct),
        grid=(b, nt),
        in_specs=[tok(d), pl.BlockSpec((1, 2, 6 * d), lambda i, t: (i, 0, 0)), _const_spec(nw.shape),
                  _const_spec(w1.shape), _const_spec(qn.shape), _const_spec(wq.shape),
                  _const_spec(wqs.shape), _const_spec(kvn.shape), _const_spec(wk.shape),
                  _const_spec(wv.shape), _const_spec(wg2.shape), _const_spec(bg2.shape),
                  tab(HEAD_SLOT), tab(HEAD_SLOT), tab(256), tab(256)],
        out_specs=[q_spec, k_spec, v_spec, tok(256), tok(256), tok(512), tok(512),
                   tok(256), tok(256), tok(512), tok(512), tok(512)],
        out_shape=[q_sds, k_sds, v_sds, sds(256), sds(256), sds(512), sds(512),
                   sds(256), sds(256), sds(512), sds(512), sds(512, F32)],
        compiler_params=_params(("arbitrary", "arbitrary")),
    )(xs, modsel, nw, w1, qn, wq, wqs, kvn, wk, wv, wg2, bg2, cosq, sinq, cosr, sinr)


def _attn_kernel(nct, n_tiles, qt_ref, k_ref, vt_ref, o_ref, *bufs):
    t = TOKEN_TILE
    heads = range(2)
    s_bufs = (bufs[0:2], bufs[2:4])
    p_bufs = (bufs[4:6], bufs[6:8])
    n_chunks = jnp.where(pl.program_id(2) < nct, nct, n_tiles)
    last = n_chunks - 1
    qts = [qt_ref[0, j] for j in heads]
    ones = jnp.ones((ATT_SUM_ROWS, t), BF16)

    def scores(j, c, s_buf):
        s = _dot(k_ref[0, j, pl.ds(pl.multiple_of(c * t, t), t), :], qts[j])
        s_buf[...] = s
        return jnp.max(s, axis=0, keepdims=True)

    def softmax(s_buf, p_buf, m, cmax):
        m_new = jnp.maximum(m, cmax)
        p_buf[...] = jnp.exp2(s_buf[...] - m_new).astype(BF16)
        return m_new, jnp.exp2(m - m_new)

    def values(j, c, p_buf, alpha, acc):
        lhs = jnp.concatenate([vt_ref[0, c, j * MLA_V:(j + 1) * MLA_V, :], ones], axis=0)
        return alpha * acc + _dot(lhs, p_buf[...])

    def step(c_prev, c_next, cur, state):
        accs = [values(j, c_prev, p_bufs[j][1 - cur], state[j][1], state[j][2]) for j in heads]
        stats = [softmax(s_bufs[j][cur], p_bufs[j][cur], state[j][0], state[j][3]) for j in heads]
        cmaxs = [scores(j, c_next, s_bufs[j][1 - cur]) for j in heads]
        return tuple((stats[j][0], stats[j][1], accs[j], cmaxs[j]) for j in heads)

    def pair(i, state):
        state = step(jnp.maximum(2 * i - 1, 0), 2 * i + 1, 0, state)
        return step(2 * i, 2 * i + 2, 1, state)

    init = []
    for j in heads:
        p_bufs[j][1][...] = jnp.zeros((t, t), BF16)
        init.append((jnp.full((1, t), -jnp.inf, F32), jnp.ones((1, t), F32),
                     jnp.zeros((MLA_V + ATT_SUM_ROWS, t), F32), scores(j, 0, s_bufs[j][0])))
    state = lax.fori_loop(0, last // 2, pair, tuple(init))
    outs = []
    for j in heads:
        m, alpha, acc, cmax = state[j]
        acc = values(j, jnp.maximum(last - 1, 0), p_bufs[j][1], alpha, acc)
        m, alpha = softmax(s_bufs[j][0], p_bufs[j][0], m, cmax)
        acc = values(j, last, p_bufs[j][0], alpha, acc)
        outs.append(acc[0:MLA_V] / acc[MLA_V:MLA_V + 1])
    o_ref[0] = jnp.concatenate(outs, axis=0).T.astype(BF16)


def _attention(qt, k, vt, nct):
    b, h, w, n = qt.shape
    t = TOKEN_TILE
    n_tiles = n // t
    assert nct % 2 == 1 and n_tiles % 2 == 1
    return pl.pallas_call(
        functools.partial(_attn_kernel, nct, n_tiles),
        grid=(b, h // 2, n_tiles),
        in_specs=[pl.BlockSpec((1, 2, w, t), lambda i, p, q: (i, p, 0, q)),
                  pl.BlockSpec((1, 2, n, w), lambda i, p, q: (i, p, 0, 0)),
                  pl.BlockSpec((1, n_tiles, 2 * MLA_V, t), lambda i, p, q: (i, 0, p, 0))],
        out_specs=pl.BlockSpec((1, t, 2 * MLA_V), lambda i, p, q: (i, q, p)),
        out_shape=jax.ShapeDtypeStruct((b, n, h * MLA_V), BF16),
        scratch_shapes=[pltpu.VMEM((t, t), F32)] * 4 + [pltpu.VMEM((t, t), BF16)] * 4,
        compiler_params=_params(("arbitrary", "arbitrary", "arbitrary")),
    )(qt, k, vt)


def _bwd_chunk(s, nct, nch):
    return jnp.where(s < nct, nct - 1 - s, nch - 1 - (s - nct))


def _ret_kernel(dl_ref, qf, kf, vf, qb, kb, vb, of_ref, ob_ref, st_ref):
    c = RET_CHUNK
    hw = RET_HEADS * RET_DK

    @pl.when(pl.program_id(1) == 0)
    def _():
        st_ref[...] = jnp.zeros_like(st_ref)

    ii = lax.broadcasted_iota(jnp.int32, (c, c), 0)
    jj = lax.broadcasted_iota(jnp.int32, (c, c), 1)
    pos_i = lax.broadcasted_iota(jnp.int32, (c, 1), 0).astype(F32)
    pos_j = lax.broadcasted_iota(jnp.int32, (1, c), 1).astype(F32)
    lane_head = lax.broadcasted_iota(jnp.int32, (c, hw), 1) // RET_DK
    for d, (q_ref, k_ref, v_ref, o_ref) in enumerate(((qf, kf, vf, of_ref), (qb, kb, vb, ob_ref))):
        rev = d == 1
        q, v = q_ref[0], v_ref[0]
        k_t = k_ref[0].astype(F32).T
        k_tb = k_t.astype(BF16)
        st = st_ref[d]
        st_b = st.astype(BF16)
        dist = ((jj - ii) if rev else (ii - jj)).astype(F32)
        for h in range(RET_HEADS):
            lg = _log_sigmoid(dl_ref[d, h])[0:1, 0:1]
            intra = jnp.where(dist >= 0.0, jnp.exp(lg * jnp.maximum(dist, 0.0)), 0.0)
            q_dec = jnp.exp(lg * ((c - pos_i) if rev else (pos_i + 1.0)))
            k_dec = jnp.exp(lg * (pos_j if rev else (c - 1.0 - pos_j)))
            rows = slice(h * RET_DK, (h + 1) * RET_DK)
            cols = slice(h * RET_DV, (h + 1) * RET_DV)
            qm = jnp.where(lane_head == h, q, jnp.zeros_like(q))
            scores = _dot(qm, k_tb) * intra
            o = _dot(scores.astype(BF16), v[:, cols]) + _dot(qm, st_b) * q_dec
            o_ref[0, :, cols] = o
            kd = (k_t[rows] * k_dec).astype(BF16)
            st_ref[d, rows, :] = st[rows] * jnp.exp(lg * float(c)) + _dot(kd, v[:, cols])


def _retention(dl, rq, rk, rv, nct):
    b, n, hw = rq.shape
    c = RET_CHUNK
    nch = n // c
    vw = rv.shape[-1]
    fwd = lambda w: pl.BlockSpec((1, c, w), lambda i, s: (i, s, 0))
    bwd = lambda w: pl.BlockSpec((1, c, w), lambda i, s: (i, _bwd_chunk(s, nct, nch), 0))
    return pl.pallas_call(
        _ret_kernel,
        grid=(b, nch),
        in_specs=[_const_spec(dl.shape), fwd(hw), fwd(hw), fwd(vw), bwd(hw), bwd(hw), bwd(vw)],
        out_specs=[fwd(vw), bwd(vw)],
        out_shape=[jax.ShapeDtypeStruct((b, n, vw), F32)] * 2,
        scratch_shapes=[pltpu.VMEM((2, hw, RET_DV), F32)],
        compiler_params=_params(("arbitrary", "arbitrary")),
    )(dl, rq, rk, rv, rq, rk, rv)


def _block_start(i, size):
    return i & ~(size - 1)


def _gla_levels():
    half = GLA_CHUNK // 2
    out = []
    while half >= GLA_BASE:
        out.append(half)
        half //= 2
    return out


def _gla_kernel(qf, kf, vf, laf, qb, kb, vb, lab, of_ref, ob_ref, st_ref):
    c = GLA_CHUNK
    hw = GLA_HEADS * GLA_DK
    levels = _gla_levels()

    @pl.when(pl.program_id(1) == 0)
    def _():
        st_ref[...] = jnp.zeros_like(st_ref)

    ii = lax.broadcasted_iota(jnp.int32, (c, c), 0)
    jj = lax.broadcasted_iota(jnp.int32, (c, c), 1)
    pos = lax.broadcasted_iota(jnp.int32, (c, 1), 0)
    lane_head = lax.broadcasted_iota(jnp.int32, (c, hw), 1) // GLA_DK
    for d, (q_ref, k_ref, v_ref, la_ref, o_ref) in enumerate(((qf, kf, vf, laf, of_ref),
                                                               (qb, kb, vb, lab, ob_ref))):
        rev = d == 1
        q, k, v, la = q_ref[0].astype(F32), k_ref[0].astype(F32), v_ref[0], la_ref[0]

        def prefix(ref_row):
            return ((jj >= ref_row) if rev else (jj <= ref_row)).astype(F32)

        mats = [prefix(ii)]
        for half in levels:
            blk = _block_start(ii, 2 * half)
            mats.append(prefix(blk + (half if rev else half - 1)))
        blk = _block_start(ii, GLA_BASE)
        mats.append(prefix(blk + (GLA_BASE - 1 if rev else 0)))
        cums = _dot(jnp.concatenate(mats, axis=0), la, HI)
        b = cums[0:c]
        b_end = b[0:1] if rev else b[c - 1:c]

        q_t = (q * jnp.exp(b)).astype(BF16)
        k_tt = (k * jnp.exp(b_end - b)).T
        dec_col = jnp.exp(jnp.broadcast_to(b_end, (LANES, hw)).T[:, 0:1])

        terms = []
        for n_l, half in enumerate(levels):
            e = cums[(n_l + 1) * c:(n_l + 2) * c]
            inner = pos & (2 * half - 1)
            late = (inner < half) if rev else (inner >= half)
            qh = jnp.where(late, q * jnp.exp(jnp.where(late, b - e, 0.0)), 0.0)
            kh = jnp.where(late, 0.0, k * jnp.exp(jnp.where(late, 0.0, e - b)))
            terms.append((qh.astype(BF16), kh.astype(BF16), _block_start(ii, 2 * half) == _block_start(jj, 2 * half)))
        r = cums[(len(levels) + 1) * c:(len(levels) + 2) * c]
        tri = (jj >= ii) if rev else (jj <= ii)
        terms.append(((q * jnp.exp(b - r)).astype(BF16), (k * jnp.exp(r - b)).astype(BF16),
                      (_block_start(ii, GLA_BASE) == _block_start(jj, GLA_BASE)) & tri))

        st = st_ref[d]
        st_b = st.astype(BF16)
        for h in range(GLA_HEADS):
            rows = slice(h * GLA_DK, (h + 1) * GLA_DK)
            cols = slice(h * GLA_DV, (h + 1) * GLA_DV)
            sel = lane_head == h
            a = jnp.zeros((c, c), F32)
            for qh, kh, mask in terms:
                a = a + jnp.where(mask, _dot_nt(jnp.where(sel, qh, jnp.zeros_like(qh)), kh), 0.0)
            vh = v[:, cols]
            o_ref[0, :, cols] = _dot(a.astype(BF16), vh) + _dot(jnp.where(sel, q_t, jnp.zeros_like(q_t)), st_b)
            st_ref[d, rows, :] = st[rows] * dec_col[rows] + _dot(k_tt[rows].astype(BF16), vh)


def _gla(gq, gk, gv, la, nct_tokens):
    b, n, hw = gq.shape
    c = GLA_CHUNK
    nch = n // c
    nct = nct_tokens // c
    vw = gv.shape[-1]
    fwd = lambda w, col=0: pl.BlockSpec((1, c, w), lambda i, s: (i, s, col))
    bwd = lambda w, col=0: pl.BlockSpec((1, c, w), lambda i, s: (i, _bwd_chunk(s, nct, nch), col))
    return pl.pallas_call(
        _gla_kernel,
        grid=(b, nch),
        in_specs=[fwd(hw), fwd(hw), fwd(vw), fwd(hw, 0), bwd(hw), bwd(hw), bwd(vw), bwd(hw, 1)],
        out_specs=[fwd(vw), bwd(vw)],
        out_shape=[jax.ShapeDtypeStruct((b, n, vw), F32)] * 2,
        scratch_shapes=[pltpu.VMEM((2, hw, GLA_DV), F32)],
        compiler_params=_params(("arbitrary", "arbitrary")),
    )(gq, gk, gv, la, gq, gk, gv, la)


def _head_norm(o, w, center):
    parts = []
    for h in range(o.shape[-1] // LANES):
        oh = o[:, h * LANES:(h + 1) * LANES]
        if center:
            oh = oh - jnp.mean(oh, axis=-1, keepdims=True)
        parts.append(oh * lax.rsqrt(jnp.mean(oh * oh, axis=-1, keepdims=True) + EPS))
    return jnp.concatenate(parts, axis=-1) * w


def _merge_kernel(nct, x_ref, mod_ref, nw_ref, oa_ref, rof_ref, rob_ref, rg_ref, gof_ref, gob_ref,
                  gr_ref, wbg_ref, wbr_ref, wout_ref, rnw_ref, gnw_ref, o_ref):
    d = D_MODEL
    row = _mod_rows(mod_ref, pl.program_id(1) < nct)
    sh, sc, g1 = row[:, 0:d], row[:, d:2 * d], row[:, 2 * d:3 * d]
    x = x_ref[0]
    u = (_rms(x, nw_ref[...]) * (1.0 + sc) + sh).astype(BF16)
    gate = _sigmoid(_dot(u, wbg_ref[...]))
    yb = _silu(rg_ref[0].astype(F32)) * _head_norm(rof_ref[0] + rob_ref[0], rnw_ref[...], True)
    yc = _silu(gr_ref[0].astype(F32)) * _head_norm(gof_ref[0] + gob_ref[0], gnw_ref[...], False)
    z = (gate[:, 0:d] * _dot(oa_ref[0], wbr_ref[0])
         + gate[:, d:2 * d] * _dot(yb.astype(BF16), wbr_ref[1])
         + gate[:, 2 * d:3 * d] * _dot(yc.astype(BF16), wbr_ref[2]))
    o_ref[0] = x + g1 * _dot(z.astype(BF16), wout_ref[...])


def _merge(xs, modsel, nw, oa, rof, rob, rg, gof, gob, gr, wbg, wbr, wout, rnw, gnw, nct):
    b, n, d = xs.shape
    tm = TOKEN_TILE
    tok = lambda w: pl.BlockSpec((1, tm, w), lambda i, t: (i, t, 0))
    return pl.pallas_call(
        functools.partial(_merge_kernel, nct),
        grid=(b, n // tm),
        in_specs=[tok(d), pl.BlockSpec((1, 2, 6 * d), lambda i, t: (i, 0, 0)), _const_spec(nw.shape),
                  tok(512), tok(512), tok(512), tok(512), tok(512), tok(512), tok(512),
                  _const_spec(wbg.shape), _const_spec(wbr.shape), _const_spec(wout.shape),
                  _const_spec(rnw.shape), _const_spec(gnw.shape)],
        out_specs=tok(d),
        out_shape=jax.ShapeDtypeStruct((b, n, d), F32),
        compiler_params=_params(("arbitrary", "arbitrary")),
    )(xs, modsel, nw, oa, rof, rob, rg, gof, gob, gr, wbg, wbr, wout, rnw, gnw)


def _norm2(x, row, nw_ref):
    d = D_MODEL
    sh, sc = row[:, 3 * d:4 * d], row[:, 4 * d:5 * d]
    return _rms(x, nw_ref[...]) * (1.0 + sc) + sh


def _finish(x, y, row, final, fnw_ref):
    d = D_MODEL
    out = x + row[:, 5 * d:6 * d] * y
    if final:
        out = _rms(out, fnw_ref[...])
    return out


def _ffn_kernel(nct, off, final, x_ref, mod_ref, nw_ref, wg_ref, wu_ref, wd_ref, fnw_ref, o_ref):
    row = _mod_rows(mod_ref, pl.program_id(1) + off < nct)
    x = x_ref[0]
    v = _norm2(x, row, nw_ref).astype(BF16)
    hdn = _silu(_dot(v, wg_ref[...])) * _dot(v, wu_ref[...])
    o_ref[0] = _finish(x, _dot(hdn.astype(BF16), wd_ref[...]), row, final, fnw_ref)


def _ffn(xs, modsel, nw, wg, wu, wd, fnw, nct, final):
    b, n, d = xs.shape
    tm = TOKEN_TILE
    off = nct if final else 0
    nt = n // tm - off
    return pl.pallas_call(
        functools.partial(_ffn_kernel, nct, off, final),
        grid=(b, nt),
        in_specs=[pl.BlockSpec((1, tm, d), lambda i, t: (i, t + off, 0)),
                  pl.BlockSpec((1, 2, 6 * d), lambda i, t: (i, 0, 0)), _const_spec(nw.shape),
                  _const_spec(wg.shape), _const_spec(wu.shape), _const_spec(wd.shape),
                  _const_spec(fnw.shape)],
        out_specs=pl.BlockSpec((1, tm, d), lambda i, t: (i, t, 0)),
        out_shape=jax.ShapeDtypeStruct((b, nt * tm, d), F32),
        compiler_params=_params(("arbitrary", "arbitrary")),
    )(xs, modsel, nw, wg, wu, wd, fnw)


def _top2_gates(logits):
    lane = lax.broadcasted_iota(jnp.int32, logits.shape, 1)
    m1 = jnp.max(logits, axis=-1, keepdims=True)
    i1 = jnp.min(jnp.where(logits == m1, lane, LANES), axis=-1, keepdims=True)
    rest = jnp.where(lane == i1, -jnp.inf, logits)
    m2 = jnp.max(rest, axis=-1, keepdims=True)
    i2 = jnp.min(jnp.where(rest == m2, lane, LANES), axis=-1, keepdims=True)
    e2 = jnp.exp(m2 - m1)
    w1 = 1.0 / (1.0 + e2)
    return jnp.where(lane == i1, w1, jnp.where(lane == i2, e2 * w1, 0.0))


def _moe_kernel(nct, off, final, x_ref, mod_ref, nw_ref, rt_ref, wg_ref, wu_ref, wd_ref, fnw_ref,
                o_ref, v_sc, gate_sc, acc_sc):
    e = pl.program_id(2)
    row = _mod_rows(mod_ref, pl.program_id(1) + off < nct)

    @pl.when(e == 0)
    def _():
        v = _norm2(x_ref[0], row, nw_ref)
        v_sc[...] = v.astype(BF16)
        lane = lax.broadcasted_iota(jnp.int32, (v.shape[0], LANES), 1)
        logits = jnp.where(lane < N_EXPERTS, _dot(v, rt_ref[...], HI), -jnp.inf)
        gate_sc[...] = _top2_gates(logits)
        acc_sc[...] = jnp.zeros_like(acc_sc)

    v = v_sc[...]
    hdn = _silu(_dot(v, wg_ref[0])) * _dot(v, wu_ref[0])
    lane = lax.broadcasted_iota(jnp.int32, gate_sc.shape, 1)
    g = jnp.sum(jnp.where(lane == e, gate_sc[...], 0.0), axis=-1, keepdims=True)
    acc_sc[...] += g * _dot(hdn.astype(BF16), wd_ref[0])

    @pl.when(e == N_EXPERTS - 1)
    def _():
        o_ref[0] = _finish(x_ref[0], acc_sc[...], row, final, fnw_ref)


def _moe(xs, modsel, nw, rt, wg, wu, wd, fnw, nct, final):
    b, n, d = xs.shape
    tm = TOKEN_TILE
    off = nct if final else 0
    nt = n // tm - off
    ne, _, f = wg.shape
    return pl.pallas_call(
        functools.partial(_moe_kernel, nct, off, final),
        grid=(b, nt, ne),
        in_specs=[pl.BlockSpec((1, tm, d), lambda i, t, e: (i, t + off, 0)),
                  pl.BlockSpec((1, 2, 6 * d), lambda i, t, e: (i, 0, 0)), _const_spec(nw.shape),
                  _const_spec(rt.shape),
                  pl.BlockSpec((1, d, f), lambda i, t, e: (e, 0, 0)),
                  pl.BlockSpec((1, d, f), lambda i, t, e: (e, 0, 0)),
                  pl.BlockSpec((1, f, d), lambda i, t, e: (e, 0, 0)),
                  _const_spec(fnw.shape)],
        out_specs=pl.BlockSpec((1, tm, d), lambda i, t, e: (i, t, 0)),
        out_shape=jax.ShapeDtypeStruct((b, nt * tm, d), F32),
        scratch_shapes=[pltpu.VMEM((tm, d), BF16), pltpu.VMEM((tm, LANES), F32), pltpu.VMEM((tm, d), F32)],
        compiler_params=_params(("arbitrary", "arbitrary", "arbitrary")),
    )(xs, modsel, nw, rt, wg, wu, wd, fnw)


def _axial_perm():
    dd = np.arange(MLA_ROPE)
    return np.where(dd % 16 < 8, dd + 8, dd - 8)


def _ret_perm():
    dd = np.arange(RET_HEADS * RET_DK)
    return np.where(dd % RET_DK < RET_DK // 2, dd + RET_DK // 2, dd - RET_DK // 2)


def _prep_w_in(w):
    offs = np.cumsum((0,) + IN_SPLITS)
    cq, ckv, kr, rq, rk, rv, rg, gq, gk, gv, gr, gz, bg = [w[:, offs[i]:offs[i + 1]] for i in range(13)]
    zeros = lambda n: jnp.zeros((w.shape[0], n), w.dtype)
    rk = rk * (RET_DK ** -0.5)
    gq = gq * (GLA_DK ** -0.5)
    w1 = jnp.concatenate([
        cq, ckv,
        zeros(MLA_NOPE), kr, zeros(HEAD_SLOT - MLA_NOPE - MLA_ROPE),
        zeros(MLA_NOPE), kr[:, _axial_perm()], zeros(HEAD_SLOT - MLA_NOPE - MLA_ROPE),
        rq, rq[:, _ret_perm()], rk, rk[:, _ret_perm()], rv, rg,
        gq, gk, gv, gr, gz, zeros(LANES - 2 * GLA_GATE_RANK)], axis=1)
    return w1.astype(BF16), bg.astype(BF16)


def _prep_mla(w_uq, w_ukv):
    r = w_uq.shape[0]
    scale = (MLA_NOPE + MLA_ROPE) ** -0.5 * np.log2(np.e)
    wq3 = w_uq.reshape(r, MLA_HEADS, MLA_NOPE + MLA_ROPE) * scale
    pad = jnp.zeros((r, MLA_HEADS, HEAD_SLOT - MLA_NOPE - MLA_ROPE), w_uq.dtype)
    wq = jnp.concatenate([wq3, pad], axis=-1)
    wqs = jnp.concatenate([jnp.zeros((r, MLA_HEADS, MLA_NOPE), w_uq.dtype),
                           wq3[:, :, MLA_NOPE:][:, :, _axial_perm()], pad], axis=-1)
    rk = w_ukv.shape[0]
    wkv3 = w_ukv.reshape(rk, MLA_HEADS, MLA_NOPE + MLA_V)
    wk = jnp.concatenate([wkv3[:, :, :MLA_NOPE],
                          jnp.zeros((rk, MLA_HEADS, HEAD_SLOT - MLA_NOPE), w_ukv.dtype)], axis=-1)
    wv = wkv3[:, :, MLA_NOPE:]
    flat = lambda a: a.reshape(a.shape[0], -1).astype(BF16)
    return flat(wq), flat(wqs), flat(wk), flat(wv)


def _prep_gla_gate(w_gate, b_gate):
    hw = GLA_HEADS * GLA_DK
    wg2 = jnp.zeros((LANES, 2 * hw), F32)
    wg2 = wg2.at[0:GLA_GATE_RANK, 0:hw].set(w_gate[0])
    wg2 = wg2.at[GLA_GATE_RANK:2 * GLA_GATE_RANK, hw:2 * hw].set(w_gate[1])
    return wg2, b_gate.reshape(1, 2 * hw)


def _rope_tables(n_ctx, seq):
    t = jnp.arange(seq, dtype=jnp.int32)
    half = MLA_ROPE // 4
    inv = ROPE_BASE ** (-jnp.arange(half, dtype=F32) / half)
    lane = np.arange(HEAD_SLOT)
    dd = lane - MLA_NOPE
    active = (dd >= 0) & (dd < MLA_ROPE)
    dd = np.clip(dd, 0, MLA_ROPE - 1)
    by_col = (dd // 16) == 1
    first = (dd % 16) < half
    pos = jnp.where(by_col[None, :], (t % GRID_W)[:, None], (t // GRID_W)[:, None]).astype(F32)
    ang = pos * inv[dd % half][None, :]
    cosq = jnp.where(active[None, :], jnp.cos(ang), 1.0)
    sinq = jnp.where(active[None, :], jnp.where(first[None, :], -jnp.sin(ang), jnp.sin(ang)), 0.0)
    rhalf = RET_DK // 2
    rinv = ROPE_BASE ** (-jnp.arange(rhalf, dtype=F32) / rhalf)
    rl = np.arange(RET_HEADS * RET_DK) % RET_DK
    rang = t.astype(F32)[:, None] * rinv[rl % rhalf][None, :]
    cosr = jnp.cos(rang)
    sinr = jnp.where((rl < rhalf)[None, :], -jnp.sin(rang), jnp.sin(rang))
    ctx = lambda a, fill: jnp.concatenate([jnp.full((n_ctx, a.shape[1]), fill, F32), a], axis=0)
    return ctx(cosq, 1.0), ctx(sinq, 0.0), ctx(cosr, 1.0), ctx(sinr, 0.0)


def kernel(x, c, ctx, c_ctx, mod_w, mod_b, norm1_w, norm2_w, w_in, mla_q_norm, mla_w_uq, mla_kv_norm, mla_w_ukv, ret_decay_logit, ret_norm_w, gla_w_gate, gla_b_gate, gla_norm_w, w_branch, w_out, ffn_w_gate, ffn_w_up, ffn_w_down, moe_router, moe_w_gate, moe_w_up, moe_w_down, final_norm_w):
    b, seq, d = x.shape
    n_ctx = ctx.shape[1]
    depth = mod_w.shape[0]
    assert d == D_MODEL and seq % GRID_W == 0
    assert n_ctx % TOKEN_TILE == 0 and n_ctx % RET_CHUNK == 0 and n_ctx % GLA_CHUNK == 0
    assert seq % TOKEN_TILE == 0 and seq % RET_CHUNK == 0
    nct = n_ctx // TOKEN_TILE

    rows = 8 * ((b + 1 + 7) // 8)
    cvec = jnp.zeros((rows, d), F32).at[0:b].set(c).at[b].set(c_ctx)
    mod = _modulation(cvec, mod_w, mod_b)
    cosq, sinq, cosr, sinr = _rope_tables(n_ctx, seq)
    fnw = final_norm_w.reshape(1, d)
    row2 = lambda a: a.reshape(1, -1)

    xs = jnp.concatenate([ctx, x], axis=1)
    for l in range(depth):
        last = l == depth - 1
        modsel = jnp.stack([jnp.broadcast_to(mod[l, b], (b, 6 * d)), mod[l, 0:b]], axis=1)
        w1, wbg = _prep_w_in(w_in[l])
        wq, wqs, wk, wv = _prep_mla(mla_w_uq[l], mla_w_ukv[l])
        wg2, bg2 = _prep_gla_gate(gla_w_gate[l], gla_b_gate[l])
        nw1, nw2 = row2(norm1_w[l]), row2(norm2_w[l])

        (q, k, v, rq, rk, rv, rg, gq, gk, gv, gr, la) = _inproj(
            xs, modsel, nw1, w1, row2(mla_q_norm[l]), wq, wqs, row2(mla_kv_norm[l]), wk, wv,
            wg2, bg2, cosq, sinq, cosr, sinr, nct)
        oa = _attention(q, k, v, nct)
        dl = jnp.broadcast_to(ret_decay_logit[l][:, :, None, None], (2, RET_HEADS, 8, LANES))
        rof, rob = _retention(dl, rq, rk, rv, n_ctx // RET_CHUNK)
        gof, gob = _gla(gq, gk, gv, la, n_ctx)
        xs = _merge(xs, modsel, nw1, oa, rof, rob, rg, gof, gob, gr, wbg,
                    w_branch[l].astype(BF16), w_out[l].astype(BF16),
                    row2(ret_norm_w[l]), row2(gla_norm_w[l]), nct)
        i = l // 2
        if l % 2 == 0:
            xs = _ffn(xs, modsel, nw2, ffn_w_gate[i].astype(BF16), ffn_w_up[i].astype(BF16),
                      ffn_w_down[i].astype(BF16), fnw, nct, last)
        else:
            rt = jnp.zeros((d, LANES), F32).at[:, 0:N_EXPERTS].set(moe_router[i])
            xs = _moe(xs, modsel, nw2, rt, moe_w_gate[i].astype(BF16), moe_w_up[i].astype(BF16),
                      moe_w_down[i].astype(BF16), fnw, nct, last)
    return xs
```

```python
import functools

import numpy as np
import jax
import jax.numpy as jnp
from jax import lax
from jax.experimental import pallas as pl
from jax.experimental.pallas import tpu as pltpu

D_MODEL = 1024
GRID_W = 64
EPS = 1e-6
ROPE_BASE = 10000.0

MLA_HEADS = 8
MLA_NOPE = 64
MLA_ROPE = 32
MLA_V = 64
MLA_Q_RANK = 256
MLA_KV_RANK = 128

RET_HEADS = 4
RET_DK = 64
RET_DV = 128

GLA_HEADS = 4
GLA_DK = 64
GLA_DV = 128
GLA_GATE_RANK = 16
GLA_TAU = 16.0

N_BRANCH = 3
BRANCH_W = 512
D_FF = 2816
N_EXPERTS = 8

IN_SPLITS = (
    MLA_Q_RANK, MLA_KV_RANK, MLA_ROPE,
    RET_HEADS * RET_DK, RET_HEADS * RET_DK, RET_HEADS * RET_DV, RET_HEADS * RET_DV,
    GLA_HEADS * GLA_DK, GLA_HEADS * GLA_DK, GLA_HEADS * GLA_DV, GLA_HEADS * GLA_DV,
    2 * GLA_GATE_RANK,
    N_BRANCH * D_MODEL,
)

F32 = jnp.float32
BF16 = jnp.bfloat16
HI = lax.Precision.HIGHEST

LANES = 128
HEAD_SLOT = 128
TOKEN_TILE = 256
ATT_SUM_ROWS = 16
RET_CHUNK = 256
GLA_CHUNK = 128
GLA_BASE = 16
MOE_TILE = 1024
MOE_BLOCK = 288
MOD_COLS_TILE = 1536
VMEM_LIMIT = 56 * 1024 * 1024


def _silu(x):
    return x / (1.0 + jnp.exp(-x))


def _sigmoid(x):
    return 1.0 / (1.0 + jnp.exp(-x))


def _log_sigmoid(z):
    return jnp.minimum(z, 0.0) - jnp.log(1.0 + jnp.exp(-jnp.abs(z)))


def _rms(x, w):
    return x * lax.rsqrt(jnp.mean(x * x, axis=-1, keepdims=True) + EPS) * w


def _dot(a, b, precision=None):
    return jnp.dot(a, b, preferred_element_type=F32, precision=precision)


def _dot_nt(a, b):
    return lax.dot_general(a, b, (((1,), (1,)), ((), ())), preferred_element_type=F32)


def _params(sem):
    return pltpu.CompilerParams(dimension_semantics=sem, vmem_limit_bytes=VMEM_LIMIT)


def _const_spec(shape):
    nd = len(shape)
    return pl.BlockSpec(shape, lambda *_: (0,) * nd, pipeline_mode=pl.Buffered(1))


def _mod_rows(mod_ref, is_ctx):
    m = mod_ref[0]
    return jnp.where(is_ctx, m[0:1], m[1:2])


def _mod_kernel(c_ref, w_ref, b_ref, o_ref):
    o_ref[0] = _dot(_silu(c_ref[...]), w_ref[0], HI) + b_ref[0]


def _modulation(cvec, mod_w, mod_b):
    depth, d, n6 = mod_w.shape
    rows = cvec.shape[0]
    tn = MOD_COLS_TILE
    return pl.pallas_call(
        _mod_kernel,
        grid=(depth, n6 // tn),
        in_specs=[pl.BlockSpec((rows, d), lambda l, j: (0, 0)),
                  pl.BlockSpec((1, d, tn), lambda l, j: (l, 0, j)),
                  pl.BlockSpec((1, 1, tn), lambda l, j: (l, 0, j))],
        out_specs=pl.BlockSpec((1, rows, tn), lambda l, j: (l, 0, j)),
        out_shape=jax.ShapeDtypeStruct((depth, rows, n6), F32),
        compiler_params=_params(("arbitrary", "arbitrary")),
    )(cvec, mod_w, mod_b.reshape(depth, 1, n6))


_C_CQ = (0, 256)
_C_CKV = (256, 384)
_C_KRA = (384, 512)
_C_KRB = (512, 640)
_C_RQ = (640, 896)
_C_RQS = (896, 1152)
_C_RK = (1152, 1408)
_C_RKS = (1408, 1664)
_C_RV = (1664, 2176)
_C_RG = (2176, 2688)
_C_GQ = (2688, 2944)
_C_GK = (2944, 3200)
_C_GV = (3200, 3712)
_C_GR = (3712, 4224)
_C_GZ = (4224, 4352)
_W1_COLS = 4352


def _inproj_kernel(nct, x_ref, mod_ref, nw_ref, w_ref, qn_ref, wq_ref, wqs_ref, kvn_ref, wk_ref,
                   wv_ref, wg2_ref, bg2_ref, cq_ref, sq_ref, cr_ref, sr_ref,
                   q_out, k_out, v_out, rq_out, rk_out, rv_out, rg_out,
                   gq_out, gk_out, gv_out, gr_out, la_out):
    d = D_MODEL
    row = _mod_rows(mod_ref, pl.program_id(1) < nct)
    sh, sc = row[:, 0:d], row[:, d:2 * d]
    u = (_rms(x_ref[0], nw_ref[...]) * (1.0 + sc) + sh).astype(BF16)

    def seg(c):
        return _dot(u, w_ref[:, c[0]:c[1]])

    cqn = _rms(seg(_C_CQ), qn_ref[...]).astype(BF16)
    q_all = _dot(cqn, wq_ref[...])
    q_swp = _dot(cqn, wqs_ref[...])
    ckvn = _rms(seg(_C_CKV), kvn_ref[...]).astype(BF16)
    k_all = _dot(ckvn, wk_ref[...])
    v_out[0, 0] = _dot(ckvn, wv_ref[...]).T.astype(BF16)
    cosq, sinq = cq_ref[...], sq_ref[...]
    k_rope = seg(_C_KRA) * cosq + seg(_C_KRB) * sinq
    for h in range(MLA_HEADS):
        sl = slice(h * HEAD_SLOT, (h + 1) * HEAD_SLOT)
        q_out[0, h] = (q_all[:, sl] * cosq + q_swp[:, sl] * sinq).T.astype(BF16)
        k_out[0, h] = (k_all[:, sl] + k_rope).astype(BF16)

    cosr, sinr = cr_ref[...], sr_ref[...]
    rq_out[0] = (seg(_C_RQ) * cosr + seg(_C_RQS) * sinr).astype(BF16)
    rk_out[0] = (seg(_C_RK) * cosr + seg(_C_RKS) * sinr).astype(BF16)
    rv_out[0] = seg(_C_RV).astype(BF16)
    rg_out[0] = seg(_C_RG).astype(BF16)

    gq_out[0] = seg(_C_GQ).astype(BF16)
    gk_out[0] = seg(_C_GK).astype(BF16)
    gv_out[0] = seg(_C_GV).astype(BF16)
    gr_out[0] = seg(_C_GR).astype(BF16)
    z = _dot(seg(_C_GZ), wg2_ref[...], HI) + bg2_ref[...]
    la_out[0] = _log_sigmoid(z) * (1.0 / GLA_TAU)


def _inproj(xs, modsel, nw, w1, qn, wq, wqs, kvn, wk, wv, wg2, bg2, cosq, sinq, cosr, sinr, nct):
    b, n, d = xs.shape
    tm = TOKEN_TILE
    nt = n // tm
    h = MLA_HEADS
    tok = lambda w: pl.BlockSpec((1, tm, w), lambda i, t: (i, t, 0))
    tab = lambda w: pl.BlockSpec((tm, w), lambda i, t: (t, 0))
    q_spec = pl.BlockSpec((1, h, HEAD_SLOT, tm), lambda i, t: (i, 0, 0, t))
    k_spec = pl.BlockSpec((1, h, tm, HEAD_SLOT), lambda i, t: (i, 0, t, 0))
    v_spec = pl.BlockSpec((1, 1, h * MLA_V, tm), lambda i, t: (i, t, 0, 0))
    sds = lambda w, dt=BF16: jax.ShapeDtypeStruct((b, n, w), dt)
    q_sds = jax.ShapeDtypeStruct((b, h, HEAD_SLOT, n), BF16)
    k_sds = jax.ShapeDtypeStruct((b, h, n, HEAD_SLOT), BF16)
    v_sds = jax.ShapeDtypeStruct((b, nt, h * MLA_V, tm), BF16)
    return pl.pallas_call(
        functools.partial(_inproj_kernel, nct),
        grid=(b, nt),
        in_specs=[tok(d), pl.BlockSpec((1, 2, 6 * d), lambda i, t: (i, 0, 0)), _const_spec(nw.shape),
                  _const_spec(w1.shape), _const_spec(qn.shape), _const_spec(wq.shape),
                  _const_spec(wqs.shape), _const_spec(kvn.shape), _const_spec(wk.shape),
                  _const_spec(wv.shape), _const_spec(wg2.shape), _const_spec(bg2.shape),
                  tab(HEAD_SLOT), tab(HEAD_SLOT), tab(256), tab(256)],
        out_specs=[q_spec, k_spec, v_spec, tok(256), tok(256), tok(512), tok(512),
                   tok(256), tok(256), tok(512), tok(512), tok(512)],
        out_shape=[q_sds, k_sds, v_sds, sds(256), sds(256), sds(512), sds(512),
                   sds(256), sds(256), sds(512), sds(512), sds(512, F32)],
        compiler_params=_params(("arbitrary", "arbitrary")),
    )(xs, modsel, nw, w1, qn, wq, wqs, kvn, wk, wv, wg2, bg2, cosq, sinq, cosr, sinr)


def _attn_kernel(nct, n_tiles, qt_ref, k_ref, vt_ref, o_ref, *bufs):
    t = TOKEN_TILE
    heads = range(2)
    s_bufs = (bufs[0:2], bufs[2:4])
    p_bufs = (bufs[4:6], bufs[6:8])
    n_chunks = jnp.where(pl.program_id(2) < nct, nct, n_tiles)
    last = n_chunks - 1
    qts = [qt_ref[0, j] for j in heads]
    ones = jnp.ones((ATT_SUM_ROWS, t), BF16)

    def scores(j, c, s_buf):
        s = _dot(k_ref[0, j, pl.ds(pl.multiple_of(c * t, t), t), :], qts[j])
        s_buf[...] = s
        return jnp.max(s, axis=0, keepdims=True)

    def softmax(s_buf, p_buf, m, cmax):
        m_new = jnp.maximum(m, cmax)
        p_buf[...] = jnp.exp2(s_buf[...] - m_new).astype(BF16)
        return m_new, jnp.exp2(m - m_new)

    def values(j, c, p_buf, alpha, acc):
        lhs = jnp.concatenate([vt_ref[0, c, j * MLA_V:(j + 1) * MLA_V, :], ones], axis=0)
        return alpha * acc + _dot(lhs, p_buf[...])

    def step(c_prev, c_next, cur, state):
        accs = [values(j, c_prev, p_bufs[j][1 - cur], state[j][1], state[j][2]) for j in heads]
        stats = [softmax(s_bufs[j][cur], p_bufs[j][cur], state[j][0], state[j][3]) for j in heads]
        cmaxs = [scores(j, c_next, s_bufs[j][1 - cur]) for j in heads]
        return tuple((stats[j][0], stats[j][1], accs[j], cmaxs[j]) for j in heads)

    def pair(i, state):
        state = step(jnp.maximum(2 * i - 1, 0), 2 * i + 1, 0, state)
        return step(2 * i, 2 * i + 2, 1, state)

    init = []
    for j in heads:
        p_bufs[j][1][...] = jnp.zeros((t, t), BF16)
        init.append((jnp.full((1, t), -jnp.inf, F32), jnp.ones((1, t), F32),
                     jnp.zeros((MLA_V + ATT_SUM_ROWS, t), F32), scores(j, 0, s_bufs[j][0])))
    state = lax.fori_loop(0, last // 2, pair, tuple(init))
    outs = []
    for j in heads:
        m, alpha, acc, cmax = state[j]
        acc = values(j, jnp.maximum(last - 1, 0), p_bufs[j][1], alpha, acc)
        m, alpha = softmax(s_bufs[j][0], p_bufs[j][0], m, cmax)
        acc = values(j, last, p_bufs[j][0], alpha, acc)
        outs.append(acc[0:MLA_V] / acc[MLA_V:MLA_V + 1])
    o_ref[0] = jnp.concatenate(outs, axis=0).T.astype(BF16)


def _attention(qt, k, vt, nct):
    b, h, w, n = qt.shape
    t = TOKEN_TILE
    n_tiles = n // t
    assert nct % 2 == 1 and n_tiles % 2 == 1
    return pl.pallas_call(
        functools.partial(_attn_kernel, nct, n_tiles),
        grid=(b, h // 2, n_tiles),
        in_specs=[pl.BlockSpec((1, 2, w, t), lambda i, p, q: (i, p, 0, q)),
                  pl.BlockSpec((1, 2, n, w), lambda i, p, q: (i, p, 0, 0)),
                  pl.BlockSpec((1, n_tiles, 2 * MLA_V, t), lambda i, p, q: (i, 0, p, 0))],
        out_specs=pl.BlockSpec((1, t, 2 * MLA_V), lambda i, p, q: (i, q, p)),
        out_shape=jax.ShapeDtypeStruct((b, n, h * MLA_V), BF16),
        scratch_shapes=[pltpu.VMEM((t, t), F32)] * 4 + [pltpu.VMEM((t, t), BF16)] * 4,
        compiler_params=_params(("arbitrary", "arbitrary", "arbitrary")),
    )(qt, k, vt)


def _bwd_chunk(s, nct, nch):
    return jnp.where(s < nct, nct - 1 - s, nch - 1 - (s - nct))


def _ret_kernel(dl_ref, qf, kf, vf, qb, kb, vb, of_ref, ob_ref, st_ref):
    c = RET_CHUNK
    hw = RET_HEADS * RET_DK

    @pl.when(pl.program_id(1) == 0)
    def _():
        st_ref[...] = jnp.zeros_like(st_ref)

    ii = lax.broadcasted_iota(jnp.int32, (c, c), 0)
    jj = lax.broadcasted_iota(jnp.int32, (c, c), 1)
    pos_i = lax.broadcasted_iota(jnp.int32, (c, 1), 0).astype(F32)
    pos_j = lax.broadcasted_iota(jnp.int32, (1, c), 1).astype(F32)
    lane_head = lax.broadcasted_iota(jnp.int32, (c, hw), 1) // RET_DK
    for d, (q_ref, k_ref, v_ref, o_ref) in enumerate(((qf, kf, vf, of_ref), (qb, kb, vb, ob_ref))):
        rev = d == 1
        q, v = q_ref[0], v_ref[0]
        k_t = k_ref[0].astype(F32).T
        k_tb = k_t.astype(BF16)
        st = st_ref[d]
        st_b = st.astype(BF16)
        dist = ((jj - ii) if rev else (ii - jj)).astype(F32)
        for h in range(RET_HEADS):
            lg = _log_sigmoid(dl_ref[d, h])[0:1, 0:1]
            intra = jnp.where(dist >= 0.0, jnp.exp(lg * jnp.maximum(dist, 0.0)), 0.0)
            q_dec = jnp.exp(lg * ((c - pos_i) if rev else (pos_i + 1.0)))
            k_dec = jnp.exp(lg * (pos_j if rev else (c - 1.0 - pos_j)))
            rows = slice(h * RET_DK, (h + 1) * RET_DK)
            cols = slice(h * RET_DV, (h + 1) * RET_DV)
            qm = jnp.where(lane_head == h, q, jnp.zeros_like(q))
            scores = _dot(qm, k_tb) * intra
            o = _dot(scores.astype(BF16), v[:, cols]) + _dot(qm, st_b) * q_dec
            o_ref[0, :, cols] = o
            kd = (k_t[rows] * k_dec).astype(BF16)
            st_ref[d, rows, :] = st[rows] * jnp.exp(lg * float(c)) + _dot(kd, v[:, cols])


def _retention(dl, rq, rk, rv, nct):
    b, n, hw = rq.shape
    c = RET_CHUNK
    nch = n // c
    vw = rv.shape[-1]
    fwd = lambda w: pl.BlockSpec((1, c, w), lambda i, s: (i, s, 0))
    bwd = lambda w: pl.BlockSpec((1, c, w), lambda i, s: (i, _bwd_chunk(s, nct, nch), 0))
    return pl.pallas_call(
        _ret_kernel,
        grid=(b, nch),
        in_specs=[_const_spec(dl.shape), fwd(hw), fwd(hw), fwd(vw), bwd(hw), bwd(hw), bwd(vw)],
        out_specs=[fwd(vw), bwd(vw)],
        out_shape=[jax.ShapeDtypeStruct((b, n, vw), F32)] * 2,
        scratch_shapes=[pltpu.VMEM((2, hw, RET_DV), F32)],
        compiler_params=_params(("arbitrary", "arbitrary")),
    )(dl, rq, rk, rv, rq, rk, rv)


def _block_start(i, size):
    return i & ~(size - 1)


def _gla_levels():
    half = GLA_CHUNK // 2
    out = []
    while half >= GLA_BASE:
        out.append(half)
        half //= 2
    return out


def _gla_kernel(qf, kf, vf, laf, qb, kb, vb, lab, of_ref, ob_ref, st_ref):
    c = GLA_CHUNK
    hw = GLA_HEADS * GLA_DK
    levels = _gla_levels()

    @pl.when(pl.program_id(1) == 0)
    def _():
        st_ref[...] = jnp.zeros_like(st_ref)

    ii = lax.broadcasted_iota(jnp.int32, (c, c), 0)
    jj = lax.broadcasted_iota(jnp.int32, (c, c), 1)
    pos = lax.broadcasted_iota(jnp.int32, (c, 1), 0)
    lane_head = lax.broadcasted_iota(jnp.int32, (c, hw), 1) // GLA_DK
    for d, (q_ref, k_ref, v_ref, la_ref, o_ref) in enumerate(((qf, kf, vf, laf, of_ref),
                                                               (qb, kb, vb, lab, ob_ref))):
        rev = d == 1
        q, k, v, la = q_ref[0].astype(F32), k_ref[0].astype(F32), v_ref[0], la_ref[0]

        def prefix(ref_row):
            return ((jj >= ref_row) if rev else (jj <= ref_row)).astype(F32)

        mats = [prefix(ii)]
        for half in levels:
            blk = _block_start(ii, 2 * half)
            mats.append(prefix(blk + (half if rev else half - 1)))
        blk = _block_start(ii, GLA_BASE)
        mats.append(prefix(blk + (GLA_BASE - 1 if rev else 0)))
        cums = _dot(jnp.concatenate(mats, axis=0), la, HI)
        b = cums[0:c]
        b_end = b[0:1] if rev else b[c - 1:c]

        q_t = (q * jnp.exp(b)).astype(BF16)
        k_tt = (k * jnp.exp(b_end - b)).T
        dec_col = jnp.exp(jnp.broadcast_to(b_end, (LANES, hw)).T[:, 0:1])

        terms = []
        for n_l, half in enumerate(levels):
            e = cums[(n_l + 1) * c:(n_l + 2) * c]
            inner = pos & (2 * half - 1)
            late = (inner < half) if rev else (inner >= half)
            qh = jnp.where(late, q * jnp.exp(jnp.where(late, b - e, 0.0)), 0.0)
            kh = jnp.where(late, 0.0, k * jnp.exp(jnp.where(late, 0.0, e - b)))
            terms.append((qh.astype(BF16), kh.astype(BF16), _block_start(ii, 2 * half) == _block_start(jj, 2 * half)))
        r = cums[(len(levels) + 1) * c:(len(levels) + 2) * c]
        tri = (jj >= ii) if rev else (jj <= ii)
        terms.append(((q * jnp.exp(b - r)).astype(BF16), (k * jnp.exp(r - b)).astype(BF16),
                      (_block_start(ii, GLA_BASE) == _block_start(jj, GLA_BASE)) & tri))

        st = st_ref[d]
        st_b = st.astype(BF16)
        for h in range(GLA_HEADS):
            rows = slice(h * GLA_DK, (h + 1) * GLA_DK)
            cols = slice(h * GLA_DV, (h + 1) * GLA_DV)
            sel = lane_head == h
            a = jnp.zeros((c, c), F32)
            for qh, kh, mask in terms:
                a = a + jnp.where(mask, _dot_nt(jnp.where(sel, qh, jnp.zeros_like(qh)), kh), 0.0)
            vh = v[:, cols]
            o_ref[0, :, cols] = _dot(a.astype(BF16), vh) + _dot(jnp.where(sel, q_t, jnp.zeros_like(q_t)), st_b)
            st_ref[d, rows, :] = st[rows] * dec_col[rows] + _dot(k_tt[rows].astype(BF16), vh)


def _gla(gq, gk, gv, la, nct_tokens):
    b, n, hw = gq.shape
    c = GLA_CHUNK
    nch = n // c
    nct = nct_tokens // c
    vw = gv.shape[-1]
    fwd = lambda w, col=0: pl.BlockSpec((1, c, w), lambda i, s: (i, s, col))
    bwd = lambda w, col=0: pl.BlockSpec((1, c, w), lambda i, s: (i, _bwd_chunk(s, nct, nch), col))
    return pl.pallas_call(
        _gla_kernel,
        grid=(b, nch),
        in_specs=[fwd(hw), fwd(hw), fwd(vw), fwd(hw, 0), bwd(hw), bwd(hw), bwd(vw), bwd(hw, 1)],
        out_specs=[fwd(vw), bwd(vw)],
        out_shape=[jax.ShapeDtypeStruct((b, n, vw), F32)] * 2,
        scratch_shapes=[pltpu.VMEM((2, hw, GLA_DV), F32)],
        compiler_params=_params(("arbitrary", "arbitrary")),
    )(gq, gk, gv, la, gq, gk, gv, la)


def _head_norm(o, w, center):
    parts = []
    for h in range(o.shape[-1] // LANES):
        oh = o[:, h * LANES:(h + 1) * LANES]
        if center:
            oh = oh - jnp.mean(oh, axis=-1, keepdims=True)
        parts.append(oh * lax.rsqrt(jnp.mean(oh * oh, axis=-1, keepdims=True) + EPS))
    return jnp.concatenate(parts, axis=-1) * w


def _merge_kernel(nct, off, x_ref, mod_ref, nw_ref, oa_ref, rof_ref, rob_ref, rg_ref, gof_ref, gob_ref,
                  gr_ref, wbg_ref, wbr_ref, wout_ref, rnw_ref, gnw_ref, o_ref):
    d = D_MODEL
    row = _mod_rows(mod_ref, pl.program_id(1) + off < nct)
    sh, sc, g1 = row[:, 0:d], row[:, d:2 * d], row[:, 2 * d:3 * d]
    x = x_ref[0]
    u = (_rms(x, nw_ref[...]) * (1.0 + sc) + sh).astype(BF16)
    gate = _sigmoid(_dot(u, wbg_ref[...]))
    yb = _silu(rg_ref[0].astype(F32)) * _head_norm(rof_ref[0] + rob_ref[0], rnw_ref[...], True)
    yc = _silu(gr_ref[0].astype(F32)) * _head_norm(gof_ref[0] + gob_ref[0], gnw_ref[...], False)
    z = (gate[:, 0:d] * _dot(oa_ref[0], wbr_ref[0])
         + gate[:, d:2 * d] * _dot(yb.astype(BF16), wbr_ref[1])
         + gate[:, 2 * d:3 * d] * _dot(yc.astype(BF16), wbr_ref[2]))
    o_ref[0] = x + g1 * _dot(z.astype(BF16), wout_ref[...])


def _merge(xs, modsel, nw, oa, rof, rob, rg, gof, gob, gr, wbg, wbr, wout, rnw, gnw, nct, latent_only):
    b, n, d = xs.shape
    tm = TOKEN_TILE
    off = nct if latent_only else 0
    nt = n // tm - off
    tok = lambda w: pl.BlockSpec((1, tm, w), lambda i, t: (i, t + off, 0))
    return pl.pallas_call(
        functools.partial(_merge_kernel, nct, off),
        grid=(b, nt),
        in_specs=[tok(d), pl.BlockSpec((1, 2, 6 * d), lambda i, t: (i, 0, 0)), _const_spec(nw.shape),
                  tok(512), tok(512), tok(512), tok(512), tok(512), tok(512), tok(512),
                  _const_spec(wbg.shape), _const_spec(wbr.shape), _const_spec(wout.shape),
                  _const_spec(rnw.shape), _const_spec(gnw.shape)],
        out_specs=pl.BlockSpec((1, tm, d), lambda i, t: (i, t, 0)),
        out_shape=jax.ShapeDtypeStruct((b, nt * tm, d), F32),
        compiler_params=_params(("arbitrary", "arbitrary")),
    )(xs, modsel, nw, oa, rof, rob, rg, gof, gob, gr, wbg, wbr, wout, rnw, gnw)


def _norm2(x, row, nw_ref):
    d = D_MODEL
    sh, sc = row[:, 3 * d:4 * d], row[:, 4 * d:5 * d]
    return _rms(x, nw_ref[...]) * (1.0 + sc) + sh


def _finish(x, y, row, final, fnw_ref):
    d = D_MODEL
    out = x + row[:, 5 * d:6 * d] * y
    if final:
        out = _rms(out, fnw_ref[...])
    return out


def _ffn_kernel(nct, off, final, x_ref, mod_ref, nw_ref, wg_ref, wu_ref, wd_ref, fnw_ref, o_ref):
    row = _mod_rows(mod_ref, pl.program_id(1) + off < nct)
    x = x_ref[0]
    v = _norm2(x, row, nw_ref).astype(BF16)
    hdn = _silu(_dot(v, wg_ref[...])) * _dot(v, wu_ref[...])
    o_ref[0] = _finish(x, _dot(hdn.astype(BF16), wd_ref[...]), row, final, fnw_ref)


def _ffn(xs, modsel, nw, wg, wu, wd, fnw, nct, final):
    b, n, d = xs.shape
    tm = TOKEN_TILE
    off = nct if final else 0
    nt = n // tm - off
    return pl.pallas_call(
        functools.partial(_ffn_kernel, nct, off, final),
        grid=(b, nt),
        in_specs=[pl.BlockSpec((1, tm, d), lambda i, t: (i, t + off, 0)),
                  pl.BlockSpec((1, 2, 6 * d), lambda i, t: (i, 0, 0)), _const_spec(nw.shape),
                  _const_spec(wg.shape), _const_spec(wu.shape), _const_spec(wd.shape),
                  _const_spec(fnw.shape)],
        out_specs=pl.BlockSpec((1, tm, d), lambda i, t: (i, t, 0)),
        out_shape=jax.ShapeDtypeStruct((b, nt * tm, d), F32),
        compiler_params=_params(("arbitrary", "arbitrary")),
    )(xs, modsel, nw, wg, wu, wd, fnw)


def _top2_gates(logits):
    lane = lax.broadcasted_iota(jnp.int32, logits.shape, 1)
    m1 = jnp.max(logits, axis=-1, keepdims=True)
    i1 = jnp.min(jnp.where(logits == m1, lane, LANES), axis=-1, keepdims=True)
    rest = jnp.where(lane == i1, -jnp.inf, logits)
    m2 = jnp.max(rest, axis=-1, keepdims=True)
    i2 = jnp.min(jnp.where(rest == m2, lane, LANES), axis=-1, keepdims=True)
    e2 = jnp.exp(m2 - m1)
    w1 = 1.0 / (1.0 + e2)
    gates = jnp.where(lane == i1, w1, jnp.where(lane == i2, e2 * w1, 0.0))
    return gates, (lane == i1) | (lane == i2)


def _one_hot(cond):
    return jnp.where(cond, 1.0, 0.0).astype(BF16)


def _route_kernel(mrow, x_ref, mod_ref, nw_ref, rt_ref, v_out, gate_out, rank_out, rankt_out):
    v = _norm2(x_ref[0], mod_ref[0][mrow:mrow + 1], nw_ref)
    v_out[0] = v.astype(BF16)
    tt = v.shape[0]
    lane = lax.broadcasted_iota(jnp.int32, (tt, LANES), 1)
    logits = jnp.where(lane < N_EXPERTS, _dot(v, rt_ref[...], HI), -jnp.inf)
    gates, routed = _top2_gates(logits)
    gate_out[0] = gates
    ti = lax.broadcasted_iota(jnp.int32, (tt, tt), 0)
    tj = lax.broadcasted_iota(jnp.int32, (tt, tt), 1)
    rank_out[0] = jnp.where(routed, _dot(_one_hot(tj < ti), _one_hot(routed)), -1.0)
    routed_t = jnp.where(routed, 1.0, 0.0).T
    rank_t = jnp.where(routed_t > 0.5, _dot(routed_t.astype(BF16), _one_hot(ti < tj)), -1.0)
    rankt_out[0, 0] = rank_t[0:N_EXPERTS]


def _experts_kernel(mrow, final, v_ref, gate_ref, rank_ref, rankt_ref, wg_ref, wu_ref, wd_ref,
                    x_ref, mod_ref, fnw_ref, acc_ref, o_ref):
    e = pl.program_id(0)
    tt = v_ref.shape[1]
    r = MOE_BLOCK
    fh = D_FF // 2
    lane = lax.broadcasted_iota(jnp.int32, (tt, LANES), 1)
    g_col = jnp.sum(jnp.where(lane == e, gate_ref[0], 0.0), axis=-1, keepdims=True)
    rk_col = jnp.sum(jnp.where(lane == e, rank_ref[0], 0.0), axis=-1, keepdims=True)
    rk_row = rankt_ref[0, 0, pl.ds(e, 1), :]
    n_routed = jnp.max(rk_col).astype(jnp.int32) + 1
    o_ref[0] = acc_ref[0]

    def block(i, carry):
        base = (i * r).astype(F32)
        slot_i = base + lax.broadcasted_iota(jnp.int32, (r, 1), 0).astype(F32)
        xg = _dot(_one_hot(rk_row == slot_i), v_ref[0]).astype(BF16)
        y = jnp.zeros((r, D_MODEL), F32)
        for c in range(2):
            cols = slice(c * fh, (c + 1) * fh)
            hdn = _silu(_dot(xg, wg_ref[0, :, cols])) * _dot(xg, wu_ref[0, :, cols])
            y = y + _dot(hdn.astype(BF16), wd_ref[0, cols, :])
        y = y.astype(BF16)
        slot_j = base + lax.broadcasted_iota(jnp.int32, (1, r), 1).astype(F32)
        for c in range(tt // TOKEN_TILE):
            rows = slice(c * TOKEN_TILE, (c + 1) * TOKEN_TILE)
            o_ref[0, rows, :] += g_col[rows] * _dot(_one_hot(rk_col[rows] == slot_j), y)
        return carry

    lax.fori_loop(0, (n_routed + r - 1) // r, block, 0)

    @pl.when(e == N_EXPERTS - 1)
    def _():
        o_ref[0] = _finish(x_ref[0], o_ref[0], mod_ref[0][mrow:mrow + 1], final, fnw_ref)


def _moe(slab, mrow, modsel, nw, rt, wg, wu, wd, fnw, final):
    b, m, d = slab.shape
    tt = min(MOE_TILE, m)
    assert m % tt == 0 and tt % TOKEN_TILE == 0
    ns = m // tt
    ne, _, f = wg.shape
    mod_spec = lambda nd: pl.BlockSpec((1, 2, 6 * d), (lambda i, s: (i, 0, 0)) if nd == 2 else
                                       (lambda e, i, s: (i, 0, 0)))
    v, gates, rank, rank_t = pl.pallas_call(
        functools.partial(_route_kernel, mrow),
        grid=(b, ns),
        in_specs=[pl.BlockSpec((1, tt, d), lambda i, s: (i, s, 0)), mod_spec(2), _const_spec(nw.shape),
                  _const_spec(rt.shape)],
        out_specs=[pl.BlockSpec((1, tt, d), lambda i, s: (i, s, 0)),
                   pl.BlockSpec((1, tt, LANES), lambda i, s: (i, s, 0)),
                   pl.BlockSpec((1, tt, LANES), lambda i, s: (i, s, 0)),
                   pl.BlockSpec((1, 1, ne, tt), lambda i, s: (i, s, 0, 0))],
        out_shape=[jax.ShapeDtypeStruct((b, m, d), BF16), jax.ShapeDtypeStruct((b, m, LANES), F32),
                   jax.ShapeDtypeStruct((b, m, LANES), F32), jax.ShapeDtypeStruct((b, ns, ne, tt), F32)],
        compiler_params=_params(("arbitrary", "arbitrary")),
    )(slab, modsel, nw, rt)

    tok = lambda w: pl.BlockSpec((1, tt, w), lambda e, i, s: (i, s, 0))
    wspec = lambda shape: pl.BlockSpec((1,) + shape, lambda e, i, s: (e, 0, 0), pipeline_mode=pl.Buffered(1))
    x_spec = pl.BlockSpec((1, tt, d), lambda e, i, s: (i, jnp.where(e == ne - 1, s, 0), 0),
                          pipeline_mode=pl.Buffered(1))
    return pl.pallas_call(
        functools.partial(_experts_kernel, mrow, final),
        grid=(ne, b, ns),
        in_specs=[tok(d), tok(LANES), tok(LANES),
                  pl.BlockSpec((1, 1, ne, tt), lambda e, i, s: (i, s, 0, 0)),
                  wspec((d, f)), wspec((d, f)), wspec((f, d)),
                  x_spec, mod_spec(3), _const_spec(fnw.shape), tok(d)],
        out_specs=tok(d),
        out_shape=jax.ShapeDtypeStruct((b, m, d), F32),
        input_output_aliases={10: 0},
        compiler_params=_params(("arbitrary", "arbitrary", "arbitrary")),
    )(v, gates, rank, rank_t, wg, wu, wd, slab, modsel, fnw, jnp.zeros((b, m, d), F32))


def _axial_perm():
    dd = np.arange(MLA_ROPE)
    return np.where(dd % 16 < 8, dd + 8, dd - 8)


def _ret_perm():
    dd = np.arange(RET_HEADS * RET_DK)
    return np.where(dd % RET_DK < RET_DK // 2, dd + RET_DK // 2, dd - RET_DK // 2)


def _prep_w_in(w):
    offs = np.cumsum((0,) + IN_SPLITS)
    cq, ckv, kr, rq, rk, rv, rg, gq, gk, gv, gr, gz, bg = [w[:, offs[i]:offs[i + 1]] for i in range(13)]
    zeros = lambda n: jnp.zeros((w.shape[0], n), w.dtype)
    rk = rk * (RET_DK ** -0.5)
    gq = gq * (GLA_DK ** -0.5)
    w1 = jnp.concatenate([
        cq, ckv,
        zeros(MLA_NOPE), kr, zeros(HEAD_SLOT - MLA_NOPE - MLA_ROPE),
        zeros(MLA_NOPE), kr[:, _axial_perm()], zeros(HEAD_SLOT - MLA_NOPE - MLA_ROPE),
        rq, rq[:, _ret_perm()], rk, rk[:, _ret_perm()], rv, rg,
        gq, gk, gv, gr, gz, zeros(LANES - 2 * GLA_GATE_RANK)], axis=1)
    return w1.astype(BF16), bg.astype(BF16)


def _prep_mla(w_uq, w_ukv):
    r = w_uq.shape[0]
    scale = (MLA_NOPE + MLA_ROPE) ** -0.5 * np.log2(np.e)
    wq3 = w_uq.reshape(r, MLA_HEADS, MLA_NOPE + MLA_ROPE) * scale
    pad = jnp.zeros((r, MLA_HEADS, HEAD_SLOT - MLA_NOPE - MLA_ROPE), w_uq.dtype)
    wq = jnp.concatenate([wq3, pad], axis=-1)
    wqs = jnp.concatenate([jnp.zeros((r, MLA_HEADS, MLA_NOPE), w_uq.dtype),
                           wq3[:, :, MLA_NOPE:][:, :, _axial_perm()], pad], axis=-1)
    rk = w_ukv.shape[0]
    wkv3 = w_ukv.reshape(rk, MLA_HEADS, MLA_NOPE + MLA_V)
    wk = jnp.concatenate([wkv3[:, :, :MLA_NOPE],
                          jnp.zeros((rk, MLA_HEADS, HEAD_SLOT - MLA_NOPE), w_ukv.dtype)], axis=-1)
    wv = wkv3[:, :, MLA_NOPE:]
    flat = lambda a: a.reshape(a.shape[0], -1).astype(BF16)
    return flat(wq), flat(wqs), flat(wk), flat(wv)


def _prep_gla_gate(w_gate, b_gate):
    hw = GLA_HEADS * GLA_DK
    wg2 = jnp.zeros((LANES, 2 * hw), F32)
    wg2 = wg2.at[0:GLA_GATE_RANK, 0:hw].set(w_gate[0])
    wg2 = wg2.at[GLA_GATE_RANK:2 * GLA_GATE_RANK, hw:2 * hw].set(w_gate[1])
    return wg2, b_gate.reshape(1, 2 * hw)


def _rope_tables(n_ctx, seq):
    t = jnp.arange(seq, dtype=jnp.int32)
    half = MLA_ROPE // 4
    inv = ROPE_BASE ** (-jnp.arange(half, dtype=F32) / half)
    lane = np.arange(HEAD_SLOT)
    dd = lane - MLA_NOPE
    active = (dd >= 0) & (dd < MLA_ROPE)
    dd = np.clip(dd, 0, MLA_ROPE - 1)
    by_col = (dd // 16) == 1
    first = (dd % 16) < half
    pos = jnp.where(by_col[None, :], (t % GRID_W)[:, None], (t // GRID_W)[:, None]).astype(F32)
    ang = pos * inv[dd % half][None, :]
    cosq = jnp.where(active[None, :], jnp.cos(ang), 1.0)
    sinq = jnp.where(active[None, :], jnp.where(first[None, :], -jnp.sin(ang), jnp.sin(ang)), 0.0)
    rhalf = RET_DK // 2
    rinv = ROPE_BASE ** (-jnp.arange(rhalf, dtype=F32) / rhalf)
    rl = np.arange(RET_HEADS * RET_DK) % RET_DK
    rang = t.astype(F32)[:, None] * rinv[rl % rhalf][None, :]
    cosr = jnp.cos(rang)
    sinr = jnp.where((rl < rhalf)[None, :], -jnp.sin(rang), jnp.sin(rang))
    ctx = lambda a, fill: jnp.concatenate([jnp.full((n_ctx, a.shape[1]), fill, F32), a], axis=0)
    return ctx(cosq, 1.0), ctx(sinq, 0.0), ctx(cosr, 1.0), ctx(sinr, 0.0)


def kernel(x, c, ctx, c_ctx, mod_w, mod_b, norm1_w, norm2_w, w_in, mla_q_norm, mla_w_uq, mla_kv_norm, mla_w_ukv, ret_decay_logit, ret_norm_w, gla_w_gate, gla_b_gate, gla_norm_w, w_branch, w_out, ffn_w_gate, ffn_w_up, ffn_w_down, moe_router, moe_w_gate, moe_w_up, moe_w_down, final_norm_w):
    b, seq, d = x.shape
    n_ctx = ctx.shape[1]
    depth = mod_w.shape[0]
    assert d == D_MODEL and seq % GRID_W == 0
    assert n_ctx % TOKEN_TILE == 0 and n_ctx % RET_CHUNK == 0 and n_ctx % GLA_CHUNK == 0
    assert seq % TOKEN_TILE == 0 and seq % RET_CHUNK == 0
    nct = n_ctx // TOKEN_TILE

    rows = 8 * ((b + 1 + 7) // 8)
    cvec = jnp.zeros((rows, d), F32).at[0:b].set(c).at[b].set(c_ctx)
    mod = _modulation(cvec, mod_w, mod_b)
    cosq, sinq, cosr, sinr = _rope_tables(n_ctx, seq)
    fnw = final_norm_w.reshape(1, d)
    row2 = lambda a: a.reshape(1, -1)

    xs = jnp.concatenate([ctx, x], axis=1)
    for l in range(depth):
        last = l == depth - 1
        modsel = jnp.stack([jnp.broadcast_to(mod[l, b], (b, 6 * d)), mod[l, 0:b]], axis=1)
        w1, wbg = _prep_w_in(w_in[l])
        wq, wqs, wk, wv = _prep_mla(mla_w_uq[l], mla_w_ukv[l])
        wg2, bg2 = _prep_gla_gate(gla_w_gate[l], gla_b_gate[l])
        nw1, nw2 = row2(norm1_w[l]), row2(norm2_w[l])

        (q, k, v, rq, rk, rv, rg, gq, gk, gv, gr, la) = _inproj(
            xs, modsel, nw1, w1, row2(mla_q_norm[l]), wq, wqs, row2(mla_kv_norm[l]), wk, wv,
            wg2, bg2, cosq, sinq, cosr, sinr, nct)
        oa = _attention(q, k, v, nct)
        dl = jnp.broadcast_to(ret_decay_logit[l][:, :, None, None], (2, RET_HEADS, 8, LANES))
        rof, rob = _retention(dl, rq, rk, rv, n_ctx // RET_CHUNK)
        gof, gob = _gla(gq, gk, gv, la, n_ctx)
        is_moe = l % 2 == 1
        xs = _merge(xs, modsel, nw1, oa, rof, rob, rg, gof, gob, gr, wbg,
                    w_branch[l].astype(BF16), w_out[l].astype(BF16),
                    row2(ret_norm_w[l]), row2(gla_norm_w[l]), nct, is_moe and last)
        i = l // 2
        if not is_moe:
            xs = _ffn(xs, modsel, nw2, ffn_w_gate[i].astype(BF16), ffn_w_up[i].astype(BF16),
                      ffn_w_down[i].astype(BF16), fnw, nct, last)
            continue
        rt = jnp.zeros((d, LANES), F32).at[:, 0:N_EXPERTS].set(moe_router[i])
        experts = (moe_w_gate[i].astype(BF16), moe_w_up[i].astype(BF16), moe_w_down[i].astype(BF16))
        if last:
            xs = _moe(xs, 1, modsel, nw2, rt, *experts, fnw, True)
        else:
            xs = jnp.concatenate([_moe(xs[:, :n_ctx], 0, modsel, nw2, rt, *experts, fnw, False),
                                  _moe(xs[:, n_ctx:], 1, modsel, nw2, rt, *experts, fnw, False)], axis=1)
    return xs
```

```python
import functools

import numpy as np
import jax
import jax.numpy as jnp
from jax import lax
from jax.experimental import pallas as pl
from jax.experimental.pallas import tpu as pltpu

D_MODEL = 1024
GRID_W = 64
EPS = 1e-6
ROPE_BASE = 10000.0

MLA_HEADS = 8
MLA_NOPE = 64
MLA_ROPE = 32
MLA_V = 64
MLA_Q_RANK = 256
MLA_KV_RANK = 128

RET_HEADS = 4
RET_DK = 64
RET_DV = 128

GLA_HEADS = 4
GLA_DK = 64
GLA_DV = 128
GLA_GATE_RANK = 16
GLA_TAU = 16.0

N_BRANCH = 3
BRANCH_W = 512
D_FF = 2816
N_EXPERTS = 8

IN_SPLITS = (
    MLA_Q_RANK, MLA_KV_RANK, MLA_ROPE,
    RET_HEADS * RET_DK, RET_HEADS * RET_DK, RET_HEADS * RET_DV, RET_HEADS * RET_DV,
    GLA_HEADS * GLA_DK, GLA_HEADS * GLA_DK, GLA_HEADS * GLA_DV, GLA_HEADS * GLA_DV,
    2 * GLA_GATE_RANK,
    N_BRANCH * D_MODEL,
)

F32 = jnp.float32
BF16 = jnp.bfloat16
HI = lax.Precision.HIGHEST

LANES = 128
HEAD_SLOT = 128
TOKEN_TILE = 256
ATT_SUM_ROWS = 16
RET_CHUNK = 256
GLA_CHUNK = 128
GLA_BASE = 16
MOE_TILE = 512
MOE_BLOCK = 160
MOD_COLS_TILE = 1536
VMEM_LIMIT = 56 * 1024 * 1024


def _silu(x):
    return x / (1.0 + jnp.exp(-x))


def _sigmoid(x):
    return 1.0 / (1.0 + jnp.exp(-x))


def _log_sigmoid(z):
    return jnp.minimum(z, 0.0) - jnp.log(1.0 + jnp.exp(-jnp.abs(z)))


def _rms(x, w):
    return x * lax.rsqrt(jnp.mean(x * x, axis=-1, keepdims=True) + EPS) * w


def _dot(a, b, precision=None):
    return jnp.dot(a, b, preferred_element_type=F32, precision=precision)


def _dot_nt(a, b):
    return lax.dot_general(a, b, (((1,), (1,)), ((), ())), preferred_element_type=F32)


def _params(sem):
    return pltpu.CompilerParams(dimension_semantics=sem, vmem_limit_bytes=VMEM_LIMIT)


def _const_spec(shape):
    nd = len(shape)
    return pl.BlockSpec(shape, lambda *_: (0,) * nd, pipeline_mode=pl.Buffered(1))


def _mod_rows(mod_ref, is_ctx):
    m = mod_ref[0]
    return jnp.where(is_ctx, m[0:1], m[1:2])


def _mod_kernel(c_ref, w_ref, b_ref, o_ref):
    o_ref[0] = _dot(_silu(c_ref[...]), w_ref[0], HI) + b_ref[0]


def _modulation(cvec, mod_w, mod_b):
    depth, d, n6 = mod_w.shape
    rows = cvec.shape[0]
    tn = MOD_COLS_TILE
    return pl.pallas_call(
        _mod_kernel,
        grid=(depth, n6 // tn),
        in_specs=[pl.BlockSpec((rows, d), lambda l, j: (0, 0)),
                  pl.BlockSpec((1, d, tn), lambda l, j: (l, 0, j)),
                  pl.BlockSpec((1, 1, tn), lambda l, j: (l, 0, j))],
        out_specs=pl.BlockSpec((1, rows, tn), lambda l, j: (l, 0, j)),
        out_shape=jax.ShapeDtypeStruct((depth, rows, n6), F32),
        compiler_params=_params(("arbitrary", "arbitrary")),
    )(cvec, mod_w, mod_b.reshape(depth, 1, n6))


_C_CQ = (0, 256)
_C_CKV = (256, 384)
_C_KRA = (384, 512)
_C_KRB = (512, 640)
_C_RQ = (640, 896)
_C_RQS = (896, 1152)
_C_RK = (1152, 1408)
_C_RKS = (1408, 1664)
_C_RV = (1664, 2176)
_C_RG = (2176, 2688)
_C_GQ = (2688, 2944)
_C_GK = (2944, 3200)
_C_GV = (3200, 3712)
_C_GR = (3712, 4224)
_C_GZ = (4224, 4352)
_W1_COLS = 4352


def _inproj_kernel(nct, x_ref, mod_ref, nw_ref, w_ref, qn_ref, wq_ref, wqs_ref, kvn_ref, wk_ref,
                   wv_ref, wg2_ref, bg2_ref, cq_ref, sq_ref, cr_ref, sr_ref,
                   q_out, k_out, v_out, rq_out, rk_out, rv_out, rg_out,
                   gq_out, gk_out, gv_out, gr_out, la_out):
    d = D_MODEL
    row = _mod_rows(mod_ref, pl.program_id(1) < nct)
    sh, sc = row[:, 0:d], row[:, d:2 * d]
    u = (_rms(x_ref[0], nw_ref[...]) * (1.0 + sc) + sh).astype(BF16)

    def seg(c):
        return _dot(u, w_ref[:, c[0]:c[1]])

    cqn = _rms(seg(_C_CQ), qn_ref[...]).astype(BF16)
    q_all = _dot(cqn, wq_ref[...])
    q_swp = _dot(cqn, wqs_ref[...])
    ckvn = _rms(seg(_C_CKV), kvn_ref[...]).astype(BF16)
    k_all = _dot(ckvn, wk_ref[...])
    v_out[0, 0] = _dot(ckvn, wv_ref[...]).T.astype(BF16)
    cosq, sinq = cq_ref[...], sq_ref[...]
    k_rope = seg(_C_KRA) * cosq + seg(_C_KRB) * sinq
    for h in range(MLA_HEADS):
        sl = slice(h * HEAD_SLOT, (h + 1) * HEAD_SLOT)
        q_out[0, h, 0] = (q_all[:, sl] * cosq + q_swp[:, sl] * sinq).T.astype(BF16)
        k_out[0, h] = (k_all[:, sl] + k_rope).astype(BF16)

    cosr, sinr = cr_ref[...], sr_ref[...]
    rq_out[0] = (seg(_C_RQ) * cosr + seg(_C_RQS) * sinr).astype(BF16)
    rk_out[0] = (seg(_C_RK) * cosr + seg(_C_RKS) * sinr).astype(BF16)
    rv_out[0] = seg(_C_RV).astype(BF16)
    rg_out[0] = seg(_C_RG).astype(BF16)

    gq_out[0] = seg(_C_GQ).astype(BF16)
    gk_out[0] = seg(_C_GK).astype(BF16)
    gv_out[0] = seg(_C_GV).astype(BF16)
    gr_out[0] = seg(_C_GR).astype(BF16)
    z = _dot(seg(_C_GZ), wg2_ref[...], HI) + bg2_ref[...]
    la_out[0] = _log_sigmoid(z) * (1.0 / GLA_TAU)


def _inproj(xs, modsel, nw, w1, qn, wq, wqs, kvn, wk, wv, wg2, bg2, cosq, sinq, cosr, sinr, nct):
    b, n, d = xs.shape
    tm = TOKEN_TILE
    nt = n // tm
    h = MLA_HEADS
    tok = lambda w: pl.BlockSpec((1, tm, w), lambda i, t: (i, t, 0))
    tab = lambda w: pl.BlockSpec((tm, w), lambda i, t: (t, 0))
    q_spec = pl.BlockSpec((1, h, 1, HEAD_SLOT, tm), lambda i, t: (i, 0, t, 0, 0))
    k_spec = pl.BlockSpec((1, h, tm, HEAD_SLOT), lambda i, t: (i, 0, t, 0))
    v_spec = pl.BlockSpec((1, 1, h * MLA_V, tm), lambda i, t: (i, t, 0, 0))
    sds = lambda w, dt=BF16: jax.ShapeDtypeStruct((b, n, w), dt)
    q_sds = jax.ShapeDtypeStruct((b, h, nt, HEAD_SLOT, tm), BF16)
    k_sds = jax.ShapeDtypeStruct((b, h, n, HEAD_SLOT), BF16)
    v_sds = jax.ShapeDtypeStruct((b, nt, h * MLA_V, tm), BF16)
    return pl.pallas_call(
        functools.partial(_inproj_kernel, nct),
        grid=(b, nt),
        in_specs=[tok(d), pl.BlockSpec((1, 2, 6 * d), lambda i, t: (i, 0, 0)), _const_spec(nw.shape),
                  _const_spec(w1.shape), _const_spec(qn.shape), _const_spec(wq.shape),
                  _const_spec(wqs.shape), _const_spec(kvn.shape), _const_spec(wk.shape),
                  _const_spec(wv.shape), _const_spec(wg2.shape), _const_spec(bg2.shape),
                  tab(HEAD_SLOT), tab(HEAD_SLOT), tab(256), tab(256)],
        out_specs=[q_spec, k_spec, v_spec, tok(256), tok(256), tok(512), tok(512),
                   tok(256), tok(256), tok(512), tok(512), tok(512)],
        out_shape=[q_sds, k_sds, v_sds, sds(256), sds(256), sds(512), sds(512),
                   sds(256), sds(256), sds(512), sds(512), sds(512, F32)],
        compiler_params=_params(("arbitrary", "arbitrary")),
    )(xs, modsel, nw, w1, qn, wq, wqs, kvn, wk, wv, wg2, bg2, cosq, sinq, cosr, sinr)


def _attn_kernel(nct, n_tiles, qt_ref, k_ref, vt_ref, o_ref, *bufs):
    t = TOKEN_TILE
    heads = range(2)
    s_bufs = (bufs[0:2], bufs[2:4])
    p_bufs = (bufs[4:6], bufs[6:8])
    n_chunks = jnp.where(pl.program_id(2) < nct, nct, n_tiles)
    last = n_chunks - 1
    qts = [qt_ref[0, j, 0] for j in heads]
    ones = jnp.ones((ATT_SUM_ROWS, t), BF16)

    def scores(j, c, s_buf):
        s = _dot(k_ref[0, j, pl.ds(pl.multiple_of(c * t, t), t), :], qts[j])
        s_buf[...] = s
        return jnp.max(s, axis=0, keepdims=True)

    def softmax(s_buf, p_buf, m, cmax):
        m_new = jnp.maximum(m, cmax)
        p_buf[...] = jnp.exp2(s_buf[...] - m_new).astype(BF16)
        return m_new, jnp.exp2(m - m_new)

    def values(j, c, p_buf, alpha, acc):
        lhs = jnp.concatenate([vt_ref[0, c, j * MLA_V:(j + 1) * MLA_V, :], ones], axis=0)
        return alpha * acc + _dot(lhs, p_buf[...])

    def step(c_prev, c_next, cur, state):
        accs = [values(j, c_prev, p_bufs[j][1 - cur], state[j][1], state[j][2]) for j in heads]
        stats = [softmax(s_bufs[j][cur], p_bufs[j][cur], state[j][0], state[j][3]) for j in heads]
        cmaxs = [scores(j, c_next, s_bufs[j][1 - cur]) for j in heads]
        return tuple((stats[j][0], stats[j][1], accs[j], cmaxs[j]) for j in heads)

    def pair(i, state):
        state = step(jnp.maximum(2 * i - 1, 0), 2 * i + 1, 0, state)
        return step(2 * i, 2 * i + 2, 1, state)

    init = []
    for j in heads:
        p_bufs[j][1][...] = jnp.zeros((t, t), BF16)
        init.append((jnp.full((1, t), -jnp.inf, F32), jnp.ones((1, t), F32),
                     jnp.zeros((MLA_V + ATT_SUM_ROWS, t), F32), scores(j, 0, s_bufs[j][0])))
    state = lax.fori_loop(0, last // 2, pair, tuple(init))
    outs = []
    for j in heads:
        m, alpha, acc, cmax = state[j]
        acc = values(j, jnp.maximum(last - 1, 0), p_bufs[j][1], alpha, acc)
        m, alpha = softmax(s_bufs[j][0], p_bufs[j][0], m, cmax)
        acc = values(j, last, p_bufs[j][0], alpha, acc)
        outs.append(acc[0:MLA_V] / acc[MLA_V:MLA_V + 1])
    o_ref[0] = jnp.concatenate(outs, axis=0).T.astype(BF16)


def _attention(qt, k, vt, nct):
    b, h, n_tiles, w, t = qt.shape
    n = n_tiles * t
    assert nct % 2 == 1 and n_tiles % 2 == 1
    return pl.pallas_call(
        functools.partial(_attn_kernel, nct, n_tiles),
        grid=(b, h // 2, n_tiles),
        in_specs=[pl.BlockSpec((1, 2, 1, w, t), lambda i, p, q: (i, p, q, 0, 0)),
                  pl.BlockSpec((1, 2, n, w), lambda i, p, q: (i, p, 0, 0)),
                  pl.BlockSpec((1, n_tiles, 2 * MLA_V, t), lambda i, p, q: (i, 0, p, 0))],
        out_specs=pl.BlockSpec((1, t, 2 * MLA_V), lambda i, p, q: (i, q, p)),
        out_shape=jax.ShapeDtypeStruct((b, n, h * MLA_V), BF16),
        scratch_shapes=[pltpu.VMEM((t, t), F32)] * 4 + [pltpu.VMEM((t, t), BF16)] * 4,
        compiler_params=_params(("arbitrary", "arbitrary", "arbitrary")),
    )(qt, k, vt)


def _bwd_chunk(s, nct, nch):
    return jnp.where(s < nct, nct - 1 - s, nch - 1 - (s - nct))


def _ret_kernel(dl_ref, qf, kf, vf, qb, kb, vb, of_ref, ob_ref, st_ref):
    c = RET_CHUNK
    hw = RET_HEADS * RET_DK

    @pl.when(pl.program_id(1) == 0)
    def _():
        st_ref[...] = jnp.zeros_like(st_ref)

    ii = lax.broadcasted_iota(jnp.int32, (c, c), 0)
    jj = lax.broadcasted_iota(jnp.int32, (c, c), 1)
    pos_i = lax.broadcasted_iota(jnp.int32, (c, 1), 0).astype(F32)
    pos_j = lax.broadcasted_iota(jnp.int32, (1, c), 1).astype(F32)
    lane_head = lax.broadcasted_iota(jnp.int32, (c, hw), 1) // RET_DK
    for d, (q_ref, k_ref, v_ref, o_ref) in enumerate(((qf, kf, vf, of_ref), (qb, kb, vb, ob_ref))):
        rev = d == 1
        q, v = q_ref[0], v_ref[0]
        k_t = k_ref[0].astype(F32).T
        k_tb = k_t.astype(BF16)
        st = st_ref[d]
        st_b = st.astype(BF16)
        dist = ((jj - ii) if rev else (ii - jj)).astype(F32)
        for h in range(RET_HEADS):
            lg = _log_sigmoid(dl_ref[d, h])[0:1, 0:1]
            intra = jnp.where(dist >= 0.0, jnp.exp(lg * jnp.maximum(dist, 0.0)), 0.0)
            q_dec = jnp.exp(lg * ((c - pos_i) if rev else (pos_i + 1.0)))
            k_dec = jnp.exp(lg * (pos_j if rev else (c - 1.0 - pos_j)))
            rows = slice(h * RET_DK, (h + 1) * RET_DK)
            cols = slice(h * RET_DV, (h + 1) * RET_DV)
            qm = jnp.where(lane_head == h, q, jnp.zeros_like(q))
            scores = _dot(qm, k_tb) * intra
            o = _dot(scores.astype(BF16), v[:, cols]) + _dot(qm, st_b) * q_dec
            o_ref[0, :, cols] = o
            kd = (k_t[rows] * k_dec).astype(BF16)
            st_ref[d, rows, :] = st[rows] * jnp.exp(lg * float(c)) + _dot(kd, v[:, cols])


def _retention(dl, rq, rk, rv, nct):
    b, n, hw = rq.shape
    c = RET_CHUNK
    nch = n // c
    vw = rv.shape[-1]
    fwd = lambda w: pl.BlockSpec((1, c, w), lambda i, s: (i, s, 0))
    bwd = lambda w: pl.BlockSpec((1, c, w), lambda i, s: (i, _bwd_chunk(s, nct, nch), 0))
    return pl.pallas_call(
        _ret_kernel,
        grid=(b, nch),
        in_specs=[_const_spec(dl.shape), fwd(hw), fwd(hw), fwd(vw), bwd(hw), bwd(hw), bwd(vw)],
        out_specs=[fwd(vw), bwd(vw)],
        out_shape=[jax.ShapeDtypeStruct((b, n, vw), F32)] * 2,
        scratch_shapes=[pltpu.VMEM((2, hw, RET_DV), F32)],
        compiler_params=_params(("arbitrary", "arbitrary")),
    )(dl, rq, rk, rv, rq, rk, rv)


def _block_start(i, size):
    return i & ~(size - 1)


def _gla_levels():
    half = GLA_CHUNK // 2
    out = []
    while half >= GLA_BASE:
        out.append(half)
        half //= 2
    return out


def _gla_kernel(qf, kf, vf, laf, qb, kb, vb, lab, of_ref, ob_ref, st_ref):
    c = GLA_CHUNK
    hw = GLA_HEADS * GLA_DK
    levels = _gla_levels()

    @pl.when(pl.program_id(1) == 0)
    def _():
        st_ref[...] = jnp.zeros_like(st_ref)

    ii = lax.broadcasted_iota(jnp.int32, (c, c), 0)
    jj = lax.broadcasted_iota(jnp.int32, (c, c), 1)
    pos = lax.broadcasted_iota(jnp.int32, (c, 1), 0)
    lane_head = lax.broadcasted_iota(jnp.int32, (c, hw), 1) // GLA_DK
    for d, (q_ref, k_ref, v_ref, la_ref, o_ref) in enumerate(((qf, kf, vf, laf, of_ref),
                                                               (qb, kb, vb, lab, ob_ref))):
        rev = d == 1
        q, k, v, la = q_ref[0].astype(F32), k_ref[0].astype(F32), v_ref[0], la_ref[0]

        b = _dot(((jj >= ii) if rev else (jj <= ii)).astype(F32), la, HI)
        b_end = b[0:1] if rev else b[c - 1:c]

        def at_row(size, idx):
            b3 = b.reshape(c // size, size, hw)
            return jnp.broadcast_to(b3[:, idx:idx + 1, :], b3.shape).reshape(c, hw)

        q_t = (q * jnp.exp(b)).astype(BF16)
        k_tt = (k * jnp.exp(b_end - b)).T
        dec_col = jnp.exp(jnp.broadcast_to(b_end, (LANES, hw)).T[:, 0:1])

        terms = []
        for half in levels:
            e = at_row(2 * half, half if rev else half - 1)
            inner = pos & (2 * half - 1)
            late = (inner < half) if rev else (inner >= half)
            qh = jnp.where(late, q * jnp.exp(jnp.where(late, b - e, 0.0)), 0.0)
            kh = jnp.where(late, 0.0, k * jnp.exp(jnp.where(late, 0.0, e - b)))
            terms.append((qh.astype(BF16), kh.astype(BF16), _block_start(ii, 2 * half) == _block_start(jj, 2 * half)))
        r = at_row(GLA_BASE, GLA_BASE - 1 if rev else 0)
        tri = (jj >= ii) if rev else (jj <= ii)
        terms.append(((q * jnp.exp(b - r)).astype(BF16), (k * jnp.exp(r - b)).astype(BF16),
                      (_block_start(ii, GLA_BASE) == _block_start(jj, GLA_BASE)) & tri))

        st = st_ref[d]
        st_b = st.astype(BF16)
        for h in range(GLA_HEADS):
            rows = slice(h * GLA_DK, (h + 1) * GLA_DK)
            cols = slice(h * GLA_DV, (h + 1) * GLA_DV)
            sel = lane_head == h
            a = jnp.zeros((c, c), F32)
            for qh, kh, mask in terms:
                a = a + jnp.where(mask, _dot_nt(jnp.where(sel, qh, jnp.zeros_like(qh)), kh), 0.0)
            vh = v[:, cols]
            o_ref[0, :, cols] = _dot(a.astype(BF16), vh) + _dot(jnp.where(sel, q_t, jnp.zeros_like(q_t)), st_b)
            st_ref[d, rows, :] = st[rows] * dec_col[rows] + _dot(k_tt[rows].astype(BF16), vh)


def _gla(gq, gk, gv, la, nct_tokens):
    b, n, hw = gq.shape
    c = GLA_CHUNK
    nch = n // c
    nct = nct_tokens // c
    vw = gv.shape[-1]
    fwd = lambda w, col=0: pl.BlockSpec((1, c, w), lambda i, s: (i, s, col))
    bwd = lambda w, col=0: pl.BlockSpec((1, c, w), lambda i, s: (i, _bwd_chunk(s, nct, nch), col))
    return pl.pallas_call(
        _gla_kernel,
        grid=(b, nch),
        in_specs=[fwd(hw), fwd(hw), fwd(vw), fwd(hw, 0), bwd(hw), bwd(hw), bwd(vw), bwd(hw, 1)],
        out_specs=[fwd(vw), bwd(vw)],
        out_shape=[jax.ShapeDtypeStruct((b, n, vw), F32)] * 2,
        scratch_shapes=[pltpu.VMEM((2, hw, GLA_DV), F32)],
        compiler_params=_params(("arbitrary", "arbitrary")),
    )(gq, gk, gv, la, gq, gk, gv, la)


def _head_norm(o, w, center):
    parts = []
    for h in range(o.shape[-1] // LANES):
        oh = o[:, h * LANES:(h + 1) * LANES]
        if center:
            oh = oh - jnp.mean(oh, axis=-1, keepdims=True)
        parts.append(oh * lax.rsqrt(jnp.mean(oh * oh, axis=-1, keepdims=True) + EPS))
    return jnp.concatenate(parts, axis=-1) * w


def _merge_kernel(nct, off, x_ref, mod_ref, nw_ref, oa_ref, rof_ref, rob_ref, rg_ref, gof_ref, gob_ref,
                  gr_ref, wbg_ref, wbr_ref, wout_ref, rnw_ref, gnw_ref, o_ref):
    d = D_MODEL
    row = _mod_rows(mod_ref, pl.program_id(1) + off < nct)
    sh, sc, g1 = row[:, 0:d], row[:, d:2 * d], row[:, 2 * d:3 * d]
    x = x_ref[0]
    u = (_rms(x, nw_ref[...]) * (1.0 + sc) + sh).astype(BF16)
    gate = _sigmoid(_dot(u, wbg_ref[...]))
    yb = _silu(rg_ref[0].astype(F32)) * _head_norm(rof_ref[0] + rob_ref[0], rnw_ref[...], True)
    yc = _silu(gr_ref[0].astype(F32)) * _head_norm(gof_ref[0] + gob_ref[0], gnw_ref[...], False)
    z = (gate[:, 0:d] * _dot(oa_ref[0], wbr_ref[0])
         + gate[:, d:2 * d] * _dot(yb.astype(BF16), wbr_ref[1])
         + gate[:, 2 * d:3 * d] * _dot(yc.astype(BF16), wbr_ref[2]))
    o_ref[0] = x + g1 * _dot(z.astype(BF16), wout_ref[...])


def _merge(xs, modsel, nw, oa, rof, rob, rg, gof, gob, gr, wbg, wbr, wout, rnw, gnw, nct, latent_only):
    b, n, d = xs.shape
    tm = TOKEN_TILE
    off = nct if latent_only else 0
    nt = n // tm - off
    tok = lambda w: pl.BlockSpec((1, tm, w), lambda i, t: (i, t + off, 0))
    return pl.pallas_call(
        functools.partial(_merge_kernel, nct, off),
        grid=(b, nt),
        in_specs=[tok(d), pl.BlockSpec((1, 2, 6 * d), lambda i, t: (i, 0, 0)), _const_spec(nw.shape),
                  tok(512), tok(512), tok(512), tok(512), tok(512), tok(512), tok(512),
                  _const_spec(wbg.shape), _const_spec(wbr.shape), _const_spec(wout.shape),
                  _const_spec(rnw.shape), _const_spec(gnw.shape)],
        out_specs=pl.BlockSpec((1, tm, d), lambda i, t: (i, t, 0)),
        out_shape=jax.ShapeDtypeStruct((b, nt * tm, d), F32),
        compiler_params=_params(("arbitrary", "arbitrary")),
    )(xs, modsel, nw, oa, rof, rob, rg, gof, gob, gr, wbg, wbr, wout, rnw, gnw)


def _norm2(x, row, nw_ref):
    d = D_MODEL
    sh, sc = row[:, 3 * d:4 * d], row[:, 4 * d:5 * d]
    return _rms(x, nw_ref[...]) * (1.0 + sc) + sh


def _finish(x, y, row, final, fnw_ref):
    d = D_MODEL
    out = x + row[:, 5 * d:6 * d] * y
    if final:
        out = _rms(out, fnw_ref[...])
    return out


def _ffn_kernel(nct, off, final, x_ref, mod_ref, nw_ref, wg_ref, wu_ref, wd_ref, fnw_ref, o_ref):
    row = _mod_rows(mod_ref, pl.program_id(1) + off < nct)
    x = x_ref[0]
    v = _norm2(x, row, nw_ref).astype(BF16)
    hdn = _silu(_dot(v, wg_ref[...])) * _dot(v, wu_ref[...])
    o_ref[0] = _finish(x, _dot(hdn.astype(BF16), wd_ref[...]), row, final, fnw_ref)


def _ffn(xs, modsel, nw, wg, wu, wd, fnw, nct, final):
    b, n, d = xs.shape
    tm = TOKEN_TILE
    off = nct if final else 0
    nt = n // tm - off
    return pl.pallas_call(
        functools.partial(_ffn_kernel, nct, off, final),
        grid=(b, nt),
        in_specs=[pl.BlockSpec((1, tm, d), lambda i, t: (i, t + off, 0)),
                  pl.BlockSpec((1, 2, 6 * d), lambda i, t: (i, 0, 0)), _const_spec(nw.shape),
                  _const_spec(wg.shape), _const_spec(wu.shape), _const_spec(wd.shape),
                  _const_spec(fnw.shape)],
        out_specs=pl.BlockSpec((1, tm, d), lambda i, t: (i, t, 0)),
        out_shape=jax.ShapeDtypeStruct((b, nt * tm, d), F32),
        compiler_params=_params(("arbitrary", "arbitrary")),
    )(xs, modsel, nw, wg, wu, wd, fnw)


def _top2_gates(logits):
    lane = lax.broadcasted_iota(jnp.int32, logits.shape, 1)
    m1 = jnp.max(logits, axis=-1, keepdims=True)
    i1 = jnp.min(jnp.where(logits == m1, lane, LANES), axis=-1, keepdims=True)
    rest = jnp.where(lane == i1, -jnp.inf, logits)
    m2 = jnp.max(rest, axis=-1, keepdims=True)
    i2 = jnp.min(jnp.where(rest == m2, lane, LANES), axis=-1, keepdims=True)
    e2 = jnp.exp(m2 - m1)
    w1 = 1.0 / (1.0 + e2)
    gates = jnp.where(lane == i1, w1, jnp.where(lane == i2, e2 * w1, 0.0))
    return gates, (lane == i1) | (lane == i2)


def _one_hot(cond):
    return jnp.where(cond, 1.0, 0.0).astype(BF16)


def _route_kernel(mrow, x_ref, mod_ref, nw_ref, rt_ref, v_out, gate_out, rank_out, rankt_out):
    v = _norm2(x_ref[0], mod_ref[0][mrow:mrow + 1], nw_ref)
    v_out[0] = v.astype(BF16)
    tt = v.shape[0]
    lane = lax.broadcasted_iota(jnp.int32, (tt, LANES), 1)
    logits = jnp.where(lane < N_EXPERTS, _dot(v, rt_ref[...], HI), -jnp.inf)
    gates, routed = _top2_gates(logits)
    gate_out[0] = gates
    ti = lax.broadcasted_iota(jnp.int32, (tt, tt), 0)
    tj = lax.broadcasted_iota(jnp.int32, (tt, tt), 1)
    rank_out[0] = jnp.where(routed, _dot(_one_hot(tj < ti), _one_hot(routed)), -1.0)
    routed_t = jnp.where(routed, 1.0, 0.0).T
    rank_t = jnp.where(routed_t > 0.5, _dot(routed_t.astype(BF16), _one_hot(ti < tj)), -1.0)
    rankt_out[0, 0] = rank_t[0:N_EXPERTS]


def _experts_kernel(mrow, final, v_ref, gate_ref, rank_ref, rankt_ref, wg_ref, wu_ref, wd_ref,
                    x_ref, mod_ref, fnw_ref, o_ref, acc_ref):
    e, s = pl.program_id(1), pl.program_id(2)
    tt = v_ref.shape[1]
    r = MOE_BLOCK
    fh = D_FF // 2
    tile0 = pl.multiple_of(s * tt, tt)
    lane = lax.broadcasted_iota(jnp.int32, (tt, LANES), 1)
    g_col = jnp.sum(jnp.where(lane == e, gate_ref[0], 0.0), axis=-1, keepdims=True)
    rk_col = jnp.sum(jnp.where(lane == e, rank_ref[0], 0.0), axis=-1, keepdims=True)
    rk_row = rankt_ref[0, 0, pl.ds(e, 1), :]
    n_routed = jnp.max(rk_col).astype(jnp.int32) + 1

    @pl.when(e == 0)
    def _():
        acc_ref[pl.ds(tile0, tt), :] = jnp.zeros((tt, D_MODEL), BF16)

    def block(i, carry):
        base = (i * r).astype(F32)
        slot_i = base + lax.broadcasted_iota(jnp.int32, (r, 1), 0).astype(F32)
        xg = _dot(_one_hot(rk_row == slot_i), v_ref[0]).astype(BF16)
        y = jnp.zeros((r, D_MODEL), F32)
        for c in range(2):
            cols = slice(c * fh, (c + 1) * fh)
            hdn = _silu(_dot(xg, wg_ref[0, :, cols])) * _dot(xg, wu_ref[0, :, cols])
            y = y + _dot(hdn.astype(BF16), wd_ref[0, cols, :])
        y = y.astype(BF16)
        slot_j = base + lax.broadcasted_iota(jnp.int32, (1, r), 1).astype(F32)
        for c in range(tt // TOKEN_TILE):
            rows = slice(c * TOKEN_TILE, (c + 1) * TOKEN_TILE)
            dst = pl.ds(tile0 + c * TOKEN_TILE, TOKEN_TILE)
            part = g_col[rows] * _dot(_one_hot(rk_col[rows] == slot_j), y)
            acc_ref[dst, :] = (acc_ref[dst, :].astype(F32) + part).astype(BF16)
        return carry

    lax.fori_loop(0, (n_routed + r - 1) // r, block, 0)

    @pl.when(e == N_EXPERTS - 1)
    def _():
        y_sum = acc_ref[pl.ds(tile0, tt), :].astype(F32)
        o_ref[0] = _finish(x_ref[0], y_sum, mod_ref[0][mrow:mrow + 1], final, fnw_ref)


def _moe(slab, mrow, modsel, nw, rt, wg, wu, wd, fnw, final):
    b, m, d = slab.shape
    tt = min(MOE_TILE, m)
    assert m % tt == 0 and tt % TOKEN_TILE == 0
    ns = m // tt
    ne, _, f = wg.shape
    mod_spec = lambda nd: pl.BlockSpec((1, 2, 6 * d), (lambda i, s: (i, 0, 0)) if nd == 2 else
                                       (lambda i, e, s: (i, 0, 0)))
    v, gates, rank, rank_t = pl.pallas_call(
        functools.partial(_route_kernel, mrow),
        grid=(b, ns),
        in_specs=[pl.BlockSpec((1, tt, d), lambda i, s: (i, s, 0)), mod_spec(2), _const_spec(nw.shape),
                  _const_spec(rt.shape)],
        out_specs=[pl.BlockSpec((1, tt, d), lambda i, s: (i, s, 0)),
                   pl.BlockSpec((1, tt, LANES), lambda i, s: (i, s, 0)),
                   pl.BlockSpec((1, tt, LANES), lambda i, s: (i, s, 0)),
                   pl.BlockSpec((1, 1, ne, tt), lambda i, s: (i, s, 0, 0))],
        out_shape=[jax.ShapeDtypeStruct((b, m, d), BF16), jax.ShapeDtypeStruct((b, m, LANES), F32),
                   jax.ShapeDtypeStruct((b, m, LANES), F32), jax.ShapeDtypeStruct((b, ns, ne, tt), F32)],
        compiler_params=_params(("arbitrary", "arbitrary")),
    )(slab, modsel, nw, rt)

    tok = lambda w: pl.BlockSpec((1, tt, w), lambda i, e, s: (i, s, 0))
    wspec = lambda shape: pl.BlockSpec((1,) + shape, lambda i, e, s: (e, 0, 0), pipeline_mode=pl.Buffered(1))
    last_pass = lambda i, e, s: (i, jnp.where(e == ne - 1, s, 0), 0)
    return pl.pallas_call(
        functools.partial(_experts_kernel, mrow, final),
        grid=(b, ne, ns),
        in_specs=[tok(d), tok(LANES), tok(LANES),
                  pl.BlockSpec((1, 1, ne, tt), lambda i, e, s: (i, s, 0, 0)),
                  wspec((d, f)), wspec((d, f)), wspec((f, d)),
                  pl.BlockSpec((1, tt, d), last_pass, pipeline_mode=pl.Buffered(1)),
                  mod_spec(3), _const_spec(fnw.shape)],
        out_specs=pl.BlockSpec((1, tt, d), last_pass),
        out_shape=jax.ShapeDtypeStruct((b, m, d), F32),
        scratch_shapes=[pltpu.VMEM((m, d), BF16)],
        compiler_params=_params(("arbitrary", "arbitrary", "arbitrary")),
    )(v, gates, rank, rank_t, wg, wu, wd, slab, modsel, fnw)


def _axial_perm():
    dd = np.arange(MLA_ROPE)
    return np.where(dd % 16 < 8, dd + 8, dd - 8)


def _ret_perm():
    dd = np.arange(RET_HEADS * RET_DK)
    return np.where(dd % RET_DK < RET_DK // 2, dd + RET_DK // 2, dd - RET_DK // 2)


def _prep_w_in(w):
    offs = np.cumsum((0,) + IN_SPLITS)
    cq, ckv, kr, rq, rk, rv, rg, gq, gk, gv, gr, gz, bg = [w[:, offs[i]:offs[i + 1]] for i in range(13)]
    zeros = lambda n: jnp.zeros((w.shape[0], n), w.dtype)
    rk = rk * (RET_DK ** -0.5)
    gq = gq * (GLA_DK ** -0.5)
    w1 = jnp.concatenate([
        cq, ckv,
        zeros(MLA_NOPE), kr, zeros(HEAD_SLOT - MLA_NOPE - MLA_ROPE),
        zeros(MLA_NOPE), kr[:, _axial_perm()], zeros(HEAD_SLOT - MLA_NOPE - MLA_ROPE),
        rq, rq[:, _ret_perm()], rk, rk[:, _ret_perm()], rv, rg,
        gq, gk, gv, gr, gz, zeros(LANES - 2 * GLA_GATE_RANK)], axis=1)
    return w1.astype(BF16), bg.astype(BF16)


def _prep_mla(w_uq, w_ukv):
    r = w_uq.shape[0]
    scale = (MLA_NOPE + MLA_ROPE) ** -0.5 * np.log2(np.e)
    wq3 = w_uq.reshape(r, MLA_HEADS, MLA_NOPE + MLA_ROPE) * scale
    pad = jnp.zeros((r, MLA_HEADS, HEAD_SLOT - MLA_NOPE - MLA_ROPE), w_uq.dtype)
    wq = jnp.concatenate([wq3, pad], axis=-1)
    wqs = jnp.concatenate([jnp.zeros((r, MLA_HEADS, MLA_NOPE), w_uq.dtype),
                           wq3[:, :, MLA_NOPE:][:, :, _axial_perm()], pad], axis=-1)
    rk = w_ukv.shape[0]
    wkv3 = w_ukv.reshape(rk, MLA_HEADS, MLA_NOPE + MLA_V)
    wk = jnp.concatenate([wkv3[:, :, :MLA_NOPE],
                          jnp.zeros((rk, MLA_HEADS, HEAD_SLOT - MLA_NOPE), w_ukv.dtype)], axis=-1)
    wv = wkv3[:, :, MLA_NOPE:]
    flat = lambda a: a.reshape(a.shape[0], -1).astype(BF16)
    return flat(wq), flat(wqs), flat(wk), flat(wv)


def _prep_gla_gate(w_gate, b_gate):
    hw = GLA_HEADS * GLA_DK
    wg2 = jnp.zeros((LANES, 2 * hw), F32)
    wg2 = wg2.at[0:GLA_GATE_RANK, 0:hw].set(w_gate[0])
    wg2 = wg2.at[GLA_GATE_RANK:2 * GLA_GATE_RANK, hw:2 * hw].set(w_gate[1])
    return wg2, b_gate.reshape(1, 2 * hw)


def _rope_tables(n_ctx, seq):
    t = jnp.arange(seq, dtype=jnp.int32)
    half = MLA_ROPE // 4
    inv = ROPE_BASE ** (-jnp.arange(half, dtype=F32) / half)
    lane = np.arange(HEAD_SLOT)
    dd = lane - MLA_NOPE
    active = (dd >= 0) & (dd < MLA_ROPE)
    dd = np.clip(dd, 0, MLA_ROPE - 1)
    by_col = (dd // 16) == 1
    first = (dd % 16) < half
    pos = jnp.where(by_col[None, :], (t % GRID_W)[:, None], (t // GRID_W)[:, None]).astype(F32)
    ang = pos * inv[dd % half][None, :]
    cosq = jnp.where(active[None, :], jnp.cos(ang), 1.0)
    sinq = jnp.where(active[None, :], jnp.where(first[None, :], -jnp.sin(ang), jnp.sin(ang)), 0.0)
    rhalf = RET_DK // 2
    rinv = ROPE_BASE ** (-jnp.arange(rhalf, dtype=F32) / rhalf)
    rl = np.arange(RET_HEADS * RET_DK) % RET_DK
    rang = t.astype(F32)[:, None] * rinv[rl % rhalf][None, :]
    cosr = jnp.cos(rang)
    sinr = jnp.where((rl < rhalf)[None, :], -jnp.sin(rang), jnp.sin(rang))
    ctx = lambda a, fill: jnp.concatenate([jnp.full((n_ctx, a.shape[1]), fill, F32), a], axis=0)
    return ctx(cosq, 1.0), ctx(sinq, 0.0), ctx(cosr, 1.0), ctx(sinr, 0.0)


def kernel(x, c, ctx, c_ctx, mod_w, mod_b, norm1_w, norm2_w, w_in, mla_q_norm, mla_w_uq, mla_kv_norm, mla_w_ukv, ret_decay_logit, ret_norm_w, gla_w_gate, gla_b_gate, gla_norm_w, w_branch, w_out, ffn_w_gate, ffn_w_up, ffn_w_down, moe_router, moe_w_gate, moe_w_up, moe_w_down, final_norm_w):
    b, seq, d = x.shape
    n_ctx = ctx.shape[1]
    depth = mod_w.shape[0]
    assert d == D_MODEL and seq % GRID_W == 0
    assert n_ctx % TOKEN_TILE == 0 and n_ctx % RET_CHUNK == 0 and n_ctx % GLA_CHUNK == 0
    assert seq % TOKEN_TILE == 0 and seq % RET_CHUNK == 0
    nct = n_ctx // TOKEN_TILE

    rows = 8 * ((b + 1 + 7) // 8)
    cvec = jnp.zeros((rows, d), F32).at[0:b].set(c).at[b].set(c_ctx)
    mod = _modulation(cvec, mod_w, mod_b)
    cosq, sinq, cosr, sinr = _rope_tables(n_ctx, seq)
    fnw = final_norm_w.reshape(1, d)
    row2 = lambda a: a.reshape(1, -1)

    xs = jnp.concatenate([ctx, x], axis=1)
    for l in range(depth):
        last = l == depth - 1
        modsel = jnp.stack([jnp.broadcast_to(mod[l, b], (b, 6 * d)), mod[l, 0:b]], axis=1)
        w1, wbg = _prep_w_in(w_in[l])
        wq, wqs, wk, wv = _prep_mla(mla_w_uq[l], mla_w_ukv[l])
        wg2, bg2 = _prep_gla_gate(gla_w_gate[l], gla_b_gate[l])
        nw1, nw2 = row2(norm1_w[l]), row2(norm2_w[l])

        (q, k, v, rq, rk, rv, rg, gq, gk, gv, gr, la) = _inproj(
            xs, modsel, nw1, w1, row2(mla_q_norm[l]), wq, wqs, row2(mla_kv_norm[l]), wk, wv,
            wg2, bg2, cosq, sinq, cosr, sinr, nct)
        oa = _attention(q, k, v, nct)
        dl = jnp.broadcast_to(ret_decay_logit[l][:, :, None, None], (2, RET_HEADS, 8, LANES))
        rof, rob = _retention(dl, rq, rk, rv, n_ctx // RET_CHUNK)
        gof, gob = _gla(gq, gk, gv, la, n_ctx)
        is_moe = l % 2 == 1
        xs = _merge(xs, modsel, nw1, oa, rof, rob, rg, gof, gob, gr, wbg,
                    w_branch[l].astype(BF16), w_out[l].astype(BF16),
                    row2(ret_norm_w[l]), row2(gla_norm_w[l]), nct, is_moe and last)
        i = l // 2
        if not is_moe:
            xs = _ffn(xs, modsel, nw2, ffn_w_gate[i].astype(BF16), ffn_w_up[i].astype(BF16),
                      ffn_w_down[i].astype(BF16), fnw, nct, last)
            continue
        rt = jnp.zeros((d, LANES), F32).at[:, 0:N_EXPERTS].set(moe_router[i])
        experts = (moe_w_gate[i].astype(BF16), moe_w_up[i].astype(BF16), moe_w_down[i].astype(BF16))
        if last:
            xs = _moe(xs, 1, modsel, nw2, rt, *experts, fnw, True)
        else:
            xs = jnp.concatenate([_moe(xs[:, :n_ctx], 0, modsel, nw2, rt, *experts, fnw, False),
                                  _moe(xs[:, n_ctx:], 1, modsel, nw2, rt, *experts, fnw, False)], axis=1)
    return xs
```

```python
import functools

import numpy as np
import jax
import jax.numpy as jnp
from jax import lax
from jax.experimental import pallas as pl
from jax.experimental.pallas import tpu as pltpu

D_MODEL = 1024
GRID_W = 64
EPS = 1e-6
ROPE_BASE = 10000.0

MLA_HEADS = 8
MLA_NOPE = 64
MLA_ROPE = 32
MLA_V = 64
MLA_Q_RANK = 256
MLA_KV_RANK = 128

RET_HEADS = 4
RET_DK = 64
RET_DV = 128

GLA_HEADS = 4
GLA_DK = 64
GLA_DV = 128
GLA_GATE_RANK = 16
GLA_TAU = 16.0

N_BRANCH = 3
BRANCH_W = 512
D_FF = 2816
N_EXPERTS = 8

IN_SPLITS = (
    MLA_Q_RANK, MLA_KV_RANK, MLA_ROPE,
    RET_HEADS * RET_DK, RET_HEADS * RET_DK, RET_HEADS * RET_DV, RET_HEADS * RET_DV,
    GLA_HEADS * GLA_DK, GLA_HEADS * GLA_DK, GLA_HEADS * GLA_DV, GLA_HEADS * GLA_DV,
    2 * GLA_GATE_RANK,
    N_BRANCH * D_MODEL,
)

F32 = jnp.float32
BF16 = jnp.bfloat16
HI = lax.Precision.HIGHEST

LANES = 128
HEAD_SLOT = 128
TOKEN_TILE = 256
ATT_DEPTH = 3
ATT_UNROLL = 10
ATT_SUM_ROWS = 16
RET_CHUNK = 256
GLA_CHUNK = 128
GLA_BASE = 16
MOE_TILE = 512
MOE_BLOCK = 160
MOD_COLS_TILE = 1536
VMEM_LIMIT = 56 * 1024 * 1024


def _silu(x):
    return x / (1.0 + jnp.exp(-x))


def _sigmoid(x):
    return 1.0 / (1.0 + jnp.exp(-x))


def _log_sigmoid(z):
    return jnp.minimum(z, 0.0) - jnp.log(1.0 + jnp.exp(-jnp.abs(z)))


def _rms(x, w):
    return x * lax.rsqrt(jnp.mean(x * x, axis=-1, keepdims=True) + EPS) * w


def _dot(a, b, precision=None):
    return jnp.dot(a, b, preferred_element_type=F32, precision=precision)


def _dot_nt(a, b):
    return lax.dot_general(a, b, (((1,), (1,)), ((), ())), preferred_element_type=F32)


def _params(sem):
    return pltpu.CompilerParams(dimension_semantics=sem, vmem_limit_bytes=VMEM_LIMIT)


def _const_spec(shape):
    nd = len(shape)
    return pl.BlockSpec(shape, lambda *_: (0,) * nd, pipeline_mode=pl.Buffered(1))


def _mod_rows(mod_ref, is_ctx):
    m = mod_ref[0]
    return jnp.where(is_ctx, m[0:1], m[1:2])


def _mod_kernel(c_ref, w_ref, b_ref, o_ref):
    o_ref[0] = _dot(_silu(c_ref[...]), w_ref[0], HI) + b_ref[0]


def _modulation(cvec, mod_w, mod_b):
    depth, d, n6 = mod_w.shape
    rows = cvec.shape[0]
    tn = MOD_COLS_TILE
    return pl.pallas_call(
        _mod_kernel,
        grid=(depth, n6 // tn),
        in_specs=[pl.BlockSpec((rows, d), lambda l, j: (0, 0)),
                  pl.BlockSpec((1, d, tn), lambda l, j: (l, 0, j)),
                  pl.BlockSpec((1, 1, tn), lambda l, j: (l, 0, j))],
        out_specs=pl.BlockSpec((1, rows, tn), lambda l, j: (l, 0, j)),
        out_shape=jax.ShapeDtypeStruct((depth, rows, n6), F32),
        compiler_params=_params(("arbitrary", "arbitrary")),
    )(cvec, mod_w, mod_b.reshape(depth, 1, n6))


_C_CQ = (0, 256)
_C_CKV = (256, 384)
_C_KRA = (384, 512)
_C_KRB = (512, 640)
_C_RQ = (640, 896)
_C_RQS = (896, 1152)
_C_RK = (1152, 1408)
_C_RKS = (1408, 1664)
_C_RV = (1664, 2176)
_C_RG = (2176, 2688)
_C_GQ = (2688, 2944)
_C_GK = (2944, 3200)
_C_GV = (3200, 3712)
_C_GR = (3712, 4224)
_C_GZ = (4224, 4352)
_W1_COLS = 4352


def _inproj_kernel(nct, x_ref, mod_ref, nw_ref, w_ref, qn_ref, wq_ref, wqs_ref, kvn_ref, wk_ref,
                   wv_ref, wg2_ref, bg2_ref, cq_ref, sq_ref, cr_ref, sr_ref,
                   q_out, k_out, v_out, rq_out, rk_out, rv_out, rg_out,
                   gq_out, gk_out, gv_out, gr_out, la_out):
    d = D_MODEL
    row = _mod_rows(mod_ref, pl.program_id(1) < nct)
    sh, sc = row[:, 0:d], row[:, d:2 * d]
    u = (_rms(x_ref[0], nw_ref[...]) * (1.0 + sc) + sh).astype(BF16)

    def seg(c):
        return _dot(u, w_ref[:, c[0]:c[1]])

    cqn = _rms(seg(_C_CQ), qn_ref[...]).astype(BF16)
    q_all = _dot(cqn, wq_ref[...])
    q_swp = _dot(cqn, wqs_ref[...])
    ckvn = _rms(seg(_C_CKV), kvn_ref[...]).astype(BF16)
    k_all = _dot(ckvn, wk_ref[...])
    v_out[0, 0] = _dot(ckvn, wv_ref[...]).T.astype(BF16)
    cosq, sinq = cq_ref[...], sq_ref[...]
    k_rope = seg(_C_KRA) * cosq + seg(_C_KRB) * sinq
    for h in range(MLA_HEADS):
        sl = slice(h * HEAD_SLOT, (h + 1) * HEAD_SLOT)
        q_out[0, h, 0] = (q_all[:, sl] * cosq + q_swp[:, sl] * sinq).T.astype(BF16)
        k_out[0, h] = (k_all[:, sl] + k_rope).astype(BF16)

    cosr, sinr = cr_ref[...], sr_ref[...]
    rq_out[0] = (seg(_C_RQ) * cosr + seg(_C_RQS) * sinr).astype(BF16)
    rk_out[0] = (seg(_C_RK) * cosr + seg(_C_RKS) * sinr).astype(BF16)
    rv_out[0] = seg(_C_RV).astype(BF16)
    rg_out[0] = seg(_C_RG).astype(BF16)

    gq_out[0] = seg(_C_GQ).astype(BF16)
    gk_out[0] = seg(_C_GK).astype(BF16)
    gv_out[0] = seg(_C_GV).astype(BF16)
    gr_out[0] = seg(_C_GR).astype(BF16)
    z = _dot(seg(_C_GZ), wg2_ref[...], HI) + bg2_ref[...]
    la_out[0] = _log_sigmoid(z) * (1.0 / GLA_TAU)


def _inproj(xs, modsel, nw, w1, qn, wq, wqs, kvn, wk, wv, wg2, bg2, cosq, sinq, cosr, sinr, nct):
    b, n, d = xs.shape
    tm = TOKEN_TILE
    nt = n // tm
    h = MLA_HEADS
    tok = lambda w: pl.BlockSpec((1, tm, w), lambda i, t: (i, t, 0))
    tab = lambda w: pl.BlockSpec((tm, w), lambda i, t: (t, 0))
    q_spec = pl.BlockSpec((1, h, 1, HEAD_SLOT, tm), lambda i, t: (i, 0, t, 0, 0))
    k_spec = pl.BlockSpec((1, h, tm, HEAD_SLOT), lambda i, t: (i, 0, t, 0))
    v_spec = pl.BlockSpec((1, 1, h * MLA_V, tm), lambda i, t: (i, t, 0, 0))
    sds = lambda w, dt=BF16: jax.ShapeDtypeStruct((b, n, w), dt)
    q_sds = jax.ShapeDtypeStruct((b, h, nt, HEAD_SLOT, tm), BF16)
    k_sds = jax.ShapeDtypeStruct((b, h, n, HEAD_SLOT), BF16)
    v_sds = jax.ShapeDtypeStruct((b, nt, h * MLA_V, tm), BF16)
    return pl.pallas_call(
        functools.partial(_inproj_kernel, nct),
        grid=(b, nt),
        in_specs=[tok(d), pl.BlockSpec((1, 2, 6 * d), lambda i, t: (i, 0, 0)), _const_spec(nw.shape),
                  _const_spec(w1.shape), _const_spec(qn.shape), _const_spec(wq.shape),
                  _const_spec(wqs.shape), _const_spec(kvn.shape), _const_spec(wk.shape),
                  _const_spec(wv.shape), _const_spec(wg2.shape), _const_spec(bg2.shape),
                  tab(HEAD_SLOT), tab(HEAD_SLOT), tab(256), tab(256)],
        out_specs=[q_spec, k_spec, v_spec, tok(256), tok(256), tok(512), tok(512),
                   tok(256), tok(256), tok(512), tok(512), tok(512)],
        out_shape=[q_sds, k_sds, v_sds, sds(256), sds(256), sds(512), sds(512),
                   sds(256), sds(256), sds(512), sds(512), sds(512, F32)],
        compiler_params=_params(("arbitrary", "arbitrary")),
    )(xs, modsel, nw, w1, qn, wq, wqs, kvn, wk, wv, wg2, bg2, cosq, sinq, cosr, sinr)


def _attn_kernel(nct, n_tiles, qt_ref, k_ref, vt_ref, o_ref, *bufs):
    t = TOKEN_TILE
    nd = ATT_DEPTH
    heads = range(2)
    s_bufs = (bufs[0:nd], bufs[nd:2 * nd])
    p_bufs = (bufs[2 * nd:3 * nd], bufs[3 * nd:4 * nd])
    qts = [qt_ref[0, j, 0] for j in heads]
    ones = jnp.ones((ATT_SUM_ROWS, t), BF16)

    def scores(j, c, s_buf):
        s = _dot(k_ref[0, j, pl.ds(pl.multiple_of(c * t, t), t), :], qts[j])
        s_buf[...] = s
        return jnp.max(s, axis=0, keepdims=True)

    def softmax(s_buf, p_buf, m, cmax):
        m_new = jnp.maximum(m, cmax)
        p_buf[...] = jnp.exp2(s_buf[...] - m_new).astype(BF16)
        return m_new, jnp.exp2(m - m_new)

    def values(j, c, p_buf, alpha, acc):
        lhs = jnp.concatenate([vt_ref[0, c, j * MLA_V:(j + 1) * MLA_V, :], ones], axis=0)
        return alpha * acc + _dot(lhs, p_buf[...])

    m_init = jnp.full((1, t), -jnp.inf, F32)
    acc_init = jnp.zeros((MLA_V + ATT_SUM_ROWS, t), F32)

    def finish(accs):
        outs = [acc[0:MLA_V] / acc[MLA_V:MLA_V + 1] for acc in accs]
        o_ref[0] = jnp.concatenate(outs, axis=0).T.astype(BF16)

    @pl.when(pl.program_id(2) < nct)
    def _():
        accs = []
        for j in heads:
            m, acc = m_init, acc_init
            for c in range(nct):
                cmax = scores(j, c, s_bufs[j][0])
                m, alpha = softmax(s_bufs[j][0], p_bufs[j][0], m, cmax)
                acc = values(j, c, p_bufs[j][0], alpha, acc)
            accs.append(acc)
        finish(accs)

    def step(c, slot, state, lookahead=True):
        ahead = [scores(j, c + 2, s_bufs[j][(slot + 2) % nd]) if lookahead else None for j in heads]
        accs = [values(j, c - 1, p_bufs[j][(slot - 1) % nd], state[j][1], state[j][2]) for j in heads]
        stats = [softmax(s_bufs[j][slot], p_bufs[j][slot], state[j][0], state[j][3]) for j in heads]
        return tuple((stats[j][0], stats[j][1], accs[j], state[j][4], ahead[j]) for j in heads)

    @pl.when(pl.program_id(2) >= nct)
    def _():
        state = []
        for j in heads:
            cmax0 = scores(j, 0, s_bufs[j][0])
            cmax1 = scores(j, 1, s_bufs[j][1])
            m, alpha = softmax(s_bufs[j][0], p_bufs[j][0], m_init, cmax0)
            state.append((m, alpha, acc_init, cmax1, scores(j, 2, s_bufs[j][2])))

        def trip(i, state):
            for k in range(nd):
                state = step(1 + nd * i + k, (1 + k) % nd, state)
            return state

        state = lax.fori_loop(0, (n_tiles - 3) // nd, trip, tuple(state), unroll=ATT_UNROLL)
        for c in (n_tiles - 2, n_tiles - 1):
            state = step(c, c % nd, state, lookahead=False)
        finish([values(j, n_tiles - 1, p_bufs[j][(n_tiles - 1) % nd], state[j][1], state[j][2])
                for j in heads])


def _attention(qt, k, vt, nct):
    b, h, n_tiles, w, t = qt.shape
    n = n_tiles * t
    assert n_tiles >= 3 and (n_tiles - 3) % ATT_DEPTH == 0
    return pl.pallas_call(
        functools.partial(_attn_kernel, nct, n_tiles),
        grid=(b, h // 2, n_tiles),
        in_specs=[pl.BlockSpec((1, 2, 1, w, t), lambda i, p, q: (i, p, q, 0, 0)),
                  pl.BlockSpec((1, 2, n, w), lambda i, p, q: (i, p, 0, 0)),
                  pl.BlockSpec((1, n_tiles, 2 * MLA_V, t), lambda i, p, q: (i, 0, p, 0))],
        out_specs=pl.BlockSpec((1, t, 2 * MLA_V), lambda i, p, q: (i, q, p)),
        out_shape=jax.ShapeDtypeStruct((b, n, h * MLA_V), BF16),
        scratch_shapes=[pltpu.VMEM((t, t), F32)] * (2 * ATT_DEPTH) + [pltpu.VMEM((t, t), BF16)] * (2 * ATT_DEPTH),
        compiler_params=_params(("arbitrary", "arbitrary", "arbitrary")),
    )(qt, k, vt)


def _bwd_chunk(s, nct, nch):
    return jnp.where(s < nct, nct - 1 - s, nch - 1 - (s - nct))


def _ret_kernel(dl_ref, qf, kf, vf, qb, kb, vb, of_ref, ob_ref, st_ref):
    c = RET_CHUNK
    hw = RET_HEADS * RET_DK

    @pl.when(pl.program_id(1) == 0)
    def _():
        st_ref[...] = jnp.zeros_like(st_ref)

    ii = lax.broadcasted_iota(jnp.int32, (c, c), 0)
    jj = lax.broadcasted_iota(jnp.int32, (c, c), 1)
    pos_i = lax.broadcasted_iota(jnp.int32, (c, 1), 0).astype(F32)
    pos_j = lax.broadcasted_iota(jnp.int32, (1, c), 1).astype(F32)
    lane_head = lax.broadcasted_iota(jnp.int32, (c, hw), 1) // RET_DK
    for d, (q_ref, k_ref, v_ref, o_ref) in enumerate(((qf, kf, vf, of_ref), (qb, kb, vb, ob_ref))):
        rev = d == 1
        q, v = q_ref[0], v_ref[0]
        k_t = k_ref[0].astype(F32).T
        k_tb = k_t.astype(BF16)
        st = st_ref[d]
        st_b = st.astype(BF16)
        dist = ((jj - ii) if rev else (ii - jj)).astype(F32)
        for h in range(RET_HEADS):
            lg = _log_sigmoid(dl_ref[d, h])[0:1, 0:1]
            intra = jnp.where(dist >= 0.0, jnp.exp(lg * jnp.maximum(dist, 0.0)), 0.0)
            q_dec = jnp.exp(lg * ((c - pos_i) if rev else (pos_i + 1.0)))
            k_dec = jnp.exp(lg * (pos_j if rev else (c - 1.0 - pos_j)))
            rows = slice(h * RET_DK, (h + 1) * RET_DK)
            cols = slice(h * RET_DV, (h + 1) * RET_DV)
            qm = jnp.where(lane_head == h, q, jnp.zeros_like(q))
            scores = _dot(qm, k_tb) * intra
            o = _dot(scores.astype(BF16), v[:, cols]) + _dot(qm, st_b) * q_dec
            o_ref[0, :, cols] = o
            kd = (k_t[rows] * k_dec).astype(BF16)
            st_ref[d, rows, :] = st[rows] * jnp.exp(lg * float(c)) + _dot(kd, v[:, cols])


def _retention(dl, rq, rk, rv, nct):
    b, n, hw = rq.shape
    c = RET_CHUNK
    nch = n // c
    vw = rv.shape[-1]
    fwd = lambda w: pl.BlockSpec((1, c, w), lambda i, s: (i, s, 0))
    bwd = lambda w: pl.BlockSpec((1, c, w), lambda i, s: (i, _bwd_chunk(s, nct, nch), 0))
    return pl.pallas_call(
        _ret_kernel,
        grid=(b, nch),
        in_specs=[_const_spec(dl.shape), fwd(hw), fwd(hw), fwd(vw), bwd(hw), bwd(hw), bwd(vw)],
        out_specs=[fwd(vw), bwd(vw)],
        out_shape=[jax.ShapeDtypeStruct((b, n, vw), F32)] * 2,
        scratch_shapes=[pltpu.VMEM((2, hw, RET_DV), F32)],
        compiler_params=_params(("arbitrary", "arbitrary")),
    )(dl, rq, rk, rv, rq, rk, rv)


def _block_start(i, size):
    return i & ~(size - 1)


def _gla_levels():
    half = GLA_CHUNK // 2
    out = []
    while half >= GLA_BASE:
        out.append(half)
        half //= 2
    return out


def _gla_kernel(qf, kf, vf, laf, qb, kb, vb, lab, of_ref, ob_ref, st_ref):
    c = GLA_CHUNK
    hw = GLA_HEADS * GLA_DK
    levels = _gla_levels()

    @pl.when(pl.program_id(1) == 0)
    def _():
        st_ref[...] = jnp.zeros_like(st_ref)

    ii = lax.broadcasted_iota(jnp.int32, (c, c), 0)
    jj = lax.broadcasted_iota(jnp.int32, (c, c), 1)
    pos = lax.broadcasted_iota(jnp.int32, (c, 1), 0)
    lane_head = lax.broadcasted_iota(jnp.int32, (c, hw), 1) // GLA_DK
    for d, (q_ref, k_ref, v_ref, la_ref, o_ref) in enumerate(((qf, kf, vf, laf, of_ref),
                                                               (qb, kb, vb, lab, ob_ref))):
        rev = d == 1
        q, k, v, la = q_ref[0].astype(F32), k_ref[0].astype(F32), v_ref[0], la_ref[0]

        b = _dot(((jj >= ii) if rev else (jj <= ii)).astype(F32), la, HI)
        b_end = b[0:1] if rev else b[c - 1:c]

        def at_row(size, idx):
            b3 = b.reshape(c // size, size, hw)
            return jnp.broadcast_to(b3[:, idx:idx + 1, :], b3.shape).reshape(c, hw)

        q_t = (q * jnp.exp(b)).astype(BF16)
        k_tt = (k * jnp.exp(b_end - b)).T
        dec_col = jnp.exp(jnp.broadcast_to(b_end, (LANES, hw)).T[:, 0:1])

        terms = []
        for half in levels:
            e = at_row(2 * half, half if rev else half - 1)
            inner = pos & (2 * half - 1)
            late = (inner < half) if rev else (inner >= half)
            qh = jnp.where(late, q * jnp.exp(jnp.where(late, b - e, 0.0)), 0.0)
            kh = jnp.where(late, 0.0, k * jnp.exp(jnp.where(late, 0.0, e - b)))
            terms.append((qh.astype(BF16), kh.astype(BF16), _block_start(ii, 2 * half) == _block_start(jj, 2 * half)))
        r = at_row(GLA_BASE, GLA_BASE - 1 if rev else 0)
        tri = (jj >= ii) if rev else (jj <= ii)
        terms.append(((q * jnp.exp(b - r)).astype(BF16), (k * jnp.exp(r - b)).astype(BF16),
                      (_block_start(ii, GLA_BASE) == _block_start(jj, GLA_BASE)) & tri))

        st = st_ref[d]
        st_b = st.astype(BF16)
        for h in range(GLA_HEADS):
            rows = slice(h * GLA_DK, (h + 1) * GLA_DK)
            cols = slice(h * GLA_DV, (h + 1) * GLA_DV)
            sel = lane_head == h
            a = jnp.zeros((c, c), F32)
            for qh, kh, mask in terms:
                a = a + jnp.where(mask, _dot_nt(jnp.where(sel, qh, jnp.zeros_like(qh)), kh), 0.0)
            vh = v[:, cols]
            o_ref[0, :, cols] = _dot(a.astype(BF16), vh) + _dot(jnp.where(sel, q_t, jnp.zeros_like(q_t)), st_b)
            st_ref[d, rows, :] = st[rows] * dec_col[rows] + _dot(k_tt[rows].astype(BF16), vh)


def _gla(gq, gk, gv, la, nct_tokens):
    b, n, hw = gq.shape
    c = GLA_CHUNK
    nch = n // c
    nct = nct_tokens // c
    vw = gv.shape[-1]
    fwd = lambda w, col=0: pl.BlockSpec((1, c, w), lambda i, s: (i, s, col))
    bwd = lambda w, col=0: pl.BlockSpec((1, c, w), lambda i, s: (i, _bwd_chunk(s, nct, nch), col))
    return pl.pallas_call(
        _gla_kernel,
        grid=(b, nch),
        in_specs=[fwd(hw), fwd(hw), fwd(vw), fwd(hw, 0), bwd(hw), bwd(hw), bwd(vw), bwd(hw, 1)],
        out_specs=[fwd(vw), bwd(vw)],
        out_shape=[jax.ShapeDtypeStruct((b, n, vw), F32)] * 2,
        scratch_shapes=[pltpu.VMEM((2, hw, GLA_DV), F32)],
        compiler_params=_params(("arbitrary", "arbitrary")),
    )(gq, gk, gv, la, gq, gk, gv, la)


def _head_norm(o, w, center):
    parts = []
    for h in range(o.shape[-1] // LANES):
        oh = o[:, h * LANES:(h + 1) * LANES]
        if center:
            oh = oh - jnp.mean(oh, axis=-1, keepdims=True)
        parts.append(oh * lax.rsqrt(jnp.mean(oh * oh, axis=-1, keepdims=True) + EPS))
    return jnp.concatenate(parts, axis=-1) * w


def _merge_kernel(nct, off, x_ref, mod_ref, nw_ref, oa_ref, rof_ref, rob_ref, rg_ref, gof_ref, gob_ref,
                  gr_ref, wbg_ref, wbr_ref, wout_ref, rnw_ref, gnw_ref, o_ref):
    d = D_MODEL
    row = _mod_rows(mod_ref, pl.program_id(1) + off < nct)
    sh, sc, g1 = row[:, 0:d], row[:, d:2 * d], row[:, 2 * d:3 * d]
    x = x_ref[0]
    u = (_rms(x, nw_ref[...]) * (1.0 + sc) + sh).astype(BF16)
    gate = _sigmoid(_dot(u, wbg_ref[...]))
    yb = _silu(rg_ref[0].astype(F32)) * _head_norm(rof_ref[0] + rob_ref[0], rnw_ref[...], True)
    yc = _silu(gr_ref[0].astype(F32)) * _head_norm(gof_ref[0] + gob_ref[0], gnw_ref[...], False)
    z = (gate[:, 0:d] * _dot(oa_ref[0], wbr_ref[0])
         + gate[:, d:2 * d] * _dot(yb.astype(BF16), wbr_ref[1])
         + gate[:, 2 * d:3 * d] * _dot(yc.astype(BF16), wbr_ref[2]))
    o_ref[0] = x + g1 * _dot(z.astype(BF16), wout_ref[...])


def _merge(xs, modsel, nw, oa, rof, rob, rg, gof, gob, gr, wbg, wbr, wout, rnw, gnw, nct, latent_only):
    b, n, d = xs.shape
    tm = TOKEN_TILE
    off = nct if latent_only else 0
    nt = n // tm - off
    tok = lambda w: pl.BlockSpec((1, tm, w), lambda i, t: (i, t + off, 0))
    return pl.pallas_call(
        functools.partial(_merge_kernel, nct, off),
        grid=(b, nt),
        in_specs=[tok(d), pl.BlockSpec((1, 2, 6 * d), lambda i, t: (i, 0, 0)), _const_spec(nw.shape),
                  tok(512), tok(512), tok(512), tok(512), tok(512), tok(512), tok(512),
                  _const_spec(wbg.shape), _const_spec(wbr.shape), _const_spec(wout.shape),
                  _const_spec(rnw.shape), _const_spec(gnw.shape)],
        out_specs=pl.BlockSpec((1, tm, d), lambda i, t: (i, t, 0)),
        out_shape=jax.ShapeDtypeStruct((b, nt * tm, d), F32),
        compiler_params=_params(("arbitrary", "arbitrary")),
    )(xs, modsel, nw, oa, rof, rob, rg, gof, gob, gr, wbg, wbr, wout, rnw, gnw)


def _norm2(x, row, nw_ref):
    d = D_MODEL
    sh, sc = row[:, 3 * d:4 * d], row[:, 4 * d:5 * d]
    return _rms(x, nw_ref[...]) * (1.0 + sc) + sh


def _finish(x, y, row, final, fnw_ref):
    d = D_MODEL
    out = x + row[:, 5 * d:6 * d] * y
    if final:
        out = _rms(out, fnw_ref[...])
    return out


def _ffn_kernel(nct, off, final, x_ref, mod_ref, nw_ref, wg_ref, wu_ref, wd_ref, fnw_ref, o_ref):
    row = _mod_rows(mod_ref, pl.program_id(1) + off < nct)
    x = x_ref[0]
    v = _norm2(x, row, nw_ref).astype(BF16)
    hdn = _silu(_dot(v, wg_ref[...])) * _dot(v, wu_ref[...])
    o_ref[0] = _finish(x, _dot(hdn.astype(BF16), wd_ref[...]), row, final, fnw_ref)


def _ffn(xs, modsel, nw, wg, wu, wd, fnw, nct, final):
    b, n, d = xs.shape
    tm = TOKEN_TILE
    off = nct if final else 0
    nt = n // tm - off
    return pl.pallas_call(
        functools.partial(_ffn_kernel, nct, off, final),
        grid=(b, nt),
        in_specs=[pl.BlockSpec((1, tm, d), lambda i, t: (i, t + off, 0)),
                  pl.BlockSpec((1, 2, 6 * d), lambda i, t: (i, 0, 0)), _const_spec(nw.shape),
                  _const_spec(wg.shape), _const_spec(wu.shape), _const_spec(wd.shape),
                  _const_spec(fnw.shape)],
        out_specs=pl.BlockSpec((1, tm, d), lambda i, t: (i, t, 0)),
        out_shape=jax.ShapeDtypeStruct((b, nt * tm, d), F32),
        compiler_params=_params(("arbitrary", "arbitrary")),
    )(xs, modsel, nw, wg, wu, wd, fnw)


def _top2_gates(logits):
    lane = lax.broadcasted_iota(jnp.int32, logits.shape, 1)
    m1 = jnp.max(logits, axis=-1, keepdims=True)
    i1 = jnp.min(jnp.where(logits == m1, lane, LANES), axis=-1, keepdims=True)
    rest = jnp.where(lane == i1, -jnp.inf, logits)
    m2 = jnp.max(rest, axis=-1, keepdims=True)
    i2 = jnp.min(jnp.where(rest == m2, lane, LANES), axis=-1, keepdims=True)
    e2 = jnp.exp(m2 - m1)
    w1 = 1.0 / (1.0 + e2)
    gates = jnp.where(lane == i1, w1, jnp.where(lane == i2, e2 * w1, 0.0))
    return gates, (lane == i1) | (lane == i2)


def _one_hot(cond):
    return jnp.where(cond, 1.0, 0.0).astype(BF16)


def _route_kernel(mrow, x_ref, mod_ref, nw_ref, rt_ref, v_out, gate_out, rank_out, rankt_out):
    v = _norm2(x_ref[0], mod_ref[0][mrow:mrow + 1], nw_ref)
    v_out[0] = v.astype(BF16)
    tt = v.shape[0]
    lane = lax.broadcasted_iota(jnp.int32, (tt, LANES), 1)
    logits = jnp.where(lane < N_EXPERTS, _dot(v, rt_ref[...], HI), -jnp.inf)
    gates, routed = _top2_gates(logits)
    gate_out[0] = gates
    ti = lax.broadcasted_iota(jnp.int32, (tt, tt), 0)
    tj = lax.broadcasted_iota(jnp.int32, (tt, tt), 1)
    rank_out[0] = jnp.where(routed, _dot(_one_hot(tj < ti), _one_hot(routed)), -1.0)
    routed_t = jnp.where(routed, 1.0, 0.0).T
    rank_t = jnp.where(routed_t > 0.5, _dot(routed_t.astype(BF16), _one_hot(ti < tj)), -1.0)
    rankt_out[0, 0] = rank_t[0:N_EXPERTS]


def _experts_kernel(mrow, final, v_ref, gate_ref, rank_ref, rankt_ref, wg_ref, wu_ref, wd_ref,
                    x_ref, mod_ref, fnw_ref, o_ref, acc_ref):
    e, s = pl.program_id(1), pl.program_id(2)
    tt = v_ref.shape[1]
    r = MOE_BLOCK
    fh = D_FF // 2
    tile0 = pl.multiple_of(s * tt, tt)
    lane = lax.broadcasted_iota(jnp.int32, (tt, LANES), 1)
    g_col = jnp.sum(jnp.where(lane == e, gate_ref[0], 0.0), axis=-1, keepdims=True)
    rk_col = jnp.sum(jnp.where(lane == e, rank_ref[0], 0.0), axis=-1, keepdims=True)
    rk_row = rankt_ref[0, 0, pl.ds(e, 1), :]
    n_routed = jnp.max(rk_col).astype(jnp.int32) + 1

    @pl.when(e == 0)
    def _():
        acc_ref[pl.ds(tile0, tt), :] = jnp.zeros((tt, D_MODEL), BF16)

    def block(i, carry):
        base = (i * r).astype(F32)
        slot_i = base + lax.broadcasted_iota(jnp.int32, (r, 1), 0).astype(F32)
        xg = _dot(_one_hot(rk_row == slot_i), v_ref[0]).astype(BF16)
        y = jnp.zeros((r, D_MODEL), F32)
        for c in range(2):
            cols = slice(c * fh, (c + 1) * fh)
            hdn = _silu(_dot(xg, wg_ref[0, :, cols])) * _dot(xg, wu_ref[0, :, cols])
            y = y + _dot(hdn.astype(BF16), wd_ref[0, cols, :])
        y = y.astype(BF16)
        slot_j = base + lax.broadcasted_iota(jnp.int32, (1, r), 1).astype(F32)
        for c in range(tt // TOKEN_TILE):
            rows = slice(c * TOKEN_TILE, (c + 1) * TOKEN_TILE)
            dst = pl.ds(tile0 + c * TOKEN_TILE, TOKEN_TILE)
            part = g_col[rows] * _dot(_one_hot(rk_col[rows] == slot_j), y)
            acc_ref[dst, :] = (acc_ref[dst, :].astype(F32) + part).astype(BF16)
        return carry

    lax.fori_loop(0, (n_routed + r - 1) // r, block, 0)

    @pl.when(e == N_EXPERTS - 1)
    def _():
        y_sum = acc_ref[pl.ds(tile0, tt), :].astype(F32)
        o_ref[0] = _finish(x_ref[0], y_sum, mod_ref[0][mrow:mrow + 1], final, fnw_ref)


def _moe(slab, mrow, modsel, nw, rt, wg, wu, wd, fnw, final):
    b, m, d = slab.shape
    tt = min(MOE_TILE, m)
    assert m % tt == 0 and tt % TOKEN_TILE == 0
    ns = m // tt
    ne, _, f = wg.shape
    mod_spec = lambda nd: pl.BlockSpec((1, 2, 6 * d), (lambda i, s: (i, 0, 0)) if nd == 2 else
                                       (lambda i, e, s: (i, 0, 0)))
    v, gates, rank, rank_t = pl.pallas_call(
        functools.partial(_route_kernel, mrow),
        grid=(b, ns),
        in_specs=[pl.BlockSpec((1, tt, d), lambda i, s: (i, s, 0)), mod_spec(2), _const_spec(nw.shape),
                  _const_spec(rt.shape)],
        out_specs=[pl.BlockSpec((1, tt, d), lambda i, s: (i, s, 0)),
                   pl.BlockSpec((1, tt, LANES), lambda i, s: (i, s, 0)),
                   pl.BlockSpec((1, tt, LANES), lambda i, s: (i, s, 0)),
                   pl.BlockSpec((1, 1, ne, tt), lambda i, s: (i, s, 0, 0))],
        out_shape=[jax.ShapeDtypeStruct((b, m, d), BF16), jax.ShapeDtypeStruct((b, m, LANES), F32),
                   jax.ShapeDtypeStruct((b, m, LANES), F32), jax.ShapeDtypeStruct((b, ns, ne, tt), F32)],
        compiler_params=_params(("arbitrary", "arbitrary")),
    )(slab, modsel, nw, rt)

    tok = lambda w: pl.BlockSpec((1, tt, w), lambda i, e, s: (i, s, 0))
    wspec = lambda shape: pl.BlockSpec((1,) + shape, lambda i, e, s: (e, 0, 0), pipeline_mode=pl.Buffered(1))
    last_pass = lambda i, e, s: (i, jnp.where(e == ne - 1, s, 0), 0)
    return pl.pallas_call(
        functools.partial(_experts_kernel, mrow, final),
        grid=(b, ne, ns),
        in_specs=[tok(d), tok(LANES), tok(LANES),
                  pl.BlockSpec((1, 1, ne, tt), lambda i, e, s: (i, s, 0, 0)),
                  wspec((d, f)), wspec((d, f)), wspec((f, d)),
                  pl.BlockSpec((1, tt, d), last_pass, pipeline_mode=pl.Buffered(1)),
                  mod_spec(3), _const_spec(fnw.shape)],
        out_specs=pl.BlockSpec((1, tt, d), last_pass),
        out_shape=jax.ShapeDtypeStruct((b, m, d), F32),
        scratch_shapes=[pltpu.VMEM((m, d), BF16)],
        compiler_params=_params(("arbitrary", "arbitrary", "arbitrary")),
    )(v, gates, rank, rank_t, wg, wu, wd, slab, modsel, fnw)


def _axial_perm():
    dd = np.arange(MLA_ROPE)
    return np.where(dd % 16 < 8, dd + 8, dd - 8)


def _ret_perm():
    dd = np.arange(RET_HEADS * RET_DK)
    return np.where(dd % RET_DK < RET_DK // 2, dd + RET_DK // 2, dd - RET_DK // 2)


def _prep_w_in(w):
    offs = np.cumsum((0,) + IN_SPLITS)
    cq, ckv, kr, rq, rk, rv, rg, gq, gk, gv, gr, gz, bg = [w[:, offs[i]:offs[i + 1]] for i in range(13)]
    zeros = lambda n: jnp.zeros((w.shape[0], n), w.dtype)
    rk = rk * (RET_DK ** -0.5)
    gq = gq * (GLA_DK ** -0.5)
    w1 = jnp.concatenate([
        cq, ckv,
        zeros(MLA_NOPE), kr, zeros(HEAD_SLOT - MLA_NOPE - MLA_ROPE),
        zeros(MLA_NOPE), kr[:, _axial_perm()], zeros(HEAD_SLOT - MLA_NOPE - MLA_ROPE),
        rq, rq[:, _ret_perm()], rk, rk[:, _ret_perm()], rv, rg,
        gq, gk, gv, gr, gz, zeros(LANES - 2 * GLA_GATE_RANK)], axis=1)
    return w1.astype(BF16), bg.astype(BF16)


def _prep_mla(w_uq, w_ukv):
    r = w_uq.shape[0]
    scale = (MLA_NOPE + MLA_ROPE) ** -0.5 * np.log2(np.e)
    wq3 = w_uq.reshape(r, MLA_HEADS, MLA_NOPE + MLA_ROPE) * scale
    pad = jnp.zeros((r, MLA_HEADS, HEAD_SLOT - MLA_NOPE - MLA_ROPE), w_uq.dtype)
    wq = jnp.concatenate([wq3, pad], axis=-1)
    wqs = jnp.concatenate([jnp.zeros((r, MLA_HEADS, MLA_NOPE), w_uq.dtype),
                           wq3[:, :, MLA_NOPE:][:, :, _axial_perm()], pad], axis=-1)
    rk = w_ukv.shape[0]
    wkv3 = w_ukv.reshape(rk, MLA_HEADS, MLA_NOPE + MLA_V)
    wk = jnp.concatenate([wkv3[:, :, :MLA_NOPE],
                          jnp.zeros((rk, MLA_HEADS, HEAD_SLOT - MLA_NOPE), w_ukv.dtype)], axis=-1)
    wv = wkv3[:, :, MLA_NOPE:]
    flat = lambda a: a.reshape(a.shape[0], -1).astype(BF16)
    return flat(wq), flat(wqs), flat(wk), flat(wv)


def _prep_gla_gate(w_gate, b_gate):
    hw = GLA_HEADS * GLA_DK
    wg2 = jnp.zeros((LANES, 2 * hw), F32)
    wg2 = wg2.at[0:GLA_GATE_RANK, 0:hw].set(w_gate[0])
    wg2 = wg2.at[GLA_GATE_RANK:2 * GLA_GATE_RANK, hw:2 * hw].set(w_gate[1])
    return wg2, b_gate.reshape(1, 2 * hw)


def _rope_tables(n_ctx, seq):
    t = jnp.arange(seq, dtype=jnp.int32)
    half = MLA_ROPE // 4
    inv = ROPE_BASE ** (-jnp.arange(half, dtype=F32) / half)
    lane = np.arange(HEAD_SLOT)
    dd = lane - MLA_NOPE
    active = (dd >= 0) & (dd < MLA_ROPE)
    dd = np.clip(dd, 0, MLA_ROPE - 1)
    by_col = (dd // 16) == 1
    first = (dd % 16) < half
    pos = jnp.where(by_col[None, :], (t % GRID_W)[:, None], (t // GRID_W)[:, None]).astype(F32)
    ang = pos * inv[dd % half][None, :]
    cosq = jnp.where(active[None, :], jnp.cos(ang), 1.0)
    sinq = jnp.where(active[None, :], jnp.where(first[None, :], -jnp.sin(ang), jnp.sin(ang)), 0.0)
    rhalf = RET_DK // 2
    rinv = ROPE_BASE ** (-jnp.arange(rhalf, dtype=F32) / rhalf)
    rl = np.arange(RET_HEADS * RET_DK) % RET_DK
    rang = t.astype(F32)[:, None] * rinv[rl % rhalf][None, :]
    cosr = jnp.cos(rang)
    sinr = jnp.where((rl < rhalf)[None, :], -jnp.sin(rang), jnp.sin(rang))
    ctx = lambda a, fill: jnp.concatenate([jnp.full((n_ctx, a.shape[1]), fill, F32), a], axis=0)
    return ctx(cosq, 1.0), ctx(sinq, 0.0), ctx(cosr, 1.0), ctx(sinr, 0.0)


def kernel(x, c, ctx, c_ctx, mod_w, mod_b, norm1_w, norm2_w, w_in, mla_q_norm, mla_w_uq, mla_kv_norm, mla_w_ukv, ret_decay_logit, ret_norm_w, gla_w_gate, gla_b_gate, gla_norm_w, w_branch, w_out, ffn_w_gate, ffn_w_up, ffn_w_down, moe_router, moe_w_gate, moe_w_up, moe_w_down, final_norm_w):
    b, seq, d = x.shape
    n_ctx = ctx.shape[1]
    depth = mod_w.shape[0]
    assert d == D_MODEL and seq % GRID_W == 0
    assert n_ctx % TOKEN_TILE == 0 and n_ctx % RET_CHUNK == 0 and n_ctx % GLA_CHUNK == 0
    assert seq % TOKEN_TILE == 0 and seq % RET_CHUNK == 0
    nct = n_ctx // TOKEN_TILE

    rows = 8 * ((b + 1 + 7) // 8)
    cvec = jnp.zeros((rows, d), F32).at[0:b].set(c).at[b].set(c_ctx)
    mod = _modulation(cvec, mod_w, mod_b)
    cosq, sinq, cosr, sinr = _rope_tables(n_ctx, seq)
    fnw = final_norm_w.reshape(1, d)
    row2 = lambda a: a.reshape(1, -1)

    xs = jnp.concatenate([ctx, x], axis=1)
    for l in range(depth):
        last = l == depth - 1
        modsel = jnp.stack([jnp.broadcast_to(mod[l, b], (b, 6 * d)), mod[l, 0:b]], axis=1)
        w1, wbg = _prep_w_in(w_in[l])
        wq, wqs, wk, wv = _prep_mla(mla_w_uq[l], mla_w_ukv[l])
        wg2, bg2 = _prep_gla_gate(gla_w_gate[l], gla_b_gate[l])
        nw1, nw2 = row2(norm1_w[l]), row2(norm2_w[l])

        (q, k, v, rq, rk, rv, rg, gq, gk, gv, gr, la) = _inproj(
            xs, modsel, nw1, w1, row2(mla_q_norm[l]), wq, wqs, row2(mla_kv_norm[l]), wk, wv,
            wg2, bg2, cosq, sinq, cosr, sinr, nct)
        oa = _attention(q, k, v, nct)
        dl = jnp.broadcast_to(ret_decay_logit[l][:, :, None, None], (2, RET_HEADS, 8, LANES))
        rof, rob = _retention(dl, rq, rk, rv, n_ctx // RET_CHUNK)
        gof, gob = _gla(gq, gk, gv, la, n_ctx)
        is_moe = l % 2 == 1
        xs = _merge(xs, modsel, nw1, oa, rof, rob, rg, gof, gob, gr, wbg,
                    w_branch[l].astype(BF16), w_out[l].astype(BF16),
                    row2(ret_norm_w[l]), row2(gla_norm_w[l]), nct, is_moe and last)
        i = l // 2
        if not is_moe:
            xs = _ffn(xs, modsel, nw2, ffn_w_gate[i].astype(BF16), ffn_w_up[i].astype(BF16),
                      ffn_w_down[i].astype(BF16), fnw, nct, last)
            continue
        rt = jnp.zeros((d, LANES), F32).at[:, 0:N_EXPERTS].set(moe_router[i])
        experts = (moe_w_gate[i].astype(BF16), moe_w_up[i].astype(BF16), moe_w_down[i].astype(BF16))
        if last:
            xs = _moe(xs, 1, modsel, nw2, rt, *experts, fnw, True)
        else:
            xs = jnp.concatenate([_moe(xs[:, :n_ctx], 0, modsel, nw2, rt, *experts, fnw, False),
                                  _moe(xs[:, n_ctx:], 1, modsel, nw2, rt, *experts, fnw, False)], axis=1)
    return xs
```

```python
import functools

import numpy as np
import jax
import jax.numpy as jnp
from jax import lax
from jax.experimental import pallas as pl
from jax.experimental.pallas import tpu as pltpu

D_MODEL = 1024
GRID_W = 64
EPS = 1e-6
ROPE_BASE = 10000.0

MLA_HEADS = 8
MLA_NOPE = 64
MLA_ROPE = 32
MLA_V = 64
MLA_Q_RANK = 256
MLA_KV_RANK = 128

RET_HEADS = 4
RET_DK = 64
RET_DV = 128

GLA_HEADS = 4
GLA_DK = 64
GLA_DV = 128
GLA_GATE_RANK = 16
GLA_TAU = 16.0

N_BRANCH = 3
BRANCH_W = 512
D_FF = 2816
N_EXPERTS = 8

IN_SPLITS = (
    MLA_Q_RANK, MLA_KV_RANK, MLA_ROPE,
    RET_HEADS * RET_DK, RET_HEADS * RET_DK, RET_HEADS * RET_DV, RET_HEADS * RET_DV,
    GLA_HEADS * GLA_DK, GLA_HEADS * GLA_DK, GLA_HEADS * GLA_DV, GLA_HEADS * GLA_DV,
    2 * GLA_GATE_RANK,
    N_BRANCH * D_MODEL,
)

F32 = jnp.float32
BF16 = jnp.bfloat16
HI = lax.Precision.HIGHEST

LANES = 128
HEAD_SLOT = 128
TOKEN_TILE = 256
ATT_DEPTH = 3
ATT_UNROLL = 10
ATT_SUM_ROWS = 16
RET_CHUNK = 256
GLA_CHUNK = 256
GLA_BASE = 16
MOE_TILE = 512
MOE_BLOCK = 160
MOD_COLS_TILE = 1536
VMEM_LIMIT = 56 * 1024 * 1024


def _silu(x):
    return x / (1.0 + jnp.exp(-x))


def _sigmoid(x):
    return 1.0 / (1.0 + jnp.exp(-x))


def _log_sigmoid(z):
    return jnp.minimum(z, 0.0) - jnp.log(1.0 + jnp.exp(-jnp.abs(z)))


def _rms(x, w):
    return x * lax.rsqrt(jnp.mean(x * x, axis=-1, keepdims=True) + EPS) * w


def _dot(a, b, precision=None):
    return jnp.dot(a, b, preferred_element_type=F32, precision=precision)


def _dot_nt(a, b):
    return lax.dot_general(a, b, (((1,), (1,)), ((), ())), preferred_element_type=F32)


def _params(sem):
    return pltpu.CompilerParams(dimension_semantics=sem, vmem_limit_bytes=VMEM_LIMIT)


def _const_spec(shape):
    nd = len(shape)
    return pl.BlockSpec(shape, lambda *_: (0,) * nd, pipeline_mode=pl.Buffered(1))


def _mod_rows(mod_ref, is_ctx):
    m = mod_ref[0]
    return jnp.where(is_ctx, m[0:1], m[1:2])


def _mod_kernel(c_ref, w_ref, b_ref, o_ref):
    o_ref[0] = _dot(_silu(c_ref[...]), w_ref[0], HI) + b_ref[0]


def _modulation(cvec, mod_w, mod_b):
    depth, d, n6 = mod_w.shape
    rows = cvec.shape[0]
    tn = MOD_COLS_TILE
    return pl.pallas_call(
        _mod_kernel,
        grid=(depth, n6 // tn),
        in_specs=[pl.BlockSpec((rows, d), lambda l, j: (0, 0)),
                  pl.BlockSpec((1, d, tn), lambda l, j: (l, 0, j)),
                  pl.BlockSpec((1, 1, tn), lambda l, j: (l, 0, j))],
        out_specs=pl.BlockSpec((1, rows, tn), lambda l, j: (l, 0, j)),
        out_shape=jax.ShapeDtypeStruct((depth, rows, n6), F32),
        compiler_params=_params(("arbitrary", "arbitrary")),
    )(cvec, mod_w, mod_b.reshape(depth, 1, n6))


_C_CQ = (0, 256)
_C_CKV_KRA = (256, 512)
_C_KRB_GZ = (512, 768)
_C_RQ = (768, 1024)
_C_RQS = (1024, 1280)
_C_RK = (1280, 1536)
_C_RKS = (1536, 1792)
_C_RV = (1792, 2304)
_C_RG = (2304, 2816)
_C_GQ = (2816, 3072)
_C_GK = (3072, 3328)
_C_GV = (3328, 3840)
_C_GR = (3840, 4352)


def _seq_inputs(xs, nct, off=0):
    tm = TOKEN_TILE
    if not isinstance(xs, tuple):
        return [pl.BlockSpec((1, tm, xs.shape[-1]), lambda i, t: (i, t + off, 0))], [xs]
    d = xs[0].shape[-1]
    return ([pl.BlockSpec((1, tm, d), lambda i, t: (i, jnp.minimum(t + off, nct - 1), 0)),
             pl.BlockSpec((1, tm, d), lambda i, t: (i, jnp.maximum(t + off - nct, 0), 0))], list(xs))


def _seq_tile(x_refs, is_ctx):
    if len(x_refs) == 1:
        return x_refs[0][0]
    return jnp.where(is_ctx, x_refs[0][0], x_refs[1][0])


def _inproj_kernel(nct, nx, *refs):
    x_refs = refs[:nx]
    (mod_ref, nw_ref, w_ref, qn_ref, wq_ref, wqs_ref, kvn_ref, wk_ref, wv_ref, wg2_ref, bg2_ref,
     cq_ref, sq_ref, cr_ref, sr_ref, q_out, k_out, v_out, rq_out, rk_out, rv_out, rg_out,
     gq_out, gk_out, gv_out, gr_out, la_out) = refs[nx:]
    d = D_MODEL
    is_ctx = pl.program_id(1) < nct
    row = _mod_rows(mod_ref, is_ctx)
    sh, sc = row[:, 0:d], row[:, d:2 * d]
    u = (_rms(_seq_tile(x_refs, is_ctx), nw_ref[...]) * (1.0 + sc) + sh).astype(BF16)

    def seg(c):
        return _dot(u, w_ref[:, c[0]:c[1]])

    cqn = _rms(seg(_C_CQ), qn_ref[...]).astype(BF16)
    q_all = _dot(cqn, wq_ref[...])
    q_swp = _dot(cqn, wqs_ref[...])
    ckv_kra = seg(_C_CKV_KRA)
    krb_gz = seg(_C_KRB_GZ)
    ckvn = _rms(ckv_kra[:, 0:LANES], kvn_ref[...]).astype(BF16)
    k_all = _dot(ckvn, wk_ref[...])
    v_out[0, 0] = _dot(ckvn, wv_ref[...]).T.astype(BF16)
    cosq, sinq = cq_ref[...], sq_ref[...]
    k_rope = ckv_kra[:, LANES:] * cosq + krb_gz[:, 0:LANES] * sinq
    for h in range(MLA_HEADS):
        sl = slice(h * HEAD_SLOT, (h + 1) * HEAD_SLOT)
        q_out[0, h, 0] = (q_all[:, sl] * cosq + q_swp[:, sl] * sinq).T.astype(BF16)
        k_out[0, h] = (k_all[:, sl] + k_rope).astype(BF16)

    cosr, sinr = cr_ref[...], sr_ref[...]
    rq_out[0] = (seg(_C_RQ) * cosr + seg(_C_RQS) * sinr).astype(BF16)
    rk_out[0] = (seg(_C_RK) * cosr + seg(_C_RKS) * sinr).astype(BF16)
    rv_out[0] = seg(_C_RV).astype(BF16)
    rg_out[0] = seg(_C_RG).astype(BF16)

    gq_out[0] = seg(_C_GQ).astype(BF16)
    gk_out[0] = seg(_C_GK).astype(BF16)
    gv_out[0] = seg(_C_GV).astype(BF16)
    gr_out[0] = seg(_C_GR).astype(BF16)
    z = _dot(krb_gz[:, LANES:], wg2_ref[...], HI) + bg2_ref[...]
    la_out[0] = _log_sigmoid(z) * (1.0 / GLA_TAU)


def _inproj(xs, modsel, nw, w1, qn, wq, wqs, kvn, wk, wv, wg2, bg2, cosq, sinq, cosr, sinr, nct):
    x_specs, x_ops = _seq_inputs(xs, nct)
    b, d = x_ops[0].shape[0], x_ops[0].shape[-1]
    n = sum(a.shape[1] for a in x_ops)
    tm = TOKEN_TILE
    nt = n // tm
    h = MLA_HEADS
    tok = lambda w: pl.BlockSpec((1, tm, w), lambda i, t: (i, t, 0))
    tab = lambda w: pl.BlockSpec((tm, w), lambda i, t: (t, 0))
    q_spec = pl.BlockSpec((1, h, 1, HEAD_SLOT, tm), lambda i, t: (i, 0, t, 0, 0))
    k_spec = pl.BlockSpec((1, h, tm, HEAD_SLOT), lambda i, t: (i, 0, t, 0))
    v_spec = pl.BlockSpec((1, 1, h * MLA_V, tm), lambda i, t: (i, t, 0, 0))
    sds = lambda w, dt=BF16: jax.ShapeDtypeStruct((b, n, w), dt)
    q_sds = jax.ShapeDtypeStruct((b, h, nt, HEAD_SLOT, tm), BF16)
    k_sds = jax.ShapeDtypeStruct((b, h, n, HEAD_SLOT), BF16)
    v_sds = jax.ShapeDtypeStruct((b, nt, h * MLA_V, tm), BF16)
    return pl.pallas_call(
        functools.partial(_inproj_kernel, nct, len(x_ops)),
        grid=(b, nt),
        in_specs=x_specs + [pl.BlockSpec((1, 2, 6 * d), lambda i, t: (i, 0, 0)), _const_spec(nw.shape),
                  _const_spec(w1.shape), _const_spec(qn.shape), _const_spec(wq.shape),
                  _const_spec(wqs.shape), _const_spec(kvn.shape), _const_spec(wk.shape),
                  _const_spec(wv.shape), _const_spec(wg2.shape), _const_spec(bg2.shape),
                  tab(HEAD_SLOT), tab(HEAD_SLOT), tab(256), tab(256)],
        out_specs=[q_spec, k_spec, v_spec, tok(256), tok(256), tok(512), tok(512),
                   tok(256), tok(256), tok(512), tok(512), tok(512)],
        out_shape=[q_sds, k_sds, v_sds, sds(256), sds(256), sds(512), sds(512),
                   sds(256), sds(256), sds(512), sds(512), sds(512, F32)],
        compiler_params=_params(("arbitrary", "arbitrary")),
    )(*x_ops, modsel, nw, w1, qn, wq, wqs, kvn, wk, wv, wg2, bg2, cosq, sinq, cosr, sinr)


def _attn_kernel(nct, n_tiles, qt_ref, k_ref, vt_ref, o_ref, *bufs):
    t = TOKEN_TILE
    nd = ATT_DEPTH
    heads = range(2)
    s_bufs = (bufs[0:nd], bufs[nd:2 * nd])
    p_bufs = (bufs[2 * nd:3 * nd], bufs[3 * nd:4 * nd])
    qts = [qt_ref[0, j, 0] for j in heads]
    ones = jnp.ones((ATT_SUM_ROWS, t), BF16)

    def scores(j, c, s_buf):
        s = _dot(k_ref[0, j, pl.ds(pl.multiple_of(c * t, t), t), :], qts[j])
        s_buf[...] = s
        return jnp.max(s, axis=0, keepdims=True)

    def softmax(s_buf, p_buf, m, cmax):
        m_new = jnp.maximum(m, cmax)
        p_buf[...] = jnp.exp2(s_buf[...] - m_new).astype(BF16)
        return m_new, jnp.exp2(m - m_new)

    def values(j, c, p_buf, alpha, acc):
        lhs = jnp.concatenate([vt_ref[0, c, j * MLA_V:(j + 1) * MLA_V, :], ones], axis=0)
        return alpha * acc + _dot(lhs, p_buf[...])

    m_init = jnp.full((1, t), -jnp.inf, F32)
    acc_init = jnp.zeros((MLA_V + ATT_SUM_ROWS, t), F32)

    def finish(accs):
        outs = [acc[0:MLA_V] / acc[MLA_V:MLA_V + 1] for acc in accs]
        o_ref[0] = jnp.concatenate(outs, axis=0).T.astype(BF16)

    @pl.when(pl.program_id(2) < nct)
    def _():
        accs = []
        for j in heads:
            m, acc = m_init, acc_init
            for c in range(nct):
                cmax = scores(j, c, s_bufs[j][0])
                m, alpha = softmax(s_bufs[j][0], p_bufs[j][0], m, cmax)
                acc = values(j, c, p_bufs[j][0], alpha, acc)
            accs.append(acc)
        finish(accs)

    def step(c, slot, state, lookahead=True):
        ahead = [scores(j, c + 2, s_bufs[j][(slot + 2) % nd]) if lookahead else None for j in heads]
        accs = [values(j, c - 1, p_bufs[j][(slot - 1) % nd], state[j][1], state[j][2]) for j in heads]
        stats = [softmax(s_bufs[j][slot], p_bufs[j][slot], state[j][0], state[j][3]) for j in heads]
        return tuple((stats[j][0], stats[j][1], accs[j], state[j][4], ahead[j]) for j in heads)

    @pl.when(pl.program_id(2) >= nct)
    def _():
        state = []
        for j in heads:
            cmax0 = scores(j, 0, s_bufs[j][0])
            cmax1 = scores(j, 1, s_bufs[j][1])
            m, alpha = softmax(s_bufs[j][0], p_bufs[j][0], m_init, cmax0)
            state.append((m, alpha, acc_init, cmax1, scores(j, 2, s_bufs[j][2])))

        def trip(i, state):
            for k in range(nd):
                state = step(1 + nd * i + k, (1 + k) % nd, state)
            return state

        state = lax.fori_loop(0, (n_tiles - 3) // nd, trip, tuple(state), unroll=ATT_UNROLL)
        for c in (n_tiles - 2, n_tiles - 1):
            state = step(c, c % nd, state, lookahead=False)
        finish([values(j, n_tiles - 1, p_bufs[j][(n_tiles - 1) % nd], state[j][1], state[j][2])
                for j in heads])


def _attention(qt, k, vt, nct):
    b, h, n_tiles, w, t = qt.shape
    n = n_tiles * t
    assert n_tiles >= 3 and (n_tiles - 3) % ATT_DEPTH == 0
    return pl.pallas_call(
        functools.partial(_attn_kernel, nct, n_tiles),
        grid=(b, h // 2, n_tiles),
        in_specs=[pl.BlockSpec((1, 2, 1, w, t), lambda i, p, q: (i, p, q, 0, 0)),
                  pl.BlockSpec((1, 2, n, w), lambda i, p, q: (i, p, 0, 0)),
                  pl.BlockSpec((1, n_tiles, 2 * MLA_V, t), lambda i, p, q: (i, 0, p, 0))],
        out_specs=pl.BlockSpec((1, t, 2 * MLA_V), lambda i, p, q: (i, q, p)),
        out_shape=jax.ShapeDtypeStruct((b, n, h * MLA_V), BF16),
        scratch_shapes=[pltpu.VMEM((t, t), F32)] * (2 * ATT_DEPTH) + [pltpu.VMEM((t, t), BF16)] * (2 * ATT_DEPTH),
        compiler_params=_params(("arbitrary", "arbitrary", "arbitrary")),
    )(qt, k, vt)


def _bwd_chunk(s, nct, nch):
    return jnp.where(s < nct, nct - 1 - s, nch - 1 - (s - nct))


def _ret_kernel(dl_ref, qf, kf, vf, qb, kb, vb, of_ref, ob_ref, st_ref):
    c = RET_CHUNK
    hw = RET_HEADS * RET_DK

    @pl.when(pl.program_id(1) == 0)
    def _():
        st_ref[...] = jnp.zeros_like(st_ref)

    ii = lax.broadcasted_iota(jnp.int32, (c, c), 0)
    jj = lax.broadcasted_iota(jnp.int32, (c, c), 1)
    pos_i = lax.broadcasted_iota(jnp.int32, (c, 1), 0).astype(F32)
    pos_j = lax.broadcasted_iota(jnp.int32, (1, c), 1).astype(F32)
    lane_head = lax.broadcasted_iota(jnp.int32, (c, hw), 1) // RET_DK
    for d, (q_ref, k_ref, v_ref, o_ref) in enumerate(((qf, kf, vf, of_ref), (qb, kb, vb, ob_ref))):
        rev = d == 1
        q, v = q_ref[0], v_ref[0]
        k_t = k_ref[0].astype(F32).T
        k_tb = k_t.astype(BF16)
        st = st_ref[d]
        st_b = st.astype(BF16)
        dist = ((jj - ii) if rev else (ii - jj)).astype(F32)
        for h in range(RET_HEADS):
            lg = _log_sigmoid(dl_ref[d, h])[0:1, 0:1]
            intra = jnp.where(dist >= 0.0, jnp.exp(lg * jnp.maximum(dist, 0.0)), 0.0)
            q_dec = jnp.exp(lg * ((c - pos_i) if rev else (pos_i + 1.0)))
            k_dec = jnp.exp(lg * (pos_j if rev else (c - 1.0 - pos_j)))
            rows = slice(h * RET_DK, (h + 1) * RET_DK)
            cols = slice(h * RET_DV, (h + 1) * RET_DV)
            qm = jnp.where(lane_head == h, q, jnp.zeros_like(q))
            scores = _dot(qm, k_tb) * intra
            o = _dot(scores.astype(BF16), v[:, cols]) + _dot(qm, st_b) * q_dec
            o_ref[0, :, cols] = o
            kd = (k_t[rows] * k_dec).astype(BF16)
            st_ref[d, rows, :] = st[rows] * jnp.exp(lg * float(c)) + _dot(kd, v[:, cols])


def _retention(dl, rq, rk, rv, nct):
    b, n, hw = rq.shape
    c = RET_CHUNK
    nch = n // c
    vw = rv.shape[-1]
    fwd = lambda w: pl.BlockSpec((1, c, w), lambda i, s: (i, s, 0))
    bwd = lambda w: pl.BlockSpec((1, c, w), lambda i, s: (i, _bwd_chunk(s, nct, nch), 0))
    return pl.pallas_call(
        _ret_kernel,
        grid=(b, nch),
        in_specs=[_const_spec(dl.shape), fwd(hw), fwd(hw), fwd(vw), bwd(hw), bwd(hw), bwd(vw)],
        out_specs=[fwd(vw), bwd(vw)],
        out_shape=[jax.ShapeDtypeStruct((b, n, vw), F32)] * 2,
        scratch_shapes=[pltpu.VMEM((2, hw, RET_DV), F32)],
        compiler_params=_params(("arbitrary", "arbitrary")),
    )(dl, rq, rk, rv, rq, rk, rv)


def _block_start(i, size):
    return i & ~(size - 1)


def _gla_levels():
    half = GLA_CHUNK // 2
    out = []
    while half >= GLA_BASE:
        out.append(half)
        half //= 2
    return out


def _gla_kernel(qf, kf, vf, laf, qb, kb, vb, lab, of_ref, ob_ref, st_ref):
    c = GLA_CHUNK
    hw = GLA_HEADS * GLA_DK
    levels = _gla_levels()

    @pl.when(pl.program_id(1) == 0)
    def _():
        st_ref[...] = jnp.zeros_like(st_ref)

    ii = lax.broadcasted_iota(jnp.int32, (c, c), 0)
    jj = lax.broadcasted_iota(jnp.int32, (c, c), 1)
    pos = lax.broadcasted_iota(jnp.int32, (c, 1), 0)
    lane_head = lax.broadcasted_iota(jnp.int32, (c, hw), 1) // GLA_DK
    for d, (q_ref, k_ref, v_ref, la_ref, o_ref) in enumerate(((qf, kf, vf, laf, of_ref),
                                                               (qb, kb, vb, lab, ob_ref))):
        rev = d == 1
        q, k, v, la = q_ref[0].astype(F32), k_ref[0].astype(F32), v_ref[0], la_ref[0]

        b = _dot(((jj >= ii) if rev else (jj <= ii)).astype(F32), la, HI)
        b_end = b[0:1] if rev else b[c - 1:c]

        def at_row(size, idx):
            b3 = b.reshape(c // size, size, hw)
            return jnp.broadcast_to(b3[:, idx:idx + 1, :], b3.shape).reshape(c, hw)

        q_t = (q * jnp.exp(b)).astype(BF16)
        k_tt = (k * jnp.exp(b_end - b)).T
        dec_col = jnp.exp(jnp.broadcast_to(b_end, (LANES, hw)).T[:, 0:1])

        terms = []
        for half in levels:
            e = at_row(2 * half, half if rev else half - 1)
            inner = pos & (2 * half - 1)
            late = (inner < half) if rev else (inner >= half)
            qh = jnp.where(late, q * jnp.exp(jnp.where(late, b - e, 0.0)), 0.0)
            kh = jnp.where(late, 0.0, k * jnp.exp(jnp.where(late, 0.0, e - b)))
            terms.append((qh.astype(BF16), kh.astype(BF16), _block_start(ii, 2 * half) == _block_start(jj, 2 * half)))
        r = at_row(GLA_BASE, GLA_BASE - 1 if rev else 0)
        tri = (jj >= ii) if rev else (jj <= ii)
        terms.append(((q * jnp.exp(b - r)).astype(BF16), (k * jnp.exp(r - b)).astype(BF16),
                      (_block_start(ii, GLA_BASE) == _block_start(jj, GLA_BASE)) & tri))

        st = st_ref[d]
        st_b = st.astype(BF16)
        for h in range(GLA_HEADS):
            rows = slice(h * GLA_DK, (h + 1) * GLA_DK)
            cols = slice(h * GLA_DV, (h + 1) * GLA_DV)
            sel = lane_head == h
            a = jnp.zeros((c, c), F32)
            for qh, kh, mask in terms:
                a = a + jnp.where(mask, _dot_nt(jnp.where(sel, qh, jnp.zeros_like(qh)), kh), 0.0)
            vh = v[:, cols]
            o_ref[0, :, cols] = _dot(a.astype(BF16), vh) + _dot(jnp.where(sel, q_t, jnp.zeros_like(q_t)), st_b)
            st_ref[d, rows, :] = st[rows] * dec_col[rows] + _dot(k_tt[rows].astype(BF16), vh)


def _gla(gq, gk, gv, la, nct_tokens):
    b, n, hw = gq.shape
    c = GLA_CHUNK
    nch = n // c
    nct = nct_tokens // c
    vw = gv.shape[-1]
    fwd = lambda w, col=0: pl.BlockSpec((1, c, w), lambda i, s: (i, s, col))
    bwd = lambda w, col=0: pl.BlockSpec((1, c, w), lambda i, s: (i, _bwd_chunk(s, nct, nch), col))
    return pl.pallas_call(
        _gla_kernel,
        grid=(b, nch),
        in_specs=[fwd(hw), fwd(hw), fwd(vw), fwd(hw, 0), bwd(hw), bwd(hw), bwd(vw), bwd(hw, 1)],
        out_specs=[fwd(vw), bwd(vw)],
        out_shape=[jax.ShapeDtypeStruct((b, n, vw), F32)] * 2,
        scratch_shapes=[pltpu.VMEM((2, hw, GLA_DV), F32)],
        compiler_params=_params(("arbitrary", "arbitrary")),
    )(gq, gk, gv, la, gq, gk, gv, la)


def _head_norm(o, w, center):
    parts = []
    for h in range(o.shape[-1] // LANES):
        oh = o[:, h * LANES:(h + 1) * LANES]
        if center:
            oh = oh - jnp.mean(oh, axis=-1, keepdims=True)
        parts.append(oh * lax.rsqrt(jnp.mean(oh * oh, axis=-1, keepdims=True) + EPS))
    return jnp.concatenate(parts, axis=-1) * w


def _merge_kernel(nct, off, nx, *refs):
    x_refs = refs[:nx]
    (mod_ref, nw_ref, oa_ref, rof_ref, rob_ref, rg_ref, gof_ref, gob_ref, gr_ref, wbg_ref, wbr_ref,
     wout_ref, rnw_ref, gnw_ref, o_ref) = refs[nx:]
    d = D_MODEL
    is_ctx = pl.program_id(1) + off < nct
    row = _mod_rows(mod_ref, is_ctx)
    sh, sc, g1 = row[:, 0:d], row[:, d:2 * d], row[:, 2 * d:3 * d]
    x = _seq_tile(x_refs, is_ctx)
    u = (_rms(x, nw_ref[...]) * (1.0 + sc) + sh).astype(BF16)
    gate = _sigmoid(_dot(u, wbg_ref[...]))
    yb = _silu(rg_ref[0].astype(F32)) * _head_norm(rof_ref[0] + rob_ref[0], rnw_ref[...], True)
    yc = _silu(gr_ref[0].astype(F32)) * _head_norm(gof_ref[0] + gob_ref[0], gnw_ref[...], False)
    z = (gate[:, 0:d] * _dot(oa_ref[0], wbr_ref[0])
         + gate[:, d:2 * d] * _dot(yb.astype(BF16), wbr_ref[1])
         + gate[:, 2 * d:3 * d] * _dot(yc.astype(BF16), wbr_ref[2]))
    o_ref[0] = x + g1 * _dot(z.astype(BF16), wout_ref[...])


def _merge(xs, modsel, nw, oa, rof, rob, rg, gof, gob, gr, wbg, wbr, wout, rnw, gnw, nct, latent_only):
    tm = TOKEN_TILE
    off = nct if latent_only else 0
    x_specs, x_ops = _seq_inputs(xs, nct, off)
    b, d = x_ops[0].shape[0], x_ops[0].shape[-1]
    nt = sum(a.shape[1] for a in x_ops) // tm - off
    tok = lambda w: pl.BlockSpec((1, tm, w), lambda i, t: (i, t + off, 0))
    return pl.pallas_call(
        functools.partial(_merge_kernel, nct, off, len(x_ops)),
        grid=(b, nt),
        in_specs=x_specs + [pl.BlockSpec((1, 2, 6 * d), lambda i, t: (i, 0, 0)), _const_spec(nw.shape),
                  tok(512), tok(512), tok(512), tok(512), tok(512), tok(512), tok(512),
                  _const_spec(wbg.shape), _const_spec(wbr.shape), _const_spec(wout.shape),
                  _const_spec(rnw.shape), _const_spec(gnw.shape)],
        out_specs=pl.BlockSpec((1, tm, d), lambda i, t: (i, t, 0)),
        out_shape=jax.ShapeDtypeStruct((b, nt * tm, d), F32),
        compiler_params=_params(("arbitrary", "arbitrary")),
    )(*x_ops, modsel, nw, oa, rof, rob, rg, gof, gob, gr, wbg, wbr, wout, rnw, gnw)


def _norm2(x, row, nw_ref):
    d = D_MODEL
    sh, sc = row[:, 3 * d:4 * d], row[:, 4 * d:5 * d]
    return _rms(x, nw_ref[...]) * (1.0 + sc) + sh


def _finish(x, y, row, final, fnw_ref):
    d = D_MODEL
    out = x + row[:, 5 * d:6 * d] * y
    if final:
        out = _rms(out, fnw_ref[...])
    return out


def _ffn_kernel(nct, off, final, x_ref, mod_ref, nw_ref, wg_ref, wu_ref, wd_ref, fnw_ref, o_ref):
    row = _mod_rows(mod_ref, pl.program_id(1) + off < nct)
    x = x_ref[0]
    v = _norm2(x, row, nw_ref).astype(BF16)
    hdn = _silu(_dot(v, wg_ref[...])) * _dot(v, wu_ref[...])
    o_ref[0] = _finish(x, _dot(hdn.astype(BF16), wd_ref[...]), row, final, fnw_ref)


def _ffn(xs, modsel, nw, wg, wu, wd, fnw, nct, final):
    b, n, d = xs.shape
    tm = TOKEN_TILE
    off = nct if final else 0
    nt = n // tm - off
    return pl.pallas_call(
        functools.partial(_ffn_kernel, nct, off, final),
        grid=(b, nt),
        in_specs=[pl.BlockSpec((1, tm, d), lambda i, t: (i, t + off, 0)),
                  pl.BlockSpec((1, 2, 6 * d), lambda i, t: (i, 0, 0)), _const_spec(nw.shape),
                  _const_spec(wg.shape), _const_spec(wu.shape), _const_spec(wd.shape),
                  _const_spec(fnw.shape)],
        out_specs=pl.BlockSpec((1, tm, d), lambda i, t: (i, t, 0)),
        out_shape=jax.ShapeDtypeStruct((b, nt * tm, d), F32),
        compiler_params=_params(("arbitrary", "arbitrary")),
    )(xs, modsel, nw, wg, wu, wd, fnw)


def _top2_gates(logits):
    lane = lax.broadcasted_iota(jnp.int32, logits.shape, 1)
    m1 = jnp.max(logits, axis=-1, keepdims=True)
    i1 = jnp.min(jnp.where(logits == m1, lane, LANES), axis=-1, keepdims=True)
    rest = jnp.where(lane == i1, -jnp.inf, logits)
    m2 = jnp.max(rest, axis=-1, keepdims=True)
    i2 = jnp.min(jnp.where(rest == m2, lane, LANES), axis=-1, keepdims=True)
    e2 = jnp.exp(m2 - m1)
    w1 = 1.0 / (1.0 + e2)
    gates = jnp.where(lane == i1, w1, jnp.where(lane == i2, e2 * w1, 0.0))
    return gates, (lane == i1) | (lane == i2)


def _one_hot(cond):
    return jnp.where(cond, 1.0, 0.0).astype(BF16)


def _route_kernel(mrow, x_ref, mod_ref, nw_ref, rt_ref, v_out, gate_out, rank_out, rankt_out):
    v = _norm2(x_ref[0], mod_ref[0][mrow:mrow + 1], nw_ref)
    v_out[0] = v.astype(BF16)
    tt = v.shape[0]
    lane = lax.broadcasted_iota(jnp.int32, (tt, LANES), 1)
    logits = jnp.where(lane < N_EXPERTS, _dot(v, rt_ref[...], HI), -jnp.inf)
    gates, routed = _top2_gates(logits)
    gate_out[0] = gates
    ti = lax.broadcasted_iota(jnp.int32, (tt, tt), 0)
    tj = lax.broadcasted_iota(jnp.int32, (tt, tt), 1)
    rank_out[0] = jnp.where(routed, _dot(_one_hot(tj < ti), _one_hot(routed)), -1.0)
    routed_t = jnp.where(routed, 1.0, 0.0).T
    rank_t = jnp.where(routed_t > 0.5, _dot(routed_t.astype(BF16), _one_hot(ti < tj)), -1.0)
    rankt_out[0, 0] = rank_t[0:N_EXPERTS]


def _experts_kernel(mrow, final, v_ref, gate_ref, rank_ref, rankt_ref, wg_ref, wu_ref, wd_ref,
                    x_ref, mod_ref, fnw_ref, o_ref, acc_ref):
    e, s = pl.program_id(1), pl.program_id(2)
    tt = v_ref.shape[1]
    r = MOE_BLOCK
    fh = D_FF // 2
    tile0 = pl.multiple_of(s * tt, tt)
    lane = lax.broadcasted_iota(jnp.int32, (tt, LANES), 1)
    g_col = jnp.sum(jnp.where(lane == e, gate_ref[0], 0.0), axis=-1, keepdims=True)
    rk_col = jnp.sum(jnp.where(lane == e, rank_ref[0], 0.0), axis=-1, keepdims=True)
    rk_row = rankt_ref[0, 0, pl.ds(e, 1), :]
    n_routed = jnp.max(rk_col).astype(jnp.int32) + 1

    @pl.when(e == 0)
    def _():
        acc_ref[pl.ds(tile0, tt), :] = jnp.zeros((tt, D_MODEL), BF16)

    def block(i, carry):
        base = (i * r).astype(F32)
        slot_i = base + lax.broadcasted_iota(jnp.int32, (r, 1), 0).astype(F32)
        xg = _dot(_one_hot(rk_row == slot_i), v_ref[0]).astype(BF16)
        y = jnp.zeros((r, D_MODEL), F32)
        for c in range(2):
            cols = slice(c * fh, (c + 1) * fh)
            hdn = _silu(_dot(xg, wg_ref[0, :, cols])) * _dot(xg, wu_ref[0, :, cols])
            y = y + _dot(hdn.astype(BF16), wd_ref[0, cols, :])
        y = y.astype(BF16)
        slot_j = base + lax.broadcasted_iota(jnp.int32, (1, r), 1).astype(F32)
        for c in range(tt // TOKEN_TILE):
            rows = slice(c * TOKEN_TILE, (c + 1) * TOKEN_TILE)
            dst = pl.ds(tile0 + c * TOKEN_TILE, TOKEN_TILE)
            part = g_col[rows] * _dot(_one_hot(rk_col[rows] == slot_j), y)
            acc_ref[dst, :] = (acc_ref[dst, :].astype(F32) + part).astype(BF16)
        return carry

    lax.fori_loop(0, (n_routed + r - 1) // r, block, 0)

    @pl.when(e == N_EXPERTS - 1)
    def _():
        y_sum = acc_ref[pl.ds(tile0, tt), :].astype(F32)
        o_ref[0] = _finish(x_ref[0], y_sum, mod_ref[0][mrow:mrow + 1], final, fnw_ref)


def _moe(slab, mrow, modsel, nw, rt, wg, wu, wd, fnw, final):
    b, m, d = slab.shape
    tt = min(MOE_TILE, m)
    assert m % tt == 0 and tt % TOKEN_TILE == 0
    ns = m // tt
    ne, _, f = wg.shape
    mod_spec = lambda nd: pl.BlockSpec((1, 2, 6 * d), (lambda i, s: (i, 0, 0)) if nd == 2 else
                                       (lambda i, e, s: (i, 0, 0)))
    v, gates, rank, rank_t = pl.pallas_call(
        functools.partial(_route_kernel, mrow),
        grid=(b, ns),
        in_specs=[pl.BlockSpec((1, tt, d), lambda i, s: (i, s, 0)), mod_spec(2), _const_spec(nw.shape),
                  _const_spec(rt.shape)],
        out_specs=[pl.BlockSpec((1, tt, d), lambda i, s: (i, s, 0)),
                   pl.BlockSpec((1, tt, LANES), lambda i, s: (i, s, 0)),
                   pl.BlockSpec((1, tt, LANES), lambda i, s: (i, s, 0)),
                   pl.BlockSpec((1, 1, ne, tt), lambda i, s: (i, s, 0, 0))],
        out_shape=[jax.ShapeDtypeStruct((b, m, d), BF16), jax.ShapeDtypeStruct((b, m, LANES), F32),
                   jax.ShapeDtypeStruct((b, m, LANES), F32), jax.ShapeDtypeStruct((b, ns, ne, tt), F32)],
        compiler_params=_params(("arbitrary", "arbitrary")),
    )(slab, modsel, nw, rt)

    tok = lambda w: pl.BlockSpec((1, tt, w), lambda i, e, s: (i, s, 0))
    wspec = lambda shape: pl.BlockSpec((1,) + shape, lambda i, e, s: (e, 0, 0), pipeline_mode=pl.Buffered(1))
    last_pass = lambda i, e, s: (i, jnp.where(e == ne - 1, s, 0), 0)
    return pl.pallas_call(
        functools.partial(_experts_kernel, mrow, final),
        grid=(b, ne, ns),
        in_specs=[tok(d), tok(LANES), tok(LANES),
                  pl.BlockSpec((1, 1, ne, tt), lambda i, e, s: (i, s, 0, 0)),
                  wspec((d, f)), wspec((d, f)), wspec((f, d)),
                  pl.BlockSpec((1, tt, d), last_pass, pipeline_mode=pl.Buffered(1)),
                  mod_spec(3), _const_spec(fnw.shape)],
        out_specs=pl.BlockSpec((1, tt, d), last_pass),
        out_shape=jax.ShapeDtypeStruct((b, m, d), F32),
        scratch_shapes=[pltpu.VMEM((m, d), BF16)],
        compiler_params=_params(("arbitrary", "arbitrary", "arbitrary")),
    )(v, gates, rank, rank_t, wg, wu, wd, slab, modsel, fnw)


def _axial_perm():
    dd = np.arange(MLA_ROPE)
    return np.where(dd % 16 < 8, dd + 8, dd - 8)


def _ret_perm():
    dd = np.arange(RET_HEADS * RET_DK)
    return np.where(dd % RET_DK < RET_DK // 2, dd + RET_DK // 2, dd - RET_DK // 2)


def _prep_w_in(w):
    offs = np.cumsum((0,) + IN_SPLITS)
    cq, ckv, kr, rq, rk, rv, rg, gq, gk, gv, gr, gz, bg = [w[:, offs[i]:offs[i + 1]] for i in range(13)]
    zeros = lambda n: jnp.zeros((w.shape[0], n), w.dtype)
    rk = rk * (RET_DK ** -0.5)
    gq = gq * (GLA_DK ** -0.5)
    w1 = jnp.concatenate([
        cq, ckv,
        zeros(MLA_NOPE), kr, zeros(HEAD_SLOT - MLA_NOPE - MLA_ROPE),
        zeros(MLA_NOPE), kr[:, _axial_perm()], zeros(HEAD_SLOT - MLA_NOPE - MLA_ROPE),
        gz, zeros(LANES - 2 * GLA_GATE_RANK),
        rq, rq[:, _ret_perm()], rk, rk[:, _ret_perm()], rv, rg,
        gq, gk, gv, gr], axis=1)
    return w1.astype(BF16), bg.astype(BF16)


def _prep_mla(w_uq, w_ukv):
    r = w_uq.shape[0]
    scale = (MLA_NOPE + MLA_ROPE) ** -0.5 * np.log2(np.e)
    wq3 = w_uq.reshape(r, MLA_HEADS, MLA_NOPE + MLA_ROPE) * scale
    pad = jnp.zeros((r, MLA_HEADS, HEAD_SLOT - MLA_NOPE - MLA_ROPE), w_uq.dtype)
    wq = jnp.concatenate([wq3, pad], axis=-1)
    wqs = jnp.concatenate([jnp.zeros((r, MLA_HEADS, MLA_NOPE), w_uq.dtype),
                           wq3[:, :, MLA_NOPE:][:, :, _axial_perm()], pad], axis=-1)
    rk = w_ukv.shape[0]
    wkv3 = w_ukv.reshape(rk, MLA_HEADS, MLA_NOPE + MLA_V)
    wk = jnp.concatenate([wkv3[:, :, :MLA_NOPE],
                          jnp.zeros((rk, MLA_HEADS, HEAD_SLOT - MLA_NOPE), w_ukv.dtype)], axis=-1)
    wv = wkv3[:, :, MLA_NOPE:]
    flat = lambda a: a.reshape(a.shape[0], -1).astype(BF16)
    return flat(wq), flat(wqs), flat(wk), flat(wv)


def _prep_gla_gate(w_gate, b_gate):
    hw = GLA_HEADS * GLA_DK
    wg2 = jnp.zeros((LANES, 2 * hw), F32)
    wg2 = wg2.at[0:GLA_GATE_RANK, 0:hw].set(w_gate[0])
    wg2 = wg2.at[GLA_GATE_RANK:2 * GLA_GATE_RANK, hw:2 * hw].set(w_gate[1])
    return wg2, b_gate.reshape(1, 2 * hw)


def _rope_tables(n_ctx, seq):
    n_rows = seq // GRID_W
    half = MLA_ROPE // 4
    inv = ROPE_BASE ** (-jnp.arange(half, dtype=F32) / half)
    ang_r = jnp.arange(n_rows, dtype=F32)[:, None] * inv[None, :]
    ang_c = jnp.arange(GRID_W, dtype=F32)[:, None] * inv[None, :]
    by_row = lambda a: jnp.broadcast_to(a[:, None, :], (n_rows, GRID_W, half)).reshape(seq, half)
    by_col = lambda a: jnp.broadcast_to(a[None, :, :], (n_rows, GRID_W, half)).reshape(seq, half)
    cos_r, sin_r, cos_c, sin_c = by_row(jnp.cos(ang_r)), by_row(jnp.sin(ang_r)), by_col(jnp.cos(ang_c)), by_col(jnp.sin(ang_c))
    tail = HEAD_SLOT - MLA_NOPE - MLA_ROPE
    cosq = jnp.concatenate([jnp.ones((seq, MLA_NOPE), F32), cos_r, cos_r, cos_c, cos_c, jnp.ones((seq, tail), F32)], axis=1)
    sinq = jnp.concatenate([jnp.zeros((seq, MLA_NOPE), F32), -sin_r, sin_r, -sin_c, sin_c, jnp.zeros((seq, tail), F32)], axis=1)
    rhalf = RET_DK // 2
    rinv = ROPE_BASE ** (-jnp.arange(rhalf, dtype=F32) / rhalf)
    rang = jnp.arange(seq, dtype=F32)[:, None] * rinv[None, :]
    cos1, sin1 = jnp.cos(rang), jnp.sin(rang)
    cosr = jnp.concatenate([cos1, cos1] * RET_HEADS, axis=1)
    sinr = jnp.concatenate([-sin1, sin1] * RET_HEADS, axis=1)
    ctx = lambda a, fill: jnp.concatenate([jnp.full((n_ctx, a.shape[1]), fill, F32), a], axis=0)
    return ctx(cosq, 1.0), ctx(sinq, 0.0), ctx(cosr, 1.0), ctx(sinr, 0.0)


def kernel(x, c, ctx, c_ctx, mod_w, mod_b, norm1_w, norm2_w, w_in, mla_q_norm, mla_w_uq, mla_kv_norm, mla_w_ukv, ret_decay_logit, ret_norm_w, gla_w_gate, gla_b_gate, gla_norm_w, w_branch, w_out, ffn_w_gate, ffn_w_up, ffn_w_down, moe_router, moe_w_gate, moe_w_up, moe_w_down, final_norm_w):
    b, seq, d = x.shape
    n_ctx = ctx.shape[1]
    depth = mod_w.shape[0]
    assert d == D_MODEL and seq % GRID_W == 0
    assert n_ctx % TOKEN_TILE == 0 and n_ctx % RET_CHUNK == 0 and n_ctx % GLA_CHUNK == 0
    assert seq % TOKEN_TILE == 0 and seq % RET_CHUNK == 0
    nct = n_ctx // TOKEN_TILE

    rows = 8 * ((b + 1 + 7) // 8)
    cvec = jnp.zeros((rows, d), F32).at[0:b].set(c).at[b].set(c_ctx)
    mod = _modulation(cvec, mod_w, mod_b)
    cosq, sinq, cosr, sinr = _rope_tables(n_ctx, seq)
    fnw = final_norm_w.reshape(1, d)
    row2 = lambda a: a.reshape(1, -1)

    xs = (ctx, x)
    for l in range(depth):
        last = l == depth - 1
        modsel = jnp.stack([jnp.broadcast_to(mod[l, b], (b, 6 * d)), mod[l, 0:b]], axis=1)
        w1, wbg = _prep_w_in(w_in[l])
        wq, wqs, wk, wv = _prep_mla(mla_w_uq[l], mla_w_ukv[l])
        wg2, bg2 = _prep_gla_gate(gla_w_gate[l], gla_b_gate[l])
        nw1, nw2 = row2(norm1_w[l]), row2(norm2_w[l])

        (q, k, v, rq, rk, rv, rg, gq, gk, gv, gr, la) = _inproj(
            xs, modsel, nw1, w1, row2(mla_q_norm[l]), wq, wqs, row2(mla_kv_norm[l]), wk, wv,
            wg2, bg2, cosq, sinq, cosr, sinr, nct)
        oa = _attention(q, k, v, nct)
        dl = jnp.broadcast_to(ret_decay_logit[l][:, :, None, None], (2, RET_HEADS, 8, LANES))
        rof, rob = _retention(dl, rq, rk, rv, n_ctx // RET_CHUNK)
        gof, gob = _gla(gq, gk, gv, la, n_ctx)
        is_moe = l % 2 == 1
        xs = _merge(xs, modsel, nw1, oa, rof, rob, rg, gof, gob, gr, wbg,
                    w_branch[l].astype(BF16), w_out[l].astype(BF16),
                    row2(ret_norm_w[l]), row2(gla_norm_w[l]), nct, is_moe and last)
        i = l // 2
        if not is_moe:
            xs = _ffn(xs, modsel, nw2, ffn_w_gate[i].astype(BF16), ffn_w_up[i].astype(BF16),
                      ffn_w_down[i].astype(BF16), fnw, nct, last)
            continue
        rt = jnp.zeros((d, LANES), F32).at[:, 0:N_EXPERTS].set(moe_router[i])
        experts = (moe_w_gate[i].astype(BF16), moe_w_up[i].astype(BF16), moe_w_down[i].astype(BF16))
        if last:
            xs = _moe(xs, 1, modsel, nw2, rt, *experts, fnw, True)
        else:
            xs = jnp.concatenate([_moe(xs[:, :n_ctx], 0, modsel, nw2, rt, *experts, fnw, False),
                                  _moe(xs[:, n_ctx:], 1, modsel, nw2, rt, *experts, fnw, False)], axis=1)
    return xs
```

```python
import functools

import numpy as np
import jax
import jax.numpy as jnp
from jax import lax
from jax.experimental import pallas as pl
from jax.experimental.pallas import tpu as pltpu

D_MODEL = 1024
GRID_W = 64
EPS = 1e-6
ROPE_BASE = 10000.0

MLA_HEADS = 8
MLA_NOPE = 64
MLA_ROPE = 32
MLA_V = 64
MLA_Q_RANK = 256
MLA_KV_RANK = 128

RET_HEADS = 4
RET_DK = 64
RET_DV = 128

GLA_HEADS = 4
GLA_DK = 64
GLA_DV = 128
GLA_GATE_RANK = 16
GLA_TAU = 16.0

N_BRANCH = 3
BRANCH_W = 512
D_FF = 2816
N_EXPERTS = 8

IN_SPLITS = (
    MLA_Q_RANK, MLA_KV_RANK, MLA_ROPE,
    RET_HEADS * RET_DK, RET_HEADS * RET_DK, RET_HEADS * RET_DV, RET_HEADS * RET_DV,
    GLA_HEADS * GLA_DK, GLA_HEADS * GLA_DK, GLA_HEADS * GLA_DV, GLA_HEADS * GLA_DV,
    2 * GLA_GATE_RANK,
    N_BRANCH * D_MODEL,
)

F32 = jnp.float32
BF16 = jnp.bfloat16
HI = lax.Precision.HIGHEST

LANES = 128
HEAD_SLOT = 128
TOKEN_TILE = 256
ATT_DEPTH = 3
ATT_UNROLL = 10
ATT_SUM_ROWS = 16
RET_CHUNK = 256
GLA_CHUNK = 256
GLA_BASE = 16
MOE_TILE = 1024
MOE_BLOCK = 288
MOD_COLS_TILE = 1536
VMEM_LIMIT = 56 * 1024 * 1024


def _silu(x):
    return x / (1.0 + jnp.exp(-x))


def _sigmoid(x):
    return 1.0 / (1.0 + jnp.exp(-x))


def _log_sigmoid(z):
    return jnp.minimum(z, 0.0) - jnp.log(1.0 + jnp.exp(-jnp.abs(z)))


def _rms(x, w):
    return x * lax.rsqrt(jnp.mean(x * x, axis=-1, keepdims=True) + EPS) * w


def _dot(a, b, precision=None):
    return jnp.dot(a, b, preferred_element_type=F32, precision=precision)


def _dot_nt(a, b):
    return lax.dot_general(a, b, (((1,), (1,)), ((), ())), preferred_element_type=F32)


def _params(sem):
    return pltpu.CompilerParams(dimension_semantics=sem, vmem_limit_bytes=VMEM_LIMIT)


def _const_spec(shape):
    nd = len(shape)
    return pl.BlockSpec(shape, lambda *_: (0,) * nd, pipeline_mode=pl.Buffered(1))


def _mod_rows(mod_ref, is_ctx):
    m = mod_ref[0]
    return jnp.where(is_ctx, m[0:1], m[1:2])


def _mod_kernel(c_ref, w_ref, b_ref, o_ref):
    o_ref[0] = _dot(_silu(c_ref[...]), w_ref[0], HI) + b_ref[0]


def _modulation(cvec, mod_w, mod_b):
    depth, d, n6 = mod_w.shape
    rows = cvec.shape[0]
    tn = MOD_COLS_TILE
    return pl.pallas_call(
        _mod_kernel,
        grid=(depth, n6 // tn),
        in_specs=[pl.BlockSpec((rows, d), lambda l, j: (0, 0)),
                  pl.BlockSpec((1, d, tn), lambda l, j: (l, 0, j)),
                  pl.BlockSpec((1, 1, tn), lambda l, j: (l, 0, j))],
        out_specs=pl.BlockSpec((1, rows, tn), lambda l, j: (l, 0, j)),
        out_shape=jax.ShapeDtypeStruct((depth, rows, n6), F32),
        compiler_params=_params(("arbitrary", "arbitrary")),
    )(cvec, mod_w, mod_b.reshape(depth, 1, n6))


_C_CQ = (0, 256)
_C_CKV_KRA = (256, 512)
_C_KRB_GZ = (512, 768)
_C_RQ = (768, 1024)
_C_RQS = (1024, 1280)
_C_RK = (1280, 1536)
_C_RKS = (1536, 1792)
_C_RV = (1792, 2304)
_C_RG = (2304, 2816)
_C_GQ = (2816, 3072)
_C_GK = (3072, 3328)
_C_GV = (3328, 3840)
_C_GR = (3840, 4352)


def _seq_inputs(xs, nct, off=0):
    tm = TOKEN_TILE
    if not isinstance(xs, tuple):
        return [pl.BlockSpec((1, tm, xs.shape[-1]), lambda i, t: (i, t + off, 0))], [xs]
    d = xs[0].shape[-1]
    return ([pl.BlockSpec((1, tm, d), lambda i, t: (i, jnp.minimum(t + off, nct - 1), 0)),
             pl.BlockSpec((1, tm, d), lambda i, t: (i, jnp.maximum(t + off - nct, 0), 0))], list(xs))


def _seq_tile(x_refs, is_ctx):
    if len(x_refs) == 1:
        return x_refs[0][0]
    return jnp.where(is_ctx, x_refs[0][0], x_refs[1][0])


def _inproj_kernel(nct, nx, *refs):
    x_refs = refs[:nx]
    (mod_ref, nw_ref, w_ref, qn_ref, wq_ref, wqs_ref, kvn_ref, wk_ref, wv_ref, wg2_ref, bg2_ref,
     cq_ref, sq_ref, cr_ref, sr_ref, q_out, k_out, v_out, rq_out, rk_out, rv_out, rg_out,
     gq_out, gk_out, gv_out, gr_out, la_out) = refs[nx:]
    d = D_MODEL
    is_ctx = pl.program_id(1) < nct
    row = _mod_rows(mod_ref, is_ctx)
    sh, sc = row[:, 0:d], row[:, d:2 * d]
    u = (_rms(_seq_tile(x_refs, is_ctx), nw_ref[...]) * (1.0 + sc) + sh).astype(BF16)

    def seg(c):
        return _dot(u, w_ref[:, c[0]:c[1]])

    cqn = _rms(seg(_C_CQ), qn_ref[...]).astype(BF16)
    q_all = _dot(cqn, wq_ref[...])
    q_swp = _dot(cqn, wqs_ref[...])
    ckv_kra = seg(_C_CKV_KRA)
    krb_gz = seg(_C_KRB_GZ)
    ckvn = _rms(ckv_kra[:, 0:LANES], kvn_ref[...]).astype(BF16)
    k_all = _dot(ckvn, wk_ref[...])
    v_out[0, 0] = _dot(ckvn, wv_ref[...]).T.astype(BF16)
    cosq, sinq = cq_ref[...], sq_ref[...]
    k_rope = ckv_kra[:, LANES:] * cosq + krb_gz[:, 0:LANES] * sinq
    for h in range(MLA_HEADS):
        sl = slice(h * HEAD_SLOT, (h + 1) * HEAD_SLOT)
        q_out[0, h, 0] = (q_all[:, sl] * cosq + q_swp[:, sl] * sinq).T.astype(BF16)
        k_out[0, h] = (k_all[:, sl] + k_rope).astype(BF16)

    cosr, sinr = cr_ref[...], sr_ref[...]
    rq_out[0] = (seg(_C_RQ) * cosr + seg(_C_RQS) * sinr).astype(BF16)
    rk_out[0] = (seg(_C_RK) * cosr + seg(_C_RKS) * sinr).astype(BF16)
    rv_out[0] = seg(_C_RV).astype(BF16)
    rg_out[0] = seg(_C_RG).astype(BF16)

    gq_out[0] = seg(_C_GQ).astype(BF16)
    gk_out[0] = seg(_C_GK).astype(BF16)
    gv_out[0] = seg(_C_GV).astype(BF16)
    gr_out[0] = seg(_C_GR).astype(BF16)
    z = _dot(krb_gz[:, LANES:], wg2_ref[...], HI) + bg2_ref[...]
    la_out[0] = _log_sigmoid(z) * (1.0 / GLA_TAU)


def _inproj(xs, modsel, nw, w1, qn, wq, wqs, kvn, wk, wv, wg2, bg2, cosq, sinq, cosr, sinr, nct):
    x_specs, x_ops = _seq_inputs(xs, nct)
    b, d = x_ops[0].shape[0], x_ops[0].shape[-1]
    n = sum(a.shape[1] for a in x_ops)
    tm = TOKEN_TILE
    nt = n // tm
    h = MLA_HEADS
    tok = lambda w: pl.BlockSpec((1, tm, w), lambda i, t: (i, t, 0))
    tab = lambda w: pl.BlockSpec((tm, w), lambda i, t: (t, 0))
    q_spec = pl.BlockSpec((1, h, 1, HEAD_SLOT, tm), lambda i, t: (i, 0, t, 0, 0))
    k_spec = pl.BlockSpec((1, h, tm, HEAD_SLOT), lambda i, t: (i, 0, t, 0))
    v_spec = pl.BlockSpec((1, 1, h * MLA_V, tm), lambda i, t: (i, t, 0, 0))
    sds = lambda w, dt=BF16: jax.ShapeDtypeStruct((b, n, w), dt)
    q_sds = jax.ShapeDtypeStruct((b, h, nt, HEAD_SLOT, tm), BF16)
    k_sds = jax.ShapeDtypeStruct((b, h, n, HEAD_SLOT), BF16)
    v_sds = jax.ShapeDtypeStruct((b, nt, h * MLA_V, tm), BF16)
    return pl.pallas_call(
        functools.partial(_inproj_kernel, nct, len(x_ops)),
        grid=(b, nt),
        in_specs=x_specs + [pl.BlockSpec((1, 2, 6 * d), lambda i, t: (i, 0, 0)), _const_spec(nw.shape),
                  _const_spec(w1.shape), _const_spec(qn.shape), _const_spec(wq.shape),
                  _const_spec(wqs.shape), _const_spec(kvn.shape), _const_spec(wk.shape),
                  _const_spec(wv.shape), _const_spec(wg2.shape), _const_spec(bg2.shape),
                  tab(HEAD_SLOT), tab(HEAD_SLOT), tab(256), tab(256)],
        out_specs=[q_spec, k_spec, v_spec, tok(256), tok(256), tok(512), tok(512),
                   tok(256), tok(256), tok(512), tok(512), tok(512)],
        out_shape=[q_sds, k_sds, v_sds, sds(256), sds(256), sds(512), sds(512),
                   sds(256), sds(256), sds(512), sds(512), sds(512, F32)],
        compiler_params=_params(("arbitrary", "arbitrary")),
    )(*x_ops, modsel, nw, w1, qn, wq, wqs, kvn, wk, wv, wg2, bg2, cosq, sinq, cosr, sinr)


def _attn_kernel(nct, n_tiles, qt_ref, k_ref, vt_ref, o_ref, *bufs):
    t = TOKEN_TILE
    nd = ATT_DEPTH
    heads = range(2)
    s_bufs = (bufs[0:nd], bufs[nd:2 * nd])
    p_bufs = (bufs[2 * nd:3 * nd], bufs[3 * nd:4 * nd])
    qts = [qt_ref[0, j, 0] for j in heads]
    ones = jnp.ones((ATT_SUM_ROWS, t), BF16)

    def scores(j, c, s_buf):
        s = _dot(k_ref[0, j, pl.ds(pl.multiple_of(c * t, t), t), :], qts[j])
        s_buf[...] = s
        return jnp.max(s, axis=0, keepdims=True)

    def softmax(s_buf, p_buf, m, cmax):
        m_new = jnp.maximum(m, cmax)
        p_buf[...] = jnp.exp2(s_buf[...] - m_new).astype(BF16)
        return m_new, jnp.exp2(m - m_new)

    def values(j, c, p_buf, alpha, acc):
        lhs = jnp.concatenate([vt_ref[0, c, j * MLA_V:(j + 1) * MLA_V, :], ones], axis=0)
        return alpha * acc + _dot(lhs, p_buf[...])

    m_init = jnp.full((1, t), -jnp.inf, F32)
    acc_init = jnp.zeros((MLA_V + ATT_SUM_ROWS, t), F32)

    def finish(accs):
        outs = [acc[0:MLA_V] / acc[MLA_V:MLA_V + 1] for acc in accs]
        o_ref[0] = jnp.concatenate(outs, axis=0).T.astype(BF16)

    @pl.when(pl.program_id(2) < nct)
    def _():
        accs = []
        for j in heads:
            m, acc = m_init, acc_init
            for c in range(nct):
                cmax = scores(j, c, s_bufs[j][0])
                m, alpha = softmax(s_bufs[j][0], p_bufs[j][0], m, cmax)
                acc = values(j, c, p_bufs[j][0], alpha, acc)
            accs.append(acc)
        finish(accs)

    def step(c, slot, state, lookahead=True):
        ahead = [scores(j, c + 2, s_bufs[j][(slot + 2) % nd]) if lookahead else None for j in heads]
        accs = [values(j, c - 1, p_bufs[j][(slot - 1) % nd], state[j][1], state[j][2]) for j in heads]
        stats = [softmax(s_bufs[j][slot], p_bufs[j][slot], state[j][0], state[j][3]) for j in heads]
        return tuple((stats[j][0], stats[j][1], accs[j], state[j][4], ahead[j]) for j in heads)

    @pl.when(pl.program_id(2) >= nct)
    def _():
        state = []
        for j in heads:
            cmax0 = scores(j, 0, s_bufs[j][0])
            cmax1 = scores(j, 1, s_bufs[j][1])
            m, alpha = softmax(s_bufs[j][0], p_bufs[j][0], m_init, cmax0)
            state.append((m, alpha, acc_init, cmax1, scores(j, 2, s_bufs[j][2])))

        def trip(i, state):
            for k in range(nd):
                state = step(1 + nd * i + k, (1 + k) % nd, state)
            return state

        state = lax.fori_loop(0, (n_tiles - 3) // nd, trip, tuple(state), unroll=ATT_UNROLL)
        for c in (n_tiles - 2, n_tiles - 1):
            state = step(c, c % nd, state, lookahead=False)
        finish([values(j, n_tiles - 1, p_bufs[j][(n_tiles - 1) % nd], state[j][1], state[j][2])
                for j in heads])


def _attention(qt, k, vt, nct):
    b, h, n_tiles, w, t = qt.shape
    n = n_tiles * t
    assert n_tiles >= 3 and (n_tiles - 3) % ATT_DEPTH == 0
    return pl.pallas_call(
        functools.partial(_attn_kernel, nct, n_tiles),
        grid=(b, h // 2, n_tiles),
        in_specs=[pl.BlockSpec((1, 2, 1, w, t), lambda i, p, q: (i, p, q, 0, 0)),
                  pl.BlockSpec((1, 2, n, w), lambda i, p, q: (i, p, 0, 0)),
                  pl.BlockSpec((1, n_tiles, 2 * MLA_V, t), lambda i, p, q: (i, 0, p, 0))],
        out_specs=pl.BlockSpec((1, t, 2 * MLA_V), lambda i, p, q: (i, q, p)),
        out_shape=jax.ShapeDtypeStruct((b, n, h * MLA_V), BF16),
        scratch_shapes=[pltpu.VMEM((t, t), F32)] * (2 * ATT_DEPTH) + [pltpu.VMEM((t, t), BF16)] * (2 * ATT_DEPTH),
        compiler_params=_params(("arbitrary", "arbitrary", "arbitrary")),
    )(qt, k, vt)


def _bwd_chunk(s, nct, nch):
    return jnp.where(s < nct, nct - 1 - s, nch - 1 - (s - nct))


def _ret_kernel(dl_ref, qf, kf, vf, qb, kb, vb, of_ref, ob_ref, st_ref):
    c = RET_CHUNK
    hw = RET_HEADS * RET_DK

    @pl.when(pl.program_id(1) == 0)
    def _():
        st_ref[...] = jnp.zeros_like(st_ref)

    ii = lax.broadcasted_iota(jnp.int32, (c, c), 0)
    jj = lax.broadcasted_iota(jnp.int32, (c, c), 1)
    pos_i = lax.broadcasted_iota(jnp.int32, (c, 1), 0).astype(F32)
    pos_j = lax.broadcasted_iota(jnp.int32, (1, c), 1).astype(F32)
    lane_head = lax.broadcasted_iota(jnp.int32, (c, hw), 1) // RET_DK
    for d, (q_ref, k_ref, v_ref, o_ref) in enumerate(((qf, kf, vf, of_ref), (qb, kb, vb, ob_ref))):
        rev = d == 1
        q, v = q_ref[0], v_ref[0]
        k_t = k_ref[0].astype(F32).T
        k_tb = k_t.astype(BF16)
        st = st_ref[d]
        st_b = st.astype(BF16)
        dist = ((jj - ii) if rev else (ii - jj)).astype(F32)
        for h in range(RET_HEADS):
            lg = _log_sigmoid(dl_ref[d, h])[0:1, 0:1]
            intra = jnp.where(dist >= 0.0, jnp.exp(lg * jnp.maximum(dist, 0.0)), 0.0)
            q_dec = jnp.exp(lg * ((c - pos_i) if rev else (pos_i + 1.0)))
            k_dec = jnp.exp(lg * (pos_j if rev else (c - 1.0 - pos_j)))
            rows = slice(h * RET_DK, (h + 1) * RET_DK)
            cols = slice(h * RET_DV, (h + 1) * RET_DV)
            qm = jnp.where(lane_head == h, q, jnp.zeros_like(q))
            scores = _dot(qm, k_tb) * intra
            o = _dot(scores.astype(BF16), v[:, cols]) + _dot(qm, st_b) * q_dec
            o_ref[0, :, cols] = o
            kd = (k_t[rows] * k_dec).astype(BF16)
            st_ref[d, rows, :] = st[rows] * jnp.exp(lg * float(c)) + _dot(kd, v[:, cols])


def _retention(dl, rq, rk, rv, nct):
    b, n, hw = rq.shape
    c = RET_CHUNK
    nch = n // c
    vw = rv.shape[-1]
    fwd = lambda w: pl.BlockSpec((1, c, w), lambda i, s: (i, s, 0))
    bwd = lambda w: pl.BlockSpec((1, c, w), lambda i, s: (i, _bwd_chunk(s, nct, nch), 0))
    return pl.pallas_call(
        _ret_kernel,
        grid=(b, nch),
        in_specs=[_const_spec(dl.shape), fwd(hw), fwd(hw), fwd(vw), bwd(hw), bwd(hw), bwd(vw)],
        out_specs=[fwd(vw), bwd(vw)],
        out_shape=[jax.ShapeDtypeStruct((b, n, vw), F32)] * 2,
        scratch_shapes=[pltpu.VMEM((2, hw, RET_DV), F32)],
        compiler_params=_params(("arbitrary", "arbitrary")),
    )(dl, rq, rk, rv, rq, rk, rv)


def _block_start(i, size):
    return i & ~(size - 1)


def _gla_levels():
    half = GLA_CHUNK // 2
    out = []
    while half >= GLA_BASE:
        out.append(half)
        half //= 2
    return out


def _gla_kernel(qf, kf, vf, laf, qb, kb, vb, lab, of_ref, ob_ref, st_ref):
    c = GLA_CHUNK
    hw = GLA_HEADS * GLA_DK
    levels = _gla_levels()

    @pl.when(pl.program_id(1) == 0)
    def _():
        st_ref[...] = jnp.zeros_like(st_ref)

    ii = lax.broadcasted_iota(jnp.int32, (c, c), 0)
    jj = lax.broadcasted_iota(jnp.int32, (c, c), 1)
    pos = lax.broadcasted_iota(jnp.int32, (c, 1), 0)
    lane_head = lax.broadcasted_iota(jnp.int32, (c, hw), 1) // GLA_DK
    for d, (q_ref, k_ref, v_ref, la_ref, o_ref) in enumerate(((qf, kf, vf, laf, of_ref),
                                                               (qb, kb, vb, lab, ob_ref))):
        rev = d == 1
        q, k, v, la = q_ref[0].astype(F32), k_ref[0].astype(F32), v_ref[0], la_ref[0]

        b = _dot(((jj >= ii) if rev else (jj <= ii)).astype(F32), la, HI)
        b_end = b[0:1] if rev else b[c - 1:c]

        def at_row(size, idx):
            b3 = b.reshape(c // size, size, hw)
            return jnp.broadcast_to(b3[:, idx:idx + 1, :], b3.shape).reshape(c, hw)

        q_t = (q * jnp.exp(b)).astype(BF16)
        k_tt = (k * jnp.exp(b_end - b)).T
        dec_col = jnp.exp(jnp.broadcast_to(b_end, (LANES, hw)).T[:, 0:1])

        terms = []
        for half in levels:
            e = at_row(2 * half, half if rev else half - 1)
            inner = pos & (2 * half - 1)
            late = (inner < half) if rev else (inner >= half)
            qh = jnp.where(late, q * jnp.exp(jnp.where(late, b - e, 0.0)), 0.0)
            kh = jnp.where(late, 0.0, k * jnp.exp(jnp.where(late, 0.0, e - b)))
            terms.append((qh.astype(BF16), kh.astype(BF16), _block_start(ii, 2 * half) == _block_start(jj, 2 * half)))
        r = at_row(GLA_BASE, GLA_BASE - 1 if rev else 0)
        tri = (jj >= ii) if rev else (jj <= ii)
        terms.append(((q * jnp.exp(b - r)).astype(BF16), (k * jnp.exp(r - b)).astype(BF16),
                      (_block_start(ii, GLA_BASE) == _block_start(jj, GLA_BASE)) & tri))

        st = st_ref[d]
        st_b = st.astype(BF16)
        for h in range(GLA_HEADS):
            rows = slice(h * GLA_DK, (h + 1) * GLA_DK)
            cols = slice(h * GLA_DV, (h + 1) * GLA_DV)
            sel = lane_head == h
            a = jnp.zeros((c, c), F32)
            for qh, kh, mask in terms:
                a = a + jnp.where(mask, _dot_nt(jnp.where(sel, qh, jnp.zeros_like(qh)), kh), 0.0)
            vh = v[:, cols]
            o_ref[0, :, cols] = _dot(a.astype(BF16), vh) + _dot(jnp.where(sel, q_t, jnp.zeros_like(q_t)), st_b)
            st_ref[d, rows, :] = st[rows] * dec_col[rows] + _dot(k_tt[rows].astype(BF16), vh)


def _gla(gq, gk, gv, la, nct_tokens):
    b, n, hw = gq.shape
    c = GLA_CHUNK
    nch = n // c
    nct = nct_tokens // c
    vw = gv.shape[-1]
    fwd = lambda w, col=0: pl.BlockSpec((1, c, w), lambda i, s: (i, s, col))
    bwd = lambda w, col=0: pl.BlockSpec((1, c, w), lambda i, s: (i, _bwd_chunk(s, nct, nch), col))
    return pl.pallas_call(
        _gla_kernel,
        grid=(b, nch),
        in_specs=[fwd(hw), fwd(hw), fwd(vw), fwd(hw, 0), bwd(hw), bwd(hw), bwd(vw), bwd(hw, 1)],
        out_specs=[fwd(vw), bwd(vw)],
        out_shape=[jax.ShapeDtypeStruct((b, n, vw), F32)] * 2,
        scratch_shapes=[pltpu.VMEM((2, hw, GLA_DV), F32)],
        compiler_params=_params(("arbitrary", "arbitrary")),
    )(gq, gk, gv, la, gq, gk, gv, la)


def _head_norm(o, w, center):
    parts = []
    for h in range(o.shape[-1] // LANES):
        oh = o[:, h * LANES:(h + 1) * LANES]
        if center:
            oh = oh - jnp.mean(oh, axis=-1, keepdims=True)
        parts.append(oh * lax.rsqrt(jnp.mean(oh * oh, axis=-1, keepdims=True) + EPS))
    return jnp.concatenate(parts, axis=-1) * w


def _merge_kernel(nct, off, nx, *refs):
    x_refs = refs[:nx]
    (mod_ref, nw_ref, oa_ref, rof_ref, rob_ref, rg_ref, gof_ref, gob_ref, gr_ref, wbg_ref, wbr_ref,
     wout_ref, rnw_ref, gnw_ref, o_ref) = refs[nx:]
    d = D_MODEL
    is_ctx = pl.program_id(1) + off < nct
    row = _mod_rows(mod_ref, is_ctx)
    sh, sc, g1 = row[:, 0:d], row[:, d:2 * d], row[:, 2 * d:3 * d]
    x = _seq_tile(x_refs, is_ctx)
    u = (_rms(x, nw_ref[...]) * (1.0 + sc) + sh).astype(BF16)
    gate = _sigmoid(_dot(u, wbg_ref[...]))
    yb = _silu(rg_ref[0].astype(F32)) * _head_norm(rof_ref[0] + rob_ref[0], rnw_ref[...], True)
    yc = _silu(gr_ref[0].astype(F32)) * _head_norm(gof_ref[0] + gob_ref[0], gnw_ref[...], False)
    z = (gate[:, 0:d] * _dot(oa_ref[0], wbr_ref[0])
         + gate[:, d:2 * d] * _dot(yb.astype(BF16), wbr_ref[1])
         + gate[:, 2 * d:3 * d] * _dot(yc.astype(BF16), wbr_ref[2]))
    o_ref[0] = x + g1 * _dot(z.astype(BF16), wout_ref[...])


def _merge(xs, modsel, nw, oa, rof, rob, rg, gof, gob, gr, wbg, wbr, wout, rnw, gnw, nct, latent_only):
    tm = TOKEN_TILE
    off = nct if latent_only else 0
    x_specs, x_ops = _seq_inputs(xs, nct, off)
    b, d = x_ops[0].shape[0], x_ops[0].shape[-1]
    nt = sum(a.shape[1] for a in x_ops) // tm - off
    tok = lambda w: pl.BlockSpec((1, tm, w), lambda i, t: (i, t + off, 0))
    return pl.pallas_call(
        functools.partial(_merge_kernel, nct, off, len(x_ops)),
        grid=(b, nt),
        in_specs=x_specs + [pl.BlockSpec((1, 2, 6 * d), lambda i, t: (i, 0, 0)), _const_spec(nw.shape),
                  tok(512), tok(512), tok(512), tok(512), tok(512), tok(512), tok(512),
                  _const_spec(wbg.shape), _const_spec(wbr.shape), _const_spec(wout.shape),
                  _const_spec(rnw.shape), _const_spec(gnw.shape)],
        out_specs=pl.BlockSpec((1, tm, d), lambda i, t: (i, t, 0)),
        out_shape=jax.ShapeDtypeStruct((b, nt * tm, d), F32),
        compiler_params=_params(("arbitrary", "arbitrary")),
    )(*x_ops, modsel, nw, oa, rof, rob, rg, gof, gob, gr, wbg, wbr, wout, rnw, gnw)


def _norm2(x, row, nw_ref):
    d = D_MODEL
    sh, sc = row[:, 3 * d:4 * d], row[:, 4 * d:5 * d]
    return _rms(x, nw_ref[...]) * (1.0 + sc) + sh


def _finish(x, y, row, final, fnw_ref):
    d = D_MODEL
    out = x + row[:, 5 * d:6 * d] * y
    if final:
        out = _rms(out, fnw_ref[...])
    return out


def _ffn_kernel(nct, off, final, x_ref, mod_ref, nw_ref, wg_ref, wu_ref, wd_ref, fnw_ref, o_ref):
    row = _mod_rows(mod_ref, pl.program_id(1) + off < nct)
    x = x_ref[0]
    v = _norm2(x, row, nw_ref).astype(BF16)
    hdn = _silu(_dot(v, wg_ref[...])) * _dot(v, wu_ref[...])
    o_ref[0] = _finish(x, _dot(hdn.astype(BF16), wd_ref[...]), row, final, fnw_ref)


def _ffn(xs, modsel, nw, wg, wu, wd, fnw, nct, final):
    b, n, d = xs.shape
    tm = TOKEN_TILE
    off = nct if final else 0
    nt = n // tm - off
    return pl.pallas_call(
        functools.partial(_ffn_kernel, nct, off, final),
        grid=(b, nt),
        in_specs=[pl.BlockSpec((1, tm, d), lambda i, t: (i, t + off, 0)),
                  pl.BlockSpec((1, 2, 6 * d), lambda i, t: (i, 0, 0)), _const_spec(nw.shape),
                  _const_spec(wg.shape), _const_spec(wu.shape), _const_spec(wd.shape),
                  _const_spec(fnw.shape)],
        out_specs=pl.BlockSpec((1, tm, d), lambda i, t: (i, t, 0)),
        out_shape=jax.ShapeDtypeStruct((b, nt * tm, d), F32),
        compiler_params=_params(("arbitrary", "arbitrary")),
    )(xs, modsel, nw, wg, wu, wd, fnw)


def _top2_gates(logits):
    lane = lax.broadcasted_iota(jnp.int32, logits.shape, 1)
    m1 = jnp.max(logits, axis=-1, keepdims=True)
    i1 = jnp.min(jnp.where(logits == m1, lane, LANES), axis=-1, keepdims=True)
    rest = jnp.where(lane == i1, -jnp.inf, logits)
    m2 = jnp.max(rest, axis=-1, keepdims=True)
    i2 = jnp.min(jnp.where(rest == m2, lane, LANES), axis=-1, keepdims=True)
    e2 = jnp.exp(m2 - m1)
    w1 = 1.0 / (1.0 + e2)
    gates = jnp.where(lane == i1, w1, jnp.where(lane == i2, e2 * w1, 0.0))
    return gates, (lane == i1) | (lane == i2)


def _one_hot(cond):
    return jnp.where(cond, 1.0, 0.0).astype(BF16)


def _route_kernel(mrow, x_ref, mod_ref, nw_ref, rt_ref, v_out, gate_out, rank_out, rankt_out):
    v = _norm2(x_ref[0], mod_ref[0][mrow:mrow + 1], nw_ref)
    v_out[0] = v.astype(BF16)
    tt = v.shape[0]
    lane = lax.broadcasted_iota(jnp.int32, (tt, LANES), 1)
    logits = jnp.where(lane < N_EXPERTS, _dot(v, rt_ref[...], HI), -jnp.inf)
    gates, routed = _top2_gates(logits)
    gate_out[0] = gates
    ti = lax.broadcasted_iota(jnp.int32, (tt, tt), 0)
    tj = lax.broadcasted_iota(jnp.int32, (tt, tt), 1)
    rank_out[0] = jnp.where(routed, _dot(_one_hot(tj < ti), _one_hot(routed)), -1.0)
    routed_t = jnp.where(routed, 1.0, 0.0).T
    rank_t = jnp.where(routed_t > 0.5, _dot(routed_t.astype(BF16), _one_hot(ti < tj)), -1.0)
    rankt_out[0, 0] = rank_t[0:N_EXPERTS]


def _experts_kernel(v_ref, gate_ref, rank_ref, rankt_ref, wg_ref, wu_ref, wd_ref, o_ref, acc_ref):
    e, s = pl.program_id(1), pl.program_id(2)
    tt = v_ref.shape[1]
    r = MOE_BLOCK
    fh = D_FF // 2
    tile0 = pl.multiple_of(s * tt, tt)
    lane = lax.broadcasted_iota(jnp.int32, (tt, LANES), 1)
    g_col = jnp.sum(jnp.where(lane == e, gate_ref[0], 0.0), axis=-1, keepdims=True)
    rk_col = jnp.sum(jnp.where(lane == e, rank_ref[0], 0.0), axis=-1, keepdims=True)
    rk_row = rankt_ref[0, 0, pl.ds(e, 1), :]
    n_routed = jnp.max(rk_col).astype(jnp.int32) + 1

    @pl.when(e == 0)
    def _():
        acc_ref[pl.ds(tile0, tt), :] = jnp.zeros((tt, D_MODEL), BF16)

    def block(i, carry):
        base = (i * r).astype(F32)
        slot_i = base + lax.broadcasted_iota(jnp.int32, (r, 1), 0).astype(F32)
        xg = _dot(_one_hot(rk_row == slot_i), v_ref[0]).astype(BF16)
        y = jnp.zeros((r, D_MODEL), F32)
        for c in range(2):
            cols = slice(c * fh, (c + 1) * fh)
            hdn = _silu(_dot(xg, wg_ref[0, :, cols])) * _dot(xg, wu_ref[0, :, cols])
            y = y + _dot(hdn.astype(BF16), wd_ref[0, cols, :])
        y = y.astype(BF16)
        slot_j = base + lax.broadcasted_iota(jnp.int32, (1, r), 1).astype(F32)
        for c in range(tt // TOKEN_TILE):
            rows = slice(c * TOKEN_TILE, (c + 1) * TOKEN_TILE)
            dst = pl.ds(tile0 + c * TOKEN_TILE, TOKEN_TILE)
            part = g_col[rows] * _dot(_one_hot(rk_col[rows] == slot_j), y)
            acc_ref[dst, :] = (acc_ref[dst, :].astype(F32) + part).astype(BF16)
        return carry

    lax.fori_loop(0, (n_routed + r - 1) // r, block, 0)

    @pl.when(e == N_EXPERTS - 1)
    def _():
        o_ref[0] = acc_ref[pl.ds(tile0, tt), :]


def _moe_finish_kernel(mrow, final, x_ref, y_ref, mod_ref, fnw_ref, o_ref):
    o_ref[0] = _finish(x_ref[0], y_ref[0].astype(F32), mod_ref[0][mrow:mrow + 1], final, fnw_ref)


def _moe(slab, mrow, modsel, nw, rt, wg, wu, wd, fnw, final):
    b, m, d = slab.shape
    tt = min(MOE_TILE, m)
    assert m % tt == 0 and tt % TOKEN_TILE == 0
    ns = m // tt
    ne, _, f = wg.shape
    mod_spec = lambda nd: pl.BlockSpec((1, 2, 6 * d), (lambda i, s: (i, 0, 0)) if nd == 2 else
                                       (lambda i, e, s: (i, 0, 0)))
    v, gates, rank, rank_t = pl.pallas_call(
        functools.partial(_route_kernel, mrow),
        grid=(b, ns),
        in_specs=[pl.BlockSpec((1, tt, d), lambda i, s: (i, s, 0)), mod_spec(2), _const_spec(nw.shape),
                  _const_spec(rt.shape)],
        out_specs=[pl.BlockSpec((1, tt, d), lambda i, s: (i, s, 0)),
                   pl.BlockSpec((1, tt, LANES), lambda i, s: (i, s, 0)),
                   pl.BlockSpec((1, tt, LANES), lambda i, s: (i, s, 0)),
                   pl.BlockSpec((1, 1, ne, tt), lambda i, s: (i, s, 0, 0))],
        out_shape=[jax.ShapeDtypeStruct((b, m, d), BF16), jax.ShapeDtypeStruct((b, m, LANES), F32),
                   jax.ShapeDtypeStruct((b, m, LANES), F32), jax.ShapeDtypeStruct((b, ns, ne, tt), F32)],
        compiler_params=_params(("arbitrary", "arbitrary")),
    )(slab, modsel, nw, rt)

    tok = lambda w: pl.BlockSpec((1, tt, w), lambda i, e, s: (i, s, 0))
    wspec = lambda shape: pl.BlockSpec((1,) + shape, lambda i, e, s: (e, 0, 0), pipeline_mode=pl.Buffered(1))
    last_pass = lambda i, e, s: (i, jnp.where(e == ne - 1, s, 0), 0)
    y_sum = pl.pallas_call(
        _experts_kernel,
        grid=(b, ne, ns),
        in_specs=[tok(d), tok(LANES), tok(LANES),
                  pl.BlockSpec((1, 1, ne, tt), lambda i, e, s: (i, s, 0, 0)),
                  wspec((d, f)), wspec((d, f)), wspec((f, d))],
        out_specs=pl.BlockSpec((1, tt, d), last_pass),
        out_shape=jax.ShapeDtypeStruct((b, m, d), BF16),
        scratch_shapes=[pltpu.VMEM((m, d), BF16)],
        compiler_params=_params(("arbitrary", "arbitrary", "arbitrary")),
    )(v, gates, rank, rank_t, wg, wu, wd)

    tm = TOKEN_TILE
    row_tile = lambda: pl.BlockSpec((1, tm, d), lambda i, t: (i, t, 0))
    return pl.pallas_call(
        functools.partial(_moe_finish_kernel, mrow, final),
        grid=(b, m // tm),
        in_specs=[row_tile(), row_tile(), mod_spec(2), _const_spec(fnw.shape)],
        out_specs=row_tile(),
        out_shape=jax.ShapeDtypeStruct((b, m, d), F32),
        compiler_params=_params(("arbitrary", "arbitrary")),
    )(slab, y_sum, modsel, fnw)


def _axial_perm():
    dd = np.arange(MLA_ROPE)
    return np.where(dd % 16 < 8, dd + 8, dd - 8)


def _ret_perm():
    dd = np.arange(RET_HEADS * RET_DK)
    return np.where(dd % RET_DK < RET_DK // 2, dd + RET_DK // 2, dd - RET_DK // 2)


def _prep_w_in(w):
    offs = np.cumsum((0,) + IN_SPLITS)
    cq, ckv, kr, rq, rk, rv, rg, gq, gk, gv, gr, gz, bg = [w[:, offs[i]:offs[i + 1]] for i in range(13)]
    zeros = lambda n: jnp.zeros((w.shape[0], n), w.dtype)
    rk = rk * (RET_DK ** -0.5)
    gq = gq * (GLA_DK ** -0.5)
    w1 = jnp.concatenate([
        cq, ckv,
        zeros(MLA_NOPE), kr, zeros(HEAD_SLOT - MLA_NOPE - MLA_ROPE),
        zeros(MLA_NOPE), kr[:, _axial_perm()], zeros(HEAD_SLOT - MLA_NOPE - MLA_ROPE),
        gz, zeros(LANES - 2 * GLA_GATE_RANK),
        rq, rq[:, _ret_perm()], rk, rk[:, _ret_perm()], rv, rg,
        gq, gk, gv, gr], axis=1)
    return w1.astype(BF16), bg.astype(BF16)


def _prep_mla(w_uq, w_ukv):
    r = w_uq.shape[0]
    scale = (MLA_NOPE + MLA_ROPE) ** -0.5 * np.log2(np.e)
    wq3 = w_uq.reshape(r, MLA_HEADS, MLA_NOPE + MLA_ROPE) * scale
    pad = jnp.zeros((r, MLA_HEADS, HEAD_SLOT - MLA_NOPE - MLA_ROPE), w_uq.dtype)
    wq = jnp.concatenate([wq3, pad], axis=-1)
    wqs = jnp.concatenate([jnp.zeros((r, MLA_HEADS, MLA_NOPE), w_uq.dtype),
                           wq3[:, :, MLA_NOPE:][:, :, _axial_perm()], pad], axis=-1)
    rk = w_ukv.shape[0]
    wkv3 = w_ukv.reshape(rk, MLA_HEADS, MLA_NOPE + MLA_V)
    wk = jnp.concatenate([wkv3[:, :, :MLA_NOPE],
                          jnp.zeros((rk, MLA_HEADS, HEAD_SLOT - MLA_NOPE), w_ukv.dtype)], axis=-1)
    wv = wkv3[:, :, MLA_NOPE:]
    flat = lambda a: a.reshape(a.shape[0], -1).astype(BF16)
    return flat(wq), flat(wqs), flat(wk), flat(wv)


def _prep_gla_gate(w_gate, b_gate):
    hw = GLA_HEADS * GLA_DK
    wg2 = jnp.zeros((LANES, 2 * hw), F32)
    wg2 = wg2.at[0:GLA_GATE_RANK, 0:hw].set(w_gate[0])
    wg2 = wg2.at[GLA_GATE_RANK:2 * GLA_GATE_RANK, hw:2 * hw].set(w_gate[1])
    return wg2, b_gate.reshape(1, 2 * hw)


def _rope_tables(n_ctx, seq):
    n_rows = seq // GRID_W
    half = MLA_ROPE // 4
    inv = ROPE_BASE ** (-jnp.arange(half, dtype=F32) / half)
    ang_r = jnp.arange(n_rows, dtype=F32)[:, None] * inv[None, :]
    ang_c = jnp.arange(GRID_W, dtype=F32)[:, None] * inv[None, :]
    by_row = lambda a: jnp.broadcast_to(a[:, None, :], (n_rows, GRID_W, half)).reshape(seq, half)
    by_col = lambda a: jnp.broadcast_to(a[None, :, :], (n_rows, GRID_W, half)).reshape(seq, half)
    cos_r, sin_r, cos_c, sin_c = by_row(jnp.cos(ang_r)), by_row(jnp.sin(ang_r)), by_col(jnp.cos(ang_c)), by_col(jnp.sin(ang_c))
    ctx = lambda a, fill: jnp.concatenate([jnp.full((n_ctx, a.shape[1]), fill, F32), a], axis=0)
    spread = lambda a, m: jnp.dot(a, jnp.asarray(m, F32), precision=HI)
    dd = np.arange(HEAD_SLOT) - MLA_NOPE
    active = (dd >= 0) & (dd < MLA_ROPE)
    src = (np.clip(dd, 0, MLA_ROPE - 1) // 16) * half + np.clip(dd, 0, MLA_ROPE - 1) % half
    pick = (np.arange(2 * half)[:, None] == src[None, :]) & active[None, :]
    sign = np.where((dd % 16) < half, -1.0, 1.0)[None, :]
    cosq = spread(ctx(jnp.concatenate([cos_r, cos_c], axis=1), 1.0), pick) + jnp.asarray(~active, F32)[None, :]
    sinq = spread(ctx(jnp.concatenate([sin_r, sin_c], axis=1), 0.0), pick * sign)
    rhalf = RET_DK // 2
    rinv = ROPE_BASE ** (-jnp.arange(rhalf, dtype=F32) / rhalf)
    rang = jnp.arange(seq, dtype=F32)[:, None] * rinv[None, :]
    rl = np.arange(RET_HEADS * RET_DK) % RET_DK
    rpick = np.arange(rhalf)[:, None] == (rl % rhalf)[None, :]
    rsign = np.where(rl < rhalf, -1.0, 1.0)[None, :]
    cosr = spread(ctx(jnp.cos(rang), 1.0), rpick)
    sinr = spread(ctx(jnp.sin(rang), 0.0), rpick * rsign)
    return cosq, sinq, cosr, sinr


def kernel(x, c, ctx, c_ctx, mod_w, mod_b, norm1_w, norm2_w, w_in, mla_q_norm, mla_w_uq, mla_kv_norm, mla_w_ukv, ret_decay_logit, ret_norm_w, gla_w_gate, gla_b_gate, gla_norm_w, w_branch, w_out, ffn_w_gate, ffn_w_up, ffn_w_down, moe_router, moe_w_gate, moe_w_up, moe_w_down, final_norm_w):
    b, seq, d = x.shape
    n_ctx = ctx.shape[1]
    depth = mod_w.shape[0]
    assert d == D_MODEL and seq % GRID_W == 0
    assert n_ctx % TOKEN_TILE == 0 and n_ctx % RET_CHUNK == 0 and n_ctx % GLA_CHUNK == 0
    assert seq % TOKEN_TILE == 0 and seq % RET_CHUNK == 0
    nct = n_ctx // TOKEN_TILE

    rows = 8 * ((b + 1 + 7) // 8)
    cvec = jnp.zeros((rows, d), F32).at[0:b].set(c).at[b].set(c_ctx)
    mod = _modulation(cvec, mod_w, mod_b)
    cosq, sinq, cosr, sinr = _rope_tables(n_ctx, seq)
    fnw = final_norm_w.reshape(1, d)
    row2 = lambda a: a.reshape(1, -1)

    xs = (ctx, x)
    for l in range(depth):
        last = l == depth - 1
        modsel = jnp.stack([jnp.broadcast_to(mod[l, b], (b, 6 * d)), mod[l, 0:b]], axis=1)
        w1, wbg = _prep_w_in(w_in[l])
        wq, wqs, wk, wv = _prep_mla(mla_w_uq[l], mla_w_ukv[l])
        wg2, bg2 = _prep_gla_gate(gla_w_gate[l], gla_b_gate[l])
        nw1, nw2 = row2(norm1_w[l]), row2(norm2_w[l])

        (q, k, v, rq, rk, rv, rg, gq, gk, gv, gr, la) = _inproj(
            xs, modsel, nw1, w1, row2(mla_q_norm[l]), wq, wqs, row2(mla_kv_norm[l]), wk, wv,
            wg2, bg2, cosq, sinq, cosr, sinr, nct)
        oa = _attention(q, k, v, nct)
        dl = jnp.broadcast_to(ret_decay_logit[l][:, :, None, None], (2, RET_HEADS, 8, LANES))
        rof, rob = _retention(dl, rq, rk, rv, n_ctx // RET_CHUNK)
        gof, gob = _gla(gq, gk, gv, la, n_ctx)
        is_moe = l % 2 == 1
        xs = _merge(xs, modsel, nw1, oa, rof, rob, rg, gof, gob, gr, wbg,
                    w_branch[l].astype(BF16), w_out[l].astype(BF16),
                    row2(ret_norm_w[l]), row2(gla_norm_w[l]), nct, is_moe and last)
        i = l // 2
        if not is_moe:
            xs = _ffn(xs, modsel, nw2, ffn_w_gate[i].astype(BF16), ffn_w_up[i].astype(BF16),
                      ffn_w_down[i].astype(BF16), fnw, nct, last)
            continue
        rt = jnp.zeros((d, LANES), F32).at[:, 0:N_EXPERTS].set(moe_router[i])
        experts = (moe_w_gate[i].astype(BF16), moe_w_up[i].astype(BF16), moe_w_down[i].astype(BF16))
        if last:
            xs = _moe(xs, 1, modsel, nw2, rt, *experts, fnw, True)
        else:
            xs = jnp.concatenate([_moe(xs[:, :n_ctx], 0, modsel, nw2, rt, *experts, fnw, False),
                                  _moe(xs[:, n_ctx:], 1, modsel, nw2, rt, *experts, fnw, False)], axis=1)
    return xs
```

```python
import functools

import numpy as np
import jax
import jax.numpy as jnp
from jax import lax
from jax.experimental import pallas as pl
from jax.experimental.pallas import tpu as pltpu

D_MODEL = 1024
GRID_W = 64
EPS = 1e-6
ROPE_BASE = 10000.0

MLA_HEADS = 8
MLA_NOPE = 64
MLA_ROPE = 32
MLA_V = 64
MLA_Q_RANK = 256
MLA_KV_RANK = 128

RET_HEADS = 4
RET_DK = 64
RET_DV = 128

GLA_HEADS = 4
GLA_DK = 64
GLA_DV = 128
GLA_GATE_RANK = 16
GLA_TAU = 16.0

N_BRANCH = 3
BRANCH_W = 512
D_FF = 2816
N_EXPERTS = 8

IN_SPLITS = (
    MLA_Q_RANK, MLA_KV_RANK, MLA_ROPE,
    RET_HEADS * RET_DK, RET_HEADS * RET_DK, RET_HEADS * RET_DV, RET_HEADS * RET_DV,
    GLA_HEADS * GLA_DK, GLA_HEADS * GLA_DK, GLA_HEADS * GLA_DV, GLA_HEADS * GLA_DV,
    2 * GLA_GATE_RANK,
    N_BRANCH * D_MODEL,
)

F32 = jnp.float32
BF16 = jnp.bfloat16
HI = lax.Precision.HIGHEST

LANES = 128
HEAD_SLOT = 128
TOKEN_TILE = 256
ATT_DEPTH = 3
ATT_UNROLL = 10
ATT_SUM_ROWS = 16
RET_CHUNK = 256
GLA_CHUNK = 256
GLA_BASE = 16
MOE_TILE = 1024
MOE_BLOCK = 256
MOD_COLS_TILE = 1536
VMEM_LIMIT = 56 * 1024 * 1024


def _silu(x):
    return x / (1.0 + jnp.exp(-x))


def _sigmoid(x):
    return 1.0 / (1.0 + jnp.exp(-x))


def _log_sigmoid(z):
    return jnp.minimum(z, 0.0) - jnp.log(1.0 + jnp.exp(-jnp.abs(z)))


def _rms(x, w):
    return x * lax.rsqrt(jnp.mean(x * x, axis=-1, keepdims=True) + EPS) * w


def _dot(a, b, precision=None):
    return jnp.dot(a, b, preferred_element_type=F32, precision=precision)


def _dot_nt(a, b):
    return lax.dot_general(a, b, (((1,), (1,)), ((), ())), preferred_element_type=F32)


def _params(sem):
    return pltpu.CompilerParams(dimension_semantics=sem, vmem_limit_bytes=VMEM_LIMIT)


def _const_spec(shape):
    nd = len(shape)
    return pl.BlockSpec(shape, lambda *_: (0,) * nd, pipeline_mode=pl.Buffered(1))


def _mod_rows(mod_ref, is_ctx):
    m = mod_ref[0]
    return jnp.where(is_ctx, m[0:1], m[1:2])


def _mod_kernel(c_ref, w_ref, b_ref, o_ref):
    o_ref[0] = _dot(_silu(c_ref[...]), w_ref[0], HI) + b_ref[0]


def _modulation(cvec, mod_w, mod_b):
    depth, d, n6 = mod_w.shape
    rows = cvec.shape[0]
    tn = MOD_COLS_TILE
    return pl.pallas_call(
        _mod_kernel,
        grid=(depth, n6 // tn),
        in_specs=[pl.BlockSpec((rows, d), lambda l, j: (0, 0)),
                  pl.BlockSpec((1, d, tn), lambda l, j: (l, 0, j)),
                  pl.BlockSpec((1, 1, tn), lambda l, j: (l, 0, j))],
        out_specs=pl.BlockSpec((1, rows, tn), lambda l, j: (l, 0, j)),
        out_shape=jax.ShapeDtypeStruct((depth, rows, n6), F32),
        compiler_params=_params(("arbitrary", "arbitrary")),
    )(cvec, mod_w, mod_b.reshape(depth, 1, n6))


_C_CQ = (0, 256)
_C_CKV_KRA = (256, 512)
_C_KRB_GZ = (512, 768)
_C_RQ = (768, 1024)
_C_RQS = (1024, 1280)
_C_RK = (1280, 1536)
_C_RKS = (1536, 1792)
_C_RV = (1792, 2304)
_C_RG = (2304, 2816)
_C_GQ = (2816, 3072)
_C_GK = (3072, 3328)
_C_GV = (3328, 3840)
_C_GR = (3840, 4352)


def _seq_inputs(xs, nct, off=0):
    tm = TOKEN_TILE
    if not isinstance(xs, tuple):
        return [pl.BlockSpec((1, tm, xs.shape[-1]), lambda i, t: (i, t + off, 0))], [xs]
    d = xs[0].shape[-1]
    return ([pl.BlockSpec((1, tm, d), lambda i, t: (i, jnp.minimum(t + off, nct - 1), 0)),
             pl.BlockSpec((1, tm, d), lambda i, t: (i, jnp.maximum(t + off - nct, 0), 0))], list(xs))


def _seq_tile(x_refs, is_ctx):
    if len(x_refs) == 1:
        return x_refs[0][0]
    return jnp.where(is_ctx, x_refs[0][0], x_refs[1][0])


def _inproj_kernel(nct, nx, *refs):
    x_refs = refs[:nx]
    (mod_ref, nw_ref, w_ref, qn_ref, wq_ref, wqs_ref, kvn_ref, wk_ref, wv_ref, wg2_ref, bg2_ref,
     cq_ref, sq_ref, cr_ref, sr_ref, q_out, k_out, v_out, rq_out, rk_out, rv_out, rg_out,
     gq_out, gk_out, gv_out, gr_out, la_out) = refs[nx:]
    d = D_MODEL
    is_ctx = pl.program_id(1) < nct
    row = _mod_rows(mod_ref, is_ctx)
    sh, sc = row[:, 0:d], row[:, d:2 * d]
    u = (_rms(_seq_tile(x_refs, is_ctx), nw_ref[...]) * (1.0 + sc) + sh).astype(BF16)

    def seg(c):
        return _dot(u, w_ref[:, c[0]:c[1]])

    cqn = _rms(seg(_C_CQ), qn_ref[...]).astype(BF16)
    q_all = _dot(cqn, wq_ref[...])
    q_swp = _dot(cqn, wqs_ref[...])
    ckv_kra = seg(_C_CKV_KRA)
    krb_gz = seg(_C_KRB_GZ)
    ckvn = _rms(ckv_kra[:, 0:LANES], kvn_ref[...]).astype(BF16)
    k_all = _dot(ckvn, wk_ref[...])
    v_out[0, 0] = _dot(ckvn, wv_ref[...]).T.astype(BF16)
    cosq, sinq = cq_ref[...], sq_ref[...]
    k_rope = ckv_kra[:, LANES:] * cosq + krb_gz[:, 0:LANES] * sinq
    for h in range(MLA_HEADS):
        sl = slice(h * HEAD_SLOT, (h + 1) * HEAD_SLOT)
        q_out[0, h, 0] = (q_all[:, sl] * cosq + q_swp[:, sl] * sinq).T.astype(BF16)
        k_out[0, h] = (k_all[:, sl] + k_rope).astype(BF16)

    cosr, sinr = cr_ref[...], sr_ref[...]
    rq_out[0] = (seg(_C_RQ) * cosr + seg(_C_RQS) * sinr).astype(BF16)
    rk_out[0] = (seg(_C_RK) * cosr + seg(_C_RKS) * sinr).astype(BF16)
    rv_out[0] = seg(_C_RV).astype(BF16)
    rg_out[0] = seg(_C_RG).astype(BF16)

    gq_out[0] = seg(_C_GQ).astype(BF16)
    gk_out[0] = seg(_C_GK).astype(BF16)
    gv_out[0] = seg(_C_GV).astype(BF16)
    gr_out[0] = seg(_C_GR).astype(BF16)
    z = _dot(krb_gz[:, LANES:], wg2_ref[...], HI) + bg2_ref[...]
    la_out[0] = _log_sigmoid(z) * (1.0 / GLA_TAU)


def _inproj(xs, modsel, nw, w1, qn, wq, wqs, kvn, wk, wv, wg2, bg2, cosq, sinq, cosr, sinr, nct):
    x_specs, x_ops = _seq_inputs(xs, nct)
    b, d = x_ops[0].shape[0], x_ops[0].shape[-1]
    n = sum(a.shape[1] for a in x_ops)
    tm = TOKEN_TILE
    nt = n // tm
    h = MLA_HEADS
    tok = lambda w: pl.BlockSpec((1, tm, w), lambda i, t: (i, t, 0))
    tab = lambda w: pl.BlockSpec((tm, w), lambda i, t: (t, 0))
    q_spec = pl.BlockSpec((1, h, 1, HEAD_SLOT, tm), lambda i, t: (i, 0, t, 0, 0))
    k_spec = pl.BlockSpec((1, h, tm, HEAD_SLOT), lambda i, t: (i, 0, t, 0))
    v_spec = pl.BlockSpec((1, 1, h * MLA_V, tm), lambda i, t: (i, t, 0, 0))
    sds = lambda w, dt=BF16: jax.ShapeDtypeStruct((b, n, w), dt)
    q_sds = jax.ShapeDtypeStruct((b, h, nt, HEAD_SLOT, tm), BF16)
    k_sds = jax.ShapeDtypeStruct((b, h, n, HEAD_SLOT), BF16)
    v_sds = jax.ShapeDtypeStruct((b, nt, h * MLA_V, tm), BF16)
    return pl.pallas_call(
        functools.partial(_inproj_kernel, nct, len(x_ops)),
        grid=(b, nt),
        in_specs=x_specs + [pl.BlockSpec((1, 2, 6 * d), lambda i, t: (i, 0, 0)), _const_spec(nw.shape),
                  _const_spec(w1.shape), _const_spec(qn.shape), _const_spec(wq.shape),
                  _const_spec(wqs.shape), _const_spec(kvn.shape), _const_spec(wk.shape),
                  _const_spec(wv.shape), _const_spec(wg2.shape), _const_spec(bg2.shape),
                  tab(HEAD_SLOT), tab(HEAD_SLOT), tab(256), tab(256)],
        out_specs=[q_spec, k_spec, v_spec, tok(256), tok(256), tok(512), tok(512),
                   tok(256), tok(256), tok(512), tok(512), tok(512)],
        out_shape=[q_sds, k_sds, v_sds, sds(256), sds(256), sds(512), sds(512),
                   sds(256), sds(256), sds(512), sds(512), sds(512, F32)],
        compiler_params=_params(("arbitrary", "arbitrary")),
    )(*x_ops, modsel, nw, w1, qn, wq, wqs, kvn, wk, wv, wg2, bg2, cosq, sinq, cosr, sinr)


def _attn_kernel(nct, n_tiles, qt_ref, k_ref, vt_ref, o_ref, *bufs):
    t = TOKEN_TILE
    nd = ATT_DEPTH
    heads = range(2)
    s_bufs = (bufs[0:nd], bufs[nd:2 * nd])
    p_bufs = (bufs[2 * nd:3 * nd], bufs[3 * nd:4 * nd])
    qts = [qt_ref[0, j, 0] for j in heads]
    ones = jnp.ones((ATT_SUM_ROWS, t), BF16)

    def scores(j, c, s_buf):
        s = _dot(k_ref[0, j, pl.ds(pl.multiple_of(c * t, t), t), :], qts[j])
        s_buf[...] = s
        return jnp.max(s, axis=0, keepdims=True)

    def softmax(s_buf, p_buf, m, cmax):
        m_new = jnp.maximum(m, cmax)
        p_buf[...] = jnp.exp2(s_buf[...] - m_new).astype(BF16)
        return m_new, jnp.exp2(m - m_new)

    def values(j, c, p_buf, alpha, acc):
        lhs = jnp.concatenate([vt_ref[0, c, j * MLA_V:(j + 1) * MLA_V, :], ones], axis=0)
        return alpha * acc + _dot(lhs, p_buf[...])

    m_init = jnp.full((1, t), -jnp.inf, F32)
    acc_init = jnp.zeros((MLA_V + ATT_SUM_ROWS, t), F32)

    def finish(accs):
        outs = [acc[0:MLA_V] / acc[MLA_V:MLA_V + 1] for acc in accs]
        o_ref[0] = jnp.concatenate(outs, axis=0).T.astype(BF16)

    @pl.when(pl.program_id(2) < nct)
    def _():
        accs = []
        for j in heads:
            m, acc = m_init, acc_init
            for c in range(nct):
                cmax = scores(j, c, s_bufs[j][0])
                m, alpha = softmax(s_bufs[j][0], p_bufs[j][0], m, cmax)
                acc = values(j, c, p_bufs[j][0], alpha, acc)
            accs.append(acc)
        finish(accs)

    def step(c, slot, state, lookahead=True):
        ahead = [scores(j, c + 2, s_bufs[j][(slot + 2) % nd]) if lookahead else None for j in heads]
        accs = [values(j, c - 1, p_bufs[j][(slot - 1) % nd], state[j][1], state[j][2]) for j in heads]
        stats = [softmax(s_bufs[j][slot], p_bufs[j][slot], state[j][0], state[j][3]) for j in heads]
        return tuple((stats[j][0], stats[j][1], accs[j], state[j][4], ahead[j]) for j in heads)

    @pl.when(pl.program_id(2) >= nct)
    def _():
        state = []
        for j in heads:
            cmax0 = scores(j, 0, s_bufs[j][0])
            cmax1 = scores(j, 1, s_bufs[j][1])
            m, alpha = softmax(s_bufs[j][0], p_bufs[j][0], m_init, cmax0)
            state.append((m, alpha, acc_init, cmax1, scores(j, 2, s_bufs[j][2])))

        def trip(i, state):
            for k in range(nd):
                state = step(1 + nd * i + k, (1 + k) % nd, state)
            return state

        state = lax.fori_loop(0, (n_tiles - 3) // nd, trip, tuple(state), unroll=ATT_UNROLL)
        for c in (n_tiles - 2, n_tiles - 1):
            state = step(c, c % nd, state, lookahead=False)
        finish([values(j, n_tiles - 1, p_bufs[j][(n_tiles - 1) % nd], state[j][1], state[j][2])
                for j in heads])


def _attention(qt, k, vt, nct):
    b, h, n_tiles, w, t = qt.shape
    n = n_tiles * t
    assert n_tiles >= 3 and (n_tiles - 3) % ATT_DEPTH == 0
    return pl.pallas_call(
        functools.partial(_attn_kernel, nct, n_tiles),
        grid=(b, h // 2, n_tiles),
        in_specs=[pl.BlockSpec((1, 2, 1, w, t), lambda i, p, q: (i, p, q, 0, 0)),
                  pl.BlockSpec((1, 2, n, w), lambda i, p, q: (i, p, 0, 0)),
                  pl.BlockSpec((1, n_tiles, 2 * MLA_V, t), lambda i, p, q: (i, 0, p, 0))],
        out_specs=pl.BlockSpec((1, t, 2 * MLA_V), lambda i, p, q: (i, q, p)),
        out_shape=jax.ShapeDtypeStruct((b, n, h * MLA_V), BF16),
        scratch_shapes=[pltpu.VMEM((t, t), F32)] * (2 * ATT_DEPTH) + [pltpu.VMEM((t, t), BF16)] * (2 * ATT_DEPTH),
        compiler_params=_params(("arbitrary", "arbitrary", "arbitrary")),
    )(qt, k, vt)


def _bwd_chunk(s, nct, nch):
    return jnp.where(s < nct, nct - 1 - s, nch - 1 - (s - nct))


def _ret_kernel(dl_ref, qf, kf, vf, qb, kb, vb, of_ref, ob_ref, st_ref):
    c = RET_CHUNK
    hw = RET_HEADS * RET_DK

    @pl.when(pl.program_id(1) == 0)
    def _():
        st_ref[...] = jnp.zeros_like(st_ref)

    ii = lax.broadcasted_iota(jnp.int32, (c, c), 0)
    jj = lax.broadcasted_iota(jnp.int32, (c, c), 1)
    pos_i = lax.broadcasted_iota(jnp.int32, (c, 1), 0).astype(F32)
    pos_j = lax.broadcasted_iota(jnp.int32, (1, c), 1).astype(F32)
    lane_head = lax.broadcasted_iota(jnp.int32, (c, hw), 1) // RET_DK
    for d, (q_ref, k_ref, v_ref, o_ref) in enumerate(((qf, kf, vf, of_ref), (qb, kb, vb, ob_ref))):
        rev = d == 1
        q, v = q_ref[0], v_ref[0]
        k_t = k_ref[0].astype(F32).T
        k_tb = k_t.astype(BF16)
        st = st_ref[d]
        st_b = st.astype(BF16)
        dist = ((jj - ii) if rev else (ii - jj)).astype(F32)
        for h in range(RET_HEADS):
            lg = _log_sigmoid(dl_ref[d, h])[0:1, 0:1]
            intra = jnp.where(dist >= 0.0, jnp.exp(lg * jnp.maximum(dist, 0.0)), 0.0)
            q_dec = jnp.exp(lg * ((c - pos_i) if rev else (pos_i + 1.0)))
            k_dec = jnp.exp(lg * (pos_j if rev else (c - 1.0 - pos_j)))
            rows = slice(h * RET_DK, (h + 1) * RET_DK)
            cols = slice(h * RET_DV, (h + 1) * RET_DV)
            qm = jnp.where(lane_head == h, q, jnp.zeros_like(q))
            scores = _dot(qm, k_tb) * intra
            o = _dot(scores.astype(BF16), v[:, cols]) + _dot(qm, st_b) * q_dec
            o_ref[0, :, cols] = o
            kd = (k_t[rows] * k_dec).astype(BF16)
            st_ref[d, rows, :] = st[rows] * jnp.exp(lg * float(c)) + _dot(kd, v[:, cols])


def _retention(dl, rq, rk, rv, nct):
    b, n, hw = rq.shape
    c = RET_CHUNK
    nch = n // c
    vw = rv.shape[-1]
    fwd = lambda w: pl.BlockSpec((1, c, w), lambda i, s: (i, s, 0))
    bwd = lambda w: pl.BlockSpec((1, c, w), lambda i, s: (i, _bwd_chunk(s, nct, nch), 0))
    return pl.pallas_call(
        _ret_kernel,
        grid=(b, nch),
        in_specs=[_const_spec(dl.shape), fwd(hw), fwd(hw), fwd(vw), bwd(hw), bwd(hw), bwd(vw)],
        out_specs=[fwd(vw), bwd(vw)],
        out_shape=[jax.ShapeDtypeStruct((b, n, vw), F32)] * 2,
        scratch_shapes=[pltpu.VMEM((2, hw, RET_DV), F32)],
        compiler_params=_params(("arbitrary", "arbitrary")),
    )(dl, rq, rk, rv, rq, rk, rv)


def _block_start(i, size):
    return i & ~(size - 1)


def _gla_levels():
    half = GLA_CHUNK // 2
    out = []
    while half >= GLA_BASE:
        out.append(half)
        half //= 2
    return out


def _gla_kernel(qf, kf, vf, laf, qb, kb, vb, lab, of_ref, ob_ref, st_ref):
    c = GLA_CHUNK
    hw = GLA_HEADS * GLA_DK
    levels = _gla_levels()

    @pl.when(pl.program_id(1) == 0)
    def _():
        st_ref[...] = jnp.zeros_like(st_ref)

    ii = lax.broadcasted_iota(jnp.int32, (c, c), 0)
    jj = lax.broadcasted_iota(jnp.int32, (c, c), 1)
    pos = lax.broadcasted_iota(jnp.int32, (c, 1), 0)
    lane_head = lax.broadcasted_iota(jnp.int32, (c, hw), 1) // GLA_DK
    for d, (q_ref, k_ref, v_ref, la_ref, o_ref) in enumerate(((qf, kf, vf, laf, of_ref),
                                                               (qb, kb, vb, lab, ob_ref))):
        rev = d == 1
        q, k, v, la = q_ref[0].astype(F32), k_ref[0].astype(F32), v_ref[0], la_ref[0]

        b = _dot(((jj >= ii) if rev else (jj <= ii)).astype(F32), la, HI)
        b_end = b[0:1] if rev else b[c - 1:c]

        def at_row(size, idx):
            b3 = b.reshape(c // size, size, hw)
            return jnp.broadcast_to(b3[:, idx:idx + 1, :], b3.shape).reshape(c, hw)

        q_t = (q * jnp.exp(b)).astype(BF16)
        k_tt = (k * jnp.exp(b_end - b)).T
        dec_col = jnp.exp(jnp.broadcast_to(b_end, (LANES, hw)).T[:, 0:1])

        terms = []
        for half in levels:
            e = at_row(2 * half, half if rev else half - 1)
            inner = pos & (2 * half - 1)
            late = (inner < half) if rev else (inner >= half)
            qh = jnp.where(late, q * jnp.exp(jnp.where(late, b - e, 0.0)), 0.0)
            kh = jnp.where(late, 0.0, k * jnp.exp(jnp.where(late, 0.0, e - b)))
            terms.append((qh.astype(BF16), kh.astype(BF16), _block_start(ii, 2 * half) == _block_start(jj, 2 * half)))
        r = at_row(GLA_BASE, GLA_BASE - 1 if rev else 0)
        tri = (jj >= ii) if rev else (jj <= ii)
        terms.append(((q * jnp.exp(b - r)).astype(BF16), (k * jnp.exp(r - b)).astype(BF16),
                      (_block_start(ii, GLA_BASE) == _block_start(jj, GLA_BASE)) & tri))

        st = st_ref[d]
        st_b = st.astype(BF16)
        for h in range(GLA_HEADS):
            rows = slice(h * GLA_DK, (h + 1) * GLA_DK)
            cols = slice(h * GLA_DV, (h + 1) * GLA_DV)
            sel = lane_head == h
            a = jnp.zeros((c, c), F32)
            for qh, kh, mask in terms:
                a = a + jnp.where(mask, _dot_nt(jnp.where(sel, qh, jnp.zeros_like(qh)), kh), 0.0)
            vh = v[:, cols]
            o_ref[0, :, cols] = _dot(a.astype(BF16), vh) + _dot(jnp.where(sel, q_t, jnp.zeros_like(q_t)), st_b)
            st_ref[d, rows, :] = st[rows] * dec_col[rows] + _dot(k_tt[rows].astype(BF16), vh)


def _gla(gq, gk, gv, la, nct_tokens):
    b, n, hw = gq.shape
    c = GLA_CHUNK
    nch = n // c
    nct = nct_tokens // c
    vw = gv.shape[-1]
    fwd = lambda w, col=0: pl.BlockSpec((1, c, w), lambda i, s: (i, s, col))
    bwd = lambda w, col=0: pl.BlockSpec((1, c, w), lambda i, s: (i, _bwd_chunk(s, nct, nch), col))
    return pl.pallas_call(
        _gla_kernel,
        grid=(b, nch),
        in_specs=[fwd(hw), fwd(hw), fwd(vw), fwd(hw, 0), bwd(hw), bwd(hw), bwd(vw), bwd(hw, 1)],
        out_specs=[fwd(vw), bwd(vw)],
        out_shape=[jax.ShapeDtypeStruct((b, n, vw), F32)] * 2,
        scratch_shapes=[pltpu.VMEM((2, hw, GLA_DV), F32)],
        compiler_params=_params(("arbitrary", "arbitrary")),
    )(gq, gk, gv, la, gq, gk, gv, la)


def _head_norm(o, w, center):
    parts = []
    for h in range(o.shape[-1] // LANES):
        oh = o[:, h * LANES:(h + 1) * LANES]
        if center:
            oh = oh - jnp.mean(oh, axis=-1, keepdims=True)
        parts.append(oh * lax.rsqrt(jnp.mean(oh * oh, axis=-1, keepdims=True) + EPS))
    return jnp.concatenate(parts, axis=-1) * w


def _merge_kernel(nct, off, nx, *refs):
    x_refs = refs[:nx]
    (mod_ref, nw_ref, oa_ref, rof_ref, rob_ref, rg_ref, gof_ref, gob_ref, gr_ref, wbg_ref, wbr_ref,
     wout_ref, rnw_ref, gnw_ref, o_ref) = refs[nx:]
    d = D_MODEL
    is_ctx = pl.program_id(1) + off < nct
    row = _mod_rows(mod_ref, is_ctx)
    sh, sc, g1 = row[:, 0:d], row[:, d:2 * d], row[:, 2 * d:3 * d]
    x = _seq_tile(x_refs, is_ctx)
    u = (_rms(x, nw_ref[...]) * (1.0 + sc) + sh).astype(BF16)
    gate = _sigmoid(_dot(u, wbg_ref[...]))
    yb = _silu(rg_ref[0].astype(F32)) * _head_norm(rof_ref[0] + rob_ref[0], rnw_ref[...], True)
    yc = _silu(gr_ref[0].astype(F32)) * _head_norm(gof_ref[0] + gob_ref[0], gnw_ref[...], False)
    z = (gate[:, 0:d] * _dot(oa_ref[0], wbr_ref[0])
         + gate[:, d:2 * d] * _dot(yb.astype(BF16), wbr_ref[1])
         + gate[:, 2 * d:3 * d] * _dot(yc.astype(BF16), wbr_ref[2]))
    o_ref[0] = x + g1 * _dot(z.astype(BF16), wout_ref[...])


def _merge(xs, modsel, nw, oa, rof, rob, rg, gof, gob, gr, wbg, wbr, wout, rnw, gnw, nct, latent_only):
    tm = TOKEN_TILE
    off = nct if latent_only else 0
    x_specs, x_ops = _seq_inputs(xs, nct, off)
    b, d = x_ops[0].shape[0], x_ops[0].shape[-1]
    nt = sum(a.shape[1] for a in x_ops) // tm - off
    tok = lambda w: pl.BlockSpec((1, tm, w), lambda i, t: (i, t + off, 0))
    return pl.pallas_call(
        functools.partial(_merge_kernel, nct, off, len(x_ops)),
        grid=(b, nt),
        in_specs=x_specs + [pl.BlockSpec((1, 2, 6 * d), lambda i, t: (i, 0, 0)), _const_spec(nw.shape),
                  tok(512), tok(512), tok(512), tok(512), tok(512), tok(512), tok(512),
                  _const_spec(wbg.shape), _const_spec(wbr.shape), _const_spec(wout.shape),
                  _const_spec(rnw.shape), _const_spec(gnw.shape)],
        out_specs=pl.BlockSpec((1, tm, d), lambda i, t: (i, t, 0)),
        out_shape=jax.ShapeDtypeStruct((b, nt * tm, d), F32),
        compiler_params=_params(("arbitrary", "arbitrary")),
    )(*x_ops, modsel, nw, oa, rof, rob, rg, gof, gob, gr, wbg, wbr, wout, rnw, gnw)


def _norm2(x, row, nw_ref):
    d = D_MODEL
    sh, sc = row[:, 3 * d:4 * d], row[:, 4 * d:5 * d]
    return _rms(x, nw_ref[...]) * (1.0 + sc) + sh


def _finish(x, y, row, final, fnw_ref):
    d = D_MODEL
    out = x + row[:, 5 * d:6 * d] * y
    if final:
        out = _rms(out, fnw_ref[...])
    return out


def _ffn_kernel(nct, off, final, x_ref, mod_ref, nw_ref, wg_ref, wu_ref, wd_ref, fnw_ref, o_ref):
    row = _mod_rows(mod_ref, pl.program_id(1) + off < nct)
    x = x_ref[0]
    v = _norm2(x, row, nw_ref).astype(BF16)
    hdn = _silu(_dot(v, wg_ref[...])) * _dot(v, wu_ref[...])
    o_ref[0] = _finish(x, _dot(hdn.astype(BF16), wd_ref[...]), row, final, fnw_ref)


def _ffn(xs, modsel, nw, wg, wu, wd, fnw, nct, final):
    b, n, d = xs.shape
    tm = TOKEN_TILE
    off = nct if final else 0
    nt = n // tm - off
    return pl.pallas_call(
        functools.partial(_ffn_kernel, nct, off, final),
        grid=(b, nt),
        in_specs=[pl.BlockSpec((1, tm, d), lambda i, t: (i, t + off, 0)),
                  pl.BlockSpec((1, 2, 6 * d), lambda i, t: (i, 0, 0)), _const_spec(nw.shape),
                  _const_spec(wg.shape), _const_spec(wu.shape), _const_spec(wd.shape),
                  _const_spec(fnw.shape)],
        out_specs=pl.BlockSpec((1, tm, d), lambda i, t: (i, t, 0)),
        out_shape=jax.ShapeDtypeStruct((b, nt * tm, d), F32),
        compiler_params=_params(("arbitrary", "arbitrary")),
    )(xs, modsel, nw, wg, wu, wd, fnw)


def _top2_gates(logits):
    lane = lax.broadcasted_iota(jnp.int32, logits.shape, 1)
    m1 = jnp.max(logits, axis=-1, keepdims=True)
    i1 = jnp.min(jnp.where(logits == m1, lane, LANES), axis=-1, keepdims=True)
    rest = jnp.where(lane == i1, -jnp.inf, logits)
    m2 = jnp.max(rest, axis=-1, keepdims=True)
    i2 = jnp.min(jnp.where(rest == m2, lane, LANES), axis=-1, keepdims=True)
    e2 = jnp.exp(m2 - m1)
    w1 = 1.0 / (1.0 + e2)
    gates = jnp.where(lane == i1, w1, jnp.where(lane == i2, e2 * w1, 0.0))
    return gates, (lane == i1) | (lane == i2)


def _one_hot(cond):
    return jnp.where(cond, 1.0, 0.0).astype(BF16)


def _route_kernel(mrow, x_ref, mod_ref, nw_ref, rt_ref, v_out, gate_out, rank_out, rankt_out):
    v = _norm2(x_ref[0], mod_ref[0][mrow:mrow + 1], nw_ref)
    v_out[0] = v.astype(BF16)
    tt = v.shape[0]
    lane = lax.broadcasted_iota(jnp.int32, (tt, LANES), 1)
    logits = jnp.where(lane < N_EXPERTS, _dot(v, rt_ref[...], HI), -jnp.inf)
    gates, routed = _top2_gates(logits)
    gate_out[0] = gates
    ti = lax.broadcasted_iota(jnp.int32, (tt, tt), 0)
    tj = lax.broadcasted_iota(jnp.int32, (tt, tt), 1)
    rank_out[0] = jnp.where(routed, _dot(_one_hot(tj < ti), _one_hot(routed)), -1.0)
    routed_t = jnp.where(routed, 1.0, 0.0).T
    rank_t = jnp.where(routed_t > 0.5, _dot(routed_t.astype(BF16), _one_hot(ti < tj)), -1.0)
    rankt_out[0, 0] = rank_t[0:N_EXPERTS]


def _experts_kernel(v_ref, gate_ref, rank_ref, rankt_ref, wg_ref, wu_ref, wd_ref, o_ref, acc_ref):
    e, s = pl.program_id(1), pl.program_id(2)
    tt = v_ref.shape[1]
    fh = D_FF // 2
    tile0 = pl.multiple_of(s * tt, tt)
    lane = lax.broadcasted_iota(jnp.int32, (tt, LANES), 1)
    g_col = jnp.sum(jnp.where(lane == e, gate_ref[0], 0.0), axis=-1, keepdims=True)
    rk_col = jnp.sum(jnp.where(lane == e, rank_ref[0], 0.0), axis=-1, keepdims=True)
    rk_row = rankt_ref[0, 0, pl.ds(e, 1), :]
    n_routed = jnp.max(rk_col).astype(jnp.int32) + 1

    @pl.when(e == 0)
    def _():
        acc_ref[pl.ds(tile0, tt), :] = jnp.zeros((tt, D_MODEL), BF16)

    def block(first_slot, r):
        base = first_slot.astype(F32)
        slot_i = base + lax.broadcasted_iota(jnp.int32, (r, 1), 0).astype(F32)
        xg = _dot(_one_hot(rk_row == slot_i), v_ref[0]).astype(BF16)
        y = jnp.zeros((r, D_MODEL), F32)
        for c in range(2):
            cols = slice(c * fh, (c + 1) * fh)
            hdn = _silu(_dot(xg, wg_ref[0, :, cols])) * _dot(xg, wu_ref[0, :, cols])
            y = y + _dot(hdn.astype(BF16), wd_ref[0, cols, :])
        y = y.astype(BF16)
        slot_j = base + lax.broadcasted_iota(jnp.int32, (1, r), 1).astype(F32)
        for c in range(tt // TOKEN_TILE):
            rows = slice(c * TOKEN_TILE, (c + 1) * TOKEN_TILE)
            dst = pl.ds(tile0 + c * TOKEN_TILE, TOKEN_TILE)
            part = g_col[rows] * _dot(_one_hot(rk_col[rows] == slot_j), y)
            acc_ref[dst, :] = (acc_ref[dst, :].astype(F32) + part).astype(BF16)

    r, half = MOE_BLOCK, MOE_BLOCK // 2
    n_full = n_routed // r
    left = n_routed - n_full * r

    def full_block(i, carry):
        block(i * r, r)
        return carry

    lax.fori_loop(0, n_full + (left > half).astype(jnp.int32), full_block, 0)

    @pl.when((left > 0) & (left <= half))
    def _():
        block(n_full * r, half)

    @pl.when(e == N_EXPERTS - 1)
    def _():
        o_ref[0] = acc_ref[pl.ds(tile0, tt), :]


def _moe_finish_kernel(mrow, final, x_ref, y_ref, mod_ref, fnw_ref, o_ref):
    o_ref[0] = _finish(x_ref[0], y_ref[0].astype(F32), mod_ref[0][mrow:mrow + 1], final, fnw_ref)


def _moe(slab, mrow, modsel, nw, rt, wg, wu, wd, fnw, final):
    b, m, d = slab.shape
    tt = min(MOE_TILE, m)
    assert m % tt == 0 and tt % TOKEN_TILE == 0
    ns = m // tt
    ne, _, f = wg.shape
    mod_spec = lambda nd: pl.BlockSpec((1, 2, 6 * d), (lambda i, s: (i, 0, 0)) if nd == 2 else
                                       (lambda i, e, s: (i, 0, 0)))
    v, gates, rank, rank_t = pl.pallas_call(
        functools.partial(_route_kernel, mrow),
        grid=(b, ns),
        in_specs=[pl.BlockSpec((1, tt, d), lambda i, s: (i, s, 0)), mod_spec(2), _const_spec(nw.shape),
                  _const_spec(rt.shape)],
        out_specs=[pl.BlockSpec((1, tt, d), lambda i, s: (i, s, 0)),
                   pl.BlockSpec((1, tt, LANES), lambda i, s: (i, s, 0)),
                   pl.BlockSpec((1, tt, LANES), lambda i, s: (i, s, 0)),
                   pl.BlockSpec((1, 1, ne, tt), lambda i, s: (i, s, 0, 0))],
        out_shape=[jax.ShapeDtypeStruct((b, m, d), BF16), jax.ShapeDtypeStruct((b, m, LANES), F32),
                   jax.ShapeDtypeStruct((b, m, LANES), F32), jax.ShapeDtypeStruct((b, ns, ne, tt), F32)],
        compiler_params=_params(("arbitrary", "arbitrary")),
    )(slab, modsel, nw, rt)

    tok = lambda w: pl.BlockSpec((1, tt, w), lambda i, e, s: (i, s, 0))
    wspec = lambda shape: pl.BlockSpec((1,) + shape, lambda i, e, s: (e, 0, 0), pipeline_mode=pl.Buffered(1))
    last_pass = lambda i, e, s: (i, jnp.where(e == ne - 1, s, 0), 0)
    y_sum = pl.pallas_call(
        _experts_kernel,
        grid=(b, ne, ns),
        in_specs=[tok(d), tok(LANES), tok(LANES),
                  pl.BlockSpec((1, 1, ne, tt), lambda i, e, s: (i, s, 0, 0)),
                  wspec((d, f)), wspec((d, f)), wspec((f, d))],
        out_specs=pl.BlockSpec((1, tt, d), last_pass),
        out_shape=jax.ShapeDtypeStruct((b, m, d), BF16),
        scratch_shapes=[pltpu.VMEM((m, d), BF16)],
        compiler_params=_params(("arbitrary", "arbitrary", "arbitrary")),
    )(v, gates, rank, rank_t, wg, wu, wd)

    tm = TOKEN_TILE
    row_tile = lambda: pl.BlockSpec((1, tm, d), lambda i, t: (i, t, 0))
    return pl.pallas_call(
        functools.partial(_moe_finish_kernel, mrow, final),
        grid=(b, m // tm),
        in_specs=[row_tile(), row_tile(), mod_spec(2), _const_spec(fnw.shape)],
        out_specs=row_tile(),
        out_shape=jax.ShapeDtypeStruct((b, m, d), F32),
        compiler_params=_params(("arbitrary", "arbitrary")),
    )(slab, y_sum, modsel, fnw)


def _axial_perm():
    dd = np.arange(MLA_ROPE)
    return np.where(dd % 16 < 8, dd + 8, dd - 8)


def _ret_perm():
    dd = np.arange(RET_HEADS * RET_DK)
    return np.where(dd % RET_DK < RET_DK // 2, dd + RET_DK // 2, dd - RET_DK // 2)


def _prep_w_in(w):
    offs = np.cumsum((0,) + IN_SPLITS)
    cq, ckv, kr, rq, rk, rv, rg, gq, gk, gv, gr, gz, bg = [w[:, offs[i]:offs[i + 1]] for i in range(13)]
    zeros = lambda n: jnp.zeros((w.shape[0], n), w.dtype)
    rk = rk * (RET_DK ** -0.5)
    gq = gq * (GLA_DK ** -0.5)
    w1 = jnp.concatenate([
        cq, ckv,
        zeros(MLA_NOPE), kr, zeros(HEAD_SLOT - MLA_NOPE - MLA_ROPE),
        zeros(MLA_NOPE), kr[:, _axial_perm()], zeros(HEAD_SLOT - MLA_NOPE - MLA_ROPE),
        gz, zeros(LANES - 2 * GLA_GATE_RANK),
        rq, rq[:, _ret_perm()], rk, rk[:, _ret_perm()], rv, rg,
        gq, gk, gv, gr], axis=1)
    return w1.astype(BF16), bg.astype(BF16)


def _prep_mla(w_uq, w_ukv):
    r = w_uq.shape[0]
    scale = (MLA_NOPE + MLA_ROPE) ** -0.5 * np.log2(np.e)
    wq3 = w_uq.reshape(r, MLA_HEADS, MLA_NOPE + MLA_ROPE) * scale
    pad = jnp.zeros((r, MLA_HEADS, HEAD_SLOT - MLA_NOPE - MLA_ROPE), w_uq.dtype)
    wq = jnp.concatenate([wq3, pad], axis=-1)
    wqs = jnp.concatenate([jnp.zeros((r, MLA_HEADS, MLA_NOPE), w_uq.dtype),
                           wq3[:, :, MLA_NOPE:][:, :, _axial_perm()], pad], axis=-1)
    rk = w_ukv.shape[0]
    wkv3 = w_ukv.reshape(rk, MLA_HEADS, MLA_NOPE + MLA_V)
    wk = jnp.concatenate([wkv3[:, :, :MLA_NOPE],
                          jnp.zeros((rk, MLA_HEADS, HEAD_SLOT - MLA_NOPE), w_ukv.dtype)], axis=-1)
    wv = wkv3[:, :, MLA_NOPE:]
    flat = lambda a: a.reshape(a.shape[0], -1).astype(BF16)
    return flat(wq), flat(wqs), flat(wk), flat(wv)


def _prep_gla_gate(w_gate, b_gate):
    hw = GLA_HEADS * GLA_DK
    wg2 = jnp.zeros((LANES, 2 * hw), F32)
    wg2 = wg2.at[0:GLA_GATE_RANK, 0:hw].set(w_gate[0])
    wg2 = wg2.at[GLA_GATE_RANK:2 * GLA_GATE_RANK, hw:2 * hw].set(w_gate[1])
    return wg2, b_gate.reshape(1, 2 * hw)


def _rope_tables(n_ctx, seq):
    n_rows = seq // GRID_W
    half = MLA_ROPE // 4
    inv = ROPE_BASE ** (-jnp.arange(half, dtype=F32) / half)
    ang_r = jnp.arange(n_rows, dtype=F32)[:, None] * inv[None, :]
    ang_c = jnp.arange(GRID_W, dtype=F32)[:, None] * inv[None, :]
    by_row = lambda a: jnp.broadcast_to(a[:, None, :], (n_rows, GRID_W, half)).reshape(seq, half)
    by_col = lambda a: jnp.broadcast_to(a[None, :, :], (n_rows, GRID_W, half)).reshape(seq, half)
    cos_r, sin_r, cos_c, sin_c = by_row(jnp.cos(ang_r)), by_row(jnp.sin(ang_r)), by_col(jnp.cos(ang_c)), by_col(jnp.sin(ang_c))
    ctx = lambda a, fill: jnp.concatenate([jnp.full((n_ctx, a.shape[1]), fill, F32), a], axis=0)
    spread = lambda a, m: jnp.dot(a, jnp.asarray(m, F32), precision=HI)
    dd = np.arange(HEAD_SLOT) - MLA_NOPE
    active = (dd >= 0) & (dd < MLA_ROPE)
    src = (np.clip(dd, 0, MLA_ROPE - 1) // 16) * half + np.clip(dd, 0, MLA_ROPE - 1) % half
    pick = (np.arange(2 * half)[:, None] == src[None, :]) & active[None, :]
    sign = np.where((dd % 16) < half, -1.0, 1.0)[None, :]
    cosq = spread(ctx(jnp.concatenate([cos_r, cos_c], axis=1), 1.0), pick) + jnp.asarray(~active, F32)[None, :]
    sinq = spread(ctx(jnp.concatenate([sin_r, sin_c], axis=1), 0.0), pick * sign)
    rhalf = RET_DK // 2
    rinv = ROPE_BASE ** (-jnp.arange(rhalf, dtype=F32) / rhalf)
    rang = jnp.arange(seq, dtype=F32)[:, None] * rinv[None, :]
    rl = np.arange(RET_HEADS * RET_DK) % RET_DK
    rpick = np.arange(rhalf)[:, None] == (rl % rhalf)[None, :]
    rsign = np.where(rl < rhalf, -1.0, 1.0)[None, :]
    cosr = spread(ctx(jnp.cos(rang), 1.0), rpick)
    sinr = spread(ctx(jnp.sin(rang), 0.0), rpick * rsign)
    return cosq, sinq, cosr, sinr


def kernel(x, c, ctx, c_ctx, mod_w, mod_b, norm1_w, norm2_w, w_in, mla_q_norm, mla_w_uq, mla_kv_norm, mla_w_ukv, ret_decay_logit, ret_norm_w, gla_w_gate, gla_b_gate, gla_norm_w, w_branch, w_out, ffn_w_gate, ffn_w_up, ffn_w_down, moe_router, moe_w_gate, moe_w_up, moe_w_down, final_norm_w):
    b, seq, d = x.shape
    n_ctx = ctx.shape[1]
    depth = mod_w.shape[0]
    assert d == D_MODEL and seq % GRID_W == 0
    assert n_ctx % TOKEN_TILE == 0 and n_ctx % RET_CHUNK == 0 and n_ctx % GLA_CHUNK == 0
    assert seq % TOKEN_TILE == 0 and seq % RET_CHUNK == 0
    nct = n_ctx // TOKEN_TILE

    rows = 8 * ((b + 1 + 7) // 8)
    cvec = jnp.zeros((rows, d), F32).at[0:b].set(c).at[b].set(c_ctx)
    mod = _modulation(cvec, mod_w, mod_b)
    cosq, sinq, cosr, sinr = _rope_tables(n_ctx, seq)
    fnw = final_norm_w.reshape(1, d)
    row2 = lambda a: a.reshape(1, -1)

    xs = (ctx, x)
    for l in range(depth):
        last = l == depth - 1
        modsel = jnp.stack([jnp.broadcast_to(mod[l, b], (b, 6 * d)), mod[l, 0:b]], axis=1)
        w1, wbg = _prep_w_in(w_in[l])
        wq, wqs, wk, wv = _prep_mla(mla_w_uq[l], mla_w_ukv[l])
        wg2, bg2 = _prep_gla_gate(gla_w_gate[l], gla_b_gate[l])
        nw1, nw2 = row2(norm1_w[l]), row2(norm2_w[l])

        (q, k, v, rq, rk, rv, rg, gq, gk, gv, gr, la) = _inproj(
            xs, modsel, nw1, w1, row2(mla_q_norm[l]), wq, wqs, row2(mla_kv_norm[l]), wk, wv,
            wg2, bg2, cosq, sinq, cosr, sinr, nct)
        oa = _attention(q, k, v, nct)
        dl = jnp.broadcast_to(ret_decay_logit[l][:, :, None, None], (2, RET_HEADS, 8, LANES))
        rof, rob = _retention(dl, rq, rk, rv, n_ctx // RET_CHUNK)
        gof, gob = _gla(gq, gk, gv, la, n_ctx)
        is_moe = l % 2 == 1
        xs = _merge(xs, modsel, nw1, oa, rof, rob, rg, gof, gob, gr, wbg,
                    w_branch[l].astype(BF16), w_out[l].astype(BF16),
                    row2(ret_norm_w[l]), row2(gla_norm_w[l]), nct, is_moe and last)
        i = l // 2
        if not is_moe:
            xs = _ffn(xs, modsel, nw2, ffn_w_gate[i].astype(BF16), ffn_w_up[i].astype(BF16),
                      ffn_w_down[i].astype(BF16), fnw, nct, last)
            continue
        rt = jnp.zeros((d, LANES), F32).at[:, 0:N_EXPERTS].set(moe_router[i])
        experts = (moe_w_gate[i].astype(BF16), moe_w_up[i].astype(BF16), moe_w_down[i].astype(BF16))
        if last:
            xs = _moe(xs, 1, modsel, nw2, rt, *experts, fnw, True)
        else:
            xs = jnp.concatenate([_moe(xs[:, :n_ctx], 0, modsel, nw2, rt, *experts, fnw, False),
                                  _moe(xs[:, n_ctx:], 1, modsel, nw2, rt, *experts, fnw, False)], axis=1)
    return xs
```

```python
import functools

import numpy as np
import jax
import jax.numpy as jnp
from jax import lax
from jax.experimental import pallas as pl
from jax.experimental.pallas import tpu as pltpu

D_MODEL = 1024
GRID_W = 64
EPS = 1e-6
ROPE_BASE = 10000.0

MLA_HEADS = 8
MLA_NOPE = 64
MLA_ROPE = 32
MLA_V = 64
MLA_Q_RANK = 256
MLA_KV_RANK = 128

RET_HEADS = 4
RET_DK = 64
RET_DV = 128

GLA_HEADS = 4
GLA_DK = 64
GLA_DV = 128
GLA_GATE_RANK = 16
GLA_TAU = 16.0

N_BRANCH = 3
BRANCH_W = 512
D_FF = 2816
N_EXPERTS = 8

IN_SPLITS = (
    MLA_Q_RANK, MLA_KV_RANK, MLA_ROPE,
    RET_HEADS * RET_DK, RET_HEADS * RET_DK, RET_HEADS * RET_DV, RET_HEADS * RET_DV,
    GLA_HEADS * GLA_DK, GLA_HEADS * GLA_DK, GLA_HEADS * GLA_DV, GLA_HEADS * GLA_DV,
    2 * GLA_GATE_RANK,
    N_BRANCH * D_MODEL,
)

F32 = jnp.float32
BF16 = jnp.bfloat16
HI = lax.Precision.HIGHEST

LANES = 128
HEAD_SLOT = 128
TOKEN_TILE = 256
ATT_DEPTH = 3
ATT_UNROLL = 10
ATT_SUM_ROWS = 16
SCAN_CHUNK = 256
GLA_BASE = 16
MOE_TILE = 1024
MOE_BLOCK = 256
MOD_COLS_TILE = 1536
VMEM_LIMIT = 56 * 1024 * 1024


def _silu(x):
    return x / (1.0 + jnp.exp(-x))


def _sigmoid(x):
    return 1.0 / (1.0 + jnp.exp(-x))


def _log_sigmoid(z):
    return jnp.minimum(z, 0.0) - jnp.log(1.0 + jnp.exp(-jnp.abs(z)))


def _rms(x, w):
    return x * lax.rsqrt(jnp.mean(x * x, axis=-1, keepdims=True) + EPS) * w


def _dot(a, b, precision=None):
    return jnp.dot(a, b, preferred_element_type=F32, precision=precision)


def _dot_nt(a, b):
    return lax.dot_general(a, b, (((1,), (1,)), ((), ())), preferred_element_type=F32)


def _params(sem):
    return pltpu.CompilerParams(dimension_semantics=sem, vmem_limit_bytes=VMEM_LIMIT)


def _const_spec(shape):
    nd = len(shape)
    return pl.BlockSpec(shape, lambda *_: (0,) * nd, pipeline_mode=pl.Buffered(1))


def _mod_rows(mod_ref, is_ctx):
    m = mod_ref[0]
    return jnp.where(is_ctx, m[0:1], m[1:2])


def _mod_kernel(c_ref, w_ref, b_ref, o_ref):
    o_ref[0] = _dot(_silu(c_ref[...]), w_ref[0], HI) + b_ref[0]


def _modulation(cvec, mod_w, mod_b):
    depth, d, n6 = mod_w.shape
    rows = cvec.shape[0]
    tn = MOD_COLS_TILE
    return pl.pallas_call(
        _mod_kernel,
        grid=(depth, n6 // tn),
        in_specs=[pl.BlockSpec((rows, d), lambda l, j: (0, 0)),
                  pl.BlockSpec((1, d, tn), lambda l, j: (l, 0, j)),
                  pl.BlockSpec((1, 1, tn), lambda l, j: (l, 0, j))],
        out_specs=pl.BlockSpec((1, rows, tn), lambda l, j: (l, 0, j)),
        out_shape=jax.ShapeDtypeStruct((depth, rows, n6), F32),
        compiler_params=_params(("arbitrary", "arbitrary")),
    )(cvec, mod_w, mod_b.reshape(depth, 1, n6))


_C_CQ = (0, 256)
_C_CKV_KRA = (256, 512)
_C_KRB_GZ = (512, 768)
_C_RQ = (768, 1024)
_C_RQS = (1024, 1280)
_C_RK = (1280, 1536)
_C_RKS = (1536, 1792)
_C_RV = (1792, 2304)
_C_RG = (2304, 2816)
_C_GQ = (2816, 3072)
_C_GK = (3072, 3328)
_C_GV = (3328, 3840)
_C_GR = (3840, 4352)


def _seq_inputs(xs, nct, off=0):
    tm = TOKEN_TILE
    if not isinstance(xs, tuple):
        return [pl.BlockSpec((1, tm, xs.shape[-1]), lambda i, t: (i, t + off, 0))], [xs]
    d = xs[0].shape[-1]
    return ([pl.BlockSpec((1, tm, d), lambda i, t: (i, jnp.minimum(t + off, nct - 1), 0)),
             pl.BlockSpec((1, tm, d), lambda i, t: (i, jnp.maximum(t + off - nct, 0), 0))], list(xs))


def _seq_tile(x_refs, is_ctx):
    if len(x_refs) == 1:
        return x_refs[0][0]
    return jnp.where(is_ctx, x_refs[0][0], x_refs[1][0])


def _inproj_kernel(nct, nx, *refs):
    x_refs = refs[:nx]
    (mod_ref, nw_ref, w_ref, qn_ref, wq_ref, wqs_ref, kvn_ref, wk_ref, wv_ref, wg2_ref, bg2_ref,
     cq_ref, sq_ref, cr_ref, sr_ref, q_out, k_out, v_out, rq_out, rk_out, rv_out, rg_out,
     gq_out, gk_out, gv_out, gr_out, la_out) = refs[nx:]
    d = D_MODEL
    is_ctx = pl.program_id(1) < nct
    row = _mod_rows(mod_ref, is_ctx)
    sh, sc = row[:, 0:d], row[:, d:2 * d]
    u = (_rms(_seq_tile(x_refs, is_ctx), nw_ref[...]) * (1.0 + sc) + sh).astype(BF16)

    def seg(c):
        return _dot(u, w_ref[:, c[0]:c[1]])

    cqn = _rms(seg(_C_CQ), qn_ref[...]).astype(BF16)
    q_all = _dot(cqn, wq_ref[...])
    q_swp = _dot(cqn, wqs_ref[...])
    ckv_kra = seg(_C_CKV_KRA)
    krb_gz = seg(_C_KRB_GZ)
    ckvn = _rms(ckv_kra[:, 0:LANES], kvn_ref[...]).astype(BF16)
    k_all = _dot(ckvn, wk_ref[...])
    v_out[0, 0] = _dot(ckvn, wv_ref[...]).T.astype(BF16)
    cosq, sinq = cq_ref[...], sq_ref[...]
    k_rope = ckv_kra[:, LANES:] * cosq + krb_gz[:, 0:LANES] * sinq
    for h in range(MLA_HEADS):
        sl = slice(h * HEAD_SLOT, (h + 1) * HEAD_SLOT)
        q_out[0, h, 0] = (q_all[:, sl] * cosq + q_swp[:, sl] * sinq).T.astype(BF16)
        k_out[0, h] = (k_all[:, sl] + k_rope).astype(BF16)

    cosr, sinr = cr_ref[...], sr_ref[...]
    rq_out[0] = (seg(_C_RQ) * cosr + seg(_C_RQS) * sinr).astype(BF16)
    rk_out[0] = (seg(_C_RK) * cosr + seg(_C_RKS) * sinr).astype(BF16)
    rv_out[0] = seg(_C_RV).astype(BF16)
    rg_out[0] = seg(_C_RG).astype(BF16)

    gq_out[0] = seg(_C_GQ).astype(BF16)
    gk_out[0] = seg(_C_GK).astype(BF16)
    gv_out[0] = seg(_C_GV).astype(BF16)
    gr_out[0] = seg(_C_GR).astype(BF16)
    z = _dot(krb_gz[:, LANES:], wg2_ref[...], HI) + bg2_ref[...]
    la_out[0] = _log_sigmoid(z) * (1.0 / GLA_TAU)


def _inproj(xs, modsel, nw, w1, qn, wq, wqs, kvn, wk, wv, wg2, bg2, cosq, sinq, cosr, sinr, nct):
    x_specs, x_ops = _seq_inputs(xs, nct)
    b, d = x_ops[0].shape[0], x_ops[0].shape[-1]
    n = sum(a.shape[1] for a in x_ops)
    tm = TOKEN_TILE
    nt = n // tm
    h = MLA_HEADS
    tok = lambda w: pl.BlockSpec((1, tm, w), lambda i, t: (i, t, 0))
    tab = lambda w: pl.BlockSpec((tm, w), lambda i, t: (t, 0))
    q_spec = pl.BlockSpec((1, h, 1, HEAD_SLOT, tm), lambda i, t: (i, 0, t, 0, 0))
    k_spec = pl.BlockSpec((1, h, tm, HEAD_SLOT), lambda i, t: (i, 0, t, 0))
    v_spec = pl.BlockSpec((1, 1, h * MLA_V, tm), lambda i, t: (i, t, 0, 0))
    sds = lambda w, dt=BF16: jax.ShapeDtypeStruct((b, n, w), dt)
    q_sds = jax.ShapeDtypeStruct((b, h, nt, HEAD_SLOT, tm), BF16)
    k_sds = jax.ShapeDtypeStruct((b, h, n, HEAD_SLOT), BF16)
    v_sds = jax.ShapeDtypeStruct((b, nt, h * MLA_V, tm), BF16)
    return pl.pallas_call(
        functools.partial(_inproj_kernel, nct, len(x_ops)),
        grid=(b, nt),
        in_specs=x_specs + [pl.BlockSpec((1, 2, 6 * d), lambda i, t: (i, 0, 0)), _const_spec(nw.shape),
                  _const_spec(w1.shape), _const_spec(qn.shape), _const_spec(wq.shape),
                  _const_spec(wqs.shape), _const_spec(kvn.shape), _const_spec(wk.shape),
                  _const_spec(wv.shape), _const_spec(wg2.shape), _const_spec(bg2.shape),
                  tab(HEAD_SLOT), tab(HEAD_SLOT), tab(256), tab(256)],
        out_specs=[q_spec, k_spec, v_spec, tok(256), tok(256), tok(512), tok(512),
                   tok(256), tok(256), tok(512), tok(512), tok(512)],
        out_shape=[q_sds, k_sds, v_sds, sds(256), sds(256), sds(512), sds(512),
                   sds(256), sds(256), sds(512), sds(512), sds(512, F32)],
        compiler_params=_params(("arbitrary", "arbitrary")),
    )(*x_ops, modsel, nw, w1, qn, wq, wqs, kvn, wk, wv, wg2, bg2, cosq, sinq, cosr, sinr)


def _attn_kernel(nct, n_tiles, qt_ref, k_ref, vt_ref, o_ref, *bufs):
    t = TOKEN_TILE
    nd = ATT_DEPTH
    heads = range(2)
    s_bufs = (bufs[0:nd], bufs[nd:2 * nd])
    p_bufs = (bufs[2 * nd:3 * nd], bufs[3 * nd:4 * nd])
    qts = [qt_ref[0, j, 0] for j in heads]
    ones = jnp.ones((ATT_SUM_ROWS, t), BF16)

    def scores(j, c, s_buf):
        s = _dot(k_ref[0, j, pl.ds(pl.multiple_of(c * t, t), t), :], qts[j])
        s_buf[...] = s
        return jnp.max(s, axis=0, keepdims=True)

    def softmax(s_buf, p_buf, m, cmax):
        m_new = jnp.maximum(m, cmax)
        p_buf[...] = jnp.exp2(s_buf[...] - m_new).astype(BF16)
        return m_new, jnp.exp2(m - m_new)

    def values(j, c, p_buf, alpha, acc):
        lhs = jnp.concatenate([vt_ref[0, c, j * MLA_V:(j + 1) * MLA_V, :], ones], axis=0)
        return alpha * acc + _dot(lhs, p_buf[...])

    m_init = jnp.full((1, t), -jnp.inf, F32)
    acc_init = jnp.zeros((MLA_V + ATT_SUM_ROWS, t), F32)

    def finish(accs):
        outs = [acc[0:MLA_V] / acc[MLA_V:MLA_V + 1] for acc in accs]
        o_ref[0] = jnp.concatenate(outs, axis=0).T.astype(BF16)

    @pl.when(pl.program_id(2) < nct)
    def _():
        accs = []
        for j in heads:
            m, acc = m_init, acc_init
            for c in range(nct):
                cmax = scores(j, c, s_bufs[j][0])
                m, alpha = softmax(s_bufs[j][0], p_bufs[j][0], m, cmax)
                acc = values(j, c, p_bufs[j][0], alpha, acc)
            accs.append(acc)
        finish(accs)

    def step(c, slot, state, lookahead=True):
        ahead = [scores(j, c + 2, s_bufs[j][(slot + 2) % nd]) if lookahead else None for j in heads]
        accs = [values(j, c - 1, p_bufs[j][(slot - 1) % nd], state[j][1], state[j][2]) for j in heads]
        stats = [softmax(s_bufs[j][slot], p_bufs[j][slot], state[j][0], state[j][3]) for j in heads]
        return tuple((stats[j][0], stats[j][1], accs[j], state[j][4], ahead[j]) for j in heads)

    @pl.when(pl.program_id(2) >= nct)
    def _():
        state = []
        for j in heads:
            cmax0 = scores(j, 0, s_bufs[j][0])
            cmax1 = scores(j, 1, s_bufs[j][1])
            m, alpha = softmax(s_bufs[j][0], p_bufs[j][0], m_init, cmax0)
            state.append((m, alpha, acc_init, cmax1, scores(j, 2, s_bufs[j][2])))

        def trip(i, state):
            for k in range(nd):
                state = step(1 + nd * i + k, (1 + k) % nd, state)
            return state

        state = lax.fori_loop(0, (n_tiles - 3) // nd, trip, tuple(state), unroll=ATT_UNROLL)
        for c in (n_tiles - 2, n_tiles - 1):
            state = step(c, c % nd, state, lookahead=False)
        finish([values(j, n_tiles - 1, p_bufs[j][(n_tiles - 1) % nd], state[j][1], state[j][2])
                for j in heads])


def _attention(qt, k, vt, nct):
    b, h, n_tiles, w, t = qt.shape
    n = n_tiles * t
    assert n_tiles >= 3 and (n_tiles - 3) % ATT_DEPTH == 0
    return pl.pallas_call(
        functools.partial(_attn_kernel, nct, n_tiles),
        grid=(b, h // 2, n_tiles),
        in_specs=[pl.BlockSpec((1, 2, 1, w, t), lambda i, p, q: (i, p, q, 0, 0)),
                  pl.BlockSpec((1, 2, n, w), lambda i, p, q: (i, p, 0, 0)),
                  pl.BlockSpec((1, n_tiles, 2 * MLA_V, t), lambda i, p, q: (i, 0, p, 0))],
        out_specs=pl.BlockSpec((1, t, 2 * MLA_V), lambda i, p, q: (i, q, p)),
        out_shape=jax.ShapeDtypeStruct((b, n, h * MLA_V), BF16),
        scratch_shapes=[pltpu.VMEM((t, t), F32)] * (2 * ATT_DEPTH) + [pltpu.VMEM((t, t), BF16)] * (2 * ATT_DEPTH),
        compiler_params=_params(("arbitrary", "arbitrary", "arbitrary")),
    )(qt, k, vt)


def _bwd_chunk(s, nct, nch):
    return jnp.where(s < nct, nct - 1 - s, nch - 1 - (s - nct))


def _ret_body(dl_ref, qf, kf, vf, qb, kb, vb, of_ref, ob_ref, st_ref):
    c = SCAN_CHUNK
    hw = RET_HEADS * RET_DK
    ii = lax.broadcasted_iota(jnp.int32, (c, c), 0)
    jj = lax.broadcasted_iota(jnp.int32, (c, c), 1)
    pos_i = lax.broadcasted_iota(jnp.int32, (c, 1), 0).astype(F32)
    pos_j = lax.broadcasted_iota(jnp.int32, (1, c), 1).astype(F32)
    lane_head = lax.broadcasted_iota(jnp.int32, (c, hw), 1) // RET_DK
    for d, (q_ref, k_ref, v_ref, o_ref) in enumerate(((qf, kf, vf, of_ref), (qb, kb, vb, ob_ref))):
        rev = d == 1
        q, v = q_ref[0], v_ref[0]
        k_t = k_ref[0].astype(F32).T
        k_tb = k_t.astype(BF16)
        st = st_ref[d]
        st_b = st.astype(BF16)
        dist = ((jj - ii) if rev else (ii - jj)).astype(F32)
        for h in range(RET_HEADS):
            lg = _log_sigmoid(dl_ref[d, h])[0:1, 0:1]
            intra = jnp.where(dist >= 0.0, jnp.exp(lg * jnp.maximum(dist, 0.0)), 0.0)
            q_dec = jnp.exp(lg * ((c - pos_i) if rev else (pos_i + 1.0)))
            k_dec = jnp.exp(lg * (pos_j if rev else (c - 1.0 - pos_j)))
            rows = slice(h * RET_DK, (h + 1) * RET_DK)
            cols = slice(h * RET_DV, (h + 1) * RET_DV)
            qm = jnp.where(lane_head == h, q, jnp.zeros_like(q))
            scores = _dot(qm, k_tb) * intra
            o = _dot(scores.astype(BF16), v[:, cols]) + _dot(qm, st_b) * q_dec
            o_ref[0, :, cols] = o
            kd = (k_t[rows] * k_dec).astype(BF16)
            st_ref[d, rows, :] = st[rows] * jnp.exp(lg * float(c)) + _dot(kd, v[:, cols])


def _block_start(i, size):
    return i & ~(size - 1)


def _gla_levels():
    half = SCAN_CHUNK // 2
    out = []
    while half >= GLA_BASE:
        out.append(half)
        half //= 2
    return out


def _gla_body(qf, kf, vf, laf, qb, kb, vb, lab, of_ref, ob_ref, st_ref):
    c = SCAN_CHUNK
    hw = GLA_HEADS * GLA_DK
    levels = _gla_levels()
    ii = lax.broadcasted_iota(jnp.int32, (c, c), 0)
    jj = lax.broadcasted_iota(jnp.int32, (c, c), 1)
    pos = lax.broadcasted_iota(jnp.int32, (c, 1), 0)
    lane_head = lax.broadcasted_iota(jnp.int32, (c, hw), 1) // GLA_DK
    for d, (q_ref, k_ref, v_ref, la_ref, o_ref) in enumerate(((qf, kf, vf, laf, of_ref),
                                                               (qb, kb, vb, lab, ob_ref))):
        rev = d == 1
        q, k, v, la = q_ref[0].astype(F32), k_ref[0].astype(F32), v_ref[0], la_ref[0]

        b = _dot(((jj >= ii) if rev else (jj <= ii)).astype(F32), la, HI)
        b_end = b[0:1] if rev else b[c - 1:c]

        def at_row(size, idx):
            b3 = b.reshape(c // size, size, hw)
            return jnp.broadcast_to(b3[:, idx:idx + 1, :], b3.shape).reshape(c, hw)

        q_t = (q * jnp.exp(b)).astype(BF16)
        k_tt = (k * jnp.exp(b_end - b)).T
        dec_col = jnp.exp(jnp.broadcast_to(b_end, (LANES, hw)).T[:, 0:1])

        terms = []
        for half in levels:
            e = at_row(2 * half, half if rev else half - 1)
            inner = pos & (2 * half - 1)
            late = (inner < half) if rev else (inner >= half)
            qh = jnp.where(late, q * jnp.exp(jnp.where(late, b - e, 0.0)), 0.0)
            kh = jnp.where(late, 0.0, k * jnp.exp(jnp.where(late, 0.0, e - b)))
            terms.append((qh.astype(BF16), kh.astype(BF16), _block_start(ii, 2 * half) == _block_start(jj, 2 * half)))
        r = at_row(GLA_BASE, GLA_BASE - 1 if rev else 0)
        tri = (jj >= ii) if rev else (jj <= ii)
        terms.append(((q * jnp.exp(b - r)).astype(BF16), (k * jnp.exp(r - b)).astype(BF16),
                      (_block_start(ii, GLA_BASE) == _block_start(jj, GLA_BASE)) & tri))

        st = st_ref[d]
        st_b = st.astype(BF16)
        for h in range(GLA_HEADS):
            rows = slice(h * GLA_DK, (h + 1) * GLA_DK)
            cols = slice(h * GLA_DV, (h + 1) * GLA_DV)
            sel = lane_head == h
            a = jnp.zeros((c, c), F32)
            for qh, kh, mask in terms:
                a = a + jnp.where(mask, _dot_nt(jnp.where(sel, qh, jnp.zeros_like(qh)), kh), 0.0)
            vh = v[:, cols]
            o_ref[0, :, cols] = _dot(a.astype(BF16), vh) + _dot(jnp.where(sel, q_t, jnp.zeros_like(q_t)), st_b)
            st_ref[d, rows, :] = st[rows] * dec_col[rows] + _dot(k_tt[rows].astype(BF16), vh)


def _scan_kernel(*refs):
    ret_refs, gla_refs = refs[0:7] + refs[15:17] + refs[19:20], refs[7:15] + refs[17:19] + refs[20:21]

    @pl.when(pl.program_id(1) == 0)
    def _():
        ret_refs[-1][...] = jnp.zeros_like(ret_refs[-1])
        gla_refs[-1][...] = jnp.zeros_like(gla_refs[-1])

    _ret_body(*ret_refs)
    _gla_body(*gla_refs)


def _scans(dl, rq, rk, rv, gq, gk, gv, la, nct_tokens):
    b, n, hw = rq.shape
    c = SCAN_CHUNK
    nch = n // c
    nct = nct_tokens // c
    vw = rv.shape[-1]
    fwd = lambda w, col=0: pl.BlockSpec((1, c, w), lambda i, s: (i, s, col))
    bwd = lambda w, col=0: pl.BlockSpec((1, c, w), lambda i, s: (i, _bwd_chunk(s, nct, nch), col))
    return pl.pallas_call(
        _scan_kernel,
        grid=(b, nch),
        in_specs=[_const_spec(dl.shape), fwd(hw), fwd(hw), fwd(vw), bwd(hw), bwd(hw), bwd(vw),
                  fwd(hw), fwd(hw), fwd(vw), fwd(hw, 0), bwd(hw), bwd(hw), bwd(vw), bwd(hw, 1)],
        out_specs=[fwd(vw), bwd(vw), fwd(vw), bwd(vw)],
        out_shape=[jax.ShapeDtypeStruct((b, n, vw), F32)] * 4,
        scratch_shapes=[pltpu.VMEM((2, hw, RET_DV), F32), pltpu.VMEM((2, hw, GLA_DV), F32)],
        compiler_params=_params(("arbitrary", "arbitrary")),
    )(dl, rq, rk, rv, rq, rk, rv, gq, gk, gv, la, gq, gk, gv, la)


def _head_norm(o, w, center):
    parts = []
    for h in range(o.shape[-1] // LANES):
        oh = o[:, h * LANES:(h + 1) * LANES]
        if center:
            oh = oh - jnp.mean(oh, axis=-1, keepdims=True)
        parts.append(oh * lax.rsqrt(jnp.mean(oh * oh, axis=-1, keepdims=True) + EPS))
    return jnp.concatenate(parts, axis=-1) * w


def _merge_kernel(nct, off, nx, *refs):
    x_refs = refs[:nx]
    (mod_ref, nw_ref, oa_ref, rof_ref, rob_ref, rg_ref, gof_ref, gob_ref, gr_ref, wbg_ref, wbr_ref,
     wout_ref, rnw_ref, gnw_ref, o_ref) = refs[nx:]
    d = D_MODEL
    is_ctx = pl.program_id(1) + off < nct
    row = _mod_rows(mod_ref, is_ctx)
    sh, sc, g1 = row[:, 0:d], row[:, d:2 * d], row[:, 2 * d:3 * d]
    x = _seq_tile(x_refs, is_ctx)
    u = (_rms(x, nw_ref[...]) * (1.0 + sc) + sh).astype(BF16)
    gate = _sigmoid(_dot(u, wbg_ref[...]))
    yb = _silu(rg_ref[0].astype(F32)) * _head_norm(rof_ref[0] + rob_ref[0], rnw_ref[...], True)
    yc = _silu(gr_ref[0].astype(F32)) * _head_norm(gof_ref[0] + gob_ref[0], gnw_ref[...], False)
    z = (gate[:, 0:d] * _dot(oa_ref[0], wbr_ref[0])
         + gate[:, d:2 * d] * _dot(yb.astype(BF16), wbr_ref[1])
         + gate[:, 2 * d:3 * d] * _dot(yc.astype(BF16), wbr_ref[2]))
    o_ref[0] = x + g1 * _dot(z.astype(BF16), wout_ref[...])


def _merge(xs, modsel, nw, oa, rof, rob, rg, gof, gob, gr, wbg, wbr, wout, rnw, gnw, nct, latent_only):
    tm = TOKEN_TILE
    off = nct if latent_only else 0
    x_specs, x_ops = _seq_inputs(xs, nct, off)
    b, d = x_ops[0].shape[0], x_ops[0].shape[-1]
    nt = sum(a.shape[1] for a in x_ops) // tm - off
    tok = lambda w: pl.BlockSpec((1, tm, w), lambda i, t: (i, t + off, 0))
    return pl.pallas_call(
        functools.partial(_merge_kernel, nct, off, len(x_ops)),
        grid=(b, nt),
        in_specs=x_specs + [pl.BlockSpec((1, 2, 6 * d), lambda i, t: (i, 0, 0)), _const_spec(nw.shape),
                  tok(512), tok(512), tok(512), tok(512), tok(512), tok(512), tok(512),
                  _const_spec(wbg.shape), _const_spec(wbr.shape), _const_spec(wout.shape),
                  _const_spec(rnw.shape), _const_spec(gnw.shape)],
        out_specs=pl.BlockSpec((1, tm, d), lambda i, t: (i, t, 0)),
        out_shape=jax.ShapeDtypeStruct((b, nt * tm, d), F32),
        compiler_params=_params(("arbitrary", "arbitrary")),
    )(*x_ops, modsel, nw, oa, rof, rob, rg, gof, gob, gr, wbg, wbr, wout, rnw, gnw)


def _norm2(x, row, nw_ref):
    d = D_MODEL
    sh, sc = row[:, 3 * d:4 * d], row[:, 4 * d:5 * d]
    return _rms(x, nw_ref[...]) * (1.0 + sc) + sh


def _finish(x, y, row, final, fnw_ref):
    d = D_MODEL
    out = x + row[:, 5 * d:6 * d] * y
    if final:
        out = _rms(out, fnw_ref[...])
    return out


def _ffn_kernel(nct, off, final, x_ref, mod_ref, nw_ref, wg_ref, wu_ref, wd_ref, fnw_ref, o_ref):
    row = _mod_rows(mod_ref, pl.program_id(1) + off < nct)
    x = x_ref[0]
    v = _norm2(x, row, nw_ref).astype(BF16)
    hdn = _silu(_dot(v, wg_ref[...])) * _dot(v, wu_ref[...])
    o_ref[0] = _finish(x, _dot(hdn.astype(BF16), wd_ref[...]), row, final, fnw_ref)


def _ffn(xs, modsel, nw, wg, wu, wd, fnw, nct, final):
    b, n, d = xs.shape
    tm = TOKEN_TILE
    off = nct if final else 0
    nt = n // tm - off
    return pl.pallas_call(
        functools.partial(_ffn_kernel, nct, off, final),
        grid=(b, nt),
        in_specs=[pl.BlockSpec((1, tm, d), lambda i, t: (i, t + off, 0)),
                  pl.BlockSpec((1, 2, 6 * d), lambda i, t: (i, 0, 0)), _const_spec(nw.shape),
                  _const_spec(wg.shape), _const_spec(wu.shape), _const_spec(wd.shape),
                  _const_spec(fnw.shape)],
        out_specs=pl.BlockSpec((1, tm, d), lambda i, t: (i, t, 0)),
        out_shape=jax.ShapeDtypeStruct((b, nt * tm, d), F32),
        compiler_params=_params(("arbitrary", "arbitrary")),
    )(xs, modsel, nw, wg, wu, wd, fnw)


def _top2_gates(logits):
    lane = lax.broadcasted_iota(jnp.int32, logits.shape, 1)
    m1 = jnp.max(logits, axis=-1, keepdims=True)
    i1 = jnp.min(jnp.where(logits == m1, lane, LANES), axis=-1, keepdims=True)
    rest = jnp.where(lane == i1, -jnp.inf, logits)
    m2 = jnp.max(rest, axis=-1, keepdims=True)
    i2 = jnp.min(jnp.where(rest == m2, lane, LANES), axis=-1, keepdims=True)
    e2 = jnp.exp(m2 - m1)
    w1 = 1.0 / (1.0 + e2)
    gates = jnp.where(lane == i1, w1, jnp.where(lane == i2, e2 * w1, 0.0))
    return gates, (lane == i1) | (lane == i2)


def _one_hot(cond):
    return jnp.where(cond, 1.0, 0.0).astype(BF16)


def _route_kernel(mrow, x_ref, mod_ref, nw_ref, rt_ref, v_out, gate_out, rank_out, rankt_out):
    v = _norm2(x_ref[0], mod_ref[0][mrow:mrow + 1], nw_ref)
    v_out[0] = v.astype(BF16)
    tt = v.shape[0]
    lane = lax.broadcasted_iota(jnp.int32, (tt, LANES), 1)
    logits = jnp.where(lane < N_EXPERTS, _dot(v, rt_ref[...], HI), -jnp.inf)
    gates, routed = _top2_gates(logits)
    gate_out[0] = gates
    ti = lax.broadcasted_iota(jnp.int32, (tt, tt), 0)
    tj = lax.broadcasted_iota(jnp.int32, (tt, tt), 1)
    rank_out[0] = jnp.where(routed, _dot(_one_hot(tj < ti), _one_hot(routed)), -1.0)
    routed_t = jnp.where(routed, 1.0, 0.0).T
    rank_t = jnp.where(routed_t > 0.5, _dot(routed_t.astype(BF16), _one_hot(ti < tj)), -1.0)
    rankt_out[0, 0] = rank_t[0:N_EXPERTS]


def _experts_kernel(v_ref, gate_ref, rank_ref, rankt_ref, wg_ref, wu_ref, wd_ref, o_ref, acc_ref):
    e, s = pl.program_id(1), pl.program_id(2)
    tt = v_ref.shape[1]
    fh = D_FF // 2
    tile0 = pl.multiple_of(s * tt, tt)
    lane = lax.broadcasted_iota(jnp.int32, (tt, LANES), 1)
    g_col = jnp.sum(jnp.where(lane == e, gate_ref[0], 0.0), axis=-1, keepdims=True)
    rk_col = jnp.sum(jnp.where(lane == e, rank_ref[0], 0.0), axis=-1, keepdims=True)
    rk_row = rankt_ref[0, 0, pl.ds(e, 1), :]
    n_routed = jnp.max(rk_col).astype(jnp.int32) + 1

    @pl.when(e == 0)
    def _():
        acc_ref[pl.ds(tile0, tt), :] = jnp.zeros((tt, D_MODEL), BF16)

    def block(first_slot, r):
        base = first_slot.astype(F32)
        slot_i = base + lax.broadcasted_iota(jnp.int32, (r, 1), 0).astype(F32)
        xg = _dot(_one_hot(rk_row == slot_i), v_ref[0]).astype(BF16)
        y = jnp.zeros((r, D_MODEL), F32)
        for c in range(2):
            cols = slice(c * fh, (c + 1) * fh)
            hdn = _silu(_dot(xg, wg_ref[0, :, cols])) * _dot(xg, wu_ref[0, :, cols])
            y = y + _dot(hdn.astype(BF16), wd_ref[0, cols, :])
        y = y.astype(BF16)
        slot_j = base + lax.broadcasted_iota(jnp.int32, (1, r), 1).astype(F32)
        for c in range(tt // TOKEN_TILE):
            rows = slice(c * TOKEN_TILE, (c + 1) * TOKEN_TILE)
            dst = pl.ds(tile0 + c * TOKEN_TILE, TOKEN_TILE)
            part = g_col[rows] * _dot(_one_hot(rk_col[rows] == slot_j), y)
            acc_ref[dst, :] = (acc_ref[dst, :].astype(F32) + part).astype(BF16)

    r, half = MOE_BLOCK, MOE_BLOCK // 2
    n_full = n_routed // r
    left = n_routed - n_full * r

    def full_block(i, carry):
        block(i * r, r)
        return carry

    lax.fori_loop(0, n_full + (left > half).astype(jnp.int32), full_block, 0)

    @pl.when((left > 0) & (left <= half))
    def _():
        block(n_full * r, half)

    @pl.when(e == N_EXPERTS - 1)
    def _():
        o_ref[0] = acc_ref[pl.ds(tile0, tt), :]


def _moe_finish_kernel(mrow, final, x_ref, y_ref, mod_ref, fnw_ref, o_ref):
    o_ref[0] = _finish(x_ref[0], y_ref[0].astype(F32), mod_ref[0][mrow:mrow + 1], final, fnw_ref)


def _moe(slab, mrow, modsel, nw, rt, wg, wu, wd, fnw, final):
    b, m, d = slab.shape
    tt = min(MOE_TILE, m)
    assert m % tt == 0 and tt % TOKEN_TILE == 0
    ns = m // tt
    ne, _, f = wg.shape
    mod_spec = lambda nd: pl.BlockSpec((1, 2, 6 * d), (lambda i, s: (i, 0, 0)) if nd == 2 else
                                       (lambda i, e, s: (i, 0, 0)))
    v, gates, rank, rank_t = pl.pallas_call(
        functools.partial(_route_kernel, mrow),
        grid=(b, ns),
        in_specs=[pl.BlockSpec((1, tt, d), lambda i, s: (i, s, 0)), mod_spec(2), _const_spec(nw.shape),
                  _const_spec(rt.shape)],
        out_specs=[pl.BlockSpec((1, tt, d), lambda i, s: (i, s, 0)),
                   pl.BlockSpec((1, tt, LANES), lambda i, s: (i, s, 0)),
                   pl.BlockSpec((1, tt, LANES), lambda i, s: (i, s, 0)),
                   pl.BlockSpec((1, 1, ne, tt), lambda i, s: (i, s, 0, 0))],
        out_shape=[jax.ShapeDtypeStruct((b, m, d), BF16), jax.ShapeDtypeStruct((b, m, LANES), F32),
                   jax.ShapeDtypeStruct((b, m, LANES), F32), jax.ShapeDtypeStruct((b, ns, ne, tt), F32)],
        compiler_params=_params(("arbitrary", "arbitrary")),
    )(slab, modsel, nw, rt)

    tok = lambda w: pl.BlockSpec((1, tt, w), lambda i, e, s: (i, s, 0))
    wspec = lambda shape: pl.BlockSpec((1,) + shape, lambda i, e, s: (e, 0, 0), pipeline_mode=pl.Buffered(1))
    last_pass = lambda i, e, s: (i, jnp.where(e == ne - 1, s, 0), 0)
    y_sum = pl.pallas_call(
        _experts_kernel,
        grid=(b, ne, ns),
        in_specs=[tok(d), tok(LANES), tok(LANES),
                  pl.BlockSpec((1, 1, ne, tt), lambda i, e, s: (i, s, 0, 0)),
                  wspec((d, f)), wspec((d, f)), wspec((f, d))],
        out_specs=pl.BlockSpec((1, tt, d), last_pass),
        out_shape=jax.ShapeDtypeStruct((b, m, d), BF16),
        scratch_shapes=[pltpu.VMEM((m, d), BF16)],
        compiler_params=_params(("arbitrary", "arbitrary", "arbitrary")),
    )(v, gates, rank, rank_t, wg, wu, wd)

    tm = TOKEN_TILE
    row_tile = lambda: pl.BlockSpec((1, tm, d), lambda i, t: (i, t, 0))
    return pl.pallas_call(
        functools.partial(_moe_finish_kernel, mrow, final),
        grid=(b, m // tm),
        in_specs=[row_tile(), row_tile(), mod_spec(2), _const_spec(fnw.shape)],
        out_specs=row_tile(),
        out_shape=jax.ShapeDtypeStruct((b, m, d), F32),
        compiler_params=_params(("arbitrary", "arbitrary")),
    )(slab, y_sum, modsel, fnw)


def _axial_perm():
    dd = np.arange(MLA_ROPE)
    return np.where(dd % 16 < 8, dd + 8, dd - 8)


def _ret_perm():
    dd = np.arange(RET_HEADS * RET_DK)
    return np.where(dd % RET_DK < RET_DK // 2, dd + RET_DK // 2, dd - RET_DK // 2)


def _prep_w_in(w):
    offs = np.cumsum((0,) + IN_SPLITS)
    cq, ckv, kr, rq, rk, rv, rg, gq, gk, gv, gr, gz, bg = [w[:, offs[i]:offs[i + 1]] for i in range(13)]
    zeros = lambda n: jnp.zeros((w.shape[0], n), w.dtype)
    rk = rk * (RET_DK ** -0.5)
    gq = gq * (GLA_DK ** -0.5)
    w1 = jnp.concatenate([
        cq, ckv,
        zeros(MLA_NOPE), kr, zeros(HEAD_SLOT - MLA_NOPE - MLA_ROPE),
        zeros(MLA_NOPE), kr[:, _axial_perm()], zeros(HEAD_SLOT - MLA_NOPE - MLA_ROPE),
        gz, zeros(LANES - 2 * GLA_GATE_RANK),
        rq, rq[:, _ret_perm()], rk, rk[:, _ret_perm()], rv, rg,
        gq, gk, gv, gr], axis=1)
    return w1.astype(BF16), bg.astype(BF16)


def _prep_mla(w_uq, w_ukv):
    r = w_uq.shape[0]
    scale = (MLA_NOPE + MLA_ROPE) ** -0.5 * np.log2(np.e)
    wq3 = w_uq.reshape(r, MLA_HEADS, MLA_NOPE + MLA_ROPE) * scale
    pad = jnp.zeros((r, MLA_HEADS, HEAD_SLOT - MLA_NOPE - MLA_ROPE), w_uq.dtype)
    wq = jnp.concatenate([wq3, pad], axis=-1)
    wqs = jnp.concatenate([jnp.zeros((r, MLA_HEADS, MLA_NOPE), w_uq.dtype),
                           wq3[:, :, MLA_NOPE:][:, :, _axial_perm()], pad], axis=-1)
    rk = w_ukv.shape[0]
    wkv3 = w_ukv.reshape(rk, MLA_HEADS, MLA_NOPE + MLA_V)
    wk = jnp.concatenate([wkv3[:, :, :MLA_NOPE],
                          jnp.zeros((rk, MLA_HEADS, HEAD_SLOT - MLA_NOPE), w_ukv.dtype)], axis=-1)
    wv = wkv3[:, :, MLA_NOPE:]
    flat = lambda a: a.reshape(a.shape[0], -1).astype(BF16)
    return flat(wq), flat(wqs), flat(wk), flat(wv)


def _prep_gla_gate(w_gate, b_gate):
    hw = GLA_HEADS * GLA_DK
    wg2 = jnp.zeros((LANES, 2 * hw), F32)
    wg2 = wg2.at[0:GLA_GATE_RANK, 0:hw].set(w_gate[0])
    wg2 = wg2.at[GLA_GATE_RANK:2 * GLA_GATE_RANK, hw:2 * hw].set(w_gate[1])
    return wg2, b_gate.reshape(1, 2 * hw)


def _rope_tables(n_ctx, seq):
    n_rows = seq // GRID_W
    half = MLA_ROPE // 4
    inv = ROPE_BASE ** (-jnp.arange(half, dtype=F32) / half)
    ang_r = jnp.arange(n_rows, dtype=F32)[:, None] * inv[None, :]
    ang_c = jnp.arange(GRID_W, dtype=F32)[:, None] * inv[None, :]
    by_row = lambda a: jnp.broadcast_to(a[:, None, :], (n_rows, GRID_W, half)).reshape(seq, half)
    by_col = lambda a: jnp.broadcast_to(a[None, :, :], (n_rows, GRID_W, half)).reshape(seq, half)
    cos_r, sin_r, cos_c, sin_c = by_row(jnp.cos(ang_r)), by_row(jnp.sin(ang_r)), by_col(jnp.cos(ang_c)), by_col(jnp.sin(ang_c))
    ctx = lambda a, fill: jnp.concatenate([jnp.full((n_ctx, a.shape[1]), fill, F32), a], axis=0)
    spread = lambda a, m: jnp.dot(a, jnp.asarray(m, F32), precision=HI)
    dd = np.arange(HEAD_SLOT) - MLA_NOPE
    active = (dd >= 0) & (dd < MLA_ROPE)
    src = (np.clip(dd, 0, MLA_ROPE - 1) // 16) * half + np.clip(dd, 0, MLA_ROPE - 1) % half
    pick = (np.arange(2 * half)[:, None] == src[None, :]) & active[None, :]
    sign = np.where((dd % 16) < half, -1.0, 1.0)[None, :]
    cosq = spread(ctx(jnp.concatenate([cos_r, cos_c], axis=1), 1.0), pick) + jnp.asarray(~active, F32)[None, :]
    sinq = spread(ctx(jnp.concatenate([sin_r, sin_c], axis=1), 0.0), pick * sign)
    rhalf = RET_DK // 2
    rinv = ROPE_BASE ** (-jnp.arange(rhalf, dtype=F32) / rhalf)
    a_hi = (GRID_W * jnp.arange(n_rows, dtype=F32))[:, None, None] * rinv
    a_lo = jnp.arange(GRID_W, dtype=F32)[None, :, None] * rinv
    cos1 = (jnp.cos(a_hi) * jnp.cos(a_lo) - jnp.sin(a_hi) * jnp.sin(a_lo)).reshape(seq, rhalf)
    sin1 = (jnp.sin(a_hi) * jnp.cos(a_lo) + jnp.cos(a_hi) * jnp.sin(a_lo)).reshape(seq, rhalf)
    rl = np.arange(RET_HEADS * RET_DK) % RET_DK
    rpick = np.arange(rhalf)[:, None] == (rl % rhalf)[None, :]
    rsign = np.where(rl < rhalf, -1.0, 1.0)[None, :]
    cosr = spread(ctx(cos1, 1.0), rpick)
    sinr = spread(ctx(sin1, 0.0), rpick * rsign)
    return cosq, sinq, cosr, sinr


def kernel(x, c, ctx, c_ctx, mod_w, mod_b, norm1_w, norm2_w, w_in, mla_q_norm, mla_w_uq, mla_kv_norm, mla_w_ukv, ret_decay_logit, ret_norm_w, gla_w_gate, gla_b_gate, gla_norm_w, w_branch, w_out, ffn_w_gate, ffn_w_up, ffn_w_down, moe_router, moe_w_gate, moe_w_up, moe_w_down, final_norm_w):
    b, seq, d = x.shape
    n_ctx = ctx.shape[1]
    depth = mod_w.shape[0]
    assert d == D_MODEL and seq % GRID_W == 0
    assert n_ctx % TOKEN_TILE == 0 and n_ctx % SCAN_CHUNK == 0
    assert seq % TOKEN_TILE == 0 and seq % SCAN_CHUNK == 0
    nct = n_ctx // TOKEN_TILE

    rows = 8 * ((b + 1 + 7) // 8)
    cvec = jnp.zeros((rows, d), F32).at[0:b].set(c).at[b].set(c_ctx)
    mod = _modulation(cvec, mod_w, mod_b)
    cosq, sinq, cosr, sinr = _rope_tables(n_ctx, seq)
    fnw = final_norm_w.reshape(1, d)
    row2 = lambda a: a.reshape(1, -1)

    xs = (ctx, x)
    for l in range(depth):
        last = l == depth - 1
        modsel = jnp.stack([jnp.broadcast_to(mod[l, b], (b, 6 * d)), mod[l, 0:b]], axis=1)
        w1, wbg = _prep_w_in(w_in[l])
        wq, wqs, wk, wv = _prep_mla(mla_w_uq[l], mla_w_ukv[l])
        wg2, bg2 = _prep_gla_gate(gla_w_gate[l], gla_b_gate[l])
        nw1, nw2 = row2(norm1_w[l]), row2(norm2_w[l])

        (q, k, v, rq, rk, rv, rg, gq, gk, gv, gr, la) = _inproj(
            xs, modsel, nw1, w1, row2(mla_q_norm[l]), wq, wqs, row2(mla_kv_norm[l]), wk, wv,
            wg2, bg2, cosq, sinq, cosr, sinr, nct)
        oa = _attention(q, k, v, nct)
        dl = jnp.broadcast_to(ret_decay_logit[l][:, :, None, None], (2, RET_HEADS, 8, LANES))
        rof, rob, gof, gob = _scans(dl, rq, rk, rv, gq, gk, gv, la, n_ctx)
        is_moe = l % 2 == 1
        xs = _merge(xs, modsel, nw1, oa, rof, rob, rg, gof, gob, gr, wbg,
                    w_branch[l].astype(BF16), w_out[l].astype(BF16),
                    row2(ret_norm_w[l]), row2(gla_norm_w[l]), nct, is_moe and last)
        i = l // 2
        if not is_moe:
            xs = _ffn(xs, modsel, nw2, ffn_w_gate[i].astype(BF16), ffn_w_up[i].astype(BF16),
                      ffn_w_down[i].astype(BF16), fnw, nct, last)
            continue
        rt = jnp.zeros((d, LANES), F32).at[:, 0:N_EXPERTS].set(moe_router[i])
        experts = (moe_w_gate[i].astype(BF16), moe_w_up[i].astype(BF16), moe_w_down[i].astype(BF16))
        if last:
            xs = _moe(xs, 1, modsel, nw2, rt, *experts, fnw, True)
        else:
            xs = jnp.concatenate([_moe(xs[:, :n_ctx], 0, modsel, nw2, rt, *experts, fnw, False),
                                  _moe(xs[:, n_ctx:], 1, modsel, nw2, rt, *experts, fnw, False)], axis=1)
    return xs
```

```python
import functools

import numpy as np
import jax
import jax.numpy as jnp
from jax import lax
from jax.experimental import pallas as pl
from jax.experimental.pallas import tpu as pltpu

D_MODEL = 1024
GRID_W = 64
EPS = 1e-6
ROPE_BASE = 10000.0

MLA_HEADS = 8
MLA_NOPE = 64
MLA_ROPE = 32
MLA_V = 64
MLA_Q_RANK = 256
MLA_KV_RANK = 128

RET_HEADS = 4
RET_DK = 64
RET_DV = 128

GLA_HEADS = 4
GLA_DK = 64
GLA_DV = 128
GLA_GATE_RANK = 16
GLA_TAU = 16.0

N_BRANCH = 3
BRANCH_W = 512
D_FF = 2816
N_EXPERTS = 8

IN_SPLITS = (
    MLA_Q_RANK, MLA_KV_RANK, MLA_ROPE,
    RET_HEADS * RET_DK, RET_HEADS * RET_DK, RET_HEADS * RET_DV, RET_HEADS * RET_DV,
    GLA_HEADS * GLA_DK, GLA_HEADS * GLA_DK, GLA_HEADS * GLA_DV, GLA_HEADS * GLA_DV,
    2 * GLA_GATE_RANK,
    N_BRANCH * D_MODEL,
)

F32 = jnp.float32
BF16 = jnp.bfloat16
HI = lax.Precision.HIGHEST

LANES = 128
HEAD_SLOT = 128
TOKEN_TILE = 256
ATT_DEPTH = 3
ATT_UNROLL = 10
ATT_SUM_ROWS = 16
SCAN_CHUNK = 256
GLA_BASE = 32
MOE_TILE = 1024
MOE_BLOCK = 256
MOD_COLS_TILE = 1536
VMEM_LIMIT = 56 * 1024 * 1024


def _silu(x):
    return x / (1.0 + jnp.exp(-x))


def _sigmoid(x):
    return 1.0 / (1.0 + jnp.exp(-x))


def _log_sigmoid(z):
    return jnp.minimum(z, 0.0) - jnp.log(1.0 + jnp.exp(-jnp.abs(z)))


def _rms(x, w):
    return x * lax.rsqrt(jnp.mean(x * x, axis=-1, keepdims=True) + EPS) * w


def _dot(a, b, precision=None):
    return jnp.dot(a, b, preferred_element_type=F32, precision=precision)


def _dot_nt(a, b):
    return lax.dot_general(a, b, (((1,), (1,)), ((), ())), preferred_element_type=F32)


def _bf16_pieces(x, n):
    out = []
    for _ in range(n):
        piece = x.astype(BF16)
        out.append(piece)
        x = x - piece.astype(F32)
    return out


def _dot_split(a, b):
    (a0, a1), (b0, b1) = _bf16_pieces(a, 2), _bf16_pieces(b, 2)
    return _dot(a0, b0) + (_dot(a0, b1) + _dot(a1, b0))


def _params(sem):
    return pltpu.CompilerParams(dimension_semantics=sem, vmem_limit_bytes=VMEM_LIMIT)


def _const_spec(shape):
    nd = len(shape)
    return pl.BlockSpec(shape, lambda *_: (0,) * nd, pipeline_mode=pl.Buffered(1))


def _mod_rows(mod_ref, is_ctx):
    m = mod_ref[0]
    return jnp.where(is_ctx, m[0:1], m[1:2])


def _mod_kernel(c_ref, w_ref, b_ref, o_ref):
    o_ref[0] = _dot(_silu(c_ref[...]), w_ref[0], HI) + b_ref[0]


def _modulation(cvec, mod_w, mod_b):
    depth, d, n6 = mod_w.shape
    rows = cvec.shape[0]
    tn = MOD_COLS_TILE
    return pl.pallas_call(
        _mod_kernel,
        grid=(depth, n6 // tn),
        in_specs=[pl.BlockSpec((rows, d), lambda l, j: (0, 0)),
                  pl.BlockSpec((1, d, tn), lambda l, j: (l, 0, j)),
                  pl.BlockSpec((1, 1, tn), lambda l, j: (l, 0, j))],
        out_specs=pl.BlockSpec((1, rows, tn), lambda l, j: (l, 0, j)),
        out_shape=jax.ShapeDtypeStruct((depth, rows, n6), F32),
        compiler_params=_params(("arbitrary", "arbitrary")),
    )(cvec, mod_w, mod_b.reshape(depth, 1, n6))


_C_CQ = (0, 256)
_C_CKV_KRA = (256, 512)
_C_KRB_GZ = (512, 768)
_C_RQ = (768, 1024)
_C_RQS = (1024, 1280)
_C_RK = (1280, 1536)
_C_RKS = (1536, 1792)
_C_RV = (1792, 2304)
_C_RG = (2304, 2816)
_C_GQ = (2816, 3072)
_C_GK = (3072, 3328)
_C_GV = (3328, 3840)
_C_GR = (3840, 4352)


def _seq_inputs(xs, nct, off=0):
    tm = TOKEN_TILE
    if not isinstance(xs, tuple):
        return [pl.BlockSpec((1, tm, xs.shape[-1]), lambda i, t: (i, t + off, 0))], [xs]
    d = xs[0].shape[-1]
    return ([pl.BlockSpec((1, tm, d), lambda i, t: (i, jnp.minimum(t + off, nct - 1), 0)),
             pl.BlockSpec((1, tm, d), lambda i, t: (i, jnp.maximum(t + off - nct, 0), 0))], list(xs))


def _seq_tile(x_refs, is_ctx):
    if len(x_refs) == 1:
        return x_refs[0][0]
    return jnp.where(is_ctx, x_refs[0][0], x_refs[1][0])


def _inproj_kernel(nct, nx, *refs):
    x_refs = refs[:nx]
    (mod_ref, nw_ref, w_ref, qn_ref, wq_ref, wqs_ref, kvn_ref, wk_ref, wv_ref, wg2_ref, bg2_ref,
     cq_ref, sq_ref, cr_ref, sr_ref, q_out, k_out, v_out, rq_out, rk_out, rv_out, rg_out,
     gq_out, gk_out, gv_out, gr_out, la_out) = refs[nx:]
    d = D_MODEL
    is_ctx = pl.program_id(1) < nct
    row = _mod_rows(mod_ref, is_ctx)
    sh, sc = row[:, 0:d], row[:, d:2 * d]
    u = (_rms(_seq_tile(x_refs, is_ctx), nw_ref[...]) * (1.0 + sc) + sh).astype(BF16)

    def seg(c):
        return _dot(u, w_ref[:, c[0]:c[1]])

    cqn = _rms(seg(_C_CQ), qn_ref[...]).astype(BF16)
    q_all = _dot(cqn, wq_ref[...])
    q_swp = _dot(cqn, wqs_ref[...])
    ckv_kra = seg(_C_CKV_KRA)
    krb_gz = seg(_C_KRB_GZ)
    ckvn = _rms(ckv_kra[:, 0:LANES], kvn_ref[...]).astype(BF16)
    k_all = _dot(ckvn, wk_ref[...])
    v_out[0, 0] = _dot(ckvn, wv_ref[...]).T.astype(BF16)
    cosq, sinq = cq_ref[...], sq_ref[...]
    k_rope = ckv_kra[:, LANES:] * cosq + krb_gz[:, 0:LANES] * sinq
    for h in range(MLA_HEADS):
        sl = slice(h * HEAD_SLOT, (h + 1) * HEAD_SLOT)
        q_out[0, h, 0] = (q_all[:, sl] * cosq + q_swp[:, sl] * sinq).T.astype(BF16)
        k_out[0, h] = (k_all[:, sl] + k_rope).astype(BF16)

    cosr, sinr = cr_ref[...], sr_ref[...]
    rq_out[0] = (seg(_C_RQ) * cosr + seg(_C_RQS) * sinr).astype(BF16)
    rk_out[0] = (seg(_C_RK) * cosr + seg(_C_RKS) * sinr).astype(BF16)
    rv_out[0] = seg(_C_RV).astype(BF16)
    rg_out[0] = seg(_C_RG).astype(BF16)

    gq_out[0] = seg(_C_GQ).astype(BF16)
    gk_out[0] = seg(_C_GK).astype(BF16)
    gv_out[0] = seg(_C_GV).astype(BF16)
    gr_out[0] = seg(_C_GR).astype(BF16)
    z = _dot_split(krb_gz[:, LANES:], wg2_ref[...]) + bg2_ref[...]
    la_out[0] = _log_sigmoid(z) * (1.0 / GLA_TAU)


def _inproj(xs, modsel, nw, w1, qn, wq, wqs, kvn, wk, wv, wg2, bg2, cosq, sinq, cosr, sinr, nct):
    x_specs, x_ops = _seq_inputs(xs, nct)
    b, d = x_ops[0].shape[0], x_ops[0].shape[-1]
    n = sum(a.shape[1] for a in x_ops)
    tm = TOKEN_TILE
    nt = n // tm
    h = MLA_HEADS
    tok = lambda w: pl.BlockSpec((1, tm, w), lambda i, t: (i, t, 0))
    tab = lambda w: pl.BlockSpec((tm, w), lambda i, t: (t, 0))
    q_spec = pl.BlockSpec((1, h, 1, HEAD_SLOT, tm), lambda i, t: (i, 0, t, 0, 0))
    k_spec = pl.BlockSpec((1, h, tm, HEAD_SLOT), lambda i, t: (i, 0, t, 0))
    v_spec = pl.BlockSpec((1, 1, h * MLA_V, tm), lambda i, t: (i, t, 0, 0))
    sds = lambda w, dt=BF16: jax.ShapeDtypeStruct((b, n, w), dt)
    q_sds = jax.ShapeDtypeStruct((b, h, nt, HEAD_SLOT, tm), BF16)
    k_sds = jax.ShapeDtypeStruct((b, h, n, HEAD_SLOT), BF16)
    v_sds = jax.ShapeDtypeStruct((b, nt, h * MLA_V, tm), BF16)
    return pl.pallas_call(
        functools.partial(_inproj_kernel, nct, len(x_ops)),
        grid=(b, nt),
        in_specs=x_specs + [pl.BlockSpec((1, 2, 6 * d), lambda i, t: (i, 0, 0)), _const_spec(nw.shape),
                  _const_spec(w1.shape), _const_spec(qn.shape), _const_spec(wq.shape),
                  _const_spec(wqs.shape), _const_spec(kvn.shape), _const_spec(wk.shape),
                  _const_spec(wv.shape), _const_spec(wg2.shape), _const_spec(bg2.shape),
                  tab(HEAD_SLOT), tab(HEAD_SLOT), tab(256), tab(256)],
        out_specs=[q_spec, k_spec, v_spec, tok(256), tok(256), tok(512), tok(512),
                   tok(256), tok(256), tok(512), tok(512), tok(512)],
        out_shape=[q_sds, k_sds, v_sds, sds(256), sds(256), sds(512), sds(512),
                   sds(256), sds(256), sds(512), sds(512), sds(512, F32)],
        compiler_params=_params(("arbitrary", "arbitrary")),
    )(*x_ops, modsel, nw, w1, qn, wq, wqs, kvn, wk, wv, wg2, bg2, cosq, sinq, cosr, sinr)


def _attn_kernel(nct, n_tiles, qt_ref, k_ref, vt_ref, o_ref, *bufs):
    t = TOKEN_TILE
    nd = ATT_DEPTH
    heads = range(2)
    s_bufs = (bufs[0:nd], bufs[nd:2 * nd])
    p_bufs = (bufs[2 * nd:3 * nd], bufs[3 * nd:4 * nd])
    qts = [qt_ref[0, j, 0] for j in heads]
    ones = jnp.ones((ATT_SUM_ROWS, t), BF16)

    def scores(j, c, s_buf):
        s = _dot(k_ref[0, j, pl.ds(pl.multiple_of(c * t, t), t), :], qts[j])
        s_buf[...] = s
        return jnp.max(s, axis=0, keepdims=True)

    def softmax(s_buf, p_buf, m, cmax):
        m_new = jnp.maximum(m, cmax)
        p_buf[...] = jnp.exp2(s_buf[...] - m_new).astype(BF16)
        return m_new, jnp.exp2(m - m_new)

    def values(j, c, p_buf, alpha, acc):
        lhs = jnp.concatenate([vt_ref[0, c, j * MLA_V:(j + 1) * MLA_V, :], ones], axis=0)
        return alpha * acc + _dot(lhs, p_buf[...])

    m_init = jnp.full((1, t), -jnp.inf, F32)
    acc_init = jnp.zeros((MLA_V + ATT_SUM_ROWS, t), F32)

    def finish(accs):
        outs = [acc[0:MLA_V] / acc[MLA_V:MLA_V + 1] for acc in accs]
        o_ref[0] = jnp.concatenate(outs, axis=0).T.astype(BF16)

    @pl.when(pl.program_id(2) < nct)
    def _():
        accs = []
        for j in heads:
            m, acc = m_init, acc_init
            for c in range(nct):
                cmax = scores(j, c, s_bufs[j][0])
                m, alpha = softmax(s_bufs[j][0], p_bufs[j][0], m, cmax)
                acc = values(j, c, p_bufs[j][0], alpha, acc)
            accs.append(acc)
        finish(accs)

    def step(c, slot, state, lookahead=True):
        ahead = [scores(j, c + 2, s_bufs[j][(slot + 2) % nd]) if lookahead else None for j in heads]
        accs = [values(j, c - 1, p_bufs[j][(slot - 1) % nd], state[j][1], state[j][2]) for j in heads]
        stats = [softmax(s_bufs[j][slot], p_bufs[j][slot], state[j][0], state[j][3]) for j in heads]
        return tuple((stats[j][0], stats[j][1], accs[j], state[j][4], ahead[j]) for j in heads)

    @pl.when(pl.program_id(2) >= nct)
    def _():
        state = []
        for j in heads:
            cmax0 = scores(j, 0, s_bufs[j][0])
            cmax1 = scores(j, 1, s_bufs[j][1])
            m, alpha = softmax(s_bufs[j][0], p_bufs[j][0], m_init, cmax0)
            state.append((m, alpha, acc_init, cmax1, scores(j, 2, s_bufs[j][2])))

        def trip(i, state):
            for k in range(nd):
                state = step(1 + nd * i + k, (1 + k) % nd, state)
            return state

        state = lax.fori_loop(0, (n_tiles - 3) // nd, trip, tuple(state), unroll=ATT_UNROLL)
        for c in (n_tiles - 2, n_tiles - 1):
            state = step(c, c % nd, state, lookahead=False)
        finish([values(j, n_tiles - 1, p_bufs[j][(n_tiles - 1) % nd], state[j][1], state[j][2])
                for j in heads])


def _attention(qt, k, vt, nct):
    b, h, n_tiles, w, t = qt.shape
    n = n_tiles * t
    assert n_tiles >= 3 and (n_tiles - 3) % ATT_DEPTH == 0
    return pl.pallas_call(
        functools.partial(_attn_kernel, nct, n_tiles),
        grid=(b, h // 2, n_tiles),
        in_specs=[pl.BlockSpec((1, 2, 1, w, t), lambda i, p, q: (i, p, q, 0, 0)),
                  pl.BlockSpec((1, 2, n, w), lambda i, p, q: (i, p, 0, 0)),
                  pl.BlockSpec((1, n_tiles, 2 * MLA_V, t), lambda i, p, q: (i, 0, p, 0))],
        out_specs=pl.BlockSpec((1, t, 2 * MLA_V), lambda i, p, q: (i, q, p)),
        out_shape=jax.ShapeDtypeStruct((b, n, h * MLA_V), BF16),
        scratch_shapes=[pltpu.VMEM((t, t), F32)] * (2 * ATT_DEPTH) + [pltpu.VMEM((t, t), BF16)] * (2 * ATT_DEPTH),
        compiler_params=_params(("arbitrary", "arbitrary", "arbitrary")),
    )(qt, k, vt)


def _bwd_chunk(s, nct, nch):
    return jnp.where(s < nct, nct - 1 - s, nch - 1 - (s - nct))


def _ret_body(dl_ref, qf, kf, vf, qb, kb, vb, of_ref, ob_ref, st_ref):
    c = SCAN_CHUNK
    hw = RET_HEADS * RET_DK
    ii = lax.broadcasted_iota(jnp.int32, (c, c), 0)
    jj = lax.broadcasted_iota(jnp.int32, (c, c), 1)
    pos_i = lax.broadcasted_iota(jnp.int32, (c, 1), 0).astype(F32)
    pos_j = lax.broadcasted_iota(jnp.int32, (1, c), 1).astype(F32)
    lane_head = lax.broadcasted_iota(jnp.int32, (c, hw), 1) // RET_DK
    for d, (q_ref, k_ref, v_ref, o_ref) in enumerate(((qf, kf, vf, of_ref), (qb, kb, vb, ob_ref))):
        rev = d == 1
        q, v = q_ref[0], v_ref[0]
        k_t = k_ref[0].astype(F32).T
        k_tb = k_t.astype(BF16)
        st = st_ref[d]
        st_b = st.astype(BF16)
        dist = ((jj - ii) if rev else (ii - jj)).astype(F32)
        for h in range(RET_HEADS):
            lg = _log_sigmoid(dl_ref[d, h])[0:1, 0:1]
            intra = jnp.where(dist >= 0.0, jnp.exp(lg * jnp.maximum(dist, 0.0)), 0.0)
            q_dec = jnp.exp(lg * ((c - pos_i) if rev else (pos_i + 1.0)))
            k_dec = jnp.exp(lg * (pos_j if rev else (c - 1.0 - pos_j)))
            rows = slice(h * RET_DK, (h + 1) * RET_DK)
            cols = slice(h * RET_DV, (h + 1) * RET_DV)
            qm = jnp.where(lane_head == h, q, jnp.zeros_like(q))
            scores = _dot(qm, k_tb) * intra
            o = _dot(scores.astype(BF16), v[:, cols]) + _dot(qm, st_b) * q_dec
            o_ref[0, :, cols] = o
            kd = (k_t[rows] * k_dec).astype(BF16)
            st_ref[d, rows, :] = st[rows] * jnp.exp(lg * float(c)) + _dot(kd, v[:, cols])


def _block_start(i, size):
    return i & ~(size - 1)


def _gla_levels():
    half = SCAN_CHUNK // 2
    out = []
    while half >= GLA_BASE:
        out.append(half)
        half //= 2
    return out


def _gla_body(qf, kf, vf, laf, qb, kb, vb, lab, of_ref, ob_ref, st_ref):
    c = SCAN_CHUNK
    hw = GLA_HEADS * GLA_DK
    levels = _gla_levels()
    ii = lax.broadcasted_iota(jnp.int32, (c, c), 0)
    jj = lax.broadcasted_iota(jnp.int32, (c, c), 1)
    pos = lax.broadcasted_iota(jnp.int32, (c, 1), 0)
    lane_head = lax.broadcasted_iota(jnp.int32, (c, hw), 1) // GLA_DK
    for d, (q_ref, k_ref, v_ref, la_ref, o_ref) in enumerate(((qf, kf, vf, laf, of_ref),
                                                               (qb, kb, vb, lab, ob_ref))):
        rev = d == 1
        q, k, v, la = q_ref[0].astype(F32), k_ref[0].astype(F32), v_ref[0], la_ref[0]

        prefix = _one_hot((jj >= ii) if rev else (jj <= ii))
        b = sum(_dot(prefix, piece) for piece in _bf16_pieces(la, 3))
        b_end = b[0:1] if rev else b[c - 1:c]

        def at_row(size, idx):
            b3 = b.reshape(c // size, size, hw)
            return jnp.broadcast_to(b3[:, idx:idx + 1, :], b3.shape).reshape(c, hw)

        q_t = (q * jnp.exp(b)).astype(BF16)
        k_tt = (k * jnp.exp(b_end - b)).T
        dec_col = jnp.exp(jnp.broadcast_to(b_end, (LANES, hw)).T[:, 0:1])

        terms = []
        for half in levels:
            e = at_row(2 * half, half if rev else half - 1)
            inner = pos & (2 * half - 1)
            late = (inner < half) if rev else (inner >= half)
            qh = jnp.where(late, q * jnp.exp(jnp.where(late, b - e, 0.0)), 0.0)
            kh = jnp.where(late, 0.0, k * jnp.exp(jnp.where(late, 0.0, e - b)))
            terms.append((qh.astype(BF16), kh.astype(BF16), _block_start(ii, 2 * half) == _block_start(jj, 2 * half)))
        r = at_row(GLA_BASE, GLA_BASE - 1 if rev else 0)
        tri = (jj >= ii) if rev else (jj <= ii)
        terms.append(((q * jnp.exp(b - r)).astype(BF16), (k * jnp.exp(r - b)).astype(BF16),
                      (_block_start(ii, GLA_BASE) == _block_start(jj, GLA_BASE)) & tri))

        st = st_ref[d]
        st_b = st.astype(BF16)
        for h in range(GLA_HEADS):
            rows = slice(h * GLA_DK, (h + 1) * GLA_DK)
            cols = slice(h * GLA_DV, (h + 1) * GLA_DV)
            sel = lane_head == h
            a = jnp.zeros((c, c), F32)
            for qh, kh, mask in terms:
                a = a + jnp.where(mask, _dot_nt(jnp.where(sel, qh, jnp.zeros_like(qh)), kh), 0.0)
            vh = v[:, cols]
            o_ref[0, :, cols] = _dot(a.astype(BF16), vh) + _dot(jnp.where(sel, q_t, jnp.zeros_like(q_t)), st_b)
            st_ref[d, rows, :] = st[rows] * dec_col[rows] + _dot(k_tt[rows].astype(BF16), vh)


def _scan_kernel(*refs):
    ret_refs, gla_refs = refs[0:7] + refs[15:17] + refs[19:20], refs[7:15] + refs[17:19] + refs[20:21]

    @pl.when(pl.program_id(1) == 0)
    def _():
        ret_refs[-1][...] = jnp.zeros_like(ret_refs[-1])
        gla_refs[-1][...] = jnp.zeros_like(gla_refs[-1])

    _ret_body(*ret_refs)
    _gla_body(*gla_refs)


def _scans(dl, rq, rk, rv, gq, gk, gv, la, nct_tokens):
    b, n, hw = rq.shape
    c = SCAN_CHUNK
    nch = n // c
    nct = nct_tokens // c
    vw = rv.shape[-1]
    fwd = lambda w, col=0: pl.BlockSpec((1, c, w), lambda i, s: (i, s, col))
    bwd = lambda w, col=0: pl.BlockSpec((1, c, w), lambda i, s: (i, _bwd_chunk(s, nct, nch), col))
    return pl.pallas_call(
        _scan_kernel,
        grid=(b, nch),
        in_specs=[_const_spec(dl.shape), fwd(hw), fwd(hw), fwd(vw), bwd(hw), bwd(hw), bwd(vw),
                  fwd(hw), fwd(hw), fwd(vw), fwd(hw, 0), bwd(hw), bwd(hw), bwd(vw), bwd(hw, 1)],
        out_specs=[fwd(vw), bwd(vw), fwd(vw), bwd(vw)],
        out_shape=[jax.ShapeDtypeStruct((b, n, vw), F32)] * 4,
        scratch_shapes=[pltpu.VMEM((2, hw, RET_DV), F32), pltpu.VMEM((2, hw, GLA_DV), F32)],
        compiler_params=_params(("arbitrary", "arbitrary")),
    )(dl, rq, rk, rv, rq, rk, rv, gq, gk, gv, la, gq, gk, gv, la)


def _head_norm(o, w, center):
    parts = []
    for h in range(o.shape[-1] // LANES):
        oh = o[:, h * LANES:(h + 1) * LANES]
        if center:
            oh = oh - jnp.mean(oh, axis=-1, keepdims=True)
        parts.append(oh * lax.rsqrt(jnp.mean(oh * oh, axis=-1, keepdims=True) + EPS))
    return jnp.concatenate(parts, axis=-1) * w


def _merge_kernel(nct, off, nx, *refs):
    x_refs = refs[:nx]
    (mod_ref, nw_ref, oa_ref, rof_ref, rob_ref, rg_ref, gof_ref, gob_ref, gr_ref, wbg_ref, wbr_ref,
     wout_ref, rnw_ref, gnw_ref, o_ref) = refs[nx:]
    d = D_MODEL
    is_ctx = pl.program_id(1) + off < nct
    row = _mod_rows(mod_ref, is_ctx)
    sh, sc, g1 = row[:, 0:d], row[:, d:2 * d], row[:, 2 * d:3 * d]
    x = _seq_tile(x_refs, is_ctx)
    u = (_rms(x, nw_ref[...]) * (1.0 + sc) + sh).astype(BF16)
    gate = _sigmoid(_dot(u, wbg_ref[...]))
    yb = _silu(rg_ref[0].astype(F32)) * _head_norm(rof_ref[0] + rob_ref[0], rnw_ref[...], True)
    yc = _silu(gr_ref[0].astype(F32)) * _head_norm(gof_ref[0] + gob_ref[0], gnw_ref[...], False)
    z = (gate[:, 0:d] * _dot(oa_ref[0], wbr_ref[0])
         + gate[:, d:2 * d] * _dot(yb.astype(BF16), wbr_ref[1])
         + gate[:, 2 * d:3 * d] * _dot(yc.astype(BF16), wbr_ref[2]))
    o_ref[0] = x + g1 * _dot(z.astype(BF16), wout_ref[...])


def _merge(xs, modsel, nw, oa, rof, rob, rg, gof, gob, gr, wbg, wbr, wout, rnw, gnw, nct, latent_only):
    tm = TOKEN_TILE
    off = nct if latent_only else 0
    x_specs, x_ops = _seq_inputs(xs, nct, off)
    b, d = x_ops[0].shape[0], x_ops[0].shape[-1]
    nt = sum(a.shape[1] for a in x_ops) // tm - off
    tok = lambda w: pl.BlockSpec((1, tm, w), lambda i, t: (i, t + off, 0))
    return pl.pallas_call(
        functools.partial(_merge_kernel, nct, off, len(x_ops)),
        grid=(b, nt),
        in_specs=x_specs + [pl.BlockSpec((1, 2, 6 * d), lambda i, t: (i, 0, 0)), _const_spec(nw.shape),
                  tok(512), tok(512), tok(512), tok(512), tok(512), tok(512), tok(512),
                  _const_spec(wbg.shape), _const_spec(wbr.shape), _const_spec(wout.shape),
                  _const_spec(rnw.shape), _const_spec(gnw.shape)],
        out_specs=pl.BlockSpec((1, tm, d), lambda i, t: (i, t, 0)),
        out_shape=jax.ShapeDtypeStruct((b, nt * tm, d), F32),
        compiler_params=_params(("arbitrary", "arbitrary")),
    )(*x_ops, modsel, nw, oa, rof, rob, rg, gof, gob, gr, wbg, wbr, wout, rnw, gnw)


def _norm2(x, row, nw_ref):
    d = D_MODEL
    sh, sc = row[:, 3 * d:4 * d], row[:, 4 * d:5 * d]
    return _rms(x, nw_ref[...]) * (1.0 + sc) + sh


def _finish(x, y, row, final, fnw_ref):
    d = D_MODEL
    out = x + row[:, 5 * d:6 * d] * y
    if final:
        out = _rms(out, fnw_ref[...])
    return out


def _ffn_kernel(nct, off, final, x_ref, mod_ref, nw_ref, wg_ref, wu_ref, wd_ref, fnw_ref, o_ref):
    row = _mod_rows(mod_ref, pl.program_id(1) + off < nct)
    x = x_ref[0]
    v = _norm2(x, row, nw_ref).astype(BF16)
    hdn = _silu(_dot(v, wg_ref[...])) * _dot(v, wu_ref[...])
    o_ref[0] = _finish(x, _dot(hdn.astype(BF16), wd_ref[...]), row, final, fnw_ref)


def _ffn(xs, modsel, nw, wg, wu, wd, fnw, nct, final):
    b, n, d = xs.shape
    tm = TOKEN_TILE
    off = nct if final else 0
    nt = n // tm - off
    return pl.pallas_call(
        functools.partial(_ffn_kernel, nct, off, final),
        grid=(b, nt),
        in_specs=[pl.BlockSpec((1, tm, d), lambda i, t: (i, t + off, 0)),
                  pl.BlockSpec((1, 2, 6 * d), lambda i, t: (i, 0, 0)), _const_spec(nw.shape),
                  _const_spec(wg.shape), _const_spec(wu.shape), _const_spec(wd.shape),
                  _const_spec(fnw.shape)],
        out_specs=pl.BlockSpec((1, tm, d), lambda i, t: (i, t, 0)),
        out_shape=jax.ShapeDtypeStruct((b, nt * tm, d), F32),
        compiler_params=_params(("arbitrary", "arbitrary")),
    )(xs, modsel, nw, wg, wu, wd, fnw)


def _top2_gates(logits):
    lane = lax.broadcasted_iota(jnp.int32, logits.shape, 1)
    m1 = jnp.max(logits, axis=-1, keepdims=True)
    i1 = jnp.min(jnp.where(logits == m1, lane, LANES), axis=-1, keepdims=True)
    rest = jnp.where(lane == i1, -jnp.inf, logits)
    m2 = jnp.max(rest, axis=-1, keepdims=True)
    i2 = jnp.min(jnp.where(rest == m2, lane, LANES), axis=-1, keepdims=True)
    e2 = jnp.exp(m2 - m1)
    w1 = 1.0 / (1.0 + e2)
    gates = jnp.where(lane == i1, w1, jnp.where(lane == i2, e2 * w1, 0.0))
    return gates, (lane == i1) | (lane == i2)


def _one_hot(cond):
    return jnp.where(cond, 1.0, 0.0).astype(BF16)


def _route_kernel(mrow, x_ref, mod_ref, nw_ref, rt_ref, v_out, gate_out, rank_out, rankt_out):
    v = _norm2(x_ref[0], mod_ref[0][mrow:mrow + 1], nw_ref)
    v_out[0] = v.astype(BF16)
    tt = v.shape[0]
    lane = lax.broadcasted_iota(jnp.int32, (tt, LANES), 1)
    logits = jnp.where(lane < N_EXPERTS, _dot_split(v, rt_ref[...]), -jnp.inf)
    gates, routed = _top2_gates(logits)
    gate_out[0] = gates
    ti = lax.broadcasted_iota(jnp.int32, (tt, tt), 0)
    tj = lax.broadcasted_iota(jnp.int32, (tt, tt), 1)
    rank_out[0] = jnp.where(routed, _dot(_one_hot(tj < ti), _one_hot(routed)), -1.0)
    routed_t = jnp.where(routed, 1.0, 0.0).T
    rank_t = jnp.where(routed_t > 0.5, _dot(routed_t.astype(BF16), _one_hot(ti < tj)), -1.0)
    rankt_out[0, 0] = rank_t[0:N_EXPERTS]


def _experts_kernel(v_ref, gate_ref, rank_ref, rankt_ref, wg_ref, wu_ref, wd_ref, o_ref, acc_ref):
    e, s = pl.program_id(1), pl.program_id(2)
    tt = v_ref.shape[1]
    fh = D_FF // 2
    tile0 = pl.multiple_of(s * tt, tt)
    lane = lax.broadcasted_iota(jnp.int32, (tt, LANES), 1)
    g_col = jnp.sum(jnp.where(lane == e, gate_ref[0], 0.0), axis=-1, keepdims=True)
    rk_col = jnp.sum(jnp.where(lane == e, rank_ref[0], 0.0), axis=-1, keepdims=True)
    rk_row = rankt_ref[0, 0, pl.ds(e, 1), :]
    n_routed = jnp.max(rk_col).astype(jnp.int32) + 1

    @pl.when(e == 0)
    def _():
        acc_ref[pl.ds(tile0, tt), :] = jnp.zeros((tt, D_MODEL), BF16)

    def block(first_slot, r):
        base = first_slot.astype(F32)
        slot_i = base + lax.broadcasted_iota(jnp.int32, (r, 1), 0).astype(F32)
        xg = _dot(_one_hot(rk_row == slot_i), v_ref[0]).astype(BF16)
        y = jnp.zeros((r, D_MODEL), F32)
        for c in range(2):
            cols = slice(c * fh, (c + 1) * fh)
            hdn = _silu(_dot(xg, wg_ref[0, :, cols])) * _dot(xg, wu_ref[0, :, cols])
            y = y + _dot(hdn.astype(BF16), wd_ref[0, cols, :])
        y = y.astype(BF16)
        slot_j = base + lax.broadcasted_iota(jnp.int32, (1, r), 1).astype(F32)
        for c in range(tt // TOKEN_TILE):
            rows = slice(c * TOKEN_TILE, (c + 1) * TOKEN_TILE)
            dst = pl.ds(tile0 + c * TOKEN_TILE, TOKEN_TILE)
            part = g_col[rows] * _dot(_one_hot(rk_col[rows] == slot_j), y)
            acc_ref[dst, :] = (acc_ref[dst, :].astype(F32) + part).astype(BF16)

    r, half = MOE_BLOCK, MOE_BLOCK // 2
    n_full = n_routed // r
    left = n_routed - n_full * r

    def full_block(i, carry):
        block(i * r, r)
        return carry

    lax.fori_loop(0, n_full + (left > half).astype(jnp.int32), full_block, 0)

    @pl.when((left > 0) & (left <= half))
    def _():
        block(n_full * r, half)

    @pl.when(e == N_EXPERTS - 1)
    def _():
        o_ref[0] = acc_ref[pl.ds(tile0, tt), :]


def _moe_finish_kernel(mrow, final, x_ref, y_ref, mod_ref, fnw_ref, o_ref):
    o_ref[0] = _finish(x_ref[0], y_ref[0].astype(F32), mod_ref[0][mrow:mrow + 1], final, fnw_ref)


def _moe(slab, mrow, modsel, nw, rt, wg, wu, wd, fnw, final):
    b, m, d = slab.shape
    tt = min(MOE_TILE, m)
    assert m % tt == 0 and tt % TOKEN_TILE == 0
    ns = m // tt
    ne, _, f = wg.shape
    mod_spec = lambda nd: pl.BlockSpec((1, 2, 6 * d), (lambda i, s: (i, 0, 0)) if nd == 2 else
                                       (lambda i, e, s: (i, 0, 0)))
    v, gates, rank, rank_t = pl.pallas_call(
        functools.partial(_route_kernel, mrow),
        grid=(b, ns),
        in_specs=[pl.BlockSpec((1, tt, d), lambda i, s: (i, s, 0)), mod_spec(2), _const_spec(nw.shape),
                  _const_spec(rt.shape)],
        out_specs=[pl.BlockSpec((1, tt, d), lambda i, s: (i, s, 0)),
                   pl.BlockSpec((1, tt, LANES), lambda i, s: (i, s, 0)),
                   pl.BlockSpec((1, tt, LANES), lambda i, s: (i, s, 0)),
                   pl.BlockSpec((1, 1, ne, tt), lambda i, s: (i, s, 0, 0))],
        out_shape=[jax.ShapeDtypeStruct((b, m, d), BF16), jax.ShapeDtypeStruct((b, m, LANES), F32),
                   jax.ShapeDtypeStruct((b, m, LANES), F32), jax.ShapeDtypeStruct((b, ns, ne, tt), F32)],
        compiler_params=_params(("arbitrary", "arbitrary")),
    )(slab, modsel, nw, rt)

    tok = lambda w: pl.BlockSpec((1, tt, w), lambda i, e, s: (i, s, 0))
    wspec = lambda shape, bufs=1: pl.BlockSpec((1,) + shape, lambda i, e, s: (e, 0, 0),
                                               pipeline_mode=pl.Buffered(bufs))
    last_pass = lambda i, e, s: (i, jnp.where(e == ne - 1, s, 0), 0)
    y_sum = pl.pallas_call(
        _experts_kernel,
        grid=(b, ne, ns),
        in_specs=[tok(d), tok(LANES), tok(LANES),
                  pl.BlockSpec((1, 1, ne, tt), lambda i, e, s: (i, s, 0, 0)),
                  wspec((d, f), 2), wspec((d, f)), wspec((f, d))],
        out_specs=pl.BlockSpec((1, tt, d), last_pass),
        out_shape=jax.ShapeDtypeStruct((b, m, d), BF16),
        scratch_shapes=[pltpu.VMEM((m, d), BF16)],
        compiler_params=_params(("arbitrary", "arbitrary", "arbitrary")),
    )(v, gates, rank, rank_t, wg, wu, wd)

    tm = TOKEN_TILE
    row_tile = lambda: pl.BlockSpec((1, tm, d), lambda i, t: (i, t, 0))
    return pl.pallas_call(
        functools.partial(_moe_finish_kernel, mrow, final),
        grid=(b, m // tm),
        in_specs=[row_tile(), row_tile(), mod_spec(2), _const_spec(fnw.shape)],
        out_specs=row_tile(),
        out_shape=jax.ShapeDtypeStruct((b, m, d), F32),
        compiler_params=_params(("arbitrary", "arbitrary")),
    )(slab, y_sum, modsel, fnw)


def _axial_perm():
    dd = np.arange(MLA_ROPE)
    return np.where(dd % 16 < 8, dd + 8, dd - 8)


def _ret_perm():
    dd = np.arange(RET_HEADS * RET_DK)
    return np.where(dd % RET_DK < RET_DK // 2, dd + RET_DK // 2, dd - RET_DK // 2)


def _prep_w_in(w):
    offs = np.cumsum((0,) + IN_SPLITS)
    cq, ckv, kr, rq, rk, rv, rg, gq, gk, gv, gr, gz, bg = [w[:, offs[i]:offs[i + 1]] for i in range(13)]
    zeros = lambda n: jnp.zeros((w.shape[0], n), w.dtype)
    rk = rk * (RET_DK ** -0.5)
    gq = gq * (GLA_DK ** -0.5)
    w1 = jnp.concatenate([
        cq, ckv,
        zeros(MLA_NOPE), kr, zeros(HEAD_SLOT - MLA_NOPE - MLA_ROPE),
        zeros(MLA_NOPE), kr[:, _axial_perm()], zeros(HEAD_SLOT - MLA_NOPE - MLA_ROPE),
        gz, zeros(LANES - 2 * GLA_GATE_RANK),
        rq, rq[:, _ret_perm()], rk, rk[:, _ret_perm()], rv, rg,
        gq, gk, gv, gr], axis=1)
    return w1.astype(BF16), bg.astype(BF16)


def _prep_mla(w_uq, w_ukv):
    r = w_uq.shape[0]
    scale = (MLA_NOPE + MLA_ROPE) ** -0.5 * np.log2(np.e)
    wq3 = w_uq.reshape(r, MLA_HEADS, MLA_NOPE + MLA_ROPE) * scale
    pad = jnp.zeros((r, MLA_HEADS, HEAD_SLOT - MLA_NOPE - MLA_ROPE), w_uq.dtype)
    wq = jnp.concatenate([wq3, pad], axis=-1)
    wqs = jnp.concatenate([jnp.zeros((r, MLA_HEADS, MLA_NOPE), w_uq.dtype),
                           wq3[:, :, MLA_NOPE:][:, :, _axial_perm()], pad], axis=-1)
    rk = w_ukv.shape[0]
    wkv3 = w_ukv.reshape(rk, MLA_HEADS, MLA_NOPE + MLA_V)
    wk = jnp.concatenate([wkv3[:, :, :MLA_NOPE],
                          jnp.zeros((rk, MLA_HEADS, HEAD_SLOT - MLA_NOPE), w_ukv.dtype)], axis=-1)
    wv = wkv3[:, :, MLA_NOPE:]
    flat = lambda a: a.reshape(a.shape[0], -1).astype(BF16)
    return flat(wq), flat(wqs), flat(wk), flat(wv)


def _prep_gla_gate(w_gate, b_gate):
    hw = GLA_HEADS * GLA_DK
    wg2 = jnp.zeros((LANES, 2 * hw), F32)
    wg2 = wg2.at[0:GLA_GATE_RANK, 0:hw].set(w_gate[0])
    wg2 = wg2.at[GLA_GATE_RANK:2 * GLA_GATE_RANK, hw:2 * hw].set(w_gate[1])
    return wg2, b_gate.reshape(1, 2 * hw)


def _rope_tables(n_ctx, seq):
    n_rows = seq // GRID_W
    half = MLA_ROPE // 4
    inv = ROPE_BASE ** (-jnp.arange(half, dtype=F32) / half)
    ang_r = jnp.arange(n_rows, dtype=F32)[:, None] * inv[None, :]
    ang_c = jnp.arange(GRID_W, dtype=F32)[:, None] * inv[None, :]
    by_row = lambda a: jnp.broadcast_to(a[:, None, :], (n_rows, GRID_W, half)).reshape(seq, half)
    by_col = lambda a: jnp.broadcast_to(a[None, :, :], (n_rows, GRID_W, half)).reshape(seq, half)
    cos_r, sin_r, cos_c, sin_c = by_row(jnp.cos(ang_r)), by_row(jnp.sin(ang_r)), by_col(jnp.cos(ang_c)), by_col(jnp.sin(ang_c))
    ctx = lambda a, fill: jnp.concatenate([jnp.full((n_ctx, a.shape[1]), fill, F32), a], axis=0)
    spread = lambda a, m: jnp.dot(a, jnp.asarray(m, F32), precision=HI)
    dd = np.arange(HEAD_SLOT) - MLA_NOPE
    active = (dd >= 0) & (dd < MLA_ROPE)
    src = (np.clip(dd, 0, MLA_ROPE - 1) // 16) * half + np.clip(dd, 0, MLA_ROPE - 1) % half
    pick = (np.arange(2 * half)[:, None] == src[None, :]) & active[None, :]
    sign = np.where((dd % 16) < half, -1.0, 1.0)[None, :]
    cosq = spread(ctx(jnp.concatenate([cos_r, cos_c], axis=1), 1.0), pick) + jnp.asarray(~active, F32)[None, :]
    sinq = spread(ctx(jnp.concatenate([sin_r, sin_c], axis=1), 0.0), pick * sign)
    rhalf = RET_DK // 2
    rinv = ROPE_BASE ** (-jnp.arange(rhalf, dtype=F32) / rhalf)
    a_hi = (GRID_W * jnp.arange(n_rows, dtype=F32))[:, None, None] * rinv
    a_lo = jnp.arange(GRID_W, dtype=F32)[None, :, None] * rinv
    cos1 = (jnp.cos(a_hi) * jnp.cos(a_lo) - jnp.sin(a_hi) * jnp.sin(a_lo)).reshape(seq, rhalf)
    sin1 = (jnp.sin(a_hi) * jnp.cos(a_lo) + jnp.cos(a_hi) * jnp.sin(a_lo)).reshape(seq, rhalf)
    rl = np.arange(RET_HEADS * RET_DK) % RET_DK
    rpick = np.arange(rhalf)[:, None] == (rl % rhalf)[None, :]
    rsign = np.where(rl < rhalf, -1.0, 1.0)[None, :]
    cosr = spread(ctx(cos1, 1.0), rpick)
    sinr = spread(ctx(sin1, 0.0), rpick * rsign)
    return cosq, sinq, cosr, sinr


def kernel(x, c, ctx, c_ctx, mod_w, mod_b, norm1_w, norm2_w, w_in, mla_q_norm, mla_w_uq, mla_kv_norm, mla_w_ukv, ret_decay_logit, ret_norm_w, gla_w_gate, gla_b_gate, gla_norm_w, w_branch, w_out, ffn_w_gate, ffn_w_up, ffn_w_down, moe_router, moe_w_gate, moe_w_up, moe_w_down, final_norm_w):
    b, seq, d = x.shape
    n_ctx = ctx.shape[1]
    depth = mod_w.shape[0]
    assert d == D_MODEL and seq % GRID_W == 0
    assert n_ctx % TOKEN_TILE == 0 and n_ctx % SCAN_CHUNK == 0
    assert seq % TOKEN_TILE == 0 and seq % SCAN_CHUNK == 0
    nct = n_ctx // TOKEN_TILE

    rows = 8 * ((b + 1 + 7) // 8)
    cvec = jnp.zeros((rows, d), F32).at[0:b].set(c).at[b].set(c_ctx)
    mod = _modulation(cvec, mod_w, mod_b)
    cosq, sinq, cosr, sinr = _rope_tables(n_ctx, seq)
    fnw = final_norm_w.reshape(1, d)
    row2 = lambda a: a.reshape(1, -1)

    xs = (ctx, x)
    for l in range(depth):
        last = l == depth - 1
        modsel = jnp.stack([jnp.broadcast_to(mod[l, b], (b, 6 * d)), mod[l, 0:b]], axis=1)
        w1, wbg = _prep_w_in(w_in[l])
        wq, wqs, wk, wv = _prep_mla(mla_w_uq[l], mla_w_ukv[l])
        wg2, bg2 = _prep_gla_gate(gla_w_gate[l], gla_b_gate[l])
        nw1, nw2 = row2(norm1_w[l]), row2(norm2_w[l])

        (q, k, v, rq, rk, rv, rg, gq, gk, gv, gr, la) = _inproj(
            xs, modsel, nw1, w1, row2(mla_q_norm[l]), wq, wqs, row2(mla_kv_norm[l]), wk, wv,
            wg2, bg2, cosq, sinq, cosr, sinr, nct)
        oa = _attention(q, k, v, nct)
        dl = jnp.broadcast_to(ret_decay_logit[l][:, :, None, None], (2, RET_HEADS, 8, LANES))
        rof, rob, gof, gob = _scans(dl, rq, rk, rv, gq, gk, gv, la, n_ctx)
        is_moe = l % 2 == 1
        xs = _merge(xs, modsel, nw1, oa, rof, rob, rg, gof, gob, gr, wbg,
                    w_branch[l].astype(BF16), w_out[l].astype(BF16),
                    row2(ret_norm_w[l]), row2(gla_norm_w[l]), nct, is_moe and last)
        i = l // 2
        if not is_moe:
            xs = _ffn(xs, modsel, nw2, ffn_w_gate[i].astype(BF16), ffn_w_up[i].astype(BF16),
                      ffn_w_down[i].astype(BF16), fnw, nct, last)
            continue
        rt = jnp.pad(moe_router[i], ((0, 0), (0, LANES - N_EXPERTS)))
        experts = (moe_w_gate[i].astype(BF16), moe_w_up[i].astype(BF16), moe_w_down[i].astype(BF16))
        if last:
            xs = _moe(xs, 1, modsel, nw2, rt, *experts, fnw, True)
        else:
            xs = jnp.concatenate([_moe(xs[:, :n_ctx], 0, modsel, nw2, rt, *experts, fnw, False),
                                  _moe(xs[:, n_ctx:], 1, modsel, nw2, rt, *experts, fnw, False)], axis=1)
    return xs
```

```python
import functools

import numpy as np
import jax
import jax.numpy as jnp
from jax import lax
from jax.experimental import pallas as pl
from jax.experimental.pallas import tpu as pltpu

D_MODEL = 1024
GRID_W = 64
EPS = 1e-6
ROPE_BASE = 10000.0

MLA_HEADS = 8
MLA_NOPE = 64
MLA_ROPE = 32
MLA_V = 64
MLA_Q_RANK = 256
MLA_KV_RANK = 128

RET_HEADS = 4
RET_DK = 64
RET_DV = 128

GLA_HEADS = 4
GLA_DK = 64
GLA_DV = 128
GLA_GATE_RANK = 16
GLA_TAU = 16.0

N_BRANCH = 3
BRANCH_W = 512
D_FF = 2816
N_EXPERTS = 8

IN_SPLITS = (
    MLA_Q_RANK, MLA_KV_RANK, MLA_ROPE,
    RET_HEADS * RET_DK, RET_HEADS * RET_DK, RET_HEADS * RET_DV, RET_HEADS * RET_DV,
    GLA_HEADS * GLA_DK, GLA_HEADS * GLA_DK, GLA_HEADS * GLA_DV, GLA_HEADS * GLA_DV,
    2 * GLA_GATE_RANK,
    N_BRANCH * D_MODEL,
)

F32 = jnp.float32
BF16 = jnp.bfloat16
HI = lax.Precision.HIGHEST

LANES = 128
HEAD_SLOT = 128
TOKEN_TILE = 256
ATT_DEPTH = 3
ATT_UNROLL = 10
ATT_SUM_ROWS = 16
SCAN_CHUNK = 256
GLA_BASE = 32
MOE_TILE = 1024
MOE_BLOCK = 256
MOE_TAIL_BLOCKS = (128, 192)
MOD_COLS_TILE = 1536
VMEM_LIMIT = 56 * 1024 * 1024


def _silu(x):
    return x / (1.0 + jnp.exp(-x))


def _sigmoid(x):
    return 1.0 / (1.0 + jnp.exp(-x))


def _log_sigmoid(z):
    return jnp.minimum(z, 0.0) - jnp.log(1.0 + jnp.exp(-jnp.abs(z)))


def _rms(x, w):
    return x * lax.rsqrt(jnp.mean(x * x, axis=-1, keepdims=True) + EPS) * w


def _dot(a, b, precision=None):
    return jnp.dot(a, b, preferred_element_type=F32, precision=precision)


def _dot_nt(a, b):
    return lax.dot_general(a, b, (((1,), (1,)), ((), ())), preferred_element_type=F32)


def _bf16_pieces(x, n):
    out = []
    for _ in range(n):
        piece = x.astype(BF16)
        out.append(piece)
        x = x - piece.astype(F32)
    return out


def _dot_split(a, b):
    (a0, a1), (b0, b1) = _bf16_pieces(a, 2), _bf16_pieces(b, 2)
    return _dot(a0, b0) + (_dot(a0, b1) + _dot(a1, b0))


def _params(sem):
    return pltpu.CompilerParams(dimension_semantics=sem, vmem_limit_bytes=VMEM_LIMIT)


def _const_spec(shape):
    nd = len(shape)
    return pl.BlockSpec(shape, lambda *_: (0,) * nd, pipeline_mode=pl.Buffered(1))


def _mod_rows(mod_ref, is_ctx):
    m = mod_ref[0]
    return jnp.where(is_ctx, m[0:1], m[1:2])


def _mod_kernel(c_ref, w_ref, b_ref, o_ref):
    o_ref[0] = _dot(_silu(c_ref[...]), w_ref[0], HI) + b_ref[0]


def _modulation(cvec, mod_w, mod_b):
    depth, d, n6 = mod_w.shape
    rows = cvec.shape[0]
    tn = MOD_COLS_TILE
    return pl.pallas_call(
        _mod_kernel,
        grid=(depth, n6 // tn),
        in_specs=[pl.BlockSpec((rows, d), lambda l, j: (0, 0)),
                  pl.BlockSpec((1, d, tn), lambda l, j: (l, 0, j)),
                  pl.BlockSpec((1, 1, tn), lambda l, j: (l, 0, j))],
        out_specs=pl.BlockSpec((1, rows, tn), lambda l, j: (l, 0, j)),
        out_shape=jax.ShapeDtypeStruct((depth, rows, n6), F32),
        compiler_params=_params(("arbitrary", "arbitrary")),
    )(cvec, mod_w, mod_b.reshape(depth, 1, n6))


_C_CQ = (0, 256)
_C_CKV_KRA = (256, 512)
_C_KRB_GZ = (512, 768)
_C_RQ = (768, 1024)
_C_RQS = (1024, 1280)
_C_RK = (1280, 1536)
_C_RKS = (1536, 1792)
_C_RV = (1792, 2304)
_C_RG = (2304, 2816)
_C_GQ = (2816, 3072)
_C_GK = (3072, 3328)
_C_GV = (3328, 3840)
_C_GR = (3840, 4352)


def _seq_inputs(xs, nct, off=0):
    tm = TOKEN_TILE
    if not isinstance(xs, tuple):
        return [pl.BlockSpec((1, tm, xs.shape[-1]), lambda i, t: (i, t + off, 0))], [xs]
    d = xs[0].shape[-1]
    return ([pl.BlockSpec((1, tm, d), lambda i, t: (i, jnp.minimum(t + off, nct - 1), 0)),
             pl.BlockSpec((1, tm, d), lambda i, t: (i, jnp.maximum(t + off - nct, 0), 0))], list(xs))


def _seq_tile(x_refs, is_ctx):
    if len(x_refs) == 1:
        return x_refs[0][0]
    return jnp.where(is_ctx, x_refs[0][0], x_refs[1][0])


def _inproj_kernel(nct, nx, *refs):
    x_refs = refs[:nx]
    (mod_ref, nw_ref, w_ref, qn_ref, wq_ref, wqs_ref, kvn_ref, wk_ref, wv_ref, wg2_ref, bg2_ref,
     cq_ref, sq_ref, cr_ref, sr_ref, q_out, k_out, v_out, rq_out, rk_out, rv_out, rg_out,
     gq_out, gk_out, gv_out, gr_out, la_out) = refs[nx:]
    d = D_MODEL
    is_ctx = pl.program_id(1) < nct
    row = _mod_rows(mod_ref, is_ctx)
    sh, sc = row[:, 0:d], row[:, d:2 * d]
    u = (_rms(_seq_tile(x_refs, is_ctx), nw_ref[...]) * (1.0 + sc) + sh).astype(BF16)

    def seg(c):
        return _dot(u, w_ref[:, c[0]:c[1]])

    cqn = _rms(seg(_C_CQ), qn_ref[...]).astype(BF16)
    q_all = _dot(cqn, wq_ref[...])
    q_swp = _dot(cqn, wqs_ref[...])
    ckv_kra = seg(_C_CKV_KRA)
    krb_gz = seg(_C_KRB_GZ)
    ckvn = _rms(ckv_kra[:, 0:LANES], kvn_ref[...]).astype(BF16)
    k_all = _dot(ckvn, wk_ref[...])
    v_out[0, 0] = _dot(ckvn, wv_ref[...]).T.astype(BF16)
    cosq, sinq = cq_ref[...], sq_ref[...]
    k_rope = ckv_kra[:, LANES:] * cosq + krb_gz[:, 0:LANES] * sinq
    for h in range(MLA_HEADS):
        sl = slice(h * HEAD_SLOT, (h + 1) * HEAD_SLOT)
        q_out[0, h, 0] = (q_all[:, sl] * cosq + q_swp[:, sl] * sinq).T.astype(BF16)
        k_out[0, h] = (k_all[:, sl] + k_rope).astype(BF16)

    cosr, sinr = cr_ref[...], sr_ref[...]
    rq_out[0] = (seg(_C_RQ) * cosr + seg(_C_RQS) * sinr).astype(BF16)
    rk_out[0] = (seg(_C_RK) * cosr + seg(_C_RKS) * sinr).astype(BF16)
    rv_out[0] = seg(_C_RV).astype(BF16)
    rg_out[0] = seg(_C_RG).astype(BF16)

    gq_out[0] = seg(_C_GQ).astype(BF16)
    gk_out[0] = seg(_C_GK).astype(BF16)
    gv_out[0] = seg(_C_GV).astype(BF16)
    gr_out[0] = seg(_C_GR).astype(BF16)
    z = _dot_split(krb_gz[:, LANES:], wg2_ref[...]) + bg2_ref[...]
    la_out[0] = _log_sigmoid(z) * (1.0 / GLA_TAU)


def _inproj(xs, modsel, nw, w1, qn, wq, wqs, kvn, wk, wv, wg2, bg2, cosq, sinq, cosr, sinr, nct):
    x_specs, x_ops = _seq_inputs(xs, nct)
    b, d = x_ops[0].shape[0], x_ops[0].shape[-1]
    n = sum(a.shape[1] for a in x_ops)
    tm = TOKEN_TILE
    nt = n // tm
    h = MLA_HEADS
    tok = lambda w: pl.BlockSpec((1, tm, w), lambda i, t: (i, t, 0))
    tab = lambda w: pl.BlockSpec((tm, w), lambda i, t: (t, 0))
    q_spec = pl.BlockSpec((1, h, 1, HEAD_SLOT, tm), lambda i, t: (i, 0, t, 0, 0))
    k_spec = pl.BlockSpec((1, h, tm, HEAD_SLOT), lambda i, t: (i, 0, t, 0))
    v_spec = pl.BlockSpec((1, 1, h * MLA_V, tm), lambda i, t: (i, t, 0, 0))
    sds = lambda w, dt=BF16: jax.ShapeDtypeStruct((b, n, w), dt)
    q_sds = jax.ShapeDtypeStruct((b, h, nt, HEAD_SLOT, tm), BF16)
    k_sds = jax.ShapeDtypeStruct((b, h, n, HEAD_SLOT), BF16)
    v_sds = jax.ShapeDtypeStruct((b, nt, h * MLA_V, tm), BF16)
    return pl.pallas_call(
        functools.partial(_inproj_kernel, nct, len(x_ops)),
        grid=(b, nt),
        in_specs=x_specs + [pl.BlockSpec((1, 2, 6 * d), lambda i, t: (i, 0, 0)), _const_spec(nw.shape),
                  _const_spec(w1.shape), _const_spec(qn.shape), _const_spec(wq.shape),
                  _const_spec(wqs.shape), _const_spec(kvn.shape), _const_spec(wk.shape),
                  _const_spec(wv.shape), _const_spec(wg2.shape), _const_spec(bg2.shape),
                  tab(HEAD_SLOT), tab(HEAD_SLOT), tab(256), tab(256)],
        out_specs=[q_spec, k_spec, v_spec, tok(256), tok(256), tok(512), tok(512),
                   tok(256), tok(256), tok(512), tok(512), tok(512)],
        out_shape=[q_sds, k_sds, v_sds, sds(256), sds(256), sds(512), sds(512),
                   sds(256), sds(256), sds(512), sds(512), sds(512, F32)],
        compiler_params=_params(("arbitrary", "arbitrary")),
    )(*x_ops, modsel, nw, w1, qn, wq, wqs, kvn, wk, wv, wg2, bg2, cosq, sinq, cosr, sinr)


def _attn_kernel(nct, n_tiles, qt_ref, k_ref, vt_ref, o_ref, *bufs):
    t = TOKEN_TILE
    nd = ATT_DEPTH
    heads = range(2)
    s_bufs = (bufs[0:nd], bufs[nd:2 * nd])
    p_bufs = (bufs[2 * nd:3 * nd], bufs[3 * nd:4 * nd])
    qts = [qt_ref[0, j, 0] for j in heads]
    ones = jnp.ones((ATT_SUM_ROWS, t), BF16)

    def scores(j, c, s_buf):
        s = _dot(k_ref[0, j, pl.ds(pl.multiple_of(c * t, t), t), :], qts[j])
        s_buf[...] = s
        return jnp.max(s, axis=0, keepdims=True)

    def softmax(s_buf, p_buf, m, cmax):
        m_new = jnp.maximum(m, cmax)
        p_buf[...] = jnp.exp2(s_buf[...] - m_new).astype(BF16)
        return m_new, jnp.exp2(m - m_new)

    def values(j, c, p_buf, alpha, acc):
        lhs = jnp.concatenate([vt_ref[0, c, j * MLA_V:(j + 1) * MLA_V, :], ones], axis=0)
        return alpha * acc + _dot(lhs, p_buf[...])

    m_init = jnp.full((1, t), -jnp.inf, F32)
    acc_init = jnp.zeros((MLA_V + ATT_SUM_ROWS, t), F32)

    def finish(accs):
        outs = [acc[0:MLA_V] / acc[MLA_V:MLA_V + 1] for acc in accs]
        o_ref[0] = jnp.concatenate(outs, axis=0).T.astype(BF16)

    @pl.when(pl.program_id(2) < nct)
    def _():
        accs = []
        for j in heads:
            m, acc = m_init, acc_init
            for c in range(nct):
                cmax = scores(j, c, s_bufs[j][0])
                m, alpha = softmax(s_bufs[j][0], p_bufs[j][0], m, cmax)
                acc = values(j, c, p_bufs[j][0], alpha, acc)
            accs.append(acc)
        finish(accs)

    def step(c, slot, state, lookahead=True):
        ahead = [scores(j, c + 2, s_bufs[j][(slot + 2) % nd]) if lookahead else None for j in heads]
        accs = [values(j, c - 1, p_bufs[j][(slot - 1) % nd], state[j][1], state[j][2]) for j in heads]
        stats = [softmax(s_bufs[j][slot], p_bufs[j][slot], state[j][0], state[j][3]) for j in heads]
        return tuple((stats[j][0], stats[j][1], accs[j], state[j][4], ahead[j]) for j in heads)

    @pl.when(pl.program_id(2) >= nct)
    def _():
        state = []
        for j in heads:
            cmax0 = scores(j, 0, s_bufs[j][0])
            cmax1 = scores(j, 1, s_bufs[j][1])
            m, alpha = softmax(s_bufs[j][0], p_bufs[j][0], m_init, cmax0)
            state.append((m, alpha, acc_init, cmax1, scores(j, 2, s_bufs[j][2])))

        def trip(i, state):
            for k in range(nd):
                state = step(1 + nd * i + k, (1 + k) % nd, state)
            return state

        state = lax.fori_loop(0, (n_tiles - 3) // nd, trip, tuple(state), unroll=ATT_UNROLL)
        for c in (n_tiles - 2, n_tiles - 1):
            state = step(c, c % nd, state, lookahead=False)
        finish([values(j, n_tiles - 1, p_bufs[j][(n_tiles - 1) % nd], state[j][1], state[j][2])
                for j in heads])


def _attention(qt, k, vt, nct):
    b, h, n_tiles, w, t = qt.shape
    n = n_tiles * t
    assert n_tiles >= 3 and (n_tiles - 3) % ATT_DEPTH == 0
    return pl.pallas_call(
        functools.partial(_attn_kernel, nct, n_tiles),
        grid=(b, h // 2, n_tiles),
        in_specs=[pl.BlockSpec((1, 2, 1, w, t), lambda i, p, q: (i, p, q, 0, 0)),
                  pl.BlockSpec((1, 2, n, w), lambda i, p, q: (i, p, 0, 0)),
                  pl.BlockSpec((1, n_tiles, 2 * MLA_V, t), lambda i, p, q: (i, 0, p, 0))],
        out_specs=pl.BlockSpec((1, t, 2 * MLA_V), lambda i, p, q: (i, q, p)),
        out_shape=jax.ShapeDtypeStruct((b, n, h * MLA_V), BF16),
        scratch_shapes=[pltpu.VMEM((t, t), F32)] * (2 * ATT_DEPTH) + [pltpu.VMEM((t, t), BF16)] * (2 * ATT_DEPTH),
        compiler_params=_params(("arbitrary", "arbitrary", "arbitrary")),
    )(qt, k, vt)


def _bwd_chunk(s, nct, nch):
    return jnp.where(s < nct, nct - 1 - s, nch - 1 - (s - nct))


def _ret_body(dl_ref, qf, kf, vf, qb, kb, vb, of_ref, ob_ref, st_ref):
    c = SCAN_CHUNK
    hw = RET_HEADS * RET_DK
    ii = lax.broadcasted_iota(jnp.int32, (c, c), 0)
    jj = lax.broadcasted_iota(jnp.int32, (c, c), 1)
    pos_i = lax.broadcasted_iota(jnp.int32, (c, 1), 0).astype(F32)
    pos_j = lax.broadcasted_iota(jnp.int32, (1, c), 1).astype(F32)
    lane_head = lax.broadcasted_iota(jnp.int32, (c, hw), 1) // RET_DK
    for d, (q_ref, k_ref, v_ref, o_ref) in enumerate(((qf, kf, vf, of_ref), (qb, kb, vb, ob_ref))):
        rev = d == 1
        q, v = q_ref[0], v_ref[0]
        k_t = k_ref[0].astype(F32).T
        k_tb = k_t.astype(BF16)
        st = st_ref[d]
        st_b = st.astype(BF16)
        dist = ((jj - ii) if rev else (ii - jj)).astype(F32)
        for h in range(RET_HEADS):
            lg = _log_sigmoid(dl_ref[d, h])[0:1, 0:1]
            intra = jnp.where(dist >= 0.0, jnp.exp(lg * jnp.maximum(dist, 0.0)), 0.0)
            q_dec = jnp.exp(lg * ((c - pos_i) if rev else (pos_i + 1.0)))
            k_dec = jnp.exp(lg * (pos_j if rev else (c - 1.0 - pos_j)))
            rows = slice(h * RET_DK, (h + 1) * RET_DK)
            cols = slice(h * RET_DV, (h + 1) * RET_DV)
            qm = jnp.where(lane_head == h, q, jnp.zeros_like(q))
            scores = _dot(qm, k_tb) * intra
            o = _dot(scores.astype(BF16), v[:, cols]) + _dot(qm, st_b) * q_dec
            o_ref[0, :, cols] = o
            kd = (k_t[rows] * k_dec).astype(BF16)
            st_ref[d, rows, :] = st[rows] * jnp.exp(lg * float(c)) + _dot(kd, v[:, cols])


def _block_start(i, size):
    return i & ~(size - 1)


def _gla_levels():
    half = SCAN_CHUNK // 2
    out = []
    while half >= GLA_BASE:
        out.append(half)
        half //= 2
    return out


def _gla_body(qf, kf, vf, laf, qb, kb, vb, lab, of_ref, ob_ref, st_ref):
    c = SCAN_CHUNK
    hw = GLA_HEADS * GLA_DK
    levels = _gla_levels()
    ii = lax.broadcasted_iota(jnp.int32, (c, c), 0)
    jj = lax.broadcasted_iota(jnp.int32, (c, c), 1)
    pos = lax.broadcasted_iota(jnp.int32, (c, 1), 0)
    lane_head = lax.broadcasted_iota(jnp.int32, (c, hw), 1) // GLA_DK
    for d, (q_ref, k_ref, v_ref, la_ref, o_ref) in enumerate(((qf, kf, vf, laf, of_ref),
                                                               (qb, kb, vb, lab, ob_ref))):
        rev = d == 1
        q, k, v, la = q_ref[0].astype(F32), k_ref[0].astype(F32), v_ref[0], la_ref[0]

        prefix = _one_hot((jj >= ii) if rev else (jj <= ii))
        b = sum(_dot(prefix, piece) for piece in _bf16_pieces(la, 3))
        b_end = b[0:1] if rev else b[c - 1:c]

        def at_row(size, idx):
            b3 = b.reshape(c // size, size, hw)
            return jnp.broadcast_to(b3[:, idx:idx + 1, :], b3.shape).reshape(c, hw)

        q_t = (q * jnp.exp(b)).astype(BF16)
        k_tt = (k * jnp.exp(b_end - b)).T
        dec_col = jnp.exp(jnp.broadcast_to(b_end, (LANES, hw)).T[:, 0:1])

        terms = []
        for half in levels:
            e = at_row(2 * half, half if rev else half - 1)
            inner = pos & (2 * half - 1)
            late = (inner < half) if rev else (inner >= half)
            qh = jnp.where(late, q * jnp.exp(jnp.where(late, b - e, 0.0)), 0.0)
            kh = jnp.where(late, 0.0, k * jnp.exp(jnp.where(late, 0.0, e - b)))
            terms.append((qh.astype(BF16), kh.astype(BF16), _block_start(ii, 2 * half) == _block_start(jj, 2 * half)))
        r = at_row(GLA_BASE, GLA_BASE - 1 if rev else 0)
        tri = (jj >= ii) if rev else (jj <= ii)
        terms.append(((q * jnp.exp(b - r)).astype(BF16), (k * jnp.exp(r - b)).astype(BF16),
                      (_block_start(ii, GLA_BASE) == _block_start(jj, GLA_BASE)) & tri))

        st = st_ref[d]
        st_b = st.astype(BF16)
        for h in range(GLA_HEADS):
            rows = slice(h * GLA_DK, (h + 1) * GLA_DK)
            cols = slice(h * GLA_DV, (h + 1) * GLA_DV)
            sel = lane_head == h
            a = jnp.zeros((c, c), F32)
            for qh, kh, mask in terms:
                a = a + jnp.where(mask, _dot_nt(jnp.where(sel, qh, jnp.zeros_like(qh)), kh), 0.0)
            vh = v[:, cols]
            o_ref[0, :, cols] = _dot(a.astype(BF16), vh) + _dot(jnp.where(sel, q_t, jnp.zeros_like(q_t)), st_b)
            st_ref[d, rows, :] = st[rows] * dec_col[rows] + _dot(k_tt[rows].astype(BF16), vh)


def _scan_kernel(*refs):
    ret_refs, gla_refs = refs[0:7] + refs[15:17] + refs[19:20], refs[7:15] + refs[17:19] + refs[20:21]

    @pl.when(pl.program_id(1) == 0)
    def _():
        ret_refs[-1][...] = jnp.zeros_like(ret_refs[-1])
        gla_refs[-1][...] = jnp.zeros_like(gla_refs[-1])

    _ret_body(*ret_refs)
    _gla_body(*gla_refs)


def _scans(dl, rq, rk, rv, gq, gk, gv, la, nct_tokens):
    b, n, hw = rq.shape
    c = SCAN_CHUNK
    nch = n // c
    nct = nct_tokens // c
    vw = rv.shape[-1]
    fwd = lambda w, col=0: pl.BlockSpec((1, c, w), lambda i, s: (i, s, col))
    bwd = lambda w, col=0: pl.BlockSpec((1, c, w), lambda i, s: (i, _bwd_chunk(s, nct, nch), col))
    return pl.pallas_call(
        _scan_kernel,
        grid=(b, nch),
        in_specs=[_const_spec(dl.shape), fwd(hw), fwd(hw), fwd(vw), bwd(hw), bwd(hw), bwd(vw),
                  fwd(hw), fwd(hw), fwd(vw), fwd(hw, 0), bwd(hw), bwd(hw), bwd(vw), bwd(hw, 1)],
        out_specs=[fwd(vw), bwd(vw), fwd(vw), bwd(vw)],
        out_shape=[jax.ShapeDtypeStruct((b, n, vw), F32)] * 4,
        scratch_shapes=[pltpu.VMEM((2, hw, RET_DV), F32), pltpu.VMEM((2, hw, GLA_DV), F32)],
        compiler_params=_params(("arbitrary", "arbitrary")),
    )(dl, rq, rk, rv, rq, rk, rv, gq, gk, gv, la, gq, gk, gv, la)


def _head_norm(o, w, center):
    parts = []
    for h in range(o.shape[-1] // LANES):
        oh = o[:, h * LANES:(h + 1) * LANES]
        if center:
            oh = oh - jnp.mean(oh, axis=-1, keepdims=True)
        parts.append(oh * lax.rsqrt(jnp.mean(oh * oh, axis=-1, keepdims=True) + EPS))
    return jnp.concatenate(parts, axis=-1) * w


def _merge_kernel(nct, off, nx, *refs):
    x_refs = refs[:nx]
    (mod_ref, nw_ref, oa_ref, rof_ref, rob_ref, rg_ref, gof_ref, gob_ref, gr_ref, wbg_ref, wbr_ref,
     wout_ref, rnw_ref, gnw_ref, o_ref) = refs[nx:]
    d = D_MODEL
    is_ctx = pl.program_id(1) + off < nct
    row = _mod_rows(mod_ref, is_ctx)
    sh, sc, g1 = row[:, 0:d], row[:, d:2 * d], row[:, 2 * d:3 * d]
    x = _seq_tile(x_refs, is_ctx)
    u = (_rms(x, nw_ref[...]) * (1.0 + sc) + sh).astype(BF16)
    gate = _sigmoid(_dot(u, wbg_ref[...]))
    yb = _silu(rg_ref[0].astype(F32)) * _head_norm(rof_ref[0] + rob_ref[0], rnw_ref[...], True)
    yc = _silu(gr_ref[0].astype(F32)) * _head_norm(gof_ref[0] + gob_ref[0], gnw_ref[...], False)
    z = (gate[:, 0:d] * _dot(oa_ref[0], wbr_ref[0])
         + gate[:, d:2 * d] * _dot(yb.astype(BF16), wbr_ref[1])
         + gate[:, 2 * d:3 * d] * _dot(yc.astype(BF16), wbr_ref[2]))
    o_ref[0] = x + g1 * _dot(z.astype(BF16), wout_ref[...])


def _merge(xs, modsel, nw, oa, rof, rob, rg, gof, gob, gr, wbg, wbr, wout, rnw, gnw, nct, latent_only):
    tm = TOKEN_TILE
    off = nct if latent_only else 0
    x_specs, x_ops = _seq_inputs(xs, nct, off)
    b, d = x_ops[0].shape[0], x_ops[0].shape[-1]
    nt = sum(a.shape[1] for a in x_ops) // tm - off
    tok = lambda w: pl.BlockSpec((1, tm, w), lambda i, t: (i, t + off, 0))
    return pl.pallas_call(
        functools.partial(_merge_kernel, nct, off, len(x_ops)),
        grid=(b, nt),
        in_specs=x_specs + [pl.BlockSpec((1, 2, 6 * d), lambda i, t: (i, 0, 0)), _const_spec(nw.shape),
                  tok(512), tok(512), tok(512), tok(512), tok(512), tok(512), tok(512),
                  _const_spec(wbg.shape), _const_spec(wbr.shape), _const_spec(wout.shape),
                  _const_spec(rnw.shape), _const_spec(gnw.shape)],
        out_specs=pl.BlockSpec((1, tm, d), lambda i, t: (i, t, 0)),
        out_shape=jax.ShapeDtypeStruct((b, nt * tm, d), F32),
        compiler_params=_params(("arbitrary", "arbitrary")),
    )(*x_ops, modsel, nw, oa, rof, rob, rg, gof, gob, gr, wbg, wbr, wout, rnw, gnw)


def _norm2(x, row, nw_ref):
    d = D_MODEL
    sh, sc = row[:, 3 * d:4 * d], row[:, 4 * d:5 * d]
    return _rms(x, nw_ref[...]) * (1.0 + sc) + sh


def _finish(x, y, row, final, fnw_ref):
    d = D_MODEL
    out = x + row[:, 5 * d:6 * d] * y
    if final:
        out = _rms(out, fnw_ref[...])
    return out


def _ffn_kernel(nct, off, final, x_ref, mod_ref, nw_ref, wg_ref, wu_ref, wd_ref, fnw_ref, o_ref):
    row = _mod_rows(mod_ref, pl.program_id(1) + off < nct)
    x = x_ref[0]
    v = _norm2(x, row, nw_ref).astype(BF16)
    hdn = _silu(_dot(v, wg_ref[...])) * _dot(v, wu_ref[...])
    o_ref[0] = _finish(x, _dot(hdn.astype(BF16), wd_ref[...]), row, final, fnw_ref)


def _ffn(xs, modsel, nw, wg, wu, wd, fnw, nct, final):
    b, n, d = xs.shape
    tm = TOKEN_TILE
    off = nct if final else 0
    nt = n // tm - off
    return pl.pallas_call(
        functools.partial(_ffn_kernel, nct, off, final),
        grid=(b, nt),
        in_specs=[pl.BlockSpec((1, tm, d), lambda i, t: (i, t + off, 0)),
                  pl.BlockSpec((1, 2, 6 * d), lambda i, t: (i, 0, 0)), _const_spec(nw.shape),
                  _const_spec(wg.shape), _const_spec(wu.shape), _const_spec(wd.shape),
                  _const_spec(fnw.shape)],
        out_specs=pl.BlockSpec((1, tm, d), lambda i, t: (i, t, 0)),
        out_shape=jax.ShapeDtypeStruct((b, nt * tm, d), F32),
        compiler_params=_params(("arbitrary", "arbitrary")),
    )(xs, modsel, nw, wg, wu, wd, fnw)


def _top2_gates(logits):
    lane = lax.broadcasted_iota(jnp.int32, logits.shape, 1)
    m1 = jnp.max(logits, axis=-1, keepdims=True)
    i1 = jnp.min(jnp.where(logits == m1, lane, LANES), axis=-1, keepdims=True)
    rest = jnp.where(lane == i1, -jnp.inf, logits)
    m2 = jnp.max(rest, axis=-1, keepdims=True)
    i2 = jnp.min(jnp.where(rest == m2, lane, LANES), axis=-1, keepdims=True)
    e2 = jnp.exp(m2 - m1)
    w1 = 1.0 / (1.0 + e2)
    gates = jnp.where(lane == i1, w1, jnp.where(lane == i2, e2 * w1, 0.0))
    return gates, (lane == i1) | (lane == i2)


def _one_hot(cond):
    return jnp.where(cond, 1.0, 0.0).astype(BF16)


def _route_kernel(mrow, x_ref, mod_ref, nw_ref, rt_ref, v_out, gate_out, rank_out, rankt_out):
    v = _norm2(x_ref[0], mod_ref[0][mrow:mrow + 1], nw_ref)
    v_out[0] = v.astype(BF16)
    tt = v.shape[0]
    lane = lax.broadcasted_iota(jnp.int32, (tt, LANES), 1)
    logits = jnp.where(lane < N_EXPERTS, _dot_split(v, rt_ref[...]), -jnp.inf)
    gates, routed = _top2_gates(logits)
    gate_out[0] = gates
    ti = lax.broadcasted_iota(jnp.int32, (tt, tt), 0)
    tj = lax.broadcasted_iota(jnp.int32, (tt, tt), 1)
    rank_out[0] = jnp.where(routed, _dot(_one_hot(tj < ti), _one_hot(routed)), -1.0)
    routed_t = jnp.where(routed, 1.0, 0.0).T
    rank_t = jnp.where(routed_t > 0.5, _dot(routed_t.astype(BF16), _one_hot(ti < tj)), -1.0)
    rankt_out[0, 0] = rank_t[0:N_EXPERTS]


def _experts_kernel(v_ref, gate_ref, rank_ref, rankt_ref, wg_ref, wu_ref, wd_ref, o_ref, acc_ref):
    e, s = pl.program_id(1), pl.program_id(2)
    tt = v_ref.shape[1]
    fh = D_FF // 2
    tile0 = pl.multiple_of(s * tt, tt)
    lane = lax.broadcasted_iota(jnp.int32, (tt, LANES), 1)
    g_col = jnp.sum(jnp.where(lane == e, gate_ref[0], 0.0), axis=-1, keepdims=True)
    rk_col = jnp.sum(jnp.where(lane == e, rank_ref[0], 0.0), axis=-1, keepdims=True)
    rk_row = rankt_ref[0, 0, pl.ds(e, 1), :]
    n_routed = jnp.max(rk_col).astype(jnp.int32) + 1

    @pl.when(e == 0)
    def _():
        acc_ref[pl.ds(tile0, tt), :] = jnp.zeros((tt, D_MODEL), BF16)

    def block(first_slot, r):
        base = first_slot.astype(F32)
        slot_i = base + lax.broadcasted_iota(jnp.int32, (r, 1), 0).astype(F32)
        xg = _dot(_one_hot(rk_row == slot_i), v_ref[0]).astype(BF16)
        y = jnp.zeros((r, D_MODEL), F32)
        for c in range(2):
            cols = slice(c * fh, (c + 1) * fh)
            hdn = _silu(_dot(xg, wg_ref[0, :, cols])) * _dot(xg, wu_ref[0, :, cols])
            y = y + _dot(hdn.astype(BF16), wd_ref[0, cols, :])
        y = y.astype(BF16)
        slot_j = base + lax.broadcasted_iota(jnp.int32, (1, r), 1).astype(F32)
        for c in range(tt // TOKEN_TILE):
            rows = slice(c * TOKEN_TILE, (c + 1) * TOKEN_TILE)
            dst = pl.ds(tile0 + c * TOKEN_TILE, TOKEN_TILE)
            part = g_col[rows] * _dot(_one_hot(rk_col[rows] == slot_j), y)
            acc_ref[dst, :] = (acc_ref[dst, :].astype(F32) + part).astype(BF16)

    r = MOE_BLOCK
    n_full = n_routed // r
    left = n_routed - n_full * r

    def full_block(i, carry):
        block(i * r, r)
        return carry

    lax.fori_loop(0, n_full + (left > MOE_TAIL_BLOCKS[-1]).astype(jnp.int32), full_block, 0)
    for below, size in zip((0,) + MOE_TAIL_BLOCKS, MOE_TAIL_BLOCKS):
        @pl.when((left > below) & (left <= size))
        def _(size=size):
            block(n_full * r, size)

    @pl.when(e == N_EXPERTS - 1)
    def _():
        o_ref[0] = acc_ref[pl.ds(tile0, tt), :]


def _moe_finish_kernel(mrow, final, x_ref, y_ref, mod_ref, fnw_ref, o_ref):
    o_ref[0] = _finish(x_ref[0], y_ref[0].astype(F32), mod_ref[0][mrow:mrow + 1], final, fnw_ref)


def _moe(slab, mrow, modsel, nw, rt, wg, wu, wd, fnw, final):
    b, m, d = slab.shape
    tt = min(MOE_TILE, m)
    assert m % tt == 0 and tt % TOKEN_TILE == 0
    ns = m // tt
    ne, _, f = wg.shape
    mod_spec = lambda: pl.BlockSpec((1, 2, 6 * d), lambda i, s: (i, 0, 0))
    v, gates, rank, rank_t = pl.pallas_call(
        functools.partial(_route_kernel, mrow),
        grid=(b, ns),
        in_specs=[pl.BlockSpec((1, tt, d), lambda i, s: (i, s, 0)), mod_spec(), _const_spec(nw.shape),
                  _const_spec(rt.shape)],
        out_specs=[pl.BlockSpec((1, tt, d), lambda i, s: (i, s, 0)),
                   pl.BlockSpec((1, tt, LANES), lambda i, s: (i, s, 0)),
                   pl.BlockSpec((1, tt, LANES), lambda i, s: (i, s, 0)),
                   pl.BlockSpec((1, 1, ne, tt), lambda i, s: (i, s, 0, 0))],
        out_shape=[jax.ShapeDtypeStruct((b, m, d), BF16), jax.ShapeDtypeStruct((b, m, LANES), F32),
                   jax.ShapeDtypeStruct((b, m, LANES), F32), jax.ShapeDtypeStruct((b, ns, ne, tt), F32)],
        compiler_params=_params(("arbitrary", "arbitrary")),
    )(slab, modsel, nw, rt)

    tok = lambda w: pl.BlockSpec((1, tt, w), lambda i, e, s: (i, s, 0))
    wspec = lambda shape, bufs=1: pl.BlockSpec((1,) + shape, lambda i, e, s: (e, 0, 0),
                                               pipeline_mode=pl.Buffered(bufs))
    last_pass = lambda i, e, s: (i, jnp.where(e == ne - 1, s, 0), 0)
    y_sum = pl.pallas_call(
        _experts_kernel,
        grid=(b, ne, ns),
        in_specs=[tok(d), tok(LANES), tok(LANES),
                  pl.BlockSpec((1, 1, ne, tt), lambda i, e, s: (i, s, 0, 0)),
                  wspec((d, f), 2), wspec((d, f)), wspec((f, d))],
        out_specs=pl.BlockSpec((1, tt, d), last_pass),
        out_shape=jax.ShapeDtypeStruct((b, m, d), BF16),
        scratch_shapes=[pltpu.VMEM((m, d), BF16)],
        compiler_params=_params(("arbitrary", "arbitrary", "arbitrary")),
    )(v, gates, rank, rank_t, wg, wu, wd)

    tm = TOKEN_TILE
    row_tile = lambda: pl.BlockSpec((1, tm, d), lambda i, t: (i, t, 0))
    return pl.pallas_call(
        functools.partial(_moe_finish_kernel, mrow, final),
        grid=(b, m // tm),
        in_specs=[row_tile(), row_tile(), mod_spec(), _const_spec(fnw.shape)],
        out_specs=row_tile(),
        out_shape=jax.ShapeDtypeStruct((b, m, d), F32),
        compiler_params=_params(("arbitrary", "arbitrary")),
    )(slab, y_sum, modsel, fnw)


def _axial_perm():
    dd = np.arange(MLA_ROPE)
    return np.where(dd % 16 < 8, dd + 8, dd - 8)


def _ret_perm():
    dd = np.arange(RET_HEADS * RET_DK)
    return np.where(dd % RET_DK < RET_DK // 2, dd + RET_DK // 2, dd - RET_DK // 2)


def _prep_w_in(w):
    offs = np.cumsum((0,) + IN_SPLITS)
    cq, ckv, kr, rq, rk, rv, rg, gq, gk, gv, gr, gz, bg = [w[:, offs[i]:offs[i + 1]] for i in range(13)]
    zeros = lambda n: jnp.zeros((w.shape[0], n), w.dtype)
    rk = rk * (RET_DK ** -0.5)
    gq = gq * (GLA_DK ** -0.5)
    w1 = jnp.concatenate([
        cq, ckv,
        zeros(MLA_NOPE), kr, zeros(HEAD_SLOT - MLA_NOPE - MLA_ROPE),
        zeros(MLA_NOPE), kr[:, _axial_perm()], zeros(HEAD_SLOT - MLA_NOPE - MLA_ROPE),
        gz, zeros(LANES - 2 * GLA_GATE_RANK),
        rq, rq[:, _ret_perm()], rk, rk[:, _ret_perm()], rv, rg,
        gq, gk, gv, gr], axis=1)
    return w1.astype(BF16), bg.astype(BF16)


def _prep_mla(w_uq, w_ukv):
    r = w_uq.shape[0]
    scale = (MLA_NOPE + MLA_ROPE) ** -0.5 * np.log2(np.e)
    wq3 = w_uq.reshape(r, MLA_HEADS, MLA_NOPE + MLA_ROPE) * scale
    pad = jnp.zeros((r, MLA_HEADS, HEAD_SLOT - MLA_NOPE - MLA_ROPE), w_uq.dtype)
    wq = jnp.concatenate([wq3, pad], axis=-1)
    wqs = jnp.concatenate([jnp.zeros((r, MLA_HEADS, MLA_NOPE), w_uq.dtype),
                           wq3[:, :, MLA_NOPE:][:, :, _axial_perm()], pad], axis=-1)
    rk = w_ukv.shape[0]
    wkv3 = w_ukv.reshape(rk, MLA_HEADS, MLA_NOPE + MLA_V)
    wk = jnp.concatenate([wkv3[:, :, :MLA_NOPE],
                          jnp.zeros((rk, MLA_HEADS, HEAD_SLOT - MLA_NOPE), w_ukv.dtype)], axis=-1)
    wv = wkv3[:, :, MLA_NOPE:]
    flat = lambda a: a.reshape(a.shape[0], -1).astype(BF16)
    return flat(wq), flat(wqs), flat(wk), flat(wv)


def _prep_gla_gate(w_gate, b_gate):
    hw = GLA_HEADS * GLA_DK
    wg2 = jnp.zeros((LANES, 2 * hw), F32)
    wg2 = wg2.at[0:GLA_GATE_RANK, 0:hw].set(w_gate[0])
    wg2 = wg2.at[GLA_GATE_RANK:2 * GLA_GATE_RANK, hw:2 * hw].set(w_gate[1])
    return wg2, b_gate.reshape(1, 2 * hw)


def _rope_tables(n_ctx, seq):
    n_rows = seq // GRID_W
    half = MLA_ROPE // 4
    inv = ROPE_BASE ** (-jnp.arange(half, dtype=F32) / half)
    ang_r = jnp.arange(n_rows, dtype=F32)[:, None] * inv[None, :]
    ang_c = jnp.arange(GRID_W, dtype=F32)[:, None] * inv[None, :]
    by_row = lambda a: jnp.broadcast_to(a[:, None, :], (n_rows, GRID_W, half)).reshape(seq, half)
    by_col = lambda a: jnp.broadcast_to(a[None, :, :], (n_rows, GRID_W, half)).reshape(seq, half)
    cos_r, sin_r, cos_c, sin_c = by_row(jnp.cos(ang_r)), by_row(jnp.sin(ang_r)), by_col(jnp.cos(ang_c)), by_col(jnp.sin(ang_c))
    ctx = lambda a, fill: jnp.concatenate([jnp.full((n_ctx, a.shape[1]), fill, F32), a], axis=0)
    spread = lambda a, m: jnp.dot(a, jnp.asarray(m, F32), precision=HI)
    dd = np.arange(HEAD_SLOT) - MLA_NOPE
    active = (dd >= 0) & (dd < MLA_ROPE)
    src = (np.clip(dd, 0, MLA_ROPE - 1) // 16) * half + np.clip(dd, 0, MLA_ROPE - 1) % half
    pick = (np.arange(2 * half)[:, None] == src[None, :]) & active[None, :]
    sign = np.where((dd % 16) < half, -1.0, 1.0)[None, :]
    cosq = spread(ctx(jnp.concatenate([cos_r, cos_c], axis=1), 1.0), pick) + jnp.asarray(~active, F32)[None, :]
    sinq = spread(ctx(jnp.concatenate([sin_r, sin_c], axis=1), 0.0), pick * sign)
    rhalf = RET_DK // 2
    rinv = ROPE_BASE ** (-jnp.arange(rhalf, dtype=F32) / rhalf)
    a_hi = (GRID_W * jnp.arange(n_rows, dtype=F32))[:, None, None] * rinv
    a_lo = jnp.arange(GRID_W, dtype=F32)[None, :, None] * rinv
    cos1 = (jnp.cos(a_hi) * jnp.cos(a_lo) - jnp.sin(a_hi) * jnp.sin(a_lo)).reshape(seq, rhalf)
    sin1 = (jnp.sin(a_hi) * jnp.cos(a_lo) + jnp.cos(a_hi) * jnp.sin(a_lo)).reshape(seq, rhalf)
    rl = np.arange(RET_HEADS * RET_DK) % RET_DK
    rpick = np.arange(rhalf)[:, None] == (rl % rhalf)[None, :]
    rsign = np.where(rl < rhalf, -1.0, 1.0)[None, :]
    cosr = spread(ctx(cos1, 1.0), rpick)
    sinr = spread(ctx(sin1, 0.0), rpick * rsign)
    return cosq, sinq, cosr, sinr


def kernel(x, c, ctx, c_ctx, mod_w, mod_b, norm1_w, norm2_w, w_in, mla_q_norm, mla_w_uq, mla_kv_norm, mla_w_ukv, ret_decay_logit, ret_norm_w, gla_w_gate, gla_b_gate, gla_norm_w, w_branch, w_out, ffn_w_gate, ffn_w_up, ffn_w_down, moe_router, moe_w_gate, moe_w_up, moe_w_down, final_norm_w):
    b, seq, d = x.shape
    n_ctx = ctx.shape[1]
    depth = mod_w.shape[0]
    assert d == D_MODEL and seq % GRID_W == 0
    assert n_ctx % TOKEN_TILE == 0 and n_ctx % SCAN_CHUNK == 0
    assert seq % TOKEN_TILE == 0 and seq % SCAN_CHUNK == 0
    nct = n_ctx // TOKEN_TILE

    rows = 8 * ((b + 1 + 7) // 8)
    cvec = jnp.zeros((rows, d), F32).at[0:b].set(c).at[b].set(c_ctx)
    mod = _modulation(cvec, mod_w, mod_b)
    cosq, sinq, cosr, sinr = _rope_tables(n_ctx, seq)
    fnw = final_norm_w.reshape(1, d)
    row2 = lambda a: a.reshape(1, -1)

    xs = (ctx, x)
    for l in range(depth):
        last = l == depth - 1
        modsel = jnp.stack([jnp.broadcast_to(mod[l, b], (b, 6 * d)), mod[l, 0:b]], axis=1)
        w1, wbg = _prep_w_in(w_in[l])
        wq, wqs, wk, wv = _prep_mla(mla_w_uq[l], mla_w_ukv[l])
        wg2, bg2 = _prep_gla_gate(gla_w_gate[l], gla_b_gate[l])
        nw1, nw2 = row2(norm1_w[l]), row2(norm2_w[l])

        (q, k, v, rq, rk, rv, rg, gq, gk, gv, gr, la) = _inproj(
            xs, modsel, nw1, w1, row2(mla_q_norm[l]), wq, wqs, row2(mla_kv_norm[l]), wk, wv,
            wg2, bg2, cosq, sinq, cosr, sinr, nct)
        oa = _attention(q, k, v, nct)
        dl = jnp.broadcast_to(ret_decay_logit[l][:, :, None, None], (2, RET_HEADS, 8, LANES))
        rof, rob, gof, gob = _scans(dl, rq, rk, rv, gq, gk, gv, la, n_ctx)
        is_moe = l % 2 == 1
        xs = _merge(xs, modsel, nw1, oa, rof, rob, rg, gof, gob, gr, wbg,
                    w_branch[l].astype(BF16), w_out[l].astype(BF16),
                    row2(ret_norm_w[l]), row2(gla_norm_w[l]), nct, is_moe and last)
        i = l // 2
        if not is_moe:
            xs = _ffn(xs, modsel, nw2, ffn_w_gate[i].astype(BF16), ffn_w_up[i].astype(BF16),
                      ffn_w_down[i].astype(BF16), fnw, nct, last)
            continue
        rt = jnp.pad(moe_router[i], ((0, 0), (0, LANES - N_EXPERTS)))
        experts = (moe_w_gate[i].astype(BF16), moe_w_up[i].astype(BF16), moe_w_down[i].astype(BF16))
        if last:
            xs = _moe(xs, 1, modsel, nw2, rt, *experts, fnw, True)
        else:
            xs = jnp.concatenate([_moe(xs[:, :n_ctx], 0, modsel, nw2, rt, *experts, fnw, False),
                                  _moe(xs[:, n_ctx:], 1, modsel, nw2, rt, *experts, fnw, False)], axis=1)
    return xs
```

```python
import functools

import numpy as np
import jax
import jax.numpy as jnp
from jax import lax
from jax.experimental import pallas as pl
from jax.experimental.pallas import tpu as pltpu

D_MODEL = 1024
GRID_W = 64
EPS = 1e-6
ROPE_BASE = 10000.0

MLA_HEADS = 8
MLA_NOPE = 64
MLA_ROPE = 32
MLA_V = 64
MLA_Q_RANK = 256
MLA_KV_RANK = 128

RET_HEADS = 4
RET_DK = 64
RET_DV = 128

GLA_HEADS = 4
GLA_DK = 64
GLA_DV = 128
GLA_GATE_RANK = 16
GLA_TAU = 16.0

N_BRANCH = 3
BRANCH_W = 512
D_FF = 2816
N_EXPERTS = 8

IN_SPLITS = (
    MLA_Q_RANK, MLA_KV_RANK, MLA_ROPE,
    RET_HEADS * RET_DK, RET_HEADS * RET_DK, RET_HEADS * RET_DV, RET_HEADS * RET_DV,
    GLA_HEADS * GLA_DK, GLA_HEADS * GLA_DK, GLA_HEADS * GLA_DV, GLA_HEADS * GLA_DV,
    2 * GLA_GATE_RANK,
    N_BRANCH * D_MODEL,
)

F32 = jnp.float32
BF16 = jnp.bfloat16
HI = lax.Precision.HIGHEST

LANES = 128
HEAD_SLOT = 128
TOKEN_TILE = 256
ATT_DEPTH = 3
ATT_UNROLL = 10
ATT_SUM_ROWS = 16
SCAN_CHUNK = 256
GLA_BASE = 32
MOE_TILE = 1024
MOE_BLOCK = 256
MOE_TAIL_BLOCKS = (128, 192)
MOD_COLS_TILE = 1536
VMEM_LIMIT = 56 * 1024 * 1024


def _silu(x):
    return x / (1.0 + jnp.exp(-x))


def _sigmoid(x):
    return 1.0 / (1.0 + jnp.exp(-x))


def _log_sigmoid(z):
    return jnp.minimum(z, 0.0) - jnp.log(1.0 + jnp.exp(-jnp.abs(z)))


def _rms(x, w):
    return x * lax.rsqrt(jnp.mean(x * x, axis=-1, keepdims=True) + EPS) * w


def _dot(a, b, precision=None):
    return jnp.dot(a, b, preferred_element_type=F32, precision=precision)


def _dot_nt(a, b):
    return lax.dot_general(a, b, (((1,), (1,)), ((), ())), preferred_element_type=F32)


def _bf16_pieces(x, n):
    out = []
    for _ in range(n):
        piece = x.astype(BF16)
        out.append(piece)
        x = x - piece.astype(F32)
    return out


def _dot_split(a, b):
    (a0, a1), (b0, b1) = _bf16_pieces(a, 2), _bf16_pieces(b, 2)
    return _dot(a0, b0) + (_dot(a0, b1) + _dot(a1, b0))


def _params(sem):
    return pltpu.CompilerParams(dimension_semantics=sem, vmem_limit_bytes=VMEM_LIMIT)


def _const_spec(shape):
    nd = len(shape)
    return pl.BlockSpec(shape, lambda *_: (0,) * nd, pipeline_mode=pl.Buffered(1))


def _mod_rows(mod_ref, is_ctx):
    m = mod_ref[0]
    return jnp.where(is_ctx, m[0:1], m[1:2])


def _mod_kernel(c_ref, w_ref, b_ref, o_ref):
    o_ref[0] = _dot(_silu(c_ref[...]), w_ref[0], HI) + b_ref[0]


def _modulation(cvec, mod_w, mod_b):
    depth, d, n6 = mod_w.shape
    rows = cvec.shape[0]
    tn = MOD_COLS_TILE
    return pl.pallas_call(
        _mod_kernel,
        grid=(depth, n6 // tn),
        in_specs=[pl.BlockSpec((rows, d), lambda l, j: (0, 0)),
                  pl.BlockSpec((1, d, tn), lambda l, j: (l, 0, j)),
                  pl.BlockSpec((1, 1, tn), lambda l, j: (l, 0, j))],
        out_specs=pl.BlockSpec((1, rows, tn), lambda l, j: (l, 0, j)),
        out_shape=jax.ShapeDtypeStruct((depth, rows, n6), F32),
        compiler_params=_params(("arbitrary", "arbitrary")),
    )(cvec, mod_w, mod_b.reshape(depth, 1, n6))


_C_CQ = (0, 256)
_C_CKV_KRA = (256, 512)
_C_KRB_GZ = (512, 768)
_C_RQ = (768, 1024)
_C_RQS = (1024, 1280)
_C_RK = (1280, 1536)
_C_RKS = (1536, 1792)
_C_RV = (1792, 2304)
_C_RG = (2304, 2816)
_C_GQ = (2816, 3072)
_C_GK = (3072, 3328)
_C_GV = (3328, 3840)
_C_GR = (3840, 4352)


def _seq_inputs(xs, nct, off=0):
    tm = TOKEN_TILE
    if not isinstance(xs, tuple):
        return [pl.BlockSpec((1, tm, xs.shape[-1]), lambda i, t: (i, t + off, 0))], [xs]
    d = xs[0].shape[-1]
    return ([pl.BlockSpec((1, tm, d), lambda i, t: (i, jnp.minimum(t + off, nct - 1), 0)),
             pl.BlockSpec((1, tm, d), lambda i, t: (i, jnp.maximum(t + off - nct, 0), 0))], list(xs))


def _seq_tile(x_refs, is_ctx):
    if len(x_refs) == 1:
        return x_refs[0][0]
    return jnp.where(is_ctx, x_refs[0][0], x_refs[1][0])


def _inproj_kernel(nct, nx, *refs):
    x_refs = refs[:nx]
    (mod_ref, nw_ref, w_ref, qn_ref, wq_ref, wqs_ref, kvn_ref, wk_ref, wv_ref, wg2_ref, bg2_ref,
     cq_ref, sq_ref, cr_ref, sr_ref, q_out, k_out, v_out, rq_out, rk_out, rv_out, rg_out,
     gq_out, gk_out, gv_out, gr_out, la_out) = refs[nx:]
    d = D_MODEL
    is_ctx = pl.program_id(1) < nct
    row = _mod_rows(mod_ref, is_ctx)
    sh, sc = row[:, 0:d], row[:, d:2 * d]
    u = (_rms(_seq_tile(x_refs, is_ctx), nw_ref[...]) * (1.0 + sc) + sh).astype(BF16)

    def seg(c):
        return _dot(u, w_ref[:, c[0]:c[1]])

    cqn = _rms(seg(_C_CQ), qn_ref[...]).astype(BF16)
    q_all = _dot(cqn, wq_ref[...])
    q_swp = _dot(cqn, wqs_ref[...])
    ckv_kra = seg(_C_CKV_KRA)
    krb_gz = seg(_C_KRB_GZ)
    ckvn = _rms(ckv_kra[:, 0:LANES], kvn_ref[...]).astype(BF16)
    k_all = _dot(ckvn, wk_ref[...])
    v_out[0, 0] = _dot(ckvn, wv_ref[...]).T.astype(BF16)
    cosq, sinq = cq_ref[...], sq_ref[...]
    k_rope = ckv_kra[:, LANES:] * cosq + krb_gz[:, 0:LANES] * sinq
    for h in range(MLA_HEADS):
        sl = slice(h * HEAD_SLOT, (h + 1) * HEAD_SLOT)
        q_out[0, h, 0] = (q_all[:, sl] * cosq + q_swp[:, sl] * sinq).T.astype(BF16)
        k_out[0, h] = (k_all[:, sl] + k_rope).astype(BF16)

    cosr, sinr = cr_ref[...], sr_ref[...]
    rq_out[0] = (seg(_C_RQ) * cosr + seg(_C_RQS) * sinr).astype(BF16)
    rk_out[0] = (seg(_C_RK) * cosr + seg(_C_RKS) * sinr).astype(BF16)
    rv_out[0] = seg(_C_RV).astype(BF16)
    rg_out[0] = seg(_C_RG).astype(BF16)

    gq_out[0] = seg(_C_GQ).astype(BF16)
    gk_out[0] = seg(_C_GK).astype(BF16)
    gv_out[0] = seg(_C_GV).astype(BF16)
    gr_out[0] = seg(_C_GR).astype(BF16)
    z = _dot_split(krb_gz[:, LANES:], wg2_ref[...]) + bg2_ref[...]
    la_out[0] = _log_sigmoid(z) * (1.0 / GLA_TAU)


def _inproj(xs, modsel, nw, w1, qn, wq, wqs, kvn, wk, wv, wg2, bg2, cosq, sinq, cosr, sinr, nct):
    x_specs, x_ops = _seq_inputs(xs, nct)
    b, d = x_ops[0].shape[0], x_ops[0].shape[-1]
    n = sum(a.shape[1] for a in x_ops)
    tm = TOKEN_TILE
    nt = n // tm
    h = MLA_HEADS
    tok = lambda w: pl.BlockSpec((1, tm, w), lambda i, t: (i, t, 0))
    tab = lambda w: pl.BlockSpec((tm, w), lambda i, t: (t, 0))
    q_spec = pl.BlockSpec((1, h, 1, HEAD_SLOT, tm), lambda i, t: (i, 0, t, 0, 0))
    k_spec = pl.BlockSpec((1, h, tm, HEAD_SLOT), lambda i, t: (i, 0, t, 0))
    v_spec = pl.BlockSpec((1, 1, h * MLA_V, tm), lambda i, t: (i, t, 0, 0))
    sds = lambda w, dt=BF16: jax.ShapeDtypeStruct((b, n, w), dt)
    q_sds = jax.ShapeDtypeStruct((b, h, nt, HEAD_SLOT, tm), BF16)
    k_sds = jax.ShapeDtypeStruct((b, h, n, HEAD_SLOT), BF16)
    v_sds = jax.ShapeDtypeStruct((b, nt, h * MLA_V, tm), BF16)
    return pl.pallas_call(
        functools.partial(_inproj_kernel, nct, len(x_ops)),
        grid=(b, nt),
        in_specs=x_specs + [pl.BlockSpec((1, 2, 6 * d), lambda i, t: (i, 0, 0)), _const_spec(nw.shape),
                  _const_spec(w1.shape), _const_spec(qn.shape), _const_spec(wq.shape),
                  _const_spec(wqs.shape), _const_spec(kvn.shape), _const_spec(wk.shape),
                  _const_spec(wv.shape), _const_spec(wg2.shape), _const_spec(bg2.shape),
                  tab(HEAD_SLOT), tab(HEAD_SLOT), tab(256), tab(256)],
        out_specs=[q_spec, k_spec, v_spec, tok(256), tok(256), tok(512), tok(512),
                   tok(256), tok(256), tok(512), tok(512), tok(512)],
        out_shape=[q_sds, k_sds, v_sds, sds(256), sds(256), sds(512), sds(512),
                   sds(256), sds(256), sds(512), sds(512), sds(512, F32)],
        compiler_params=_params(("arbitrary", "arbitrary")),
    )(*x_ops, modsel, nw, w1, qn, wq, wqs, kvn, wk, wv, wg2, bg2, cosq, sinq, cosr, sinr)


def _attn_kernel(nct, n_tiles, qt_ref, k_ref, vt_ref, o_ref, *bufs):
    t = TOKEN_TILE
    nd = ATT_DEPTH
    heads = range(2)
    s_bufs = (bufs[0:nd], bufs[nd:2 * nd])
    p_bufs = (bufs[2 * nd:3 * nd], bufs[3 * nd:4 * nd])
    qts = [qt_ref[0, j, 0] for j in heads]
    ones = jnp.ones((ATT_SUM_ROWS, t), BF16)

    def scores(j, c, s_buf):
        s = _dot(k_ref[0, j, pl.ds(pl.multiple_of(c * t, t), t), :], qts[j])
        s_buf[...] = s
        return jnp.max(s, axis=0, keepdims=True)

    def softmax(s_buf, p_buf, m, cmax):
        m_new = jnp.maximum(m, cmax)
        p_buf[...] = jnp.exp2(s_buf[...] - m_new).astype(BF16)
        return m_new, jnp.exp2(m - m_new)

    def values(j, c, p_buf, alpha, acc):
        lhs = jnp.concatenate([vt_ref[0, c, j * MLA_V:(j + 1) * MLA_V, :], ones], axis=0)
        return alpha * acc + _dot(lhs, p_buf[...])

    m_init = jnp.full((1, t), -jnp.inf, F32)
    acc_init = jnp.zeros((MLA_V + ATT_SUM_ROWS, t), F32)

    def finish(accs):
        outs = [acc[0:MLA_V] / acc[MLA_V:MLA_V + 1] for acc in accs]
        o_ref[0] = jnp.concatenate(outs, axis=0).T.astype(BF16)

    @pl.when(pl.program_id(2) < nct)
    def _():
        accs = []
        for j in heads:
            m, acc = m_init, acc_init
            for c in range(nct):
                cmax = scores(j, c, s_bufs[j][0])
                m, alpha = softmax(s_bufs[j][0], p_bufs[j][0], m, cmax)
                acc = values(j, c, p_bufs[j][0], alpha, acc)
            accs.append(acc)
        finish(accs)

    def step(c, slot, state, lookahead=True):
        ahead = [scores(j, c + 2, s_bufs[j][(slot + 2) % nd]) if lookahead else None for j in heads]
        accs = [values(j, c - 1, p_bufs[j][(slot - 1) % nd], state[j][1], state[j][2]) for j in heads]
        stats = [softmax(s_bufs[j][slot], p_bufs[j][slot], state[j][0], state[j][3]) for j in heads]
        return tuple((stats[j][0], stats[j][1], accs[j], state[j][4], ahead[j]) for j in heads)

    @pl.when(pl.program_id(2) >= nct)
    def _():
        state = []
        for j in heads:
            cmax0 = scores(j, 0, s_bufs[j][0])
            cmax1 = scores(j, 1, s_bufs[j][1])
            m, alpha = softmax(s_bufs[j][0], p_bufs[j][0], m_init, cmax0)
            state.append((m, alpha, acc_init, cmax1, scores(j, 2, s_bufs[j][2])))

        def trip(i, state):
            for k in range(nd):
                state = step(1 + nd * i + k, (1 + k) % nd, state)
            return state

        state = lax.fori_loop(0, (n_tiles - 3) // nd, trip, tuple(state), unroll=ATT_UNROLL)
        for c in (n_tiles - 2, n_tiles - 1):
            state = step(c, c % nd, state, lookahead=False)
        finish([values(j, n_tiles - 1, p_bufs[j][(n_tiles - 1) % nd], state[j][1], state[j][2])
                for j in heads])


def _attention(qt, k, vt, nct):
    b, h, n_tiles, w, t = qt.shape
    n = n_tiles * t
    assert n_tiles >= 3 and (n_tiles - 3) % ATT_DEPTH == 0
    return pl.pallas_call(
        functools.partial(_attn_kernel, nct, n_tiles),
        grid=(b, h // 2, n_tiles),
        in_specs=[pl.BlockSpec((1, 2, 1, w, t), lambda i, p, q: (i, p, q, 0, 0)),
                  pl.BlockSpec((1, 2, n, w), lambda i, p, q: (i, p, 0, 0)),
                  pl.BlockSpec((1, n_tiles, 2 * MLA_V, t), lambda i, p, q: (i, 0, p, 0))],
        out_specs=pl.BlockSpec((1, t, 2 * MLA_V), lambda i, p, q: (i, q, p)),
        out_shape=jax.ShapeDtypeStruct((b, n, h * MLA_V), BF16),
        scratch_shapes=[pltpu.VMEM((t, t), F32)] * (2 * ATT_DEPTH) + [pltpu.VMEM((t, t), BF16)] * (2 * ATT_DEPTH),
        compiler_params=_params(("arbitrary", "arbitrary", "arbitrary")),
    )(qt, k, vt)


def _bwd_chunk(s, nct, nch):
    return jnp.where(s < nct, nct - 1 - s, nch - 1 - (s - nct))


def _ret_body(dl_ref, qf, kf, vf, qb, kb, vb, of_ref, ob_ref, st_ref):
    c = SCAN_CHUNK
    hw = RET_HEADS * RET_DK
    ii = lax.broadcasted_iota(jnp.int32, (c, c), 0)
    jj = lax.broadcasted_iota(jnp.int32, (c, c), 1)
    pos_i = lax.broadcasted_iota(jnp.int32, (c, 1), 0).astype(F32)
    pos_j = lax.broadcasted_iota(jnp.int32, (1, c), 1).astype(F32)
    lane_head = lax.broadcasted_iota(jnp.int32, (c, hw), 1) // RET_DK
    for d, (q_ref, k_ref, v_ref, o_ref) in enumerate(((qf, kf, vf, of_ref), (qb, kb, vb, ob_ref))):
        rev = d == 1
        q, v = q_ref[0], v_ref[0]
        k_t = k_ref[0].astype(F32).T
        k_tb = k_t.astype(BF16)
        st = st_ref[d]
        st_b = st.astype(BF16)
        dist = ((jj - ii) if rev else (ii - jj)).astype(F32)
        for h in range(RET_HEADS):
            lg = _log_sigmoid(dl_ref[d, h])[0:1, 0:1]
            intra = jnp.where(dist >= 0.0, jnp.exp(lg * jnp.maximum(dist, 0.0)), 0.0)
            q_dec = jnp.exp(lg * ((c - pos_i) if rev else (pos_i + 1.0)))
            k_dec = jnp.exp(lg * (pos_j if rev else (c - 1.0 - pos_j)))
            rows = slice(h * RET_DK, (h + 1) * RET_DK)
            cols = slice(h * RET_DV, (h + 1) * RET_DV)
            qm = jnp.where(lane_head == h, q, jnp.zeros_like(q))
            scores = _dot(qm, k_tb) * intra
            o = _dot(scores.astype(BF16), v[:, cols]) + _dot(qm, st_b) * q_dec
            o_ref[0, :, cols] = o
            kd = (k_t[rows] * k_dec).astype(BF16)
            st_ref[d, rows, :] = st[rows] * jnp.exp(lg * float(c)) + _dot(kd, v[:, cols])


def _block_start(i, size):
    return i & ~(size - 1)


def _gla_levels():
    half = SCAN_CHUNK // 2
    out = []
    while half >= GLA_BASE:
        out.append(half)
        half //= 2
    return out


def _gla_body(qf, kf, vf, laf, qb, kb, vb, lab, of_ref, ob_ref, st_ref):
    c = SCAN_CHUNK
    hw = GLA_HEADS * GLA_DK
    levels = _gla_levels()
    ii = lax.broadcasted_iota(jnp.int32, (c, c), 0)
    jj = lax.broadcasted_iota(jnp.int32, (c, c), 1)
    pos = lax.broadcasted_iota(jnp.int32, (c, 1), 0)
    lane_head = lax.broadcasted_iota(jnp.int32, (c, hw), 1) // GLA_DK
    for d, (q_ref, k_ref, v_ref, la_ref, o_ref) in enumerate(((qf, kf, vf, laf, of_ref),
                                                               (qb, kb, vb, lab, ob_ref))):
        rev = d == 1
        q, k, v, la = q_ref[0].astype(F32), k_ref[0].astype(F32), v_ref[0], la_ref[0]

        prefix = _one_hot((jj >= ii) if rev else (jj <= ii))
        b = sum(_dot(prefix, piece) for piece in _bf16_pieces(la, 3))
        b_end = b[0:1] if rev else b[c - 1:c]

        def at_row(size, idx):
            b3 = b.reshape(c // size, size, hw)
            return jnp.broadcast_to(b3[:, idx:idx + 1, :], b3.shape).reshape(c, hw)

        q_t = (q * jnp.exp(b)).astype(BF16)
        k_tt = (k * jnp.exp(b_end - b)).T
        dec_col = jnp.exp(jnp.broadcast_to(b_end, (LANES, hw)).T[:, 0:1])

        terms = []
        for half in levels:
            e = at_row(2 * half, half if rev else half - 1)
            inner = pos & (2 * half - 1)
            late = (inner < half) if rev else (inner >= half)
            qh = jnp.where(late, q * jnp.exp(jnp.where(late, b - e, 0.0)), 0.0)
            kh = jnp.where(late, 0.0, k * jnp.exp(jnp.where(late, 0.0, e - b)))
            terms.append((qh.astype(BF16), kh.astype(BF16), _block_start(ii, 2 * half) == _block_start(jj, 2 * half)))
        r = at_row(GLA_BASE, GLA_BASE - 1 if rev else 0)
        tri = (jj >= ii) if rev else (jj <= ii)
        terms.append(((q * jnp.exp(b - r)).astype(BF16), (k * jnp.exp(r - b)).astype(BF16),
                      (_block_start(ii, GLA_BASE) == _block_start(jj, GLA_BASE)) & tri))

        st = st_ref[d]
        st_b = st.astype(BF16)
        for h in range(GLA_HEADS):
            rows = slice(h * GLA_DK, (h + 1) * GLA_DK)
            cols = slice(h * GLA_DV, (h + 1) * GLA_DV)
            sel = lane_head == h
            a = jnp.zeros((c, c), F32)
            for qh, kh, mask in terms:
                a = a + jnp.where(mask, _dot_nt(jnp.where(sel, qh, jnp.zeros_like(qh)), kh), 0.0)
            vh = v[:, cols]
            o_ref[0, :, cols] = _dot(a.astype(BF16), vh) + _dot(jnp.where(sel, q_t, jnp.zeros_like(q_t)), st_b)
            st_ref[d, rows, :] = st[rows] * dec_col[rows] + _dot(k_tt[rows].astype(BF16), vh)


def _scan_kernel(*refs):
    ret_refs, gla_refs = refs[0:7] + refs[15:17] + refs[19:20], refs[7:15] + refs[17:19] + refs[20:21]

    @pl.when(pl.program_id(1) == 0)
    def _():
        ret_refs[-1][...] = jnp.zeros_like(ret_refs[-1])
        gla_refs[-1][...] = jnp.zeros_like(gla_refs[-1])

    _ret_body(*ret_refs)
    _gla_body(*gla_refs)


def _scans(dl, rq, rk, rv, gq, gk, gv, la, nct_tokens):
    b, n, hw = rq.shape
    c = SCAN_CHUNK
    nch = n // c
    nct = nct_tokens // c
    vw = rv.shape[-1]
    fwd = lambda w, col=0: pl.BlockSpec((1, c, w), lambda i, s: (i, s, col))
    bwd = lambda w, col=0: pl.BlockSpec((1, c, w), lambda i, s: (i, _bwd_chunk(s, nct, nch), col))
    return pl.pallas_call(
        _scan_kernel,
        grid=(b, nch),
        in_specs=[_const_spec(dl.shape), fwd(hw), fwd(hw), fwd(vw), bwd(hw), bwd(hw), bwd(vw),
                  fwd(hw), fwd(hw), fwd(vw), fwd(hw, 0), bwd(hw), bwd(hw), bwd(vw), bwd(hw, 1)],
        out_specs=[fwd(vw), bwd(vw), fwd(vw), bwd(vw)],
        out_shape=[jax.ShapeDtypeStruct((b, n, vw), F32)] * 4,
        scratch_shapes=[pltpu.VMEM((2, hw, RET_DV), F32), pltpu.VMEM((2, hw, GLA_DV), F32)],
        compiler_params=_params(("arbitrary", "arbitrary")),
    )(dl, rq, rk, rv, rq, rk, rv, gq, gk, gv, la, gq, gk, gv, la)


def _head_norm(o, w, center):
    parts = []
    for h in range(o.shape[-1] // LANES):
        oh = o[:, h * LANES:(h + 1) * LANES]
        if center:
            oh = oh - jnp.mean(oh, axis=-1, keepdims=True)
        parts.append(oh * lax.rsqrt(jnp.mean(oh * oh, axis=-1, keepdims=True) + EPS))
    return jnp.concatenate(parts, axis=-1) * w


def _merge_kernel(nct, off, nx, *refs):
    x_refs = refs[:nx]
    (mod_ref, nw_ref, oa_ref, rof_ref, rob_ref, rg_ref, gof_ref, gob_ref, gr_ref, wbg_ref, wbr_ref,
     wout_ref, rnw_ref, gnw_ref, o_ref) = refs[nx:]
    d = D_MODEL
    is_ctx = pl.program_id(1) + off < nct
    row = _mod_rows(mod_ref, is_ctx)
    sh, sc, g1 = row[:, 0:d], row[:, d:2 * d], row[:, 2 * d:3 * d]
    x = _seq_tile(x_refs, is_ctx)
    u = (_rms(x, nw_ref[...]) * (1.0 + sc) + sh).astype(BF16)
    gate = _sigmoid(_dot(u, wbg_ref[...]))
    yb = _silu(rg_ref[0].astype(F32)) * _head_norm(rof_ref[0] + rob_ref[0], rnw_ref[...], True)
    yc = _silu(gr_ref[0].astype(F32)) * _head_norm(gof_ref[0] + gob_ref[0], gnw_ref[...], False)
    z = (gate[:, 0:d] * _dot(oa_ref[0], wbr_ref[0])
         + gate[:, d:2 * d] * _dot(yb.astype(BF16), wbr_ref[1])
         + gate[:, 2 * d:3 * d] * _dot(yc.astype(BF16), wbr_ref[2]))
    o_ref[0] = x + g1 * _dot(z.astype(BF16), wout_ref[...])


def _merge(xs, modsel, nw, oa, rof, rob, rg, gof, gob, gr, wbg, wbr, wout, rnw, gnw, nct, latent_only):
    tm = TOKEN_TILE
    off = nct if latent_only else 0
    x_specs, x_ops = _seq_inputs(xs, nct, off)
    b, d = x_ops[0].shape[0], x_ops[0].shape[-1]
    nt = sum(a.shape[1] for a in x_ops) // tm - off
    tok = lambda w: pl.BlockSpec((1, tm, w), lambda i, t: (i, t + off, 0))
    return pl.pallas_call(
        functools.partial(_merge_kernel, nct, off, len(x_ops)),
        grid=(b, nt),
        in_specs=x_specs + [pl.BlockSpec((1, 2, 6 * d), lambda i, t: (i, 0, 0)), _const_spec(nw.shape),
                  tok(512), tok(512), tok(512), tok(512), tok(512), tok(512), tok(512),
                  _const_spec(wbg.shape), _const_spec(wbr.shape), _const_spec(wout.shape),
                  _const_spec(rnw.shape), _const_spec(gnw.shape)],
        out_specs=pl.BlockSpec((1, tm, d), lambda i, t: (i, t, 0)),
        out_shape=jax.ShapeDtypeStruct((b, nt * tm, d), F32),
        compiler_params=_params(("arbitrary", "arbitrary")),
    )(*x_ops, modsel, nw, oa, rof, rob, rg, gof, gob, gr, wbg, wbr, wout, rnw, gnw)


def _norm2(x, row, nw_ref):
    d = D_MODEL
    sh, sc = row[:, 3 * d:4 * d], row[:, 4 * d:5 * d]
    return _rms(x, nw_ref[...]) * (1.0 + sc) + sh


def _finish(x, y, row, final, fnw_ref):
    d = D_MODEL
    out = x + row[:, 5 * d:6 * d] * y
    if final:
        out = _rms(out, fnw_ref[...])
    return out


def _ffn_kernel(nct, off, final, x_ref, mod_ref, nw_ref, wg_ref, wu_ref, wd_ref, fnw_ref, o_ref):
    row = _mod_rows(mod_ref, pl.program_id(1) + off < nct)
    x = x_ref[0]
    v = _norm2(x, row, nw_ref).astype(BF16)
    hdn = _silu(_dot(v, wg_ref[...])) * _dot(v, wu_ref[...])
    o_ref[0] = _finish(x, _dot(hdn.astype(BF16), wd_ref[...]), row, final, fnw_ref)


def _ffn(xs, modsel, nw, wg, wu, wd, fnw, nct, final):
    b, n, d = xs.shape
    tm = TOKEN_TILE
    off = nct if final else 0
    nt = n // tm - off
    return pl.pallas_call(
        functools.partial(_ffn_kernel, nct, off, final),
        grid=(b, nt),
        in_specs=[pl.BlockSpec((1, tm, d), lambda i, t: (i, t + off, 0)),
                  pl.BlockSpec((1, 2, 6 * d), lambda i, t: (i, 0, 0)), _const_spec(nw.shape),
                  _const_spec(wg.shape), _const_spec(wu.shape), _const_spec(wd.shape),
                  _const_spec(fnw.shape)],
        out_specs=pl.BlockSpec((1, tm, d), lambda i, t: (i, t, 0)),
        out_shape=jax.ShapeDtypeStruct((b, nt * tm, d), F32),
        compiler_params=_params(("arbitrary", "arbitrary")),
    )(xs, modsel, nw, wg, wu, wd, fnw)


def _top2_gates(logits):
    lane = lax.broadcasted_iota(jnp.int32, logits.shape, 1)
    m1 = jnp.max(logits, axis=-1, keepdims=True)
    i1 = jnp.min(jnp.where(logits == m1, lane, LANES), axis=-1, keepdims=True)
    rest = jnp.where(lane == i1, -jnp.inf, logits)
    m2 = jnp.max(rest, axis=-1, keepdims=True)
    i2 = jnp.min(jnp.where(rest == m2, lane, LANES), axis=-1, keepdims=True)
    e2 = jnp.exp(m2 - m1)
    w1 = 1.0 / (1.0 + e2)
    gates = jnp.where(lane == i1, w1, jnp.where(lane == i2, e2 * w1, 0.0))
    return gates, (lane == i1) | (lane == i2)


def _one_hot(cond):
    return jnp.where(cond, 1.0, 0.0).astype(BF16)


def _route_kernel(mrow, x_ref, mod_ref, nw_ref, rt_ref, v_out, gate_out, rank_out, rankt_out):
    v = _norm2(x_ref[0], mod_ref[0][mrow:mrow + 1], nw_ref)
    v_out[0] = v.astype(BF16)
    tt = v.shape[0]
    lane = lax.broadcasted_iota(jnp.int32, (tt, LANES), 1)
    logits = jnp.where(lane < N_EXPERTS, _dot_split(v, rt_ref[...]), -jnp.inf)
    gates, routed = _top2_gates(logits)
    gate_out[0] = gates
    ti = lax.broadcasted_iota(jnp.int32, (tt, tt), 0)
    tj = lax.broadcasted_iota(jnp.int32, (tt, tt), 1)
    rank_out[0] = jnp.where(routed, _dot(_one_hot(tj < ti), _one_hot(routed)), -1.0)
    routed_t = jnp.where(routed, 1.0, 0.0).T
    rank_t = jnp.where(routed_t > 0.5, _dot(routed_t.astype(BF16), _one_hot(ti < tj)), -1.0)
    rankt_out[0, 0] = rank_t[0:N_EXPERTS]


def _experts_kernel(v_ref, gate_ref, rank_ref, rankt_ref, wg_ref, wu_ref, wd_ref, o_ref, acc_ref):
    e, s = pl.program_id(1), pl.program_id(2)
    tt = v_ref.shape[1]
    fh = D_FF // 2
    tile0 = pl.multiple_of(s * tt, tt)
    lane = lax.broadcasted_iota(jnp.int32, (tt, LANES), 1)
    g_col = jnp.sum(jnp.where(lane == e, gate_ref[0], 0.0), axis=-1, keepdims=True)
    rk_col = jnp.sum(jnp.where(lane == e, rank_ref[0], 0.0), axis=-1, keepdims=True)
    rk_row = rankt_ref[0, 0, pl.ds(e, 1), :]
    n_routed = jnp.max(rk_col).astype(jnp.int32) + 1

    @pl.when(e == 0)
    def _():
        acc_ref[pl.ds(tile0, tt), :] = jnp.zeros((tt, D_MODEL), BF16)

    def block(first_slot, r):
        base = first_slot.astype(F32)
        slot_i = base + lax.broadcasted_iota(jnp.int32, (r, 1), 0).astype(F32)
        xg = _dot(_one_hot(rk_row == slot_i), v_ref[0]).astype(BF16)
        y = jnp.zeros((r, D_MODEL), F32)
        for c in range(2):
            cols = slice(c * fh, (c + 1) * fh)
            hdn = _silu(_dot(xg, wg_ref[0, :, cols])) * _dot(xg, wu_ref[0, :, cols])
            y = y + _dot(hdn.astype(BF16), wd_ref[0, cols, :])
        y = y.astype(BF16)
        slot_j = base + lax.broadcasted_iota(jnp.int32, (1, r), 1).astype(F32)
        for c in range(tt // TOKEN_TILE):
            rows = slice(c * TOKEN_TILE, (c + 1) * TOKEN_TILE)
            dst = pl.ds(tile0 + c * TOKEN_TILE, TOKEN_TILE)
            part = g_col[rows] * _dot(_one_hot(rk_col[rows] == slot_j), y)
            acc_ref[dst, :] = (acc_ref[dst, :].astype(F32) + part).astype(BF16)

    r = MOE_BLOCK
    n_full = n_routed // r
    left = n_routed - n_full * r

    def full_block(i, carry):
        block(i * r, r)
        return carry

    lax.fori_loop(0, n_full + (left > MOE_TAIL_BLOCKS[-1]).astype(jnp.int32), full_block, 0)
    for below, size in zip((0,) + MOE_TAIL_BLOCKS, MOE_TAIL_BLOCKS):
        @pl.when((left > below) & (left <= size))
        def _(size=size):
            block(n_full * r, size)

    @pl.when(e == N_EXPERTS - 1)
    def _():
        o_ref[0] = acc_ref[pl.ds(tile0, tt), :]


def _moe_finish_kernel(mrow, final, x_ref, y_ref, mod_ref, fnw_ref, o_ref):
    o_ref[0] = _finish(x_ref[0], y_ref[0].astype(F32), mod_ref[0][mrow:mrow + 1], final, fnw_ref)


def _moe(slab, mrow, modsel, nw, rt, wg, wu, wd, fnw, final):
    b, m, d = slab.shape
    tt = min(MOE_TILE, m)
    assert m % tt == 0 and tt % TOKEN_TILE == 0
    ns = m // tt
    ne, _, f = wg.shape
    mod_spec = lambda: pl.BlockSpec((1, 2, 6 * d), lambda i, s: (i, 0, 0))
    v, gates, rank, rank_t = pl.pallas_call(
        functools.partial(_route_kernel, mrow),
        grid=(b, ns),
        in_specs=[pl.BlockSpec((1, tt, d), lambda i, s: (i, s, 0)), mod_spec(), _const_spec(nw.shape),
                  _const_spec(rt.shape)],
        out_specs=[pl.BlockSpec((1, tt, d), lambda i, s: (i, s, 0)),
                   pl.BlockSpec((1, tt, LANES), lambda i, s: (i, s, 0)),
                   pl.BlockSpec((1, tt, LANES), lambda i, s: (i, s, 0)),
                   pl.BlockSpec((1, 1, ne, tt), lambda i, s: (i, s, 0, 0))],
        out_shape=[jax.ShapeDtypeStruct((b, m, d), BF16), jax.ShapeDtypeStruct((b, m, LANES), F32),
                   jax.ShapeDtypeStruct((b, m, LANES), F32), jax.ShapeDtypeStruct((b, ns, ne, tt), F32)],
        compiler_params=_params(("arbitrary", "arbitrary")),
    )(slab, modsel, nw, rt)

    tok = lambda w: pl.BlockSpec((1, tt, w), lambda i, e, s: (i, s, 0))
    wspec = lambda shape, bufs=1: pl.BlockSpec((1,) + shape, lambda i, e, s: (e, 0, 0),
                                               pipeline_mode=pl.Buffered(bufs))
    last_pass = lambda i, e, s: (i, jnp.where(e == ne - 1, s, 0), 0)
    y_sum = pl.pallas_call(
        _experts_kernel,
        grid=(b, ne, ns),
        in_specs=[tok(d), tok(LANES), tok(LANES),
                  pl.BlockSpec((1, 1, ne, tt), lambda i, e, s: (i, s, 0, 0)),
                  wspec((d, f), 2), wspec((d, f)), wspec((f, d))],
        out_specs=pl.BlockSpec((1, tt, d), last_pass),
        out_shape=jax.ShapeDtypeStruct((b, m, d), BF16),
        scratch_shapes=[pltpu.VMEM((m, d), BF16)],
        compiler_params=_params(("arbitrary", "arbitrary", "arbitrary")),
    )(v, gates, rank, rank_t, wg, wu, wd)

    tm = TOKEN_TILE
    row_tile = lambda: pl.BlockSpec((1, tm, d), lambda i, t: (i, t, 0))
    return pl.pallas_call(
        functools.partial(_moe_finish_kernel, mrow, final),
        grid=(b, m // tm),
        in_specs=[row_tile(), row_tile(), mod_spec(), _const_spec(fnw.shape)],
        out_specs=row_tile(),
        out_shape=jax.ShapeDtypeStruct((b, m, d), F32),
        compiler_params=_params(("arbitrary", "arbitrary")),
    )(slab, y_sum, modsel, fnw)


def _axial_perm():
    dd = np.arange(MLA_ROPE)
    return np.where(dd % 16 < 8, dd + 8, dd - 8)


def _ret_perm():
    dd = np.arange(RET_HEADS * RET_DK)
    return np.where(dd % RET_DK < RET_DK // 2, dd + RET_DK // 2, dd - RET_DK // 2)


def _w_in_pieces():
    o = dict(zip(("cq", "ckv", "kr", "rq", "rk", "rv", "rg", "gq", "gk", "gv", "gr", "gz", "bg"),
                 np.cumsum((0,) + IN_SPLITS).tolist()))
    def permuted(start, perm, scale=1.0):
        out = []
        for p in perm:
            if out and out[-1][0] + out[-1][1] == start + p:
                out[-1] = (out[-1][0], out[-1][1] + 1, scale)
            else:
                out.append((start + int(p), 1, scale))
        return out

    k_scale, q_scale = RET_DK ** -0.5, GLA_DK ** -0.5
    hw = RET_HEADS * RET_DK
    tail = HEAD_SLOT - MLA_NOPE - MLA_ROPE
    pieces = [(o["cq"], MLA_Q_RANK, 1.0), (o["ckv"], MLA_KV_RANK, 1.0),
              (None, MLA_NOPE, 0.0), (o["kr"], MLA_ROPE, 1.0), (None, tail, 0.0),
              (None, MLA_NOPE, 0.0)] + permuted(o["kr"], _axial_perm()) + [(None, tail, 0.0),
              (o["gz"], 2 * GLA_GATE_RANK, 1.0), (None, LANES - 2 * GLA_GATE_RANK, 0.0),
              (o["rq"], hw, 1.0)] + permuted(o["rq"], _ret_perm()) + [(o["rk"], hw, k_scale)] \
        + permuted(o["rk"], _ret_perm(), k_scale) + [
              (o["rv"], RET_HEADS * RET_DV, 1.0), (o["rg"], RET_HEADS * RET_DV, 1.0),
              (o["gq"], GLA_HEADS * GLA_DK, q_scale), (o["gk"], GLA_HEADS * GLA_DK, 1.0),
              (o["gv"], GLA_HEADS * GLA_DV, 1.0), (o["gr"], GLA_HEADS * GLA_DV, 1.0)]
    return pieces, o["bg"]


def _relayout_kernel(w_ref, w1_ref, bg_ref):
    pieces, bg0 = _w_in_pieces()
    rows = w_ref.shape[1]
    cols = []
    for start, width, scale in pieces:
        if start is None:
            cols.append(jnp.zeros((rows, width), F32))
        else:
            piece = w_ref[0, :, start:start + width]
            cols.append(piece if scale == 1.0 else piece * scale)
    w1_ref[...] = jnp.concatenate(cols, axis=1).astype(BF16)
    bg_ref[...] = w_ref[0, :, bg0:bg0 + N_BRANCH * D_MODEL].astype(BF16)


def _prep_w_in(w_in, l):
    _, d, n_in = w_in.shape
    rows = TOKEN_TILE
    n1 = sum(p[1] for p in _w_in_pieces()[0])
    nbg = N_BRANCH * D_MODEL
    return pl.pallas_call(
        _relayout_kernel,
        grid=(d // rows,),
        in_specs=[pl.BlockSpec((1, rows, n_in), lambda r: (l, r, 0))],
        out_specs=[pl.BlockSpec((rows, n1), lambda r: (r, 0)), pl.BlockSpec((rows, nbg), lambda r: (r, 0))],
        out_shape=[jax.ShapeDtypeStruct((d, n1), BF16), jax.ShapeDtypeStruct((d, nbg), BF16)],
        compiler_params=_params(("arbitrary",)),
    )(w_in)


def _prep_mla(w_uq, w_ukv):
    r = w_uq.shape[0]
    scale = (MLA_NOPE + MLA_ROPE) ** -0.5 * np.log2(np.e)
    wq3 = w_uq.reshape(r, MLA_HEADS, MLA_NOPE + MLA_ROPE) * scale
    pad = jnp.zeros((r, MLA_HEADS, HEAD_SLOT - MLA_NOPE - MLA_ROPE), w_uq.dtype)
    wq = jnp.concatenate([wq3, pad], axis=-1)
    wqs = jnp.concatenate([jnp.zeros((r, MLA_HEADS, MLA_NOPE), w_uq.dtype),
                           wq3[:, :, MLA_NOPE:][:, :, _axial_perm()], pad], axis=-1)
    rk = w_ukv.shape[0]
    wkv3 = w_ukv.reshape(rk, MLA_HEADS, MLA_NOPE + MLA_V)
    wk = jnp.concatenate([wkv3[:, :, :MLA_NOPE],
                          jnp.zeros((rk, MLA_HEADS, HEAD_SLOT - MLA_NOPE), w_ukv.dtype)], axis=-1)
    wv = wkv3[:, :, MLA_NOPE:]
    flat = lambda a: a.reshape(a.shape[0], -1).astype(BF16)
    return flat(wq), flat(wqs), flat(wk), flat(wv)


def _prep_gla_gate(w_gate, b_gate):
    hw = GLA_HEADS * GLA_DK
    wg2 = jnp.zeros((LANES, 2 * hw), F32)
    wg2 = wg2.at[0:GLA_GATE_RANK, 0:hw].set(w_gate[0])
    wg2 = wg2.at[GLA_GATE_RANK:2 * GLA_GATE_RANK, hw:2 * hw].set(w_gate[1])
    return wg2, b_gate.reshape(1, 2 * hw)


def _rope_tables(n_ctx, seq):
    n_rows = seq // GRID_W
    half = MLA_ROPE // 4
    inv = ROPE_BASE ** (-jnp.arange(half, dtype=F32) / half)
    ang_r = jnp.arange(n_rows, dtype=F32)[:, None] * inv[None, :]
    ang_c = jnp.arange(GRID_W, dtype=F32)[:, None] * inv[None, :]
    by_row = lambda a: jnp.broadcast_to(a[:, None, :], (n_rows, GRID_W, half)).reshape(seq, half)
    by_col = lambda a: jnp.broadcast_to(a[None, :, :], (n_rows, GRID_W, half)).reshape(seq, half)
    cos_r, sin_r, cos_c, sin_c = by_row(jnp.cos(ang_r)), by_row(jnp.sin(ang_r)), by_col(jnp.cos(ang_c)), by_col(jnp.sin(ang_c))
    ctx = lambda a, fill: jnp.concatenate([jnp.full((n_ctx, a.shape[1]), fill, F32), a], axis=0)
    spread = lambda a, m: jnp.dot(a, jnp.asarray(m, F32), precision=HI)
    dd = np.arange(HEAD_SLOT) - MLA_NOPE
    active = (dd >= 0) & (dd < MLA_ROPE)
    src = (np.clip(dd, 0, MLA_ROPE - 1) // 16) * half + np.clip(dd, 0, MLA_ROPE - 1) % half
    pick = (np.arange(2 * half)[:, None] == src[None, :]) & active[None, :]
    sign = np.where((dd % 16) < half, -1.0, 1.0)[None, :]
    cosq = spread(ctx(jnp.concatenate([cos_r, cos_c], axis=1), 1.0), pick) + jnp.asarray(~active, F32)[None, :]
    sinq = spread(ctx(jnp.concatenate([sin_r, sin_c], axis=1), 0.0), pick * sign)
    rhalf = RET_DK // 2
    rinv = ROPE_BASE ** (-jnp.arange(rhalf, dtype=F32) / rhalf)
    a_hi = (GRID_W * jnp.arange(n_rows, dtype=F32))[:, None, None] * rinv
    a_lo = jnp.arange(GRID_W, dtype=F32)[None, :, None] * rinv
    cos1 = (jnp.cos(a_hi) * jnp.cos(a_lo) - jnp.sin(a_hi) * jnp.sin(a_lo)).reshape(seq, rhalf)
    sin1 = (jnp.sin(a_hi) * jnp.cos(a_lo) + jnp.cos(a_hi) * jnp.sin(a_lo)).reshape(seq, rhalf)
    rl = np.arange(RET_HEADS * RET_DK) % RET_DK
    rpick = np.arange(rhalf)[:, None] == (rl % rhalf)[None, :]
    rsign = np.where(rl < rhalf, -1.0, 1.0)[None, :]
    cosr = spread(ctx(cos1, 1.0), rpick)
    sinr = spread(ctx(sin1, 0.0), rpick * rsign)
    return cosq, sinq, cosr, sinr


def kernel(x, c, ctx, c_ctx, mod_w, mod_b, norm1_w, norm2_w, w_in, mla_q_norm, mla_w_uq, mla_kv_norm, mla_w_ukv, ret_decay_logit, ret_norm_w, gla_w_gate, gla_b_gate, gla_norm_w, w_branch, w_out, ffn_w_gate, ffn_w_up, ffn_w_down, moe_router, moe_w_gate, moe_w_up, moe_w_down, final_norm_w):
    b, seq, d = x.shape
    n_ctx = ctx.shape[1]
    depth = mod_w.shape[0]
    assert d == D_MODEL and seq % GRID_W == 0
    assert n_ctx % TOKEN_TILE == 0 and n_ctx % SCAN_CHUNK == 0
    assert seq % TOKEN_TILE == 0 and seq % SCAN_CHUNK == 0
    nct = n_ctx // TOKEN_TILE

    rows = 8 * ((b + 1 + 7) // 8)
    cvec = jnp.zeros((rows, d), F32).at[0:b].set(c).at[b].set(c_ctx)
    mod = _modulation(cvec, mod_w, mod_b)
    cosq, sinq, cosr, sinr = _rope_tables(n_ctx, seq)
    fnw = final_norm_w.reshape(1, d)
    row2 = lambda a: a.reshape(1, -1)

    xs = (ctx, x)
    for l in range(depth):
        last = l == depth - 1
        modsel = jnp.stack([jnp.broadcast_to(mod[l, b], (b, 6 * d)), mod[l, 0:b]], axis=1)
        w1, wbg = _prep_w_in(w_in, l)
        wq, wqs, wk, wv = _prep_mla(mla_w_uq[l], mla_w_ukv[l])
        wg2, bg2 = _prep_gla_gate(gla_w_gate[l], gla_b_gate[l])
        nw1, nw2 = row2(norm1_w[l]), row2(norm2_w[l])

        (q, k, v, rq, rk, rv, rg, gq, gk, gv, gr, la) = _inproj(
            xs, modsel, nw1, w1, row2(mla_q_norm[l]), wq, wqs, row2(mla_kv_norm[l]), wk, wv,
            wg2, bg2, cosq, sinq, cosr, sinr, nct)
        oa = _attention(q, k, v, nct)
        dl = jnp.broadcast_to(ret_decay_logit[l][:, :, None, None], (2, RET_HEADS, 8, LANES))
        rof, rob, gof, gob = _scans(dl, rq, rk, rv, gq, gk, gv, la, n_ctx)
        is_moe = l % 2 == 1
        xs = _merge(xs, modsel, nw1, oa, rof, rob, rg, gof, gob, gr, wbg,
                    w_branch[l].astype(BF16), w_out[l].astype(BF16),
                    row2(ret_norm_w[l]), row2(gla_norm_w[l]), nct, is_moe and last)
        i = l // 2
        if not is_moe:
            xs = _ffn(xs, modsel, nw2, ffn_w_gate[i].astype(BF16), ffn_w_up[i].astype(BF16),
                      ffn_w_down[i].astype(BF16), fnw, nct, last)
            continue
        rt = jnp.pad(moe_router[i], ((0, 0), (0, LANES - N_EXPERTS)))
        experts = (moe_w_gate[i].astype(BF16), moe_w_up[i].astype(BF16), moe_w_down[i].astype(BF16))
        if last:
            xs = _moe(xs, 1, modsel, nw2, rt, *experts, fnw, True)
        else:
            xs = jnp.concatenate([_moe(xs[:, :n_ctx], 0, modsel, nw2, rt, *experts, fnw, False),
                                  _moe(xs[:, n_ctx:], 1, modsel, nw2, rt, *experts, fnw, False)], axis=1)
    return xs
```

```python
import functools

import numpy as np
import jax
import jax.numpy as jnp
from jax import lax
from jax.experimental import pallas as pl
from jax.experimental.pallas import tpu as pltpu

D_MODEL = 1024
GRID_W = 64
EPS = 1e-6
ROPE_BASE = 10000.0

MLA_HEADS = 8
MLA_NOPE = 64
MLA_ROPE = 32
MLA_V = 64
MLA_Q_RANK = 256
MLA_KV_RANK = 128

RET_HEADS = 4
RET_DK = 64
RET_DV = 128

GLA_HEADS = 4
GLA_DK = 64
GLA_DV = 128
GLA_GATE_RANK = 16
GLA_TAU = 16.0

N_BRANCH = 3
BRANCH_W = 512
D_FF = 2816
N_EXPERTS = 8

IN_SPLITS = (
    MLA_Q_RANK, MLA_KV_RANK, MLA_ROPE,
    RET_HEADS * RET_DK, RET_HEADS * RET_DK, RET_HEADS * RET_DV, RET_HEADS * RET_DV,
    GLA_HEADS * GLA_DK, GLA_HEADS * GLA_DK, GLA_HEADS * GLA_DV, GLA_HEADS * GLA_DV,
    2 * GLA_GATE_RANK,
    N_BRANCH * D_MODEL,
)

F32 = jnp.float32
BF16 = jnp.bfloat16
HI = lax.Precision.HIGHEST

LANES = 128
HEAD_SLOT = 128
TOKEN_TILE = 256
ATT_DEPTH = 3
ATT_UNROLL = 10
ATT_SUM_ROWS = 16
SCAN_CHUNK = 256
GLA_BASE = 32
MOE_TILE = 1024
MOE_BLOCK = 256
MOE_TAIL_BLOCKS = (128, 192)
MOD_COLS_TILE = 3072
VMEM_LIMIT = 56 * 1024 * 1024


def _silu(x):
    return x / (1.0 + jnp.exp(-x))


def _sigmoid(x):
    return 1.0 / (1.0 + jnp.exp(-x))


def _log_sigmoid(z):
    return jnp.minimum(z, 0.0) - jnp.log(1.0 + jnp.exp(-jnp.abs(z)))


def _rms(x, w):
    return x * lax.rsqrt(jnp.mean(x * x, axis=-1, keepdims=True) + EPS) * w


def _dot(a, b, precision=None):
    return jnp.dot(a, b, preferred_element_type=F32, precision=precision)


def _dot_nt(a, b):
    return lax.dot_general(a, b, (((1,), (1,)), ((), ())), preferred_element_type=F32)


def _bf16_pieces(x, n):
    out = []
    for _ in range(n):
        piece = x.astype(BF16)
        out.append(piece)
        x = x - piece.astype(F32)
    return out


def _dot_split(a, b):
    (a0, a1), (b0, b1) = _bf16_pieces(a, 2), _bf16_pieces(b, 2)
    return _dot(a0, b0) + (_dot(a0, b1) + _dot(a1, b0))


def _params(sem):
    return pltpu.CompilerParams(dimension_semantics=sem, vmem_limit_bytes=VMEM_LIMIT)


def _const_spec(shape):
    nd = len(shape)
    return pl.BlockSpec(shape, lambda *_: (0,) * nd, pipeline_mode=pl.Buffered(1))


def _mod_rows(mod_ref, is_ctx):
    m = mod_ref[0]
    return jnp.where(is_ctx, m[0:1], m[1:2])


def _mod_kernel(c_ref, w_ref, b_ref, o_ref):
    o_ref[0] = _dot(_silu(c_ref[...]), w_ref[0], HI) + b_ref[0]


def _modulation(cvec, mod_w, mod_b):
    depth, d, n6 = mod_w.shape
    rows = cvec.shape[0]
    tn = MOD_COLS_TILE
    return pl.pallas_call(
        _mod_kernel,
        grid=(depth, n6 // tn),
        in_specs=[pl.BlockSpec((rows, d), lambda l, j: (0, 0)),
                  pl.BlockSpec((1, d, tn), lambda l, j: (l, 0, j)),
                  pl.BlockSpec((1, 1, tn), lambda l, j: (l, 0, j))],
        out_specs=pl.BlockSpec((1, rows, tn), lambda l, j: (l, 0, j)),
        out_shape=jax.ShapeDtypeStruct((depth, rows, n6), F32),
        compiler_params=_params(("arbitrary", "arbitrary")),
    )(cvec, mod_w, mod_b.reshape(depth, 1, n6))


_C_CQ = (0, 256)
_C_CKV_KRA = (256, 512)
_C_KRB_GZ = (512, 768)
_C_RQ = (768, 1024)
_C_RK = (1024, 1280)
_C_RV = (1280, 1792)
_C_RG = (1792, 2304)
_C_GQ = (2304, 2560)
_C_GK = (2560, 2816)
_C_GV = (2816, 3328)
_C_GR = (3328, 3840)


def _seq_inputs(xs, nct, off=0):
    tm = TOKEN_TILE
    if not isinstance(xs, tuple):
        return [pl.BlockSpec((1, tm, xs.shape[-1]), lambda i, t: (i, t + off, 0))], [xs]
    d = xs[0].shape[-1]
    return ([pl.BlockSpec((1, tm, d), lambda i, t: (i, jnp.minimum(t + off, nct - 1), 0)),
             pl.BlockSpec((1, tm, d), lambda i, t: (i, jnp.maximum(t + off - nct, 0), 0))], list(xs))


def _seq_tile(x_refs, is_ctx):
    if len(x_refs) == 1:
        return x_refs[0][0]
    return jnp.where(is_ctx, x_refs[0][0], x_refs[1][0])


def _inproj_kernel(nct, nx, *refs):
    x_refs = refs[:nx]
    (mod_ref, nw_ref, w_ref, qn_ref, wq_ref, wqs_ref, kvn_ref, wk_ref, wv_ref, wg2_ref, bg2_ref,
     cq_ref, sq_ref, cr_ref, sr_ref, q_out, k_out, v_out, rq_out, rk_out, rv_out, rg_out,
     gq_out, gk_out, gv_out, gr_out, la_out) = refs[nx:]
    d = D_MODEL
    is_ctx = pl.program_id(1) < nct
    row = _mod_rows(mod_ref, is_ctx)
    sh, sc = row[:, 0:d], row[:, d:2 * d]
    u = (_rms(_seq_tile(x_refs, is_ctx), nw_ref[...]) * (1.0 + sc) + sh).astype(BF16)

    def seg(c):
        return _dot(u, w_ref[:, c[0]:c[1]])

    cqn = _rms(seg(_C_CQ), qn_ref[...]).astype(BF16)
    q_all = _dot(cqn, wq_ref[...])
    q_swp = _dot(cqn, wqs_ref[...])
    ckv_kra = seg(_C_CKV_KRA)
    krb_gz = seg(_C_KRB_GZ)
    ckvn = _rms(ckv_kra[:, 0:LANES], kvn_ref[...]).astype(BF16)
    k_all = _dot(ckvn, wk_ref[...])
    v_out[0, 0] = _dot(ckvn, wv_ref[...]).T.astype(BF16)
    cosq, sinq = cq_ref[...], sq_ref[...]
    k_rope = ckv_kra[:, LANES:] * cosq + krb_gz[:, 0:LANES] * sinq
    for h in range(MLA_HEADS):
        sl = slice(h * HEAD_SLOT, (h + 1) * HEAD_SLOT)
        q_out[0, h, 0] = (q_all[:, sl] * cosq + q_swp[:, sl] * sinq).T.astype(BF16)
        k_out[0, h] = (k_all[:, sl] + k_rope).astype(BF16)

    cosr, sinr = cr_ref[...], sr_ref[...]
    hw = RET_HEADS * RET_DK
    first = (lax.broadcasted_iota(jnp.int32, (TOKEN_TILE, hw), 1) % RET_DK) < RET_DK // 2
    swap = lambda a: jnp.where(first, pltpu.roll(a, hw - RET_DK // 2, 1), pltpu.roll(a, RET_DK // 2, 1))
    rq, rk = seg(_C_RQ), seg(_C_RK)
    rq_out[0] = (rq * cosr + swap(rq) * sinr).astype(BF16)
    rk_out[0] = (rk * cosr + swap(rk) * sinr).astype(BF16)
    rv_out[0] = seg(_C_RV).astype(BF16)
    rg_out[0] = seg(_C_RG).astype(BF16)

    gq_out[0] = seg(_C_GQ).astype(BF16)
    gk_out[0] = seg(_C_GK).astype(BF16)
    gv_out[0] = seg(_C_GV).astype(BF16)
    gr_out[0] = seg(_C_GR).astype(BF16)
    z = _dot_split(krb_gz[:, LANES:], wg2_ref[...]) + bg2_ref[...]
    la_out[0] = _log_sigmoid(z) * (1.0 / GLA_TAU)


def _inproj(xs, modsel, nw, w1, qn, wq, wqs, kvn, wk, wv, wg2, bg2, cosq, sinq, cosr, sinr, nct):
    x_specs, x_ops = _seq_inputs(xs, nct)
    b, d = x_ops[0].shape[0], x_ops[0].shape[-1]
    n = sum(a.shape[1] for a in x_ops)
    tm = TOKEN_TILE
    nt = n // tm
    h = MLA_HEADS
    tok = lambda w: pl.BlockSpec((1, tm, w), lambda i, t: (i, t, 0))
    tab = lambda w: pl.BlockSpec((tm, w), lambda i, t: (t, 0))
    q_spec = pl.BlockSpec((1, h, 1, HEAD_SLOT, tm), lambda i, t: (i, 0, t, 0, 0))
    k_spec = pl.BlockSpec((1, h, tm, HEAD_SLOT), lambda i, t: (i, 0, t, 0))
    v_spec = pl.BlockSpec((1, 1, h * MLA_V, tm), lambda i, t: (i, t, 0, 0))
    sds = lambda w, dt=BF16: jax.ShapeDtypeStruct((b, n, w), dt)
    q_sds = jax.ShapeDtypeStruct((b, h, nt, HEAD_SLOT, tm), BF16)
    k_sds = jax.ShapeDtypeStruct((b, h, n, HEAD_SLOT), BF16)
    v_sds = jax.ShapeDtypeStruct((b, nt, h * MLA_V, tm), BF16)
    return pl.pallas_call(
        functools.partial(_inproj_kernel, nct, len(x_ops)),
        grid=(b, nt),
        in_specs=x_specs + [pl.BlockSpec((1, 2, 6 * d), lambda i, t: (i, 0, 0)), _const_spec(nw.shape),
                  _const_spec(w1.shape), _const_spec(qn.shape), _const_spec(wq.shape),
                  _const_spec(wqs.shape), _const_spec(kvn.shape), _const_spec(wk.shape),
                  _const_spec(wv.shape), _const_spec(wg2.shape), _const_spec(bg2.shape),
                  tab(HEAD_SLOT), tab(HEAD_SLOT), tab(256), tab(256)],
        out_specs=[q_spec, k_spec, v_spec, tok(256), tok(256), tok(512), tok(512),
                   tok(256), tok(256), tok(512), tok(512), tok(512)],
        out_shape=[q_sds, k_sds, v_sds, sds(256), sds(256), sds(512), sds(512),
                   sds(256), sds(256), sds(512), sds(512), sds(512, F32)],
        compiler_params=_params(("arbitrary", "arbitrary")),
    )(*x_ops, modsel, nw, w1, qn, wq, wqs, kvn, wk, wv, wg2, bg2, cosq, sinq, cosr, sinr)


def _attn_kernel(nct, n_tiles, qt_ref, k_ref, vt_ref, o_ref, *bufs):
    t = TOKEN_TILE
    nd = ATT_DEPTH
    heads = range(2)
    s_bufs = (bufs[0:nd], bufs[nd:2 * nd])
    p_bufs = (bufs[2 * nd:3 * nd], bufs[3 * nd:4 * nd])
    qts = [qt_ref[0, j, 0] for j in heads]
    ones = jnp.ones((ATT_SUM_ROWS, t), BF16)

    def scores(j, c, s_buf):
        s = _dot(k_ref[0, j, pl.ds(pl.multiple_of(c * t, t), t), :], qts[j])
        s_buf[...] = s
        return jnp.max(s, axis=0, keepdims=True)

    def softmax(s_buf, p_buf, m, cmax):
        m_new = jnp.maximum(m, cmax)
        p_buf[...] = jnp.exp2(s_buf[...] - m_new).astype(BF16)
        return m_new, jnp.exp2(m - m_new)

    def values(j, c, p_buf, alpha, acc):
        lhs = jnp.concatenate([vt_ref[0, c, j * MLA_V:(j + 1) * MLA_V, :], ones], axis=0)
        return alpha * acc + _dot(lhs, p_buf[...])

    m_init = jnp.full((1, t), -jnp.inf, F32)
    acc_init = jnp.zeros((MLA_V + ATT_SUM_ROWS, t), F32)

    def finish(accs):
        outs = [acc[0:MLA_V] / acc[MLA_V:MLA_V + 1] for acc in accs]
        o_ref[0] = jnp.concatenate(outs, axis=0).T.astype(BF16)

    @pl.when(pl.program_id(2) < nct)
    def _():
        accs = []
        for j in heads:
            m, acc = m_init, acc_init
            for c in range(nct):
                cmax = scores(j, c, s_bufs[j][0])
                m, alpha = softmax(s_bufs[j][0], p_bufs[j][0], m, cmax)
                acc = values(j, c, p_bufs[j][0], alpha, acc)
            accs.append(acc)
        finish(accs)

    def step(c, slot, state, lookahead=True):
        ahead = [scores(j, c + 2, s_bufs[j][(slot + 2) % nd]) if lookahead else None for j in heads]
        accs = [values(j, c - 1, p_bufs[j][(slot - 1) % nd], state[j][1], state[j][2]) for j in heads]
        stats = [softmax(s_bufs[j][slot], p_bufs[j][slot], state[j][0], state[j][3]) for j in heads]
        return tuple((stats[j][0], stats[j][1], accs[j], state[j][4], ahead[j]) for j in heads)

    @pl.when(pl.program_id(2) >= nct)
    def _():
        state = []
        for j in heads:
            cmax0 = scores(j, 0, s_bufs[j][0])
            cmax1 = scores(j, 1, s_bufs[j][1])
            m, alpha = softmax(s_bufs[j][0], p_bufs[j][0], m_init, cmax0)
            state.append((m, alpha, acc_init, cmax1, scores(j, 2, s_bufs[j][2])))

        def trip(i, state):
            for k in range(nd):
                state = step(1 + nd * i + k, (1 + k) % nd, state)
            return state

        state = lax.fori_loop(0, (n_tiles - 3) // nd, trip, tuple(state), unroll=ATT_UNROLL)
        for c in (n_tiles - 2, n_tiles - 1):
            state = step(c, c % nd, state, lookahead=False)
        finish([values(j, n_tiles - 1, p_bufs[j][(n_tiles - 1) % nd], state[j][1], state[j][2])
                for j in heads])


def _attention(qt, k, vt, nct):
    b, h, n_tiles, w, t = qt.shape
    n = n_tiles * t
    assert n_tiles >= 3 and (n_tiles - 3) % ATT_DEPTH == 0
    return pl.pallas_call(
        functools.partial(_attn_kernel, nct, n_tiles),
        grid=(b, h // 2, n_tiles),
        in_specs=[pl.BlockSpec((1, 2, 1, w, t), lambda i, p, q: (i, p, q, 0, 0)),
                  pl.BlockSpec((1, 2, n, w), lambda i, p, q: (i, p, 0, 0)),
                  pl.BlockSpec((1, n_tiles, 2 * MLA_V, t), lambda i, p, q: (i, 0, p, 0))],
        out_specs=pl.BlockSpec((1, t, 2 * MLA_V), lambda i, p, q: (i, q, p)),
        out_shape=jax.ShapeDtypeStruct((b, n, h * MLA_V), BF16),
        scratch_shapes=[pltpu.VMEM((t, t), F32)] * (2 * ATT_DEPTH) + [pltpu.VMEM((t, t), BF16)] * (2 * ATT_DEPTH),
        compiler_params=_params(("arbitrary", "arbitrary", "arbitrary")),
    )(qt, k, vt)


def _bwd_chunk(s, nct, nch):
    return jnp.where(s < nct, nct - 1 - s, nch - 1 - (s - nct))


def _ret_body(dl_ref, qf, kf, vf, qb, kb, vb, of_ref, ob_ref, st_ref):
    c = SCAN_CHUNK
    hw = RET_HEADS * RET_DK
    ii = lax.broadcasted_iota(jnp.int32, (c, c), 0)
    jj = lax.broadcasted_iota(jnp.int32, (c, c), 1)
    pos_i = lax.broadcasted_iota(jnp.int32, (c, 1), 0).astype(F32)
    pos_j = lax.broadcasted_iota(jnp.int32, (1, c), 1).astype(F32)
    lane_head = lax.broadcasted_iota(jnp.int32, (c, hw), 1) // RET_DK
    for d, (q_ref, k_ref, v_ref, o_ref) in enumerate(((qf, kf, vf, of_ref), (qb, kb, vb, ob_ref))):
        rev = d == 1
        q, v = q_ref[0], v_ref[0]
        k_t = k_ref[0].astype(F32).T
        k_tb = k_t.astype(BF16)
        st = st_ref[d]
        st_b = st.astype(BF16)
        dist = ((jj - ii) if rev else (ii - jj)).astype(F32)
        for h in range(RET_HEADS):
            lg = _log_sigmoid(dl_ref[d, h])[0:1, 0:1]
            intra = jnp.where(dist >= 0.0, jnp.exp(lg * jnp.maximum(dist, 0.0)), 0.0)
            q_dec = jnp.exp(lg * ((c - pos_i) if rev else (pos_i + 1.0)))
            k_dec = jnp.exp(lg * (pos_j if rev else (c - 1.0 - pos_j)))
            rows = slice(h * RET_DK, (h + 1) * RET_DK)
            cols = slice(h * RET_DV, (h + 1) * RET_DV)
            qm = jnp.where(lane_head == h, q, jnp.zeros_like(q))
            scores = _dot(qm, k_tb) * intra
            o = _dot(scores.astype(BF16), v[:, cols]) + _dot(qm, st_b) * q_dec
            o_ref[0, :, cols] = o
            kd = (k_t[rows] * k_dec).astype(BF16)
            st_ref[d, rows, :] = st[rows] * jnp.exp(lg * float(c)) + _dot(kd, v[:, cols])


def _block_start(i, size):
    return i & ~(size - 1)


def _gla_levels():
    half = SCAN_CHUNK // 2
    out = []
    while half >= GLA_BASE:
        out.append(half)
        half //= 2
    return out


def _gla_body(qf, kf, vf, laf, qb, kb, vb, lab, of_ref, ob_ref, st_ref):
    c = SCAN_CHUNK
    hw = GLA_HEADS * GLA_DK
    levels = _gla_levels()
    ii = lax.broadcasted_iota(jnp.int32, (c, c), 0)
    jj = lax.broadcasted_iota(jnp.int32, (c, c), 1)
    pos = lax.broadcasted_iota(jnp.int32, (c, 1), 0)
    lane_head = lax.broadcasted_iota(jnp.int32, (c, hw), 1) // GLA_DK
    for d, (q_ref, k_ref, v_ref, la_ref, o_ref) in enumerate(((qf, kf, vf, laf, of_ref),
                                                               (qb, kb, vb, lab, ob_ref))):
        rev = d == 1
        q, k, v, la = q_ref[0].astype(F32), k_ref[0].astype(F32), v_ref[0], la_ref[0]

        prefix = _one_hot((jj >= ii) if rev else (jj <= ii))
        b = sum(_dot(prefix, piece) for piece in _bf16_pieces(la, 3))
        b_end = b[0:1] if rev else b[c - 1:c]

        def at_row(size, idx):
            b3 = b.reshape(c // size, size, hw)
            return jnp.broadcast_to(b3[:, idx:idx + 1, :], b3.shape).reshape(c, hw)

        q_t = (q * jnp.exp(b)).astype(BF16)
        k_tt = (k * jnp.exp(b_end - b)).T
        dec_col = jnp.exp(jnp.broadcast_to(b_end, (LANES, hw)).T[:, 0:1])

        terms = []
        for half in levels:
            e = at_row(2 * half, half if rev else half - 1)
            inner = pos & (2 * half - 1)
            late = (inner < half) if rev else (inner >= half)
            qh = jnp.where(late, q * jnp.exp(jnp.where(late, b - e, 0.0)), 0.0)
            kh = jnp.where(late, 0.0, k * jnp.exp(jnp.where(late, 0.0, e - b)))
            terms.append((qh.astype(BF16), kh.astype(BF16), _block_start(ii, 2 * half) == _block_start(jj, 2 * half)))
        r = at_row(GLA_BASE, GLA_BASE - 1 if rev else 0)
        tri = (jj >= ii) if rev else (jj <= ii)
        terms.append(((q * jnp.exp(b - r)).astype(BF16), (k * jnp.exp(r - b)).astype(BF16),
                      (_block_start(ii, GLA_BASE) == _block_start(jj, GLA_BASE)) & tri))

        st = st_ref[d]
        st_b = st.astype(BF16)
        for h in range(GLA_HEADS):
            rows = slice(h * GLA_DK, (h + 1) * GLA_DK)
            cols = slice(h * GLA_DV, (h + 1) * GLA_DV)
            sel = lane_head == h
            a = jnp.zeros((c, c), F32)
            for qh, kh, mask in terms:
                a = a + jnp.where(mask, _dot_nt(jnp.where(sel, qh, jnp.zeros_like(qh)), kh), 0.0)
            vh = v[:, cols]
            o_ref[0, :, cols] = _dot(a.astype(BF16), vh) + _dot(jnp.where(sel, q_t, jnp.zeros_like(q_t)), st_b)
            st_ref[d, rows, :] = st[rows] * dec_col[rows] + _dot(k_tt[rows].astype(BF16), vh)


def _scan_kernel(*refs):
    ret_refs, gla_refs = refs[0:7] + refs[15:17] + refs[19:20], refs[7:15] + refs[17:19] + refs[20:21]

    @pl.when(pl.program_id(1) == 0)
    def _():
        ret_refs[-1][...] = jnp.zeros_like(ret_refs[-1])
        gla_refs[-1][...] = jnp.zeros_like(gla_refs[-1])

    _ret_body(*ret_refs)
    _gla_body(*gla_refs)


def _scans(dl, rq, rk, rv, gq, gk, gv, la, nct_tokens):
    b, n, hw = rq.shape
    c = SCAN_CHUNK
    nch = n // c
    nct = nct_tokens // c
    vw = rv.shape[-1]
    fwd = lambda w, col=0: pl.BlockSpec((1, c, w), lambda i, s: (i, s, col))
    bwd = lambda w, col=0: pl.BlockSpec((1, c, w), lambda i, s: (i, _bwd_chunk(s, nct, nch), col))
    return pl.pallas_call(
        _scan_kernel,
        grid=(b, nch),
        in_specs=[_const_spec(dl.shape), fwd(hw), fwd(hw), fwd(vw), bwd(hw), bwd(hw), bwd(vw),
                  fwd(hw), fwd(hw), fwd(vw), fwd(hw, 0), bwd(hw), bwd(hw), bwd(vw), bwd(hw, 1)],
        out_specs=[fwd(vw), bwd(vw), fwd(vw), bwd(vw)],
        out_shape=[jax.ShapeDtypeStruct((b, n, vw), F32)] * 4,
        scratch_shapes=[pltpu.VMEM((2, hw, RET_DV), F32), pltpu.VMEM((2, hw, GLA_DV), F32)],
        compiler_params=_params(("arbitrary", "arbitrary")),
    )(dl, rq, rk, rv, rq, rk, rv, gq, gk, gv, la, gq, gk, gv, la)


def _head_norm(o, w, center):
    parts = []
    for h in range(o.shape[-1] // LANES):
        oh = o[:, h * LANES:(h + 1) * LANES]
        if center:
            oh = oh - jnp.mean(oh, axis=-1, keepdims=True)
        parts.append(oh * lax.rsqrt(jnp.mean(oh * oh, axis=-1, keepdims=True) + EPS))
    return jnp.concatenate(parts, axis=-1) * w


def _merge_kernel(nct, off, nx, *refs):
    x_refs = refs[:nx]
    (mod_ref, nw_ref, oa_ref, rof_ref, rob_ref, rg_ref, gof_ref, gob_ref, gr_ref, wbg_ref, wbr_ref,
     wout_ref, rnw_ref, gnw_ref, o_ref) = refs[nx:]
    d = D_MODEL
    is_ctx = pl.program_id(1) + off < nct
    row = _mod_rows(mod_ref, is_ctx)
    sh, sc, g1 = row[:, 0:d], row[:, d:2 * d], row[:, 2 * d:3 * d]
    x = _seq_tile(x_refs, is_ctx)
    u = (_rms(x, nw_ref[...]) * (1.0 + sc) + sh).astype(BF16)
    gate = _sigmoid(_dot(u, wbg_ref[...]))
    yb = _silu(rg_ref[0].astype(F32)) * _head_norm(rof_ref[0] + rob_ref[0], rnw_ref[...], True)
    yc = _silu(gr_ref[0].astype(F32)) * _head_norm(gof_ref[0] + gob_ref[0], gnw_ref[...], False)
    z = (gate[:, 0:d] * _dot(oa_ref[0], wbr_ref[0])
         + gate[:, d:2 * d] * _dot(yb.astype(BF16), wbr_ref[1])
         + gate[:, 2 * d:3 * d] * _dot(yc.astype(BF16), wbr_ref[2]))
    o_ref[0] = x + g1 * _dot(z.astype(BF16), wout_ref[...])


def _merge(xs, modsel, nw, oa, rof, rob, rg, gof, gob, gr, wbg, wbr, wout, rnw, gnw, nct, latent_only):
    tm = TOKEN_TILE
    off = nct if latent_only else 0
    x_specs, x_ops = _seq_inputs(xs, nct, off)
    b, d = x_ops[0].shape[0], x_ops[0].shape[-1]
    nt = sum(a.shape[1] for a in x_ops) // tm - off
    tok = lambda w: pl.BlockSpec((1, tm, w), lambda i, t: (i, t + off, 0))
    return pl.pallas_call(
        functools.partial(_merge_kernel, nct, off, len(x_ops)),
        grid=(b, nt),
        in_specs=x_specs + [pl.BlockSpec((1, 2, 6 * d), lambda i, t: (i, 0, 0)), _const_spec(nw.shape),
                  tok(512), tok(512), tok(512), tok(512), tok(512), tok(512), tok(512),
                  _const_spec(wbg.shape), _const_spec(wbr.shape), _const_spec(wout.shape),
                  _const_spec(rnw.shape), _const_spec(gnw.shape)],
        out_specs=pl.BlockSpec((1, tm, d), lambda i, t: (i, t, 0)),
        out_shape=jax.ShapeDtypeStruct((b, nt * tm, d), F32),
        compiler_params=_params(("arbitrary", "arbitrary")),
    )(*x_ops, modsel, nw, oa, rof, rob, rg, gof, gob, gr, wbg, wbr, wout, rnw, gnw)


def _norm2(x, row, nw_ref):
    d = D_MODEL
    sh, sc = row[:, 3 * d:4 * d], row[:, 4 * d:5 * d]
    return _rms(x, nw_ref[...]) * (1.0 + sc) + sh


def _finish(x, y, row, final, fnw_ref):
    d = D_MODEL
    out = x + row[:, 5 * d:6 * d] * y
    if final:
        out = _rms(out, fnw_ref[...])
    return out


def _ffn_kernel(nct, off, final, x_ref, mod_ref, nw_ref, wg_ref, wu_ref, wd_ref, fnw_ref, o_ref):
    row = _mod_rows(mod_ref, pl.program_id(1) + off < nct)
    x = x_ref[0]
    v = _norm2(x, row, nw_ref).astype(BF16)
    hdn = _silu(_dot(v, wg_ref[...])) * _dot(v, wu_ref[...])
    o_ref[0] = _finish(x, _dot(hdn.astype(BF16), wd_ref[...]), row, final, fnw_ref)


def _ffn(xs, modsel, nw, wg, wu, wd, fnw, nct, final):
    b, n, d = xs.shape
    tm = TOKEN_TILE
    off = nct if final else 0
    nt = n // tm - off
    return pl.pallas_call(
        functools.partial(_ffn_kernel, nct, off, final),
        grid=(b, nt),
        in_specs=[pl.BlockSpec((1, tm, d), lambda i, t: (i, t + off, 0)),
                  pl.BlockSpec((1, 2, 6 * d), lambda i, t: (i, 0, 0)), _const_spec(nw.shape),
                  _const_spec(wg.shape), _const_spec(wu.shape), _const_spec(wd.shape),
                  _const_spec(fnw.shape)],
        out_specs=pl.BlockSpec((1, tm, d), lambda i, t: (i, t, 0)),
        out_shape=jax.ShapeDtypeStruct((b, nt * tm, d), F32),
        compiler_params=_params(("arbitrary", "arbitrary")),
    )(xs, modsel, nw, wg, wu, wd, fnw)


def _top2_gates(logits):
    lane = lax.broadcasted_iota(jnp.int32, logits.shape, 1)
    m1 = jnp.max(logits, axis=-1, keepdims=True)
    i1 = jnp.min(jnp.where(logits == m1, lane, LANES), axis=-1, keepdims=True)
    rest = jnp.where(lane == i1, -jnp.inf, logits)
    m2 = jnp.max(rest, axis=-1, keepdims=True)
    i2 = jnp.min(jnp.where(rest == m2, lane, LANES), axis=-1, keepdims=True)
    e2 = jnp.exp(m2 - m1)
    w1 = 1.0 / (1.0 + e2)
    gates = jnp.where(lane == i1, w1, jnp.where(lane == i2, e2 * w1, 0.0))
    return gates, (lane == i1) | (lane == i2)


def _one_hot(cond):
    return jnp.where(cond, 1.0, 0.0).astype(BF16)


def _route_kernel(mrow, x_ref, mod_ref, nw_ref, rt_ref, v_out, gate_out, rank_out, rankt_out):
    v = _norm2(x_ref[0], mod_ref[0][mrow:mrow + 1], nw_ref)
    v_out[0] = v.astype(BF16)
    tt = v.shape[0]
    lane = lax.broadcasted_iota(jnp.int32, (tt, LANES), 1)
    logits = jnp.where(lane < N_EXPERTS, _dot_split(v, rt_ref[...]), -jnp.inf)
    gates, routed = _top2_gates(logits)
    gate_out[0] = gates
    ti = lax.broadcasted_iota(jnp.int32, (tt, tt), 0)
    tj = lax.broadcasted_iota(jnp.int32, (tt, tt), 1)
    rank_out[0] = jnp.where(routed, _dot(_one_hot(tj < ti), _one_hot(routed)), -1.0)
    routed_t = jnp.where(routed, 1.0, 0.0).T
    rank_t = jnp.where(routed_t > 0.5, _dot(routed_t.astype(BF16), _one_hot(ti < tj)), -1.0)
    rankt_out[0, 0] = rank_t[0:N_EXPERTS]


def _experts_kernel(v_ref, gate_ref, rank_ref, rankt_ref, wg_ref, wu_ref, wd_ref, o_ref, acc_ref):
    e, s = pl.program_id(1), pl.program_id(2)
    tt = v_ref.shape[1]
    fh = D_FF // 2
    tile0 = pl.multiple_of(s * tt, tt)
    lane = lax.broadcasted_iota(jnp.int32, (tt, LANES), 1)
    g_col = jnp.sum(jnp.where(lane == e, gate_ref[0], 0.0), axis=-1, keepdims=True)
    rk_col = jnp.sum(jnp.where(lane == e, rank_ref[0], 0.0), axis=-1, keepdims=True)
    rk_row = rankt_ref[0, 0, pl.ds(e, 1), :]
    n_routed = jnp.max(rk_col).astype(jnp.int32) + 1

    @pl.when(e == 0)
    def _():
        acc_ref[pl.ds(tile0, tt), :] = jnp.zeros((tt, D_MODEL), BF16)

    def block(first_slot, r):
        base = first_slot.astype(F32)
        slot_i = base + lax.broadcasted_iota(jnp.int32, (r, 1), 0).astype(F32)
        xg = _dot(_one_hot(rk_row == slot_i), v_ref[0]).astype(BF16)
        y = jnp.zeros((r, D_MODEL), F32)
        for c in range(2):
            cols = slice(c * fh, (c + 1) * fh)
            hdn = _silu(_dot(xg, wg_ref[0, :, cols])) * _dot(xg, wu_ref[0, :, cols])
            y = y + _dot(hdn.astype(BF16), wd_ref[0, cols, :])
        y = y.astype(BF16)
        slot_j = base + lax.broadcasted_iota(jnp.int32, (1, r), 1).astype(F32)
        for c in range(tt // TOKEN_TILE):
            rows = slice(c * TOKEN_TILE, (c + 1) * TOKEN_TILE)
            dst = pl.ds(tile0 + c * TOKEN_TILE, TOKEN_TILE)
            part = g_col[rows] * _dot(_one_hot(rk_col[rows] == slot_j), y)
            acc_ref[dst, :] = (acc_ref[dst, :].astype(F32) + part).astype(BF16)

    r = MOE_BLOCK
    n_full = n_routed // r
    left = n_routed - n_full * r

    def full_block(i, carry):
        block(i * r, r)
        return carry

    lax.fori_loop(0, n_full + (left > MOE_TAIL_BLOCKS[-1]).astype(jnp.int32), full_block, 0)
    for below, size in zip((0,) + MOE_TAIL_BLOCKS, MOE_TAIL_BLOCKS):
        @pl.when((left > below) & (left <= size))
        def _(size=size):
            block(n_full * r, size)

    @pl.when(e == N_EXPERTS - 1)
    def _():
        o_ref[0] = acc_ref[pl.ds(tile0, tt), :]


def _moe_finish_kernel(mrow, final, x_ref, y_ref, mod_ref, fnw_ref, o_ref):
    o_ref[0] = _finish(x_ref[0], y_ref[0].astype(F32), mod_ref[0][mrow:mrow + 1], final, fnw_ref)


def _moe(slab, mrow, modsel, nw, rt, wg, wu, wd, fnw, final):
    b, m, d = slab.shape
    tt = min(MOE_TILE, m)
    assert m % tt == 0 and tt % TOKEN_TILE == 0
    ns = m // tt
    ne, _, f = wg.shape
    mod_spec = lambda: pl.BlockSpec((1, 2, 6 * d), lambda i, s: (i, 0, 0))
    v, gates, rank, rank_t = pl.pallas_call(
        functools.partial(_route_kernel, mrow),
        grid=(b, ns),
        in_specs=[pl.BlockSpec((1, tt, d), lambda i, s: (i, s, 0)), mod_spec(), _const_spec(nw.shape),
                  _const_spec(rt.shape)],
        out_specs=[pl.BlockSpec((1, tt, d), lambda i, s: (i, s, 0)),
                   pl.BlockSpec((1, tt, LANES), lambda i, s: (i, s, 0)),
                   pl.BlockSpec((1, tt, LANES), lambda i, s: (i, s, 0)),
                   pl.BlockSpec((1, 1, ne, tt), lambda i, s: (i, s, 0, 0))],
        out_shape=[jax.ShapeDtypeStruct((b, m, d), BF16), jax.ShapeDtypeStruct((b, m, LANES), F32),
                   jax.ShapeDtypeStruct((b, m, LANES), F32), jax.ShapeDtypeStruct((b, ns, ne, tt), F32)],
        compiler_params=_params(("arbitrary", "arbitrary")),
    )(slab, modsel, nw, rt)

    tok = lambda w: pl.BlockSpec((1, tt, w), lambda i, e, s: (i, s, 0))
    wspec = lambda shape, bufs=1: pl.BlockSpec((1,) + shape, lambda i, e, s: (e, 0, 0),
                                               pipeline_mode=pl.Buffered(bufs))
    last_pass = lambda i, e, s: (i, jnp.where(e == ne - 1, s, 0), 0)
    y_sum = pl.pallas_call(
        _experts_kernel,
        grid=(b, ne, ns),
        in_specs=[tok(d), tok(LANES), tok(LANES),
                  pl.BlockSpec((1, 1, ne, tt), lambda i, e, s: (i, s, 0, 0)),
                  wspec((d, f), 2), wspec((d, f)), wspec((f, d))],
        out_specs=pl.BlockSpec((1, tt, d), last_pass),
        out_shape=jax.ShapeDtypeStruct((b, m, d), BF16),
        scratch_shapes=[pltpu.VMEM((m, d), BF16)],
        compiler_params=_params(("arbitrary", "arbitrary", "arbitrary")),
    )(v, gates, rank, rank_t, wg, wu, wd)

    tm = tt
    row_tile = lambda: pl.BlockSpec((1, tm, d), lambda i, t: (i, t, 0))
    return pl.pallas_call(
        functools.partial(_moe_finish_kernel, mrow, final),
        grid=(b, m // tm),
        in_specs=[row_tile(), row_tile(), mod_spec(), _const_spec(fnw.shape)],
        out_specs=row_tile(),
        out_shape=jax.ShapeDtypeStruct((b, m, d), F32),
        compiler_params=_params(("arbitrary", "arbitrary")),
    )(slab, y_sum, modsel, fnw)


def _axial_perm():
    dd = np.arange(MLA_ROPE)
    return np.where(dd % 16 < 8, dd + 8, dd - 8)


def _w_in_pieces():
    o = dict(zip(("cq", "ckv", "kr", "rq", "rk", "rv", "rg", "gq", "gk", "gv", "gr", "gz", "bg"),
                 np.cumsum((0,) + IN_SPLITS).tolist()))
    def permuted(start, perm, scale=1.0):
        out = []
        for p in perm:
            if out and out[-1][0] + out[-1][1] == start + p:
                out[-1] = (out[-1][0], out[-1][1] + 1, scale)
            else:
                out.append((start + int(p), 1, scale))
        return out

    k_scale, q_scale = RET_DK ** -0.5, GLA_DK ** -0.5
    hw = RET_HEADS * RET_DK
    tail = HEAD_SLOT - MLA_NOPE - MLA_ROPE
    pieces = [(o["cq"], MLA_Q_RANK, 1.0), (o["ckv"], MLA_KV_RANK, 1.0),
              (None, MLA_NOPE, 0.0), (o["kr"], MLA_ROPE, 1.0), (None, tail, 0.0),
              (None, MLA_NOPE, 0.0)] + permuted(o["kr"], _axial_perm()) + [(None, tail, 0.0),
              (o["gz"], 2 * GLA_GATE_RANK, 1.0), (None, LANES - 2 * GLA_GATE_RANK, 0.0),
              (o["rq"], hw, 1.0), (o["rk"], hw, k_scale),
              (o["rv"], RET_HEADS * RET_DV, 1.0), (o["rg"], RET_HEADS * RET_DV, 1.0),
              (o["gq"], GLA_HEADS * GLA_DK, q_scale), (o["gk"], GLA_HEADS * GLA_DK, 1.0),
              (o["gv"], GLA_HEADS * GLA_DV, 1.0), (o["gr"], GLA_HEADS * GLA_DV, 1.0)]
    return pieces, o["bg"]


def _relayout_kernel(w_ref, w1_ref, bg_ref):
    pieces, bg0 = _w_in_pieces()
    rows = w_ref.shape[1]
    cols = []
    for start, width, scale in pieces:
        if start is None:
            cols.append(jnp.zeros((rows, width), F32))
        else:
            piece = w_ref[0, :, start:start + width]
            cols.append(piece if scale == 1.0 else piece * scale)
    w1_ref[...] = jnp.concatenate(cols, axis=1).astype(BF16)
    bg_ref[...] = w_ref[0, :, bg0:bg0 + N_BRANCH * D_MODEL].astype(BF16)


def _prep_w_in(w_in, l):
    _, d, n_in = w_in.shape
    rows = TOKEN_TILE
    n1 = sum(p[1] for p in _w_in_pieces()[0])
    nbg = N_BRANCH * D_MODEL
    return pl.pallas_call(
        _relayout_kernel,
        grid=(d // rows,),
        in_specs=[pl.BlockSpec((1, rows, n_in), lambda r: (l, r, 0))],
        out_specs=[pl.BlockSpec((rows, n1), lambda r: (r, 0)), pl.BlockSpec((rows, nbg), lambda r: (r, 0))],
        out_shape=[jax.ShapeDtypeStruct((d, n1), BF16), jax.ShapeDtypeStruct((d, nbg), BF16)],
        compiler_params=_params(("arbitrary",)),
    )(w_in)


def _prep_mla(w_uq, w_ukv):
    r = w_uq.shape[0]
    scale = (MLA_NOPE + MLA_ROPE) ** -0.5 * np.log2(np.e)
    wq3 = w_uq.reshape(r, MLA_HEADS, MLA_NOPE + MLA_ROPE) * scale
    pad = jnp.zeros((r, MLA_HEADS, HEAD_SLOT - MLA_NOPE - MLA_ROPE), w_uq.dtype)
    wq = jnp.concatenate([wq3, pad], axis=-1)
    wqs = jnp.concatenate([jnp.zeros((r, MLA_HEADS, MLA_NOPE), w_uq.dtype),
                           wq3[:, :, MLA_NOPE:][:, :, _axial_perm()], pad], axis=-1)
    rk = w_ukv.shape[0]
    wkv3 = w_ukv.reshape(rk, MLA_HEADS, MLA_NOPE + MLA_V)
    wk = jnp.concatenate([wkv3[:, :, :MLA_NOPE],
                          jnp.zeros((rk, MLA_HEADS, HEAD_SLOT - MLA_NOPE), w_ukv.dtype)], axis=-1)
    wv = wkv3[:, :, MLA_NOPE:]
    flat = lambda a: a.reshape(a.shape[0], -1).astype(BF16)
    return flat(wq), flat(wqs), flat(wk), flat(wv)


def _prep_gla_gate(w_gate, b_gate):
    hw = GLA_HEADS * GLA_DK
    wg2 = jnp.zeros((LANES, 2 * hw), F32)
    wg2 = wg2.at[0:GLA_GATE_RANK, 0:hw].set(w_gate[0])
    wg2 = wg2.at[GLA_GATE_RANK:2 * GLA_GATE_RANK, hw:2 * hw].set(w_gate[1])
    return wg2, b_gate.reshape(1, 2 * hw)


def _rope_tables(n_ctx, seq):
    n_rows = seq // GRID_W
    half = MLA_ROPE // 4
    inv = ROPE_BASE ** (-jnp.arange(half, dtype=F32) / half)
    ang_r = jnp.arange(n_rows, dtype=F32)[:, None] * inv[None, :]
    ang_c = jnp.arange(GRID_W, dtype=F32)[:, None] * inv[None, :]
    by_row = lambda a: jnp.broadcast_to(a[:, None, :], (n_rows, GRID_W, half)).reshape(seq, half)
    by_col = lambda a: jnp.broadcast_to(a[None, :, :], (n_rows, GRID_W, half)).reshape(seq, half)
    cos_r, sin_r, cos_c, sin_c = by_row(jnp.cos(ang_r)), by_row(jnp.sin(ang_r)), by_col(jnp.cos(ang_c)), by_col(jnp.sin(ang_c))
    ctx = lambda a, fill: jnp.concatenate([jnp.full((n_ctx, a.shape[1]), fill, F32), a], axis=0)
    spread = lambda a, m: jnp.dot(a, jnp.asarray(m, F32), precision=HI)
    dd = np.arange(HEAD_SLOT) - MLA_NOPE
    active = (dd >= 0) & (dd < MLA_ROPE)
    src = (np.clip(dd, 0, MLA_ROPE - 1) // 16) * half + np.clip(dd, 0, MLA_ROPE - 1) % half
    pick = (np.arange(2 * half)[:, None] == src[None, :]) & active[None, :]
    sign = np.where((dd % 16) < half, -1.0, 1.0)[None, :]
    cosq = spread(ctx(jnp.concatenate([cos_r, cos_c], axis=1), 1.0), pick) + jnp.asarray(~active, F32)[None, :]
    sinq = spread(ctx(jnp.concatenate([sin_r, sin_c], axis=1), 0.0), pick * sign)
    rhalf = RET_DK // 2
    rinv = ROPE_BASE ** (-jnp.arange(rhalf, dtype=F32) / rhalf)
    a_hi = (GRID_W * jnp.arange(n_rows, dtype=F32))[:, None, None] * rinv
    a_lo = jnp.arange(GRID_W, dtype=F32)[None, :, None] * rinv
    cos1 = (jnp.cos(a_hi) * jnp.cos(a_lo) - jnp.sin(a_hi) * jnp.sin(a_lo)).reshape(seq, rhalf)
    sin1 = (jnp.sin(a_hi) * jnp.cos(a_lo) + jnp.cos(a_hi) * jnp.sin(a_lo)).reshape(seq, rhalf)
    rl = np.arange(RET_HEADS * RET_DK) % RET_DK
    rpick = np.arange(rhalf)[:, None] == (rl % rhalf)[None, :]
    rsign = np.where(rl < rhalf, -1.0, 1.0)[None, :]
    cosr = spread(ctx(cos1, 1.0), rpick)
    sinr = spread(ctx(sin1, 0.0), rpick * rsign)
    return cosq, sinq, cosr, sinr


def kernel(x, c, ctx, c_ctx, mod_w, mod_b, norm1_w, norm2_w, w_in, mla_q_norm, mla_w_uq, mla_kv_norm, mla_w_ukv, ret_decay_logit, ret_norm_w, gla_w_gate, gla_b_gate, gla_norm_w, w_branch, w_out, ffn_w_gate, ffn_w_up, ffn_w_down, moe_router, moe_w_gate, moe_w_up, moe_w_down, final_norm_w):
    b, seq, d = x.shape
    n_ctx = ctx.shape[1]
    depth = mod_w.shape[0]
    assert d == D_MODEL and seq % GRID_W == 0
    assert n_ctx % TOKEN_TILE == 0 and n_ctx % SCAN_CHUNK == 0
    assert seq % TOKEN_TILE == 0 and seq % SCAN_CHUNK == 0
    nct = n_ctx // TOKEN_TILE

    rows = 8 * ((b + 1 + 7) // 8)
    cvec = jnp.zeros((rows, d), F32).at[0:b].set(c).at[b].set(c_ctx)
    mod = _modulation(cvec, mod_w, mod_b)
    cosq, sinq, cosr, sinr = _rope_tables(n_ctx, seq)
    fnw = final_norm_w.reshape(1, d)
    row2 = lambda a: a.reshape(1, -1)

    xs = (ctx, x)
    for l in range(depth):
        last = l == depth - 1
        modsel = jnp.stack([jnp.broadcast_to(mod[l, b], (b, 6 * d)), mod[l, 0:b]], axis=1)
        w1, wbg = _prep_w_in(w_in, l)
        wq, wqs, wk, wv = _prep_mla(mla_w_uq[l], mla_w_ukv[l])
        wg2, bg2 = _prep_gla_gate(gla_w_gate[l], gla_b_gate[l])
        nw1, nw2 = row2(norm1_w[l]), row2(norm2_w[l])

        (q, k, v, rq, rk, rv, rg, gq, gk, gv, gr, la) = _inproj(
            xs, modsel, nw1, w1, row2(mla_q_norm[l]), wq, wqs, row2(mla_kv_norm[l]), wk, wv,
            wg2, bg2, cosq, sinq, cosr, sinr, nct)
        oa = _attention(q, k, v, nct)
        dl = jnp.broadcast_to(ret_decay_logit[l][:, :, None, None], (2, RET_HEADS, 8, LANES))
        rof, rob, gof, gob = _scans(dl, rq, rk, rv, gq, gk, gv, la, n_ctx)
        is_moe = l % 2 == 1
        xs = _merge(xs, modsel, nw1, oa, rof, rob, rg, gof, gob, gr, wbg,
                    w_branch[l].astype(BF16), w_out[l].astype(BF16),
                    row2(ret_norm_w[l]), row2(gla_norm_w[l]), nct, is_moe and last)
        i = l // 2
        if not is_moe:
            xs = _ffn(xs, modsel, nw2, ffn_w_gate[i].astype(BF16), ffn_w_up[i].astype(BF16),
                      ffn_w_down[i].astype(BF16), fnw, nct, last)
            continue
        rt = jnp.pad(moe_router[i], ((0, 0), (0, LANES - N_EXPERTS)))
        experts = (moe_w_gate[i].astype(BF16), moe_w_up[i].astype(BF16), moe_w_down[i].astype(BF16))
        if last:
            xs = _moe(xs, 1, modsel, nw2, rt, *experts, fnw, True)
        else:
            xs = jnp.concatenate([_moe(xs[:, :n_ctx], 0, modsel, nw2, rt, *experts, fnw, False),
                                  _moe(xs[:, n_ctx:], 1, modsel, nw2, rt, *experts, fnw, False)], axis=1)
    return xs
```

```python
import functools

import numpy as np
import jax
import jax.numpy as jnp
from jax import lax
from jax.experimental import pallas as pl
from jax.experimental.pallas import tpu as pltpu

D_MODEL = 1024
GRID_W = 64
EPS = 1e-6
ROPE_BASE = 10000.0

MLA_HEADS = 8
MLA_NOPE = 64
MLA_ROPE = 32
MLA_V = 64
MLA_Q_RANK = 256
MLA_KV_RANK = 128

RET_HEADS = 4
RET_DK = 64
RET_DV = 128

GLA_HEADS = 4
GLA_DK = 64
GLA_DV = 128
GLA_GATE_RANK = 16
GLA_TAU = 16.0

N_BRANCH = 3
BRANCH_W = 512
D_FF = 2816
N_EXPERTS = 8

IN_SPLITS = (
    MLA_Q_RANK, MLA_KV_RANK, MLA_ROPE,
    RET_HEADS * RET_DK, RET_HEADS * RET_DK, RET_HEADS * RET_DV, RET_HEADS * RET_DV,
    GLA_HEADS * GLA_DK, GLA_HEADS * GLA_DK, GLA_HEADS * GLA_DV, GLA_HEADS * GLA_DV,
    2 * GLA_GATE_RANK,
    N_BRANCH * D_MODEL,
)

F32 = jnp.float32
BF16 = jnp.bfloat16
HI = lax.Precision.HIGHEST

LANES = 128
HEAD_SLOT = 128
TOKEN_TILE = 256
ATT_DEPTH = 3
ATT_UNROLL = 10
ATT_SUM_ROWS = 16
SCAN_CHUNK = 256
GLA_BASE = 32
MOE_TILE = 1024
MOE_BLOCK = 256
MOE_TAIL_BLOCKS = (128, 192)
MOD_COLS_TILE = 3072
VMEM_LIMIT = 56 * 1024 * 1024
VMEM_LIMIT_EXPERTS = 60 * 1024 * 1024


def _silu(x):
    return x / (1.0 + jnp.exp(-x))


def _sigmoid(x):
    return 1.0 / (1.0 + jnp.exp(-x))


def _log_sigmoid(z):
    return jnp.minimum(z, 0.0) - jnp.log(1.0 + jnp.exp(-jnp.abs(z)))


def _rms(x, w):
    return x * lax.rsqrt(jnp.mean(x * x, axis=-1, keepdims=True) + EPS) * w


def _dot(a, b, precision=None):
    return jnp.dot(a, b, preferred_element_type=F32, precision=precision)


def _dot_nt(a, b):
    return lax.dot_general(a, b, (((1,), (1,)), ((), ())), preferred_element_type=F32)


def _bf16_pieces(x, n):
    out = []
    for _ in range(n):
        piece = x.astype(BF16)
        out.append(piece)
        x = x - piece.astype(F32)
    return out


def _dot_split(a, b):
    (a0, a1), (b0, b1) = _bf16_pieces(a, 2), _bf16_pieces(b, 2)
    return _dot(a0, b0) + (_dot(a0, b1) + _dot(a1, b0))


def _params(sem, vmem_limit=None):
    return pltpu.CompilerParams(dimension_semantics=sem, vmem_limit_bytes=vmem_limit or VMEM_LIMIT)


def _const_spec(shape):
    nd = len(shape)
    return pl.BlockSpec(shape, lambda *_: (0,) * nd, pipeline_mode=pl.Buffered(1))


def _mod_rows(mod_ref, is_ctx):
    m = mod_ref[0]
    return jnp.where(is_ctx, m[0:1], m[1:2])


def _mod_kernel(c_ref, w_ref, b_ref, o_ref):
    o_ref[0] = _dot(_silu(c_ref[...]), w_ref[0], HI) + b_ref[0]


def _modulation(cvec, mod_w, mod_b):
    depth, d, n6 = mod_w.shape
    rows = cvec.shape[0]
    tn = MOD_COLS_TILE
    return pl.pallas_call(
        _mod_kernel,
        grid=(depth, n6 // tn),
        in_specs=[pl.BlockSpec((rows, d), lambda l, j: (0, 0)),
                  pl.BlockSpec((1, d, tn), lambda l, j: (l, 0, j)),
                  pl.BlockSpec((1, 1, tn), lambda l, j: (l, 0, j))],
        out_specs=pl.BlockSpec((1, rows, tn), lambda l, j: (l, 0, j)),
        out_shape=jax.ShapeDtypeStruct((depth, rows, n6), F32),
        compiler_params=_params(("arbitrary", "arbitrary")),
    )(cvec, mod_w, mod_b.reshape(depth, 1, n6))


_C_CQ = (0, 256)
_C_CKV_KRA = (256, 512)
_C_KRB_GZ = (512, 768)
_C_RQ = (768, 1024)
_C_RK = (1024, 1280)
_C_RV = (1280, 1792)
_C_RG = (1792, 2304)
_C_GQ = (2304, 2560)
_C_GK = (2560, 2816)
_C_GV = (2816, 3328)
_C_GR = (3328, 3840)


def _seq_inputs(xs, nct, off=0):
    tm = TOKEN_TILE
    if not isinstance(xs, tuple):
        return [pl.BlockSpec((1, tm, xs.shape[-1]), lambda i, t: (i, t + off, 0))], [xs]
    d = xs[0].shape[-1]
    return ([pl.BlockSpec((1, tm, d), lambda i, t: (i, jnp.minimum(t + off, nct - 1), 0)),
             pl.BlockSpec((1, tm, d), lambda i, t: (i, jnp.maximum(t + off - nct, 0), 0))], list(xs))


def _seq_tile(x_refs, is_ctx):
    if len(x_refs) == 1:
        return x_refs[0][0]
    return jnp.where(is_ctx, x_refs[0][0], x_refs[1][0])


def _inproj_kernel(nct, nx, *refs):
    x_refs = refs[:nx]
    (mod_ref, nw_ref, w_ref, qn_ref, wq_ref, wqs_ref, kvn_ref, wk_ref, wv_ref, wg2_ref, bg2_ref,
     cq_ref, sq_ref, cr_ref, sr_ref, q_out, k_out, v_out, rq_out, rk_out, rv_out, rg_out,
     gq_out, gk_out, gv_out, gr_out, la_out) = refs[nx:]
    d = D_MODEL
    is_ctx = pl.program_id(1) < nct
    row = _mod_rows(mod_ref, is_ctx)
    sh, sc = row[:, 0:d], row[:, d:2 * d]
    u = (_rms(_seq_tile(x_refs, is_ctx), nw_ref[...]) * (1.0 + sc) + sh).astype(BF16)

    def seg(c):
        return _dot(u, w_ref[:, c[0]:c[1]])

    cqn = _rms(seg(_C_CQ), qn_ref[...]).astype(BF16)
    q_all = _dot(cqn, wq_ref[...])
    q_swp = _dot(cqn, wqs_ref[...])
    ckv_kra = seg(_C_CKV_KRA)
    krb_gz = seg(_C_KRB_GZ)
    ckvn = _rms(ckv_kra[:, 0:LANES], kvn_ref[...]).astype(BF16)
    k_all = _dot(ckvn, wk_ref[...])
    v_out[0, 0] = _dot(ckvn, wv_ref[...]).T.astype(BF16)
    cosq, sinq = cq_ref[...], sq_ref[...]
    k_rope = ckv_kra[:, LANES:] * cosq + krb_gz[:, 0:LANES] * sinq
    for h in range(MLA_HEADS):
        sl = slice(h * HEAD_SLOT, (h + 1) * HEAD_SLOT)
        q_out[0, h, 0] = (q_all[:, sl] * cosq + q_swp[:, sl] * sinq).T.astype(BF16)
        k_out[0, h] = (k_all[:, sl] + k_rope).astype(BF16)

    cosr, sinr = cr_ref[...], sr_ref[...]
    hw = RET_HEADS * RET_DK
    first = (lax.broadcasted_iota(jnp.int32, (TOKEN_TILE, hw), 1) % RET_DK) < RET_DK // 2
    swap = lambda a: jnp.where(first, pltpu.roll(a, hw - RET_DK // 2, 1), pltpu.roll(a, RET_DK // 2, 1))
    rq, rk = seg(_C_RQ), seg(_C_RK)
    rq_out[0] = (rq * cosr + swap(rq) * sinr).astype(BF16)
    rk_out[0] = (rk * cosr + swap(rk) * sinr).astype(BF16)
    rv_out[0] = seg(_C_RV).astype(BF16)
    rg_out[0] = seg(_C_RG).astype(BF16)

    gq_out[0] = seg(_C_GQ).astype(BF16)
    gk_out[0] = seg(_C_GK).astype(BF16)
    gv_out[0] = seg(_C_GV).astype(BF16)
    gr_out[0] = seg(_C_GR).astype(BF16)
    z = _dot_split(krb_gz[:, LANES:], wg2_ref[...]) + bg2_ref[...]
    la_out[0] = _log_sigmoid(z) * (1.0 / GLA_TAU)


def _inproj(xs, modsel, nw, w1, qn, wq, wqs, kvn, wk, wv, wg2, bg2, cosq, sinq, cosr, sinr, nct):
    x_specs, x_ops = _seq_inputs(xs, nct)
    b, d = x_ops[0].shape[0], x_ops[0].shape[-1]
    n = sum(a.shape[1] for a in x_ops)
    tm = TOKEN_TILE
    nt = n // tm
    h = MLA_HEADS
    tok = lambda w: pl.BlockSpec((1, tm, w), lambda i, t: (i, t, 0))
    tab = lambda w: pl.BlockSpec((tm, w), lambda i, t: (t, 0))
    q_spec = pl.BlockSpec((1, h, 1, HEAD_SLOT, tm), lambda i, t: (i, 0, t, 0, 0))
    k_spec = pl.BlockSpec((1, h, tm, HEAD_SLOT), lambda i, t: (i, 0, t, 0))
    v_spec = pl.BlockSpec((1, 1, h * MLA_V, tm), lambda i, t: (i, t, 0, 0))
    sds = lambda w, dt=BF16: jax.ShapeDtypeStruct((b, n, w), dt)
    q_sds = jax.ShapeDtypeStruct((b, h, nt, HEAD_SLOT, tm), BF16)
    k_sds = jax.ShapeDtypeStruct((b, h, n, HEAD_SLOT), BF16)
    v_sds = jax.ShapeDtypeStruct((b, nt, h * MLA_V, tm), BF16)
    return pl.pallas_call(
        functools.partial(_inproj_kernel, nct, len(x_ops)),
        grid=(b, nt),
        in_specs=x_specs + [pl.BlockSpec((1, 2, 6 * d), lambda i, t: (i, 0, 0)), _const_spec(nw.shape),
                  _const_spec(w1.shape), _const_spec(qn.shape), _const_spec(wq.shape),
                  _const_spec(wqs.shape), _const_spec(kvn.shape), _const_spec(wk.shape),
                  _const_spec(wv.shape), _const_spec(wg2.shape), _const_spec(bg2.shape),
                  tab(HEAD_SLOT), tab(HEAD_SLOT), tab(256), tab(256)],
        out_specs=[q_spec, k_spec, v_spec, tok(256), tok(256), tok(512), tok(512),
                   tok(256), tok(256), tok(512), tok(512), tok(512)],
        out_shape=[q_sds, k_sds, v_sds, sds(256), sds(256), sds(512), sds(512),
                   sds(256), sds(256), sds(512), sds(512), sds(512, F32)],
        compiler_params=_params(("arbitrary", "arbitrary")),
    )(*x_ops, modsel, nw, w1, qn, wq, wqs, kvn, wk, wv, wg2, bg2, cosq, sinq, cosr, sinr)


def _attn_kernel(nct, n_tiles, qt_ref, k_ref, vt_ref, o_ref, *bufs):
    t = TOKEN_TILE
    nd = ATT_DEPTH
    heads = range(2)
    s_bufs = (bufs[0:nd], bufs[nd:2 * nd])
    p_bufs = (bufs[2 * nd:3 * nd], bufs[3 * nd:4 * nd])
    qts = [qt_ref[0, j, 0] for j in heads]
    ones = jnp.ones((ATT_SUM_ROWS, t), BF16)

    def scores(j, c, s_buf):
        s = _dot(k_ref[0, j, pl.ds(pl.multiple_of(c * t, t), t), :], qts[j])
        s_buf[...] = s
        return jnp.max(s, axis=0, keepdims=True)

    def softmax(s_buf, p_buf, m, cmax):
        m_new = jnp.maximum(m, cmax)
        p_buf[...] = jnp.exp2(s_buf[...] - m_new).astype(BF16)
        return m_new, jnp.exp2(m - m_new)

    def values(j, c, p_buf, alpha, acc):
        lhs = jnp.concatenate([vt_ref[0, c, j * MLA_V:(j + 1) * MLA_V, :], ones], axis=0)
        return alpha * acc + _dot(lhs, p_buf[...])

    m_init = jnp.full((1, t), -jnp.inf, F32)
    acc_init = jnp.zeros((MLA_V + ATT_SUM_ROWS, t), F32)

    def finish(accs):
        outs = [acc[0:MLA_V] / acc[MLA_V:MLA_V + 1] for acc in accs]
        o_ref[0] = jnp.concatenate(outs, axis=0).T.astype(BF16)

    @pl.when(pl.program_id(2) < nct)
    def _():
        accs = []
        for j in heads:
            m, acc = m_init, acc_init
            for c in range(nct):
                cmax = scores(j, c, s_bufs[j][0])
                m, alpha = softmax(s_bufs[j][0], p_bufs[j][0], m, cmax)
                acc = values(j, c, p_bufs[j][0], alpha, acc)
            accs.append(acc)
        finish(accs)

    def step(c, slot, state, lookahead=True):
        ahead = [scores(j, c + 2, s_bufs[j][(slot + 2) % nd]) if lookahead else None for j in heads]
        accs = [values(j, c - 1, p_bufs[j][(slot - 1) % nd], state[j][1], state[j][2]) for j in heads]
        stats = [softmax(s_bufs[j][slot], p_bufs[j][slot], state[j][0], state[j][3]) for j in heads]
        return tuple((stats[j][0], stats[j][1], accs[j], state[j][4], ahead[j]) for j in heads)

    @pl.when(pl.program_id(2) >= nct)
    def _():
        state = []
        for j in heads:
            cmax0 = scores(j, 0, s_bufs[j][0])
            cmax1 = scores(j, 1, s_bufs[j][1])
            m, alpha = softmax(s_bufs[j][0], p_bufs[j][0], m_init, cmax0)
            state.append((m, alpha, acc_init, cmax1, scores(j, 2, s_bufs[j][2])))

        def trip(i, state):
            for k in range(nd):
                state = step(1 + nd * i + k, (1 + k) % nd, state)
            return state

        state = lax.fori_loop(0, (n_tiles - 3) // nd, trip, tuple(state), unroll=ATT_UNROLL)
        for c in (n_tiles - 2, n_tiles - 1):
            state = step(c, c % nd, state, lookahead=False)
        finish([values(j, n_tiles - 1, p_bufs[j][(n_tiles - 1) % nd], state[j][1], state[j][2])
                for j in heads])


def _attention(qt, k, vt, nct):
    b, h, n_tiles, w, t = qt.shape
    n = n_tiles * t
    assert n_tiles >= 3 and (n_tiles - 3) % ATT_DEPTH == 0
    return pl.pallas_call(
        functools.partial(_attn_kernel, nct, n_tiles),
        grid=(b, h // 2, n_tiles),
        in_specs=[pl.BlockSpec((1, 2, 1, w, t), lambda i, p, q: (i, p, q, 0, 0)),
                  pl.BlockSpec((1, 2, n, w), lambda i, p, q: (i, p, 0, 0)),
                  pl.BlockSpec((1, n_tiles, 2 * MLA_V, t), lambda i, p, q: (i, 0, p, 0))],
        out_specs=pl.BlockSpec((1, t, 2 * MLA_V), lambda i, p, q: (i, q, p)),
        out_shape=jax.ShapeDtypeStruct((b, n, h * MLA_V), BF16),
        scratch_shapes=[pltpu.VMEM((t, t), F32)] * (2 * ATT_DEPTH) + [pltpu.VMEM((t, t), BF16)] * (2 * ATT_DEPTH),
        compiler_params=_params(("arbitrary", "arbitrary", "arbitrary")),
    )(qt, k, vt)


def _bwd_chunk(s, nct, nch):
    return jnp.where(s < nct, nct - 1 - s, nch - 1 - (s - nct))


def _ret_body(dl_ref, qf, kf, vf, qb, kb, vb, of_ref, ob_ref, st_ref):
    c = SCAN_CHUNK
    hw = RET_HEADS * RET_DK
    ii = lax.broadcasted_iota(jnp.int32, (c, c), 0)
    jj = lax.broadcasted_iota(jnp.int32, (c, c), 1)
    pos_i = lax.broadcasted_iota(jnp.int32, (c, 1), 0).astype(F32)
    pos_j = lax.broadcasted_iota(jnp.int32, (1, c), 1).astype(F32)
    lane_head = lax.broadcasted_iota(jnp.int32, (c, hw), 1) // RET_DK
    for d, (q_ref, k_ref, v_ref, o_ref) in enumerate(((qf, kf, vf, of_ref), (qb, kb, vb, ob_ref))):
        rev = d == 1
        q, v = q_ref[0], v_ref[0]
        k_t = k_ref[0].astype(F32).T
        k_tb = k_t.astype(BF16)
        st = st_ref[d]
        st_b = st.astype(BF16)
        dist = ((jj - ii) if rev else (ii - jj)).astype(F32)
        for h in range(RET_HEADS):
            lg = _log_sigmoid(dl_ref[d, h])[0:1, 0:1]
            intra = jnp.where(dist >= 0.0, jnp.exp(lg * jnp.maximum(dist, 0.0)), 0.0)
            q_dec = jnp.exp(lg * ((c - pos_i) if rev else (pos_i + 1.0)))
            k_dec = jnp.exp(lg * (pos_j if rev else (c - 1.0 - pos_j)))
            rows = slice(h * RET_DK, (h + 1) * RET_DK)
            cols = slice(h * RET_DV, (h + 1) * RET_DV)
            qm = jnp.where(lane_head == h, q, jnp.zeros_like(q))
            scores = _dot(qm, k_tb) * intra
            o = _dot(scores.astype(BF16), v[:, cols]) + _dot(qm, st_b) * q_dec
            o_ref[0, :, cols] = o
            kd = (k_t[rows] * k_dec).astype(BF16)
            st_ref[d, rows, :] = st[rows] * jnp.exp(lg * float(c)) + _dot(kd, v[:, cols])


def _block_start(i, size):
    return i & ~(size - 1)


def _gla_levels():
    half = SCAN_CHUNK // 2
    out = []
    while half >= GLA_BASE:
        out.append(half)
        half //= 2
    return out


def _gla_body(qf, kf, vf, laf, qb, kb, vb, lab, of_ref, ob_ref, st_ref):
    c = SCAN_CHUNK
    hw = GLA_HEADS * GLA_DK
    levels = _gla_levels()
    ii = lax.broadcasted_iota(jnp.int32, (c, c), 0)
    jj = lax.broadcasted_iota(jnp.int32, (c, c), 1)
    pos = lax.broadcasted_iota(jnp.int32, (c, 1), 0)
    lane_head = lax.broadcasted_iota(jnp.int32, (c, hw), 1) // GLA_DK
    for d, (q_ref, k_ref, v_ref, la_ref, o_ref) in enumerate(((qf, kf, vf, laf, of_ref),
                                                               (qb, kb, vb, lab, ob_ref))):
        rev = d == 1
        q, k, v, la = q_ref[0].astype(F32), k_ref[0].astype(F32), v_ref[0], la_ref[0]

        prefix = _one_hot((jj >= ii) if rev else (jj <= ii))
        b = sum(_dot(prefix, piece) for piece in _bf16_pieces(la, 3))
        b_end = b[0:1] if rev else b[c - 1:c]

        def at_row(size, idx):
            b3 = b.reshape(c // size, size, hw)
            return jnp.broadcast_to(b3[:, idx:idx + 1, :], b3.shape).reshape(c, hw)

        q_t = (q * jnp.exp(b)).astype(BF16)
        k_tt = (k * jnp.exp(b_end - b)).T
        dec_col = jnp.exp(jnp.broadcast_to(b_end, (LANES, hw)).T[:, 0:1])

        terms = []
        for half in levels:
            e = at_row(2 * half, half if rev else half - 1)
            inner = pos & (2 * half - 1)
            late = (inner < half) if rev else (inner >= half)
            qh = jnp.where(late, q * jnp.exp(jnp.where(late, b - e, 0.0)), 0.0)
            kh = jnp.where(late, 0.0, k * jnp.exp(jnp.where(late, 0.0, e - b)))
            terms.append((qh.astype(BF16), kh.astype(BF16), _block_start(ii, 2 * half) == _block_start(jj, 2 * half)))
        r = at_row(GLA_BASE, GLA_BASE - 1 if rev else 0)
        tri = (jj >= ii) if rev else (jj <= ii)
        terms.append(((q * jnp.exp(b - r)).astype(BF16), (k * jnp.exp(r - b)).astype(BF16),
                      (_block_start(ii, GLA_BASE) == _block_start(jj, GLA_BASE)) & tri))

        st = st_ref[d]
        st_b = st.astype(BF16)
        for h in range(GLA_HEADS):
            rows = slice(h * GLA_DK, (h + 1) * GLA_DK)
            cols = slice(h * GLA_DV, (h + 1) * GLA_DV)
            sel = lane_head == h
            a = jnp.zeros((c, c), F32)
            for qh, kh, mask in terms:
                a = a + jnp.where(mask, _dot_nt(jnp.where(sel, qh, jnp.zeros_like(qh)), kh), 0.0)
            vh = v[:, cols]
            o_ref[0, :, cols] = _dot(a.astype(BF16), vh) + _dot(jnp.where(sel, q_t, jnp.zeros_like(q_t)), st_b)
            st_ref[d, rows, :] = st[rows] * dec_col[rows] + _dot(k_tt[rows].astype(BF16), vh)


def _scan_kernel(*refs):
    ret_refs, gla_refs = refs[0:7] + refs[15:17] + refs[19:20], refs[7:15] + refs[17:19] + refs[20:21]

    @pl.when(pl.program_id(1) == 0)
    def _():
        ret_refs[-1][...] = jnp.zeros_like(ret_refs[-1])
        gla_refs[-1][...] = jnp.zeros_like(gla_refs[-1])

    _ret_body(*ret_refs)
    _gla_body(*gla_refs)


def _scans(dl, rq, rk, rv, gq, gk, gv, la, nct_tokens):
    b, n, hw = rq.shape
    c = SCAN_CHUNK
    nch = n // c
    nct = nct_tokens // c
    vw = rv.shape[-1]
    fwd = lambda w, col=0: pl.BlockSpec((1, c, w), lambda i, s: (i, s, col))
    bwd = lambda w, col=0: pl.BlockSpec((1, c, w), lambda i, s: (i, _bwd_chunk(s, nct, nch), col))
    return pl.pallas_call(
        _scan_kernel,
        grid=(b, nch),
        in_specs=[_const_spec(dl.shape), fwd(hw), fwd(hw), fwd(vw), bwd(hw), bwd(hw), bwd(vw),
                  fwd(hw), fwd(hw), fwd(vw), fwd(hw, 0), bwd(hw), bwd(hw), bwd(vw), bwd(hw, 1)],
        out_specs=[fwd(vw), bwd(vw), fwd(vw), bwd(vw)],
        out_shape=[jax.ShapeDtypeStruct((b, n, vw), F32)] * 4,
        scratch_shapes=[pltpu.VMEM((2, hw, RET_DV), F32), pltpu.VMEM((2, hw, GLA_DV), F32)],
        compiler_params=_params(("arbitrary", "arbitrary")),
    )(dl, rq, rk, rv, rq, rk, rv, gq, gk, gv, la, gq, gk, gv, la)


def _head_norm(o, w, center):
    parts = []
    for h in range(o.shape[-1] // LANES):
        oh = o[:, h * LANES:(h + 1) * LANES]
        if center:
            oh = oh - jnp.mean(oh, axis=-1, keepdims=True)
        parts.append(oh * lax.rsqrt(jnp.mean(oh * oh, axis=-1, keepdims=True) + EPS))
    return jnp.concatenate(parts, axis=-1) * w


def _merge_kernel(nct, off, nx, *refs):
    x_refs = refs[:nx]
    (mod_ref, nw_ref, oa_ref, rof_ref, rob_ref, rg_ref, gof_ref, gob_ref, gr_ref, wbg_ref, wbr_ref,
     wout_ref, rnw_ref, gnw_ref, o_ref) = refs[nx:]
    d = D_MODEL
    is_ctx = pl.program_id(1) + off < nct
    row = _mod_rows(mod_ref, is_ctx)
    sh, sc, g1 = row[:, 0:d], row[:, d:2 * d], row[:, 2 * d:3 * d]
    x = _seq_tile(x_refs, is_ctx)
    u = (_rms(x, nw_ref[...]) * (1.0 + sc) + sh).astype(BF16)
    gate = _sigmoid(_dot(u, wbg_ref[...]))
    yb = _silu(rg_ref[0].astype(F32)) * _head_norm(rof_ref[0] + rob_ref[0], rnw_ref[...], True)
    yc = _silu(gr_ref[0].astype(F32)) * _head_norm(gof_ref[0] + gob_ref[0], gnw_ref[...], False)
    z = (gate[:, 0:d] * _dot(oa_ref[0], wbr_ref[0])
         + gate[:, d:2 * d] * _dot(yb.astype(BF16), wbr_ref[1])
         + gate[:, 2 * d:3 * d] * _dot(yc.astype(BF16), wbr_ref[2]))
    o_ref[0] = x + g1 * _dot(z.astype(BF16), wout_ref[...])


def _merge(xs, modsel, nw, oa, rof, rob, rg, gof, gob, gr, wbg, wbr, wout, rnw, gnw, nct, latent_only):
    tm = TOKEN_TILE
    off = nct if latent_only else 0
    x_specs, x_ops = _seq_inputs(xs, nct, off)
    b, d = x_ops[0].shape[0], x_ops[0].shape[-1]
    nt = sum(a.shape[1] for a in x_ops) // tm - off
    tok = lambda w: pl.BlockSpec((1, tm, w), lambda i, t: (i, t + off, 0))
    return pl.pallas_call(
        functools.partial(_merge_kernel, nct, off, len(x_ops)),
        grid=(b, nt),
        in_specs=x_specs + [pl.BlockSpec((1, 2, 6 * d), lambda i, t: (i, 0, 0)), _const_spec(nw.shape),
                  tok(512), tok(512), tok(512), tok(512), tok(512), tok(512), tok(512),
                  _const_spec(wbg.shape), _const_spec(wbr.shape), _const_spec(wout.shape),
                  _const_spec(rnw.shape), _const_spec(gnw.shape)],
        out_specs=pl.BlockSpec((1, tm, d), lambda i, t: (i, t, 0)),
        out_shape=jax.ShapeDtypeStruct((b, nt * tm, d), F32),
        compiler_params=_params(("arbitrary", "arbitrary")),
    )(*x_ops, modsel, nw, oa, rof, rob, rg, gof, gob, gr, wbg, wbr, wout, rnw, gnw)


def _norm2(x, row, nw_ref):
    d = D_MODEL
    sh, sc = row[:, 3 * d:4 * d], row[:, 4 * d:5 * d]
    return _rms(x, nw_ref[...]) * (1.0 + sc) + sh


def _finish(x, y, row, final, fnw_ref):
    d = D_MODEL
    out = x + row[:, 5 * d:6 * d] * y
    if final:
        out = _rms(out, fnw_ref[...])
    return out


def _ffn_kernel(nct, off, final, x_ref, mod_ref, nw_ref, wg_ref, wu_ref, wd_ref, fnw_ref, o_ref):
    row = _mod_rows(mod_ref, pl.program_id(1) + off < nct)
    x = x_ref[0]
    v = _norm2(x, row, nw_ref).astype(BF16)
    hdn = _silu(_dot(v, wg_ref[...])) * _dot(v, wu_ref[...])
    o_ref[0] = _finish(x, _dot(hdn.astype(BF16), wd_ref[...]), row, final, fnw_ref)


def _ffn(xs, modsel, nw, wg, wu, wd, fnw, nct, final):
    b, n, d = xs.shape
    tm = TOKEN_TILE
    off = nct if final else 0
    nt = n // tm - off
    return pl.pallas_call(
        functools.partial(_ffn_kernel, nct, off, final),
        grid=(b, nt),
        in_specs=[pl.BlockSpec((1, tm, d), lambda i, t: (i, t + off, 0)),
                  pl.BlockSpec((1, 2, 6 * d), lambda i, t: (i, 0, 0)), _const_spec(nw.shape),
                  _const_spec(wg.shape), _const_spec(wu.shape), _const_spec(wd.shape),
                  _const_spec(fnw.shape)],
        out_specs=pl.BlockSpec((1, tm, d), lambda i, t: (i, t, 0)),
        out_shape=jax.ShapeDtypeStruct((b, nt * tm, d), F32),
        compiler_params=_params(("arbitrary", "arbitrary")),
    )(xs, modsel, nw, wg, wu, wd, fnw)


def _top2_gates(logits):
    lane = lax.broadcasted_iota(jnp.int32, logits.shape, 1)
    m1 = jnp.max(logits, axis=-1, keepdims=True)
    i1 = jnp.min(jnp.where(logits == m1, lane, LANES), axis=-1, keepdims=True)
    rest = jnp.where(lane == i1, -jnp.inf, logits)
    m2 = jnp.max(rest, axis=-1, keepdims=True)
    i2 = jnp.min(jnp.where(rest == m2, lane, LANES), axis=-1, keepdims=True)
    e2 = jnp.exp(m2 - m1)
    w1 = 1.0 / (1.0 + e2)
    gates = jnp.where(lane == i1, w1, jnp.where(lane == i2, e2 * w1, 0.0))
    return gates, (lane == i1) | (lane == i2)


def _one_hot(cond):
    return jnp.where(cond, 1.0, 0.0).astype(BF16)


def _route_kernel(mrow, x_ref, mod_ref, nw_ref, rt_ref, v_out, gate_out, rank_out, rankt_out):
    v = _norm2(x_ref[0], mod_ref[0][mrow:mrow + 1], nw_ref)
    v_out[0] = v.astype(BF16)
    tt = v.shape[0]
    lane = lax.broadcasted_iota(jnp.int32, (tt, LANES), 1)
    logits = jnp.where(lane < N_EXPERTS, _dot_split(v, rt_ref[...]), -jnp.inf)
    gates, routed = _top2_gates(logits)
    gate_out[0] = gates
    ti = lax.broadcasted_iota(jnp.int32, (tt, tt), 0)
    tj = lax.broadcasted_iota(jnp.int32, (tt, tt), 1)
    rank_out[0] = jnp.where(routed, _dot(_one_hot(tj < ti), _one_hot(routed)), -1.0)
    routed_t = jnp.where(routed, 1.0, 0.0).T
    rank_t = jnp.where(routed_t > 0.5, _dot(routed_t.astype(BF16), _one_hot(ti < tj)), -1.0)
    rankt_out[0, 0] = rank_t[0:N_EXPERTS]


def _experts_kernel(v_ref, gate_ref, rank_ref, rankt_ref, wg_ref, wu_ref, wd_ref, o_ref, acc_ref):
    e, s = pl.program_id(1), pl.program_id(2)
    tt = v_ref.shape[1]
    fh = D_FF // 2
    tile0 = pl.multiple_of(s * tt, tt)
    lane = lax.broadcasted_iota(jnp.int32, (tt, LANES), 1)
    g_col = jnp.sum(jnp.where(lane == e, gate_ref[0], 0.0), axis=-1, keepdims=True)
    rk_col = jnp.sum(jnp.where(lane == e, rank_ref[0], 0.0), axis=-1, keepdims=True)
    rk_row = rankt_ref[0, 0, pl.ds(e, 1), :]
    n_routed = jnp.max(rk_col).astype(jnp.int32) + 1

    @pl.when(e == 0)
    def _():
        acc_ref[pl.ds(tile0, tt), :] = jnp.zeros((tt, D_MODEL), BF16)

    def block(first_slot, r):
        base = first_slot.astype(F32)
        slot_i = base + lax.broadcasted_iota(jnp.int32, (r, 1), 0).astype(F32)
        xg = _dot(_one_hot(rk_row == slot_i), v_ref[0]).astype(BF16)
        y = jnp.zeros((r, D_MODEL), F32)
        for c in range(2):
            cols = slice(c * fh, (c + 1) * fh)
            hdn = _silu(_dot(xg, wg_ref[0, :, cols])) * _dot(xg, wu_ref[0, :, cols])
            y = y + _dot(hdn.astype(BF16), wd_ref[0, cols, :])
        y = y.astype(BF16)
        slot_j = base + lax.broadcasted_iota(jnp.int32, (1, r), 1).astype(F32)
        for c in range(tt // TOKEN_TILE):
            rows = slice(c * TOKEN_TILE, (c + 1) * TOKEN_TILE)
            dst = pl.ds(tile0 + c * TOKEN_TILE, TOKEN_TILE)
            part = g_col[rows] * _dot(_one_hot(rk_col[rows] == slot_j), y)
            acc_ref[dst, :] = (acc_ref[dst, :].astype(F32) + part).astype(BF16)

    r = MOE_BLOCK
    n_full = n_routed // r
    left = n_routed - n_full * r

    def full_block(i, carry):
        block(i * r, r)
        return carry

    lax.fori_loop(0, n_full + (left > MOE_TAIL_BLOCKS[-1]).astype(jnp.int32), full_block, 0)
    for below, size in zip((0,) + MOE_TAIL_BLOCKS, MOE_TAIL_BLOCKS):
        @pl.when((left > below) & (left <= size))
        def _(size=size):
            block(n_full * r, size)

    @pl.when(e == N_EXPERTS - 1)
    def _():
        o_ref[0] = acc_ref[pl.ds(tile0, tt), :]


def _moe_finish_kernel(mrow, final, x_ref, y_ref, mod_ref, fnw_ref, o_ref):
    o_ref[0] = _finish(x_ref[0], y_ref[0].astype(F32), mod_ref[0][mrow:mrow + 1], final, fnw_ref)


def _moe(slab, mrow, modsel, nw, rt, wg, wu, wd, fnw, final):
    b, m, d = slab.shape
    tt = min(MOE_TILE, m)
    assert m % tt == 0 and tt % TOKEN_TILE == 0
    ns = m // tt
    ne, _, f = wg.shape
    mod_spec = lambda: pl.BlockSpec((1, 2, 6 * d), lambda i, s: (i, 0, 0))
    v, gates, rank, rank_t = pl.pallas_call(
        functools.partial(_route_kernel, mrow),
        grid=(b, ns),
        in_specs=[pl.BlockSpec((1, tt, d), lambda i, s: (i, s, 0)), mod_spec(), _const_spec(nw.shape),
                  _const_spec(rt.shape)],
        out_specs=[pl.BlockSpec((1, tt, d), lambda i, s: (i, s, 0)),
                   pl.BlockSpec((1, tt, LANES), lambda i, s: (i, s, 0)),
                   pl.BlockSpec((1, tt, LANES), lambda i, s: (i, s, 0)),
                   pl.BlockSpec((1, 1, ne, tt), lambda i, s: (i, s, 0, 0))],
        out_shape=[jax.ShapeDtypeStruct((b, m, d), BF16), jax.ShapeDtypeStruct((b, m, LANES), F32),
                   jax.ShapeDtypeStruct((b, m, LANES), F32), jax.ShapeDtypeStruct((b, ns, ne, tt), F32)],
        compiler_params=_params(("arbitrary", "arbitrary")),
    )(slab, modsel, nw, rt)

    tok = lambda w: pl.BlockSpec((1, tt, w), lambda i, e, s: (i, s, 0))
    wspec = lambda shape, bufs=1: pl.BlockSpec((1,) + shape, lambda i, e, s: (e, 0, 0),
                                               pipeline_mode=pl.Buffered(bufs))
    last_pass = lambda i, e, s: (i, jnp.where(e == ne - 1, s, 0), 0)
    y_sum = pl.pallas_call(
        _experts_kernel,
        grid=(b, ne, ns),
        in_specs=[tok(d), tok(LANES), tok(LANES),
                  pl.BlockSpec((1, 1, ne, tt), lambda i, e, s: (i, s, 0, 0)),
                  wspec((d, f), 2), wspec((d, f), 2), wspec((f, d))],
        out_specs=pl.BlockSpec((1, tt, d), last_pass),
        out_shape=jax.ShapeDtypeStruct((b, m, d), BF16),
        scratch_shapes=[pltpu.VMEM((m, d), BF16)],
        compiler_params=_params(("arbitrary", "arbitrary", "arbitrary"), VMEM_LIMIT_EXPERTS),
    )(v, gates, rank, rank_t, wg, wu, wd)

    tm = tt
    row_tile = lambda: pl.BlockSpec((1, tm, d), lambda i, t: (i, t, 0))
    return pl.pallas_call(
        functools.partial(_moe_finish_kernel, mrow, final),
        grid=(b, m // tm),
        in_specs=[row_tile(), row_tile(), mod_spec(), _const_spec(fnw.shape)],
        out_specs=row_tile(),
        out_shape=jax.ShapeDtypeStruct((b, m, d), F32),
        compiler_params=_params(("arbitrary", "arbitrary")),
    )(slab, y_sum, modsel, fnw)


def _axial_perm():
    dd = np.arange(MLA_ROPE)
    return np.where(dd % 16 < 8, dd + 8, dd - 8)


def _w_in_pieces():
    o = dict(zip(("cq", "ckv", "kr", "rq", "rk", "rv", "rg", "gq", "gk", "gv", "gr", "gz", "bg"),
                 np.cumsum((0,) + IN_SPLITS).tolist()))
    def permuted(start, perm, scale=1.0):
        out = []
        for p in perm:
            if out and out[-1][0] + out[-1][1] == start + p:
                out[-1] = (out[-1][0], out[-1][1] + 1, scale)
            else:
                out.append((start + int(p), 1, scale))
        return out

    k_scale, q_scale = RET_DK ** -0.5, GLA_DK ** -0.5
    hw = RET_HEADS * RET_DK
    tail = HEAD_SLOT - MLA_NOPE - MLA_ROPE
    pieces = [(o["cq"], MLA_Q_RANK, 1.0), (o["ckv"], MLA_KV_RANK, 1.0),
              (None, MLA_NOPE, 0.0), (o["kr"], MLA_ROPE, 1.0), (None, tail, 0.0),
              (None, MLA_NOPE, 0.0)] + permuted(o["kr"], _axial_perm()) + [(None, tail, 0.0),
              (o["gz"], 2 * GLA_GATE_RANK, 1.0), (None, LANES - 2 * GLA_GATE_RANK, 0.0),
              (o["rq"], hw, 1.0), (o["rk"], hw, k_scale),
              (o["rv"], RET_HEADS * RET_DV, 1.0), (o["rg"], RET_HEADS * RET_DV, 1.0),
              (o["gq"], GLA_HEADS * GLA_DK, q_scale), (o["gk"], GLA_HEADS * GLA_DK, 1.0),
              (o["gv"], GLA_HEADS * GLA_DV, 1.0), (o["gr"], GLA_HEADS * GLA_DV, 1.0)]
    return pieces, o["bg"]


def _relayout_kernel(w_ref, w1_ref, bg_ref):
    pieces, bg0 = _w_in_pieces()
    rows = w_ref.shape[1]
    cols = []
    for start, width, scale in pieces:
        if start is None:
            cols.append(jnp.zeros((rows, width), BF16))
        else:
            piece = w_ref[0, :, start:start + width]
            cols.append(piece if scale == 1.0 else (piece.astype(F32) * scale).astype(BF16))
    w1_ref[...] = jnp.concatenate(cols, axis=1)
    bg_ref[...] = w_ref[0, :, bg0:bg0 + N_BRANCH * D_MODEL]


def _prep_w_in(w_in, l):
    _, d, n_in = w_in.shape
    rows = TOKEN_TILE
    n1 = sum(p[1] for p in _w_in_pieces()[0])
    nbg = N_BRANCH * D_MODEL
    return pl.pallas_call(
        _relayout_kernel,
        grid=(d // rows,),
        in_specs=[pl.BlockSpec((1, rows, n_in), lambda r: (l, r, 0))],
        out_specs=[pl.BlockSpec((rows, n1), lambda r: (r, 0)), pl.BlockSpec((rows, nbg), lambda r: (r, 0))],
        out_shape=[jax.ShapeDtypeStruct((d, n1), BF16), jax.ShapeDtypeStruct((d, nbg), BF16)],
        compiler_params=_params(("arbitrary",)),
    )(w_in)


def _prep_mla(w_uq, w_ukv):
    r = w_uq.shape[0]
    scale = (MLA_NOPE + MLA_ROPE) ** -0.5 * np.log2(np.e)
    wq3 = w_uq.reshape(r, MLA_HEADS, MLA_NOPE + MLA_ROPE) * scale
    pad = jnp.zeros((r, MLA_HEADS, HEAD_SLOT - MLA_NOPE - MLA_ROPE), w_uq.dtype)
    wq = jnp.concatenate([wq3, pad], axis=-1)
    wqs = jnp.concatenate([jnp.zeros((r, MLA_HEADS, MLA_NOPE), w_uq.dtype),
                           wq3[:, :, MLA_NOPE:][:, :, _axial_perm()], pad], axis=-1)
    rk = w_ukv.shape[0]
    wkv3 = w_ukv.reshape(rk, MLA_HEADS, MLA_NOPE + MLA_V)
    wk = jnp.concatenate([wkv3[:, :, :MLA_NOPE],
                          jnp.zeros((rk, MLA_HEADS, HEAD_SLOT - MLA_NOPE), w_ukv.dtype)], axis=-1)
    wv = wkv3[:, :, MLA_NOPE:]
    flat = lambda a: a.reshape(a.shape[0], -1).astype(BF16)
    return flat(wq), flat(wqs), flat(wk), flat(wv)


def _prep_gla_gate(w_gate, b_gate):
    hw = GLA_HEADS * GLA_DK
    wg2 = jnp.zeros((LANES, 2 * hw), F32)
    wg2 = wg2.at[0:GLA_GATE_RANK, 0:hw].set(w_gate[0])
    wg2 = wg2.at[GLA_GATE_RANK:2 * GLA_GATE_RANK, hw:2 * hw].set(w_gate[1])
    return wg2, b_gate.reshape(1, 2 * hw)


def _rope_tables(n_ctx, seq):
    n_rows = seq // GRID_W
    half = MLA_ROPE // 4
    inv = ROPE_BASE ** (-jnp.arange(half, dtype=F32) / half)
    ang_r = jnp.arange(n_rows, dtype=F32)[:, None] * inv[None, :]
    ang_c = jnp.arange(GRID_W, dtype=F32)[:, None] * inv[None, :]
    by_row = lambda a: jnp.broadcast_to(a[:, None, :], (n_rows, GRID_W, half)).reshape(seq, half)
    by_col = lambda a: jnp.broadcast_to(a[None, :, :], (n_rows, GRID_W, half)).reshape(seq, half)
    cos_r, sin_r, cos_c, sin_c = by_row(jnp.cos(ang_r)), by_row(jnp.sin(ang_r)), by_col(jnp.cos(ang_c)), by_col(jnp.sin(ang_c))
    ctx = lambda a, fill: jnp.concatenate([jnp.full((n_ctx, a.shape[1]), fill, F32), a], axis=0)
    spread = lambda a, m: jnp.dot(a, jnp.asarray(m, F32), precision=HI)
    dd = np.arange(HEAD_SLOT) - MLA_NOPE
    active = (dd >= 0) & (dd < MLA_ROPE)
    src = (np.clip(dd, 0, MLA_ROPE - 1) // 16) * half + np.clip(dd, 0, MLA_ROPE - 1) % half
    pick = (np.arange(2 * half)[:, None] == src[None, :]) & active[None, :]
    sign = np.where((dd % 16) < half, -1.0, 1.0)[None, :]
    cosq = spread(ctx(jnp.concatenate([cos_r, cos_c], axis=1), 1.0), pick) + jnp.asarray(~active, F32)[None, :]
    sinq = spread(ctx(jnp.concatenate([sin_r, sin_c], axis=1), 0.0), pick * sign)
    rhalf = RET_DK // 2
    rinv = ROPE_BASE ** (-jnp.arange(rhalf, dtype=F32) / rhalf)
    a_hi = (GRID_W * jnp.arange(n_rows, dtype=F32))[:, None, None] * rinv
    a_lo = jnp.arange(GRID_W, dtype=F32)[None, :, None] * rinv
    cos1 = (jnp.cos(a_hi) * jnp.cos(a_lo) - jnp.sin(a_hi) * jnp.sin(a_lo)).reshape(seq, rhalf)
    sin1 = (jnp.sin(a_hi) * jnp.cos(a_lo) + jnp.cos(a_hi) * jnp.sin(a_lo)).reshape(seq, rhalf)
    rl = np.arange(RET_HEADS * RET_DK) % RET_DK
    rpick = np.arange(rhalf)[:, None] == (rl % rhalf)[None, :]
    rsign = np.where(rl < rhalf, -1.0, 1.0)[None, :]
    cosr = spread(ctx(cos1, 1.0), rpick)
    sinr = spread(ctx(sin1, 0.0), rpick * rsign)
    return cosq, sinq, cosr, sinr


def kernel(x, c, ctx, c_ctx, mod_w, mod_b, norm1_w, norm2_w, w_in, mla_q_norm, mla_w_uq, mla_kv_norm, mla_w_ukv, ret_decay_logit, ret_norm_w, gla_w_gate, gla_b_gate, gla_norm_w, w_branch, w_out, ffn_w_gate, ffn_w_up, ffn_w_down, moe_router, moe_w_gate, moe_w_up, moe_w_down, final_norm_w):
    b, seq, d = x.shape
    n_ctx = ctx.shape[1]
    depth = mod_w.shape[0]
    assert d == D_MODEL and seq % GRID_W == 0
    assert n_ctx % TOKEN_TILE == 0 and n_ctx % SCAN_CHUNK == 0
    assert seq % TOKEN_TILE == 0 and seq % SCAN_CHUNK == 0
    nct = n_ctx // TOKEN_TILE

    rows = 8 * ((b + 1 + 7) // 8)
    cvec = jnp.zeros((rows, d), F32).at[0:b].set(c).at[b].set(c_ctx)
    mod = _modulation(cvec, mod_w, mod_b)
    cosq, sinq, cosr, sinr = _rope_tables(n_ctx, seq)
    fnw = final_norm_w.reshape(1, d)
    row2 = lambda a: a.reshape(1, -1)

    xs = (ctx, x)
    w_in_bf16 = w_in.astype(BF16)
    for l in range(depth):
        last = l == depth - 1
        modsel = jnp.stack([jnp.broadcast_to(mod[l, b], (b, 6 * d)), mod[l, 0:b]], axis=1)
        w1, wbg = _prep_w_in(w_in_bf16, l)
        wq, wqs, wk, wv = _prep_mla(mla_w_uq[l], mla_w_ukv[l])
        wg2, bg2 = _prep_gla_gate(gla_w_gate[l], gla_b_gate[l])
        nw1, nw2 = row2(norm1_w[l]), row2(norm2_w[l])

        (q, k, v, rq, rk, rv, rg, gq, gk, gv, gr, la) = _inproj(
            xs, modsel, nw1, w1, row2(mla_q_norm[l]), wq, wqs, row2(mla_kv_norm[l]), wk, wv,
            wg2, bg2, cosq, sinq, cosr, sinr, nct)
        oa = _attention(q, k, v, nct)
        dl = jnp.broadcast_to(ret_decay_logit[l][:, :, None, None], (2, RET_HEADS, 8, LANES))
        rof, rob, gof, gob = _scans(dl, rq, rk, rv, gq, gk, gv, la, n_ctx)
        is_moe = l % 2 == 1
        xs = _merge(xs, modsel, nw1, oa, rof, rob, rg, gof, gob, gr, wbg,
                    w_branch[l].astype(BF16), w_out[l].astype(BF16),
                    row2(ret_norm_w[l]), row2(gla_norm_w[l]), nct, is_moe and last)
        i = l // 2
        if not is_moe:
            xs = _ffn(xs, modsel, nw2, ffn_w_gate[i].astype(BF16), ffn_w_up[i].astype(BF16),
                      ffn_w_down[i].astype(BF16), fnw, nct, last)
            continue
        rt = jnp.pad(moe_router[i], ((0, 0), (0, LANES - N_EXPERTS)))
        experts = (moe_w_gate[i].astype(BF16), moe_w_up[i].astype(BF16), moe_w_down[i].astype(BF16))
        if last:
            xs = _moe(xs, 1, modsel, nw2, rt, *experts, fnw, True)
        else:
            xs = jnp.concatenate([_moe(xs[:, :n_ctx], 0, modsel, nw2, rt, *experts, fnw, False),
                                  _moe(xs[:, n_ctx:], 1, modsel, nw2, rt, *experts, fnw, False)], axis=1)
    return xs
```

```python
import functools

import numpy as np
import jax
import jax.numpy as jnp
from jax import lax
from jax.experimental import pallas as pl
from jax.experimental.pallas import tpu as pltpu

D_MODEL = 1024
GRID_W = 64
EPS = 1e-6
ROPE_BASE = 10000.0

MLA_HEADS = 8
MLA_NOPE = 64
MLA_ROPE = 32
MLA_V = 64
MLA_Q_RANK = 256
MLA_KV_RANK = 128

RET_HEADS = 4
RET_DK = 64
RET_DV = 128

GLA_HEADS = 4
GLA_DK = 64
GLA_DV = 128
GLA_GATE_RANK = 16
GLA_TAU = 16.0

N_BRANCH = 3
BRANCH_W = 512
D_FF = 2816
N_EXPERTS = 8

IN_SPLITS = (
    MLA_Q_RANK, MLA_KV_RANK, MLA_ROPE,
    RET_HEADS * RET_DK, RET_HEADS * RET_DK, RET_HEADS * RET_DV, RET_HEADS * RET_DV,
    GLA_HEADS * GLA_DK, GLA_HEADS * GLA_DK, GLA_HEADS * GLA_DV, GLA_HEADS * GLA_DV,
    2 * GLA_GATE_RANK,
    N_BRANCH * D_MODEL,
)

F32 = jnp.float32
BF16 = jnp.bfloat16
HI = lax.Precision.HIGHEST

LANES = 128
HEAD_SLOT = 128
TOKEN_TILE = 256
ATT_DEPTH = 3
ATT_UNROLL = 10
ATT_SUM_ROWS = 16
SCAN_CHUNK = 256
GLA_BASE = 32
MOE_TILE = 1024
MOE_BLOCK = 256
MOE_TAIL_BLOCKS = (128, 192)
MOD_COLS_TILE = 3072
VMEM_LIMIT = 56 * 1024 * 1024
VMEM_LIMIT_EXPERTS = 60 * 1024 * 1024


def _silu(x):
    return x / (1.0 + jnp.exp(-x))


def _sigmoid(x):
    return 1.0 / (1.0 + jnp.exp(-x))


def _log_sigmoid(z):
    return jnp.minimum(z, 0.0) - jnp.log(1.0 + jnp.exp(-jnp.abs(z)))


def _rms(x, w):
    return x * lax.rsqrt(jnp.mean(x * x, axis=-1, keepdims=True) + EPS) * w


def _dot(a, b, precision=None):
    return jnp.dot(a, b, preferred_element_type=F32, precision=precision)


def _dot_nt(a, b):
    return lax.dot_general(a, b, (((1,), (1,)), ((), ())), preferred_element_type=F32)


def _bf16_pieces(x, n):
    out = []
    for _ in range(n):
        piece = x.astype(BF16)
        out.append(piece)
        x = x - piece.astype(F32)
    return out


def _dot_split(a, b):
    (a0, a1), (b0, b1) = _bf16_pieces(a, 2), _bf16_pieces(b, 2)
    return _dot(a0, b0) + (_dot(a0, b1) + _dot(a1, b0))


def _params(sem, vmem_limit=None):
    return pltpu.CompilerParams(dimension_semantics=sem, vmem_limit_bytes=vmem_limit or VMEM_LIMIT)


def _const_spec(shape):
    nd = len(shape)
    return pl.BlockSpec(shape, lambda *_: (0,) * nd, pipeline_mode=pl.Buffered(1))


def _mod_rows(mod_ref, is_ctx):
    m = mod_ref[0]
    return jnp.where(is_ctx, m[0:1], m[1:2])


def _mod_kernel(c_ref, w_ref, b_ref, o_ref):
    o_ref[0] = _dot(_silu(c_ref[...]), w_ref[0], HI) + b_ref[0]


def _modulation(cvec, mod_w, mod_b):
    depth, d, n6 = mod_w.shape
    rows = cvec.shape[0]
    tn = MOD_COLS_TILE
    return pl.pallas_call(
        _mod_kernel,
        grid=(depth, n6 // tn),
        in_specs=[pl.BlockSpec((rows, d), lambda l, j: (0, 0)),
                  pl.BlockSpec((1, d, tn), lambda l, j: (l, 0, j)),
                  pl.BlockSpec((1, 1, tn), lambda l, j: (l, 0, j))],
        out_specs=pl.BlockSpec((1, rows, tn), lambda l, j: (l, 0, j)),
        out_shape=jax.ShapeDtypeStruct((depth, rows, n6), F32),
        compiler_params=_params(("arbitrary", "arbitrary")),
    )(cvec, mod_w, mod_b.reshape(depth, 1, n6))


_C_CQ = (0, 256)
_C_CKV_KRA = (256, 512)
_C_KRB_GZ = (512, 768)
_C_RQ = (768, 1024)
_C_RK = (1024, 1280)
_C_RV = (1280, 1792)
_C_RG = (1792, 2304)
_C_GQ = (2304, 2560)
_C_GK = (2560, 2816)
_C_GV = (2816, 3328)
_C_GR = (3328, 3840)


def _seq_inputs(xs, nct, off=0):
    tm = TOKEN_TILE
    if not isinstance(xs, tuple):
        return [pl.BlockSpec((1, tm, xs.shape[-1]), lambda i, t: (i, t + off, 0))], [xs]
    d = xs[0].shape[-1]
    return ([pl.BlockSpec((1, tm, d), lambda i, t: (i, jnp.minimum(t + off, nct - 1), 0)),
             pl.BlockSpec((1, tm, d), lambda i, t: (i, jnp.maximum(t + off - nct, 0), 0))], list(xs))


def _seq_tile(x_refs, is_ctx):
    if len(x_refs) == 1:
        return x_refs[0][0]
    return jnp.where(is_ctx, x_refs[0][0], x_refs[1][0])


def _inproj_kernel(nct, nx, *refs):
    x_refs = refs[:nx]
    (mod_ref, nw_ref, w_ref, qn_ref, wq_ref, wqs_ref, kvn_ref, wk_ref, wv_ref, wg2_ref, bg2_ref,
     cq_ref, sq_ref, cr_ref, sr_ref, q_out, k_out, v_out, rq_out, rk_out, rv_out, rg_out,
     gq_out, gk_out, gv_out, gr_out, la_out) = refs[nx:]
    d = D_MODEL
    is_ctx = pl.program_id(1) < nct
    row = _mod_rows(mod_ref, is_ctx)
    sh, sc = row[:, 0:d], row[:, d:2 * d]
    u = (_rms(_seq_tile(x_refs, is_ctx), nw_ref[...]) * (1.0 + sc) + sh).astype(BF16)

    def seg(c):
        return _dot(u, w_ref[:, c[0]:c[1]])

    cqn = _rms(seg(_C_CQ), qn_ref[...]).astype(BF16)
    q_all = _dot(cqn, wq_ref[...])
    q_swp = _dot(cqn, wqs_ref[...])
    ckv_kra = seg(_C_CKV_KRA)
    krb_gz = seg(_C_KRB_GZ)
    ckvn = _rms(ckv_kra[:, 0:LANES], kvn_ref[...]).astype(BF16)
    k_all = _dot(ckvn, wk_ref[...])
    v_out[0, 0] = _dot(ckvn, wv_ref[...]).T.astype(BF16)
    cosq, sinq = cq_ref[...], sq_ref[...]
    k_rope = ckv_kra[:, LANES:] * cosq + krb_gz[:, 0:LANES] * sinq
    for h in range(MLA_HEADS):
        sl = slice(h * HEAD_SLOT, (h + 1) * HEAD_SLOT)
        q_out[0, h, 0] = (q_all[:, sl] * cosq + q_swp[:, sl] * sinq).T.astype(BF16)
        k_out[0, h] = (k_all[:, sl] + k_rope).astype(BF16)

    cosr, sinr = cr_ref[...], sr_ref[...]
    hw = RET_HEADS * RET_DK
    first = (lax.broadcasted_iota(jnp.int32, (TOKEN_TILE, hw), 1) % RET_DK) < RET_DK // 2
    swap = lambda a: jnp.where(first, pltpu.roll(a, hw - RET_DK // 2, 1), pltpu.roll(a, RET_DK // 2, 1))
    rq, rk = seg(_C_RQ), seg(_C_RK)
    rq_out[0] = (rq * cosr + swap(rq) * sinr).astype(BF16)
    rk_out[0] = (rk * cosr + swap(rk) * sinr).astype(BF16)
    rv_out[0] = seg(_C_RV).astype(BF16)
    rg_out[0] = seg(_C_RG).astype(BF16)

    gq_out[0] = seg(_C_GQ).astype(BF16)
    gk_out[0] = seg(_C_GK).astype(BF16)
    gv_out[0] = seg(_C_GV).astype(BF16)
    gr_out[0] = seg(_C_GR).astype(BF16)
    z = _dot_split(krb_gz[:, LANES:], wg2_ref[...]) + bg2_ref[...]
    la_out[0] = _log_sigmoid(z) * (1.0 / GLA_TAU)


def _inproj(xs, modsel, nw, w1, qn, wq, wqs, kvn, wk, wv, wg2, bg2, cosq, sinq, cosr, sinr, nct):
    x_specs, x_ops = _seq_inputs(xs, nct)
    b, d = x_ops[0].shape[0], x_ops[0].shape[-1]
    n = sum(a.shape[1] for a in x_ops)
    tm = TOKEN_TILE
    nt = n // tm
    h = MLA_HEADS
    tok = lambda w: pl.BlockSpec((1, tm, w), lambda i, t: (i, t, 0))
    tab = lambda w: pl.BlockSpec((tm, w), lambda i, t: (t, 0))
    q_spec = pl.BlockSpec((1, h, 1, HEAD_SLOT, tm), lambda i, t: (i, 0, t, 0, 0))
    k_spec = pl.BlockSpec((1, h, tm, HEAD_SLOT), lambda i, t: (i, 0, t, 0))
    v_spec = pl.BlockSpec((1, 1, h * MLA_V, tm), lambda i, t: (i, t, 0, 0))
    sds = lambda w, dt=BF16: jax.ShapeDtypeStruct((b, n, w), dt)
    q_sds = jax.ShapeDtypeStruct((b, h, nt, HEAD_SLOT, tm), BF16)
    k_sds = jax.ShapeDtypeStruct((b, h, n, HEAD_SLOT), BF16)
    v_sds = jax.ShapeDtypeStruct((b, nt, h * MLA_V, tm), BF16)
    return pl.pallas_call(
        functools.partial(_inproj_kernel, nct, len(x_ops)),
        grid=(b, nt),
        in_specs=x_specs + [pl.BlockSpec((1, 2, 6 * d), lambda i, t: (i, 0, 0)), _const_spec(nw.shape),
                  _const_spec(w1.shape), _const_spec(qn.shape), _const_spec(wq.shape),
                  _const_spec(wqs.shape), _const_spec(kvn.shape), _const_spec(wk.shape),
                  _const_spec(wv.shape), _const_spec(wg2.shape), _const_spec(bg2.shape),
                  tab(HEAD_SLOT), tab(HEAD_SLOT), tab(256), tab(256)],
        out_specs=[q_spec, k_spec, v_spec, tok(256), tok(256), tok(512), tok(512),
                   tok(256), tok(256), tok(512), tok(512), tok(512)],
        out_shape=[q_sds, k_sds, v_sds, sds(256), sds(256), sds(512), sds(512),
                   sds(256), sds(256), sds(512), sds(512), sds(512, F32)],
        compiler_params=_params(("arbitrary", "arbitrary")),
    )(*x_ops, modsel, nw, w1, qn, wq, wqs, kvn, wk, wv, wg2, bg2, cosq, sinq, cosr, sinr)


def _attn_kernel(nct, n_tiles, qt_ref, k_ref, vt_ref, o_ref, *bufs):
    t = TOKEN_TILE
    nd = ATT_DEPTH
    heads = range(2)
    s_bufs = (bufs[0:nd], bufs[nd:2 * nd])
    p_bufs = (bufs[2 * nd:3 * nd], bufs[3 * nd:4 * nd])
    qts = [qt_ref[0, j, 0] for j in heads]
    ones = jnp.ones((ATT_SUM_ROWS, t), BF16)

    def scores(j, c, s_buf):
        s = _dot(k_ref[0, j, pl.ds(pl.multiple_of(c * t, t), t), :], qts[j])
        s_buf[...] = s
        return jnp.max(s, axis=0, keepdims=True)

    def softmax(s_buf, p_buf, m, cmax):
        m_new = jnp.maximum(m, cmax)
        p_buf[...] = jnp.exp2(s_buf[...] - m_new).astype(BF16)
        return m_new, jnp.exp2(m - m_new)

    def values(j, c, p_buf, alpha, acc):
        lhs = jnp.concatenate([vt_ref[0, c, j * MLA_V:(j + 1) * MLA_V, :], ones], axis=0)
        return alpha * acc + _dot(lhs, p_buf[...])

    m_init = jnp.full((1, t), -jnp.inf, F32)
    acc_init = jnp.zeros((MLA_V + ATT_SUM_ROWS, t), F32)

    def finish(accs):
        outs = [acc[0:MLA_V] / acc[MLA_V:MLA_V + 1] for acc in accs]
        o_ref[0] = jnp.concatenate(outs, axis=0).T.astype(BF16)

    @pl.when(pl.program_id(2) < nct)
    def _():
        accs = []
        for j in heads:
            m, acc = m_init, acc_init
            for c in range(nct):
                cmax = scores(j, c, s_bufs[j][0])
                m, alpha = softmax(s_bufs[j][0], p_bufs[j][0], m, cmax)
                acc = values(j, c, p_bufs[j][0], alpha, acc)
            accs.append(acc)
        finish(accs)

    def step(c, slot, state, lookahead=True):
        ahead = [scores(j, c + 2, s_bufs[j][(slot + 2) % nd]) if lookahead else None for j in heads]
        accs = [values(j, c - 1, p_bufs[j][(slot - 1) % nd], state[j][1], state[j][2]) for j in heads]
        stats = [softmax(s_bufs[j][slot], p_bufs[j][slot], state[j][0], state[j][3]) for j in heads]
        return tuple((stats[j][0], stats[j][1], accs[j], state[j][4], ahead[j]) for j in heads)

    @pl.when(pl.program_id(2) >= nct)
    def _():
        state = []
        for j in heads:
            cmax0 = scores(j, 0, s_bufs[j][0])
            cmax1 = scores(j, 1, s_bufs[j][1])
            m, alpha = softmax(s_bufs[j][0], p_bufs[j][0], m_init, cmax0)
            state.append((m, alpha, acc_init, cmax1, scores(j, 2, s_bufs[j][2])))

        def trip(i, state):
            for k in range(nd):
                state = step(1 + nd * i + k, (1 + k) % nd, state)
            return state

        state = lax.fori_loop(0, (n_tiles - 3) // nd, trip, tuple(state), unroll=ATT_UNROLL)
        for c in (n_tiles - 2, n_tiles - 1):
            state = step(c, c % nd, state, lookahead=False)
        finish([values(j, n_tiles - 1, p_bufs[j][(n_tiles - 1) % nd], state[j][1], state[j][2])
                for j in heads])


def _attention(qt, k, vt, nct):
    b, h, n_tiles, w, t = qt.shape
    n = n_tiles * t
    assert n_tiles >= 3 and (n_tiles - 3) % ATT_DEPTH == 0
    return pl.pallas_call(
        functools.partial(_attn_kernel, nct, n_tiles),
        grid=(b, h // 2, n_tiles),
        in_specs=[pl.BlockSpec((1, 2, 1, w, t), lambda i, p, q: (i, p, q, 0, 0)),
                  pl.BlockSpec((1, 2, n, w), lambda i, p, q: (i, p, 0, 0)),
                  pl.BlockSpec((1, n_tiles, 2 * MLA_V, t), lambda i, p, q: (i, 0, p, 0))],
        out_specs=pl.BlockSpec((1, t, 2 * MLA_V), lambda i, p, q: (i, q, p)),
        out_shape=jax.ShapeDtypeStruct((b, n, h * MLA_V), BF16),
        scratch_shapes=[pltpu.VMEM((t, t), F32)] * (2 * ATT_DEPTH) + [pltpu.VMEM((t, t), BF16)] * (2 * ATT_DEPTH),
        compiler_params=_params(("arbitrary", "arbitrary", "arbitrary")),
    )(qt, k, vt)


def _bwd_chunk(s, nct, nch):
    return jnp.where(s < nct, nct - 1 - s, nch - 1 - (s - nct))


def _ret_body(dl_ref, qf, kf, vf, qb, kb, vb, of_ref, ob_ref, st_ref):
    c = SCAN_CHUNK
    hw = RET_HEADS * RET_DK
    ii = lax.broadcasted_iota(jnp.int32, (c, c), 0)
    jj = lax.broadcasted_iota(jnp.int32, (c, c), 1)
    pos_i = lax.broadcasted_iota(jnp.int32, (c, 1), 0).astype(F32)
    pos_j = lax.broadcasted_iota(jnp.int32, (1, c), 1).astype(F32)
    lane_head = lax.broadcasted_iota(jnp.int32, (c, hw), 1) // RET_DK
    for d, (q_ref, k_ref, v_ref, o_ref) in enumerate(((qf, kf, vf, of_ref), (qb, kb, vb, ob_ref))):
        rev = d == 1
        q, v = q_ref[0], v_ref[0]
        k_t = k_ref[0].astype(F32).T
        k_tb = k_t.astype(BF16)
        st = st_ref[d]
        st_b = st.astype(BF16)
        dist = ((jj - ii) if rev else (ii - jj)).astype(F32)
        for h in range(RET_HEADS):
            lg = _log_sigmoid(dl_ref[d, h])[0:1, 0:1]
            intra = jnp.where(dist >= 0.0, jnp.exp(lg * jnp.maximum(dist, 0.0)), 0.0)
            q_dec = jnp.exp(lg * ((c - pos_i) if rev else (pos_i + 1.0)))
            k_dec = jnp.exp(lg * (pos_j if rev else (c - 1.0 - pos_j)))
            rows = slice(h * RET_DK, (h + 1) * RET_DK)
            cols = slice(h * RET_DV, (h + 1) * RET_DV)
            qm = jnp.where(lane_head == h, q, jnp.zeros_like(q))
            scores = _dot(qm, k_tb) * intra
            o = _dot(scores.astype(BF16), v[:, cols]) + _dot(qm, st_b) * q_dec
            o_ref[0, :, cols] = o
            kd = (k_t[rows] * k_dec).astype(BF16)
            st_ref[d, rows, :] = st[rows] * jnp.exp(lg * float(c)) + _dot(kd, v[:, cols])


def _block_start(i, size):
    return i & ~(size - 1)


def _gla_levels():
    half = SCAN_CHUNK // 2
    out = []
    while half >= GLA_BASE:
        out.append(half)
        half //= 2
    return out


def _gla_body(qf, kf, vf, laf, qb, kb, vb, lab, of_ref, ob_ref, st_ref):
    c = SCAN_CHUNK
    hw = GLA_HEADS * GLA_DK
    levels = _gla_levels()
    ii = lax.broadcasted_iota(jnp.int32, (c, c), 0)
    jj = lax.broadcasted_iota(jnp.int32, (c, c), 1)
    pos = lax.broadcasted_iota(jnp.int32, (c, 1), 0)
    lane_head = lax.broadcasted_iota(jnp.int32, (c, hw), 1) // GLA_DK
    for d, (q_ref, k_ref, v_ref, la_ref, o_ref) in enumerate(((qf, kf, vf, laf, of_ref),
                                                               (qb, kb, vb, lab, ob_ref))):
        rev = d == 1
        q, k, v, la = q_ref[0].astype(F32), k_ref[0].astype(F32), v_ref[0], la_ref[0]

        prefix = _one_hot((jj >= ii) if rev else (jj <= ii))
        b = sum(_dot(prefix, piece) for piece in _bf16_pieces(la, 3))
        b_end = b[0:1] if rev else b[c - 1:c]

        def at_row(size, idx):
            b3 = b.reshape(c // size, size, hw)
            return jnp.broadcast_to(b3[:, idx:idx + 1, :], b3.shape).reshape(c, hw)

        q_t = (q * jnp.exp(b)).astype(BF16)
        k_tt = (k * jnp.exp(b_end - b)).T
        dec_col = jnp.exp(jnp.broadcast_to(b_end, (LANES, hw)).T[:, 0:1])

        terms = []
        for half in levels:
            e = at_row(2 * half, half if rev else half - 1)
            inner = pos & (2 * half - 1)
            late = (inner < half) if rev else (inner >= half)
            qh = jnp.where(late, q * jnp.exp(jnp.where(late, b - e, 0.0)), 0.0)
            kh = jnp.where(late, 0.0, k * jnp.exp(jnp.where(late, 0.0, e - b)))
            terms.append((qh.astype(BF16), kh.astype(BF16), _block_start(ii, 2 * half) == _block_start(jj, 2 * half)))
        r = at_row(GLA_BASE, GLA_BASE - 1 if rev else 0)
        tri = (jj >= ii) if rev else (jj <= ii)
        terms.append(((q * jnp.exp(b - r)).astype(BF16), (k * jnp.exp(r - b)).astype(BF16),
                      (_block_start(ii, GLA_BASE) == _block_start(jj, GLA_BASE)) & tri))

        st = st_ref[d]
        st_b = st.astype(BF16)
        for h in range(GLA_HEADS):
            rows = slice(h * GLA_DK, (h + 1) * GLA_DK)
            cols = slice(h * GLA_DV, (h + 1) * GLA_DV)
            sel = lane_head == h
            a = jnp.zeros((c, c), F32)
            for qh, kh, mask in terms:
                a = a + jnp.where(mask, _dot_nt(jnp.where(sel, qh, jnp.zeros_like(qh)), kh), 0.0)
            vh = v[:, cols]
            o_ref[0, :, cols] = _dot(a.astype(BF16), vh) + _dot(jnp.where(sel, q_t, jnp.zeros_like(q_t)), st_b)
            st_ref[d, rows, :] = st[rows] * dec_col[rows] + _dot(k_tt[rows].astype(BF16), vh)


def _scan_kernel(*refs):
    ret_refs, gla_refs = refs[0:7] + refs[15:17] + refs[19:20], refs[7:15] + refs[17:19] + refs[20:21]

    @pl.when(pl.program_id(1) == 0)
    def _():
        ret_refs[-1][...] = jnp.zeros_like(ret_refs[-1])
        gla_refs[-1][...] = jnp.zeros_like(gla_refs[-1])

    _ret_body(*ret_refs)
    _gla_body(*gla_refs)


def _scans(dl, rq, rk, rv, gq, gk, gv, la, nct_tokens):
    b, n, hw = rq.shape
    c = SCAN_CHUNK
    nch = n // c
    nct = nct_tokens // c
    vw = rv.shape[-1]
    fwd = lambda w, col=0: pl.BlockSpec((1, c, w), lambda i, s: (i, s, col))
    bwd = lambda w, col=0: pl.BlockSpec((1, c, w), lambda i, s: (i, _bwd_chunk(s, nct, nch), col))
    return pl.pallas_call(
        _scan_kernel,
        grid=(b, nch),
        in_specs=[_const_spec(dl.shape), fwd(hw), fwd(hw), fwd(vw), bwd(hw), bwd(hw), bwd(vw),
                  fwd(hw), fwd(hw), fwd(vw), fwd(hw, 0), bwd(hw), bwd(hw), bwd(vw), bwd(hw, 1)],
        out_specs=[fwd(vw), bwd(vw), fwd(vw), bwd(vw)],
        out_shape=[jax.ShapeDtypeStruct((b, n, vw), F32)] * 4,
        scratch_shapes=[pltpu.VMEM((2, hw, RET_DV), F32), pltpu.VMEM((2, hw, GLA_DV), F32)],
        compiler_params=_params(("arbitrary", "arbitrary")),
    )(dl, rq, rk, rv, rq, rk, rv, gq, gk, gv, la, gq, gk, gv, la)


def _head_norm(o, w, center):
    parts = []
    for h in range(o.shape[-1] // LANES):
        oh = o[:, h * LANES:(h + 1) * LANES]
        if center:
            oh = oh - jnp.mean(oh, axis=-1, keepdims=True)
        parts.append(oh * lax.rsqrt(jnp.mean(oh * oh, axis=-1, keepdims=True) + EPS))
    return jnp.concatenate(parts, axis=-1) * w


def _merge_kernel(nct, off, nx, *refs):
    x_refs = refs[:nx]
    (mod_ref, nw_ref, oa_ref, rof_ref, rob_ref, rg_ref, gof_ref, gob_ref, gr_ref, wbg_ref, wbr_ref,
     wout_ref, rnw_ref, gnw_ref, o_ref) = refs[nx:]
    d = D_MODEL
    is_ctx = pl.program_id(1) + off < nct
    row = _mod_rows(mod_ref, is_ctx)
    sh, sc, g1 = row[:, 0:d], row[:, d:2 * d], row[:, 2 * d:3 * d]
    x = _seq_tile(x_refs, is_ctx)
    u = (_rms(x, nw_ref[...]) * (1.0 + sc) + sh).astype(BF16)
    gate = _sigmoid(_dot(u, wbg_ref[...]))
    yb = _silu(rg_ref[0].astype(F32)) * _head_norm(rof_ref[0] + rob_ref[0], rnw_ref[...], True)
    yc = _silu(gr_ref[0].astype(F32)) * _head_norm(gof_ref[0] + gob_ref[0], gnw_ref[...], False)
    z = (gate[:, 0:d] * _dot(oa_ref[0], wbr_ref[0])
         + gate[:, d:2 * d] * _dot(yb.astype(BF16), wbr_ref[1])
         + gate[:, 2 * d:3 * d] * _dot(yc.astype(BF16), wbr_ref[2]))
    o_ref[0] = x + g1 * _dot(z.astype(BF16), wout_ref[...])


def _merge(xs, modsel, nw, oa, rof, rob, rg, gof, gob, gr, wbg, wbr, wout, rnw, gnw, nct, latent_only):
    tm = TOKEN_TILE
    off = nct if latent_only else 0
    x_specs, x_ops = _seq_inputs(xs, nct, off)
    b, d = x_ops[0].shape[0], x_ops[0].shape[-1]
    nt = sum(a.shape[1] for a in x_ops) // tm - off
    tok = lambda w: pl.BlockSpec((1, tm, w), lambda i, t: (i, t + off, 0))
    return pl.pallas_call(
        functools.partial(_merge_kernel, nct, off, len(x_ops)),
        grid=(b, nt),
        in_specs=x_specs + [pl.BlockSpec((1, 2, 6 * d), lambda i, t: (i, 0, 0)), _const_spec(nw.shape),
                  tok(512), tok(512), tok(512), tok(512), tok(512), tok(512), tok(512),
                  _const_spec(wbg.shape), _const_spec(wbr.shape), _const_spec(wout.shape),
                  _const_spec(rnw.shape), _const_spec(gnw.shape)],
        out_specs=pl.BlockSpec((1, tm, d), lambda i, t: (i, t, 0)),
        out_shape=jax.ShapeDtypeStruct((b, nt * tm, d), F32),
        compiler_params=_params(("arbitrary", "arbitrary")),
    )(*x_ops, modsel, nw, oa, rof, rob, rg, gof, gob, gr, wbg, wbr, wout, rnw, gnw)


def _norm2(x, row, nw_ref):
    d = D_MODEL
    sh, sc = row[:, 3 * d:4 * d], row[:, 4 * d:5 * d]
    return _rms(x, nw_ref[...]) * (1.0 + sc) + sh


def _finish(x, y, row, final, fnw_ref):
    d = D_MODEL
    out = x + row[:, 5 * d:6 * d] * y
    if final:
        out = _rms(out, fnw_ref[...])
    return out


def _ffn_kernel(nct, off, final, x_ref, mod_ref, nw_ref, wg_ref, wu_ref, wd_ref, fnw_ref, o_ref):
    row = _mod_rows(mod_ref, pl.program_id(1) + off < nct)
    x = x_ref[0]
    v = _norm2(x, row, nw_ref).astype(BF16)
    hdn = _silu(_dot(v, wg_ref[...])) * _dot(v, wu_ref[...])
    o_ref[0] = _finish(x, _dot(hdn.astype(BF16), wd_ref[...]), row, final, fnw_ref)


def _ffn(xs, modsel, nw, wg, wu, wd, fnw, nct, final):
    b, n, d = xs.shape
    tm = TOKEN_TILE
    off = nct if final else 0
    nt = n // tm - off
    return pl.pallas_call(
        functools.partial(_ffn_kernel, nct, off, final),
        grid=(b, nt),
        in_specs=[pl.BlockSpec((1, tm, d), lambda i, t: (i, t + off, 0)),
                  pl.BlockSpec((1, 2, 6 * d), lambda i, t: (i, 0, 0)), _const_spec(nw.shape),
                  _const_spec(wg.shape), _const_spec(wu.shape), _const_spec(wd.shape),
                  _const_spec(fnw.shape)],
        out_specs=pl.BlockSpec((1, tm, d), lambda i, t: (i, t, 0)),
        out_shape=jax.ShapeDtypeStruct((b, nt * tm, d), F32),
        compiler_params=_params(("arbitrary", "arbitrary")),
    )(xs, modsel, nw, wg, wu, wd, fnw)


def _top2_gates(logits):
    lane = lax.broadcasted_iota(jnp.int32, logits.shape, 1)
    m1 = jnp.max(logits, axis=-1, keepdims=True)
    i1 = jnp.min(jnp.where(logits == m1, lane, LANES), axis=-1, keepdims=True)
    rest = jnp.where(lane == i1, -jnp.inf, logits)
    m2 = jnp.max(rest, axis=-1, keepdims=True)
    i2 = jnp.min(jnp.where(rest == m2, lane, LANES), axis=-1, keepdims=True)
    e2 = jnp.exp(m2 - m1)
    w1 = 1.0 / (1.0 + e2)
    gates = jnp.where(lane == i1, w1, jnp.where(lane == i2, e2 * w1, 0.0))
    return gates, (lane == i1) | (lane == i2)


def _one_hot(cond):
    return jnp.where(cond, 1.0, 0.0).astype(BF16)


def _route_kernel(mrow, x_ref, mod_ref, nw_ref, rt_ref, v_out, gate_out, rank_out, rankt_out):
    v = _norm2(x_ref[0], mod_ref[0][mrow:mrow + 1], nw_ref)
    v_out[0] = v.astype(BF16)
    tt = v.shape[0]
    lane = lax.broadcasted_iota(jnp.int32, (tt, LANES), 1)
    logits = jnp.where(lane < N_EXPERTS, _dot_split(v, rt_ref[...]), -jnp.inf)
    gates, routed = _top2_gates(logits)
    gate_out[0] = gates
    sub = TOKEN_TILE
    ti = lax.broadcasted_iota(jnp.int32, (sub, sub), 0)
    tj = lax.broadcasted_iota(jnp.int32, (sub, sub), 1)
    earlier, earlier_t = _one_hot(tj < ti), _one_hot(ti < tj)
    routed_f = jnp.where(routed, 1.0, 0.0)
    routed_t = routed_f.T
    before = jnp.zeros((1, LANES), F32)
    before_t = jnp.zeros((LANES, 1), F32)
    ranks, ranks_t = [], []
    for k in range(tt // sub):
        rows = slice(k * sub, (k + 1) * sub)
        ranks.append(_dot(earlier, routed_f[rows].astype(BF16)) + before)
        ranks_t.append(_dot(routed_t[:, rows].astype(BF16), earlier_t) + before_t)
        before = before + jnp.sum(routed_f[rows], axis=0, keepdims=True)
        before_t = before_t + jnp.sum(routed_t[:, rows], axis=1, keepdims=True)
    rank_out[0] = jnp.where(routed, jnp.concatenate(ranks, axis=0), -1.0)
    rank_t = jnp.where(routed_t > 0.5, jnp.concatenate(ranks_t, axis=1), -1.0)
    rankt_out[0, 0] = rank_t[0:N_EXPERTS]


def _experts_kernel(v_ref, gate_ref, rank_ref, rankt_ref, wg_ref, wu_ref, wd_ref, o_ref, acc_ref):
    e, s = pl.program_id(1), pl.program_id(2)
    tt = v_ref.shape[1]
    fh = D_FF // 2
    tile0 = pl.multiple_of(s * tt, tt)
    lane = lax.broadcasted_iota(jnp.int32, (tt, LANES), 1)
    g_col = jnp.sum(jnp.where(lane == e, gate_ref[0], 0.0), axis=-1, keepdims=True)
    rk_col = jnp.sum(jnp.where(lane == e, rank_ref[0], 0.0), axis=-1, keepdims=True)
    rk_row = rankt_ref[0, 0, pl.ds(e, 1), :]
    n_routed = jnp.max(rk_col).astype(jnp.int32) + 1

    @pl.when(e == 0)
    def _():
        acc_ref[pl.ds(tile0, tt), :] = jnp.zeros((tt, D_MODEL), BF16)

    def block(first_slot, r):
        base = first_slot.astype(F32)
        slot_i = base + lax.broadcasted_iota(jnp.int32, (r, 1), 0).astype(F32)
        xg = _dot(_one_hot(rk_row == slot_i), v_ref[0]).astype(BF16)
        y = jnp.zeros((r, D_MODEL), F32)
        for c in range(2):
            cols = slice(c * fh, (c + 1) * fh)
            hdn = _silu(_dot(xg, wg_ref[0, :, cols])) * _dot(xg, wu_ref[0, :, cols])
            y = y + _dot(hdn.astype(BF16), wd_ref[0, cols, :])
        y = y.astype(BF16)
        slot_j = base + lax.broadcasted_iota(jnp.int32, (1, r), 1).astype(F32)
        for c in range(tt // TOKEN_TILE):
            rows = slice(c * TOKEN_TILE, (c + 1) * TOKEN_TILE)
            dst = pl.ds(tile0 + c * TOKEN_TILE, TOKEN_TILE)
            part = g_col[rows] * _dot(_one_hot(rk_col[rows] == slot_j), y)
            acc_ref[dst, :] = (acc_ref[dst, :].astype(F32) + part).astype(BF16)

    r = MOE_BLOCK
    n_full = n_routed // r
    left = n_routed - n_full * r

    def full_block(i, carry):
        block(i * r, r)
        return carry

    lax.fori_loop(0, n_full + (left > MOE_TAIL_BLOCKS[-1]).astype(jnp.int32), full_block, 0)
    for below, size in zip((0,) + MOE_TAIL_BLOCKS, MOE_TAIL_BLOCKS):
        @pl.when((left > below) & (left <= size))
        def _(size=size):
            block(n_full * r, size)

    @pl.when(e == N_EXPERTS - 1)
    def _():
        o_ref[0] = acc_ref[pl.ds(tile0, tt), :]


def _moe_finish_kernel(mrow, final, x_ref, y_ref, mod_ref, fnw_ref, o_ref):
    o_ref[0] = _finish(x_ref[0], y_ref[0].astype(F32), mod_ref[0][mrow:mrow + 1], final, fnw_ref)


def _moe(slab, mrow, modsel, nw, rt, wg, wu, wd, fnw, final):
    b, m, d = slab.shape
    tt = min(MOE_TILE, m)
    assert m % tt == 0 and tt % TOKEN_TILE == 0
    ns = m // tt
    ne, _, f = wg.shape
    mod_spec = lambda: pl.BlockSpec((1, 2, 6 * d), lambda i, s: (i, 0, 0))
    v, gates, rank, rank_t = pl.pallas_call(
        functools.partial(_route_kernel, mrow),
        grid=(b, ns),
        in_specs=[pl.BlockSpec((1, tt, d), lambda i, s: (i, s, 0)), mod_spec(), _const_spec(nw.shape),
                  _const_spec(rt.shape)],
        out_specs=[pl.BlockSpec((1, tt, d), lambda i, s: (i, s, 0)),
                   pl.BlockSpec((1, tt, LANES), lambda i, s: (i, s, 0)),
                   pl.BlockSpec((1, tt, LANES), lambda i, s: (i, s, 0)),
                   pl.BlockSpec((1, 1, ne, tt), lambda i, s: (i, s, 0, 0))],
        out_shape=[jax.ShapeDtypeStruct((b, m, d), BF16), jax.ShapeDtypeStruct((b, m, LANES), F32),
                   jax.ShapeDtypeStruct((b, m, LANES), F32), jax.ShapeDtypeStruct((b, ns, ne, tt), F32)],
        compiler_params=_params(("arbitrary", "arbitrary")),
    )(slab, modsel, nw, rt)

    tok = lambda w: pl.BlockSpec((1, tt, w), lambda i, e, s: (i, s, 0))
    wspec = lambda shape, bufs=1: pl.BlockSpec((1,) + shape, lambda i, e, s: (e, 0, 0),
                                               pipeline_mode=pl.Buffered(bufs))
    last_pass = lambda i, e, s: (i, jnp.where(e == ne - 1, s, 0), 0)
    y_sum = pl.pallas_call(
        _experts_kernel,
        grid=(b, ne, ns),
        in_specs=[tok(d), tok(LANES), tok(LANES),
                  pl.BlockSpec((1, 1, ne, tt), lambda i, e, s: (i, s, 0, 0)),
                  wspec((d, f), 2), wspec((d, f), 2), wspec((f, d))],
        out_specs=pl.BlockSpec((1, tt, d), last_pass),
        out_shape=jax.ShapeDtypeStruct((b, m, d), BF16),
        scratch_shapes=[pltpu.VMEM((m, d), BF16)],
        compiler_params=_params(("arbitrary", "arbitrary", "arbitrary"), VMEM_LIMIT_EXPERTS),
    )(v, gates, rank, rank_t, wg, wu, wd)

    tm = tt
    row_tile = lambda: pl.BlockSpec((1, tm, d), lambda i, t: (i, t, 0))
    return pl.pallas_call(
        functools.partial(_moe_finish_kernel, mrow, final),
        grid=(b, m // tm),
        in_specs=[row_tile(), row_tile(), mod_spec(), _const_spec(fnw.shape)],
        out_specs=row_tile(),
        out_shape=jax.ShapeDtypeStruct((b, m, d), F32),
        compiler_params=_params(("arbitrary", "arbitrary")),
    )(slab, y_sum, modsel, fnw)


def _axial_perm():
    dd = np.arange(MLA_ROPE)
    return np.where(dd % 16 < 8, dd + 8, dd - 8)


def _w_in_pieces():
    o = dict(zip(("cq", "ckv", "kr", "rq", "rk", "rv", "rg", "gq", "gk", "gv", "gr", "gz", "bg"),
                 np.cumsum((0,) + IN_SPLITS).tolist()))
    def permuted(start, perm, scale=1.0):
        out = []
        for p in perm:
            if out and out[-1][0] + out[-1][1] == start + p:
                out[-1] = (out[-1][0], out[-1][1] + 1, scale)
            else:
                out.append((start + int(p), 1, scale))
        return out

    k_scale, q_scale = RET_DK ** -0.5, GLA_DK ** -0.5
    hw = RET_HEADS * RET_DK
    tail = HEAD_SLOT - MLA_NOPE - MLA_ROPE
    pieces = [(o["cq"], MLA_Q_RANK, 1.0), (o["ckv"], MLA_KV_RANK, 1.0),
              (None, MLA_NOPE, 0.0), (o["kr"], MLA_ROPE, 1.0), (None, tail, 0.0),
              (None, MLA_NOPE, 0.0)] + permuted(o["kr"], _axial_perm()) + [(None, tail, 0.0),
              (o["gz"], 2 * GLA_GATE_RANK, 1.0), (None, LANES - 2 * GLA_GATE_RANK, 0.0),
              (o["rq"], hw, 1.0), (o["rk"], hw, k_scale),
              (o["rv"], RET_HEADS * RET_DV, 1.0), (o["rg"], RET_HEADS * RET_DV, 1.0),
              (o["gq"], GLA_HEADS * GLA_DK, q_scale), (o["gk"], GLA_HEADS * GLA_DK, 1.0),
              (o["gv"], GLA_HEADS * GLA_DV, 1.0), (o["gr"], GLA_HEADS * GLA_DV, 1.0)]
    return pieces, o["bg"]


def _relayout_kernel(w_ref, w1_ref, bg_ref):
    pieces, bg0 = _w_in_pieces()
    rows = w_ref.shape[1]
    cols = []
    for start, width, scale in pieces:
        if start is None:
            cols.append(jnp.zeros((rows, width), BF16))
        else:
            piece = w_ref[0, :, start:start + width]
            cols.append(piece if scale == 1.0 else (piece.astype(F32) * scale).astype(BF16))
    w1_ref[...] = jnp.concatenate(cols, axis=1)
    bg_ref[...] = w_ref[0, :, bg0:bg0 + N_BRANCH * D_MODEL]


def _prep_w_in(w_in, l):
    _, d, n_in = w_in.shape
    rows = TOKEN_TILE
    n1 = sum(p[1] for p in _w_in_pieces()[0])
    nbg = N_BRANCH * D_MODEL
    return pl.pallas_call(
        _relayout_kernel,
        grid=(d // rows,),
        in_specs=[pl.BlockSpec((1, rows, n_in), lambda r: (l, r, 0))],
        out_specs=[pl.BlockSpec((rows, n1), lambda r: (r, 0)), pl.BlockSpec((rows, nbg), lambda r: (r, 0))],
        out_shape=[jax.ShapeDtypeStruct((d, n1), BF16), jax.ShapeDtypeStruct((d, nbg), BF16)],
        compiler_params=_params(("arbitrary",)),
    )(w_in)


def _prep_mla(w_uq, w_ukv):
    r = w_uq.shape[0]
    scale = (MLA_NOPE + MLA_ROPE) ** -0.5 * np.log2(np.e)
    wq3 = w_uq.reshape(r, MLA_HEADS, MLA_NOPE + MLA_ROPE) * scale
    pad = jnp.zeros((r, MLA_HEADS, HEAD_SLOT - MLA_NOPE - MLA_ROPE), w_uq.dtype)
    wq = jnp.concatenate([wq3, pad], axis=-1)
    wqs = jnp.concatenate([jnp.zeros((r, MLA_HEADS, MLA_NOPE), w_uq.dtype),
                           wq3[:, :, MLA_NOPE:][:, :, _axial_perm()], pad], axis=-1)
    rk = w_ukv.shape[0]
    wkv3 = w_ukv.reshape(rk, MLA_HEADS, MLA_NOPE + MLA_V)
    wk = jnp.concatenate([wkv3[:, :, :MLA_NOPE],
                          jnp.zeros((rk, MLA_HEADS, HEAD_SLOT - MLA_NOPE), w_ukv.dtype)], axis=-1)
    wv = wkv3[:, :, MLA_NOPE:]
    flat = lambda a: a.reshape(a.shape[0], -1).astype(BF16)
    return flat(wq), flat(wqs), flat(wk), flat(wv)


def _prep_gla_gate(w_gate, b_gate):
    hw = GLA_HEADS * GLA_DK
    wg2 = jnp.zeros((LANES, 2 * hw), F32)
    wg2 = wg2.at[0:GLA_GATE_RANK, 0:hw].set(w_gate[0])
    wg2 = wg2.at[GLA_GATE_RANK:2 * GLA_GATE_RANK, hw:2 * hw].set(w_gate[1])
    return wg2, b_gate.reshape(1, 2 * hw)


def _rope_tables(n_ctx, seq):
    n_rows = seq // GRID_W
    half = MLA_ROPE // 4
    inv = ROPE_BASE ** (-jnp.arange(half, dtype=F32) / half)
    ang_r = jnp.arange(n_rows, dtype=F32)[:, None] * inv[None, :]
    ang_c = jnp.arange(GRID_W, dtype=F32)[:, None] * inv[None, :]
    by_row = lambda a: jnp.broadcast_to(a[:, None, :], (n_rows, GRID_W, half)).reshape(seq, half)
    by_col = lambda a: jnp.broadcast_to(a[None, :, :], (n_rows, GRID_W, half)).reshape(seq, half)
    cos_r, sin_r, cos_c, sin_c = by_row(jnp.cos(ang_r)), by_row(jnp.sin(ang_r)), by_col(jnp.cos(ang_c)), by_col(jnp.sin(ang_c))
    ctx = lambda a, fill: jnp.concatenate([jnp.full((n_ctx, a.shape[1]), fill, F32), a], axis=0)
    spread = lambda a, m: jnp.dot(a, jnp.asarray(m, F32), precision=HI)
    dd = np.arange(HEAD_SLOT) - MLA_NOPE
    active = (dd >= 0) & (dd < MLA_ROPE)
    src = (np.clip(dd, 0, MLA_ROPE - 1) // 16) * half + np.clip(dd, 0, MLA_ROPE - 1) % half
    pick = (np.arange(2 * half)[:, None] == src[None, :]) & active[None, :]
    sign = np.where((dd % 16) < half, -1.0, 1.0)[None, :]
    cosq = spread(ctx(jnp.concatenate([cos_r, cos_c], axis=1), 1.0), pick) + jnp.asarray(~active, F32)[None, :]
    sinq = spread(ctx(jnp.concatenate([sin_r, sin_c], axis=1), 0.0), pick * sign)
    rhalf = RET_DK // 2
    rinv = ROPE_BASE ** (-jnp.arange(rhalf, dtype=F32) / rhalf)
    a_hi = (GRID_W * jnp.arange(n_rows, dtype=F32))[:, None, None] * rinv
    a_lo = jnp.arange(GRID_W, dtype=F32)[None, :, None] * rinv
    cos1 = (jnp.cos(a_hi) * jnp.cos(a_lo) - jnp.sin(a_hi) * jnp.sin(a_lo)).reshape(seq, rhalf)
    sin1 = (jnp.sin(a_hi) * jnp.cos(a_lo) + jnp.cos(a_hi) * jnp.sin(a_lo)).reshape(seq, rhalf)
    rl = np.arange(RET_HEADS * RET_DK) % RET_DK
    rpick = np.arange(rhalf)[:, None] == (rl % rhalf)[None, :]
    rsign = np.where(rl < rhalf, -1.0, 1.0)[None, :]
    cosr = spread(ctx(cos1, 1.0), rpick)
    sinr = spread(ctx(sin1, 0.0), rpick * rsign)
    return cosq, sinq, cosr, sinr


def kernel(x, c, ctx, c_ctx, mod_w, mod_b, norm1_w, norm2_w, w_in, mla_q_norm, mla_w_uq, mla_kv_norm, mla_w_ukv, ret_decay_logit, ret_norm_w, gla_w_gate, gla_b_gate, gla_norm_w, w_branch, w_out, ffn_w_gate, ffn_w_up, ffn_w_down, moe_router, moe_w_gate, moe_w_up, moe_w_down, final_norm_w):
    b, seq, d = x.shape
    n_ctx = ctx.shape[1]
    depth = mod_w.shape[0]
    assert d == D_MODEL and seq % GRID_W == 0
    assert n_ctx % TOKEN_TILE == 0 and n_ctx % SCAN_CHUNK == 0
    assert seq % TOKEN_TILE == 0 and seq % SCAN_CHUNK == 0
    nct = n_ctx // TOKEN_TILE

    rows = 8 * ((b + 1 + 7) // 8)
    cvec = jnp.zeros((rows, d), F32).at[0:b].set(c).at[b].set(c_ctx)
    mod = _modulation(cvec, mod_w, mod_b)
    cosq, sinq, cosr, sinr = _rope_tables(n_ctx, seq)
    fnw = final_norm_w.reshape(1, d)
    row2 = lambda a: a.reshape(1, -1)

    xs = (ctx, x)
    w_in_bf16 = w_in.astype(BF16)
    for l in range(depth):
        last = l == depth - 1
        modsel = jnp.stack([jnp.broadcast_to(mod[l, b], (b, 6 * d)), mod[l, 0:b]], axis=1)
        w1, wbg = _prep_w_in(w_in_bf16, l)
        wq, wqs, wk, wv = _prep_mla(mla_w_uq[l], mla_w_ukv[l])
        wg2, bg2 = _prep_gla_gate(gla_w_gate[l], gla_b_gate[l])
        nw1, nw2 = row2(norm1_w[l]), row2(norm2_w[l])

        (q, k, v, rq, rk, rv, rg, gq, gk, gv, gr, la) = _inproj(
            xs, modsel, nw1, w1, row2(mla_q_norm[l]), wq, wqs, row2(mla_kv_norm[l]), wk, wv,
            wg2, bg2, cosq, sinq, cosr, sinr, nct)
        oa = _attention(q, k, v, nct)
        dl = jnp.broadcast_to(ret_decay_logit[l][:, :, None, None], (2, RET_HEADS, 8, LANES))
        rof, rob, gof, gob = _scans(dl, rq, rk, rv, gq, gk, gv, la, n_ctx)
        is_moe = l % 2 == 1
        xs = _merge(xs, modsel, nw1, oa, rof, rob, rg, gof, gob, gr, wbg,
                    w_branch[l].astype(BF16), w_out[l].astype(BF16),
                    row2(ret_norm_w[l]), row2(gla_norm_w[l]), nct, is_moe and last)
        i = l // 2
        if not is_moe:
            xs = _ffn(xs, modsel, nw2, ffn_w_gate[i].astype(BF16), ffn_w_up[i].astype(BF16),
                      ffn_w_down[i].astype(BF16), fnw, nct, last)
            continue
        rt = jnp.pad(moe_router[i], ((0, 0), (0, LANES - N_EXPERTS)))
        experts = (moe_w_gate[i].astype(BF16), moe_w_up[i].astype(BF16), moe_w_down[i].astype(BF16))
        if last:
            xs = _moe(xs, 1, modsel, nw2, rt, *experts, fnw, True)
        else:
            xs = jnp.concatenate([_moe(xs[:, :n_ctx], 0, modsel, nw2, rt, *experts, fnw, False),
                                  _moe(xs[:, n_ctx:], 1, modsel, nw2, rt, *experts, fnw, False)], axis=1)
    return xs
```
